```python
import math
import jax
import jax.numpy as jnp
from jax import lax
import numpy as np

D_MODEL = 2048
BATCH = 4
SEQ = 2048
DEPTH = 4
DEC_BATCH = 128
DEC_SEQ = 8
PAST_LEN = 16384
PAGE_SIZE = 128

MIX_WIDTH = D_MODEL
GLA_WIDTH = 3 * MIX_WIDTH // 8
ML_WIDTH = 3 * MIX_WIDTH // 8
S5_WIDTH = MIX_WIDTH - GLA_WIDTH - ML_WIDTH
GLA_HEADS = 6
GLA_DV = GLA_WIDTH // GLA_HEADS
GLA_DK = GLA_DV // 2
GLA_RANK = 16
GLA_TAU = 16.0
GLA_CHUNK = 16
ML_HEADS = 6
ML_DH = ML_WIDTH // ML_HEADS
ML_CHUNK = 64
S5_CH = 16
S5_GROUPS = S5_WIDTH // S5_CH
S5_P = 64
PEER_HEADS = 8
PEER_DQ = 256
N_KEYS = 128
N_EXPERTS = N_KEYS * N_KEYS
PEER_TOPK = 16
PEER_BLOCK = 64
NORM_EPS = 1e-6
IN_SPLITS = (GLA_HEADS * GLA_DK, GLA_HEADS * GLA_DK, GLA_WIDTH, GLA_WIDTH, GLA_RANK,
             ML_WIDTH, ML_WIDTH, ML_WIDTH, ML_WIDTH, ML_HEADS, ML_HEADS, S5_WIDTH)
IN_COLS = 2 * GLA_HEADS * GLA_DK + 2 * GLA_WIDTH + GLA_RANK + 4 * ML_WIDTH + 2 * ML_HEADS + S5_WIDTH

kernel_name = 'hymba_gla_mlstm_s5_peer_adaln_step'


def _rms(x, gain=None):
    xf = x.astype(jnp.float32)
    y = xf * lax.rsqrt(jnp.mean(xf * xf, axis=-1, keepdims=True) + NORM_EPS)
    if gain is not None:
        y = y * gain.astype(jnp.float32)
    return y


def _split_cols(z):
    parts = []
    start = 0
    for width in IN_SPLITS:
        parts.append(z[..., start:start + width])
        start += width
    return parts


def _to_chunks(t, n, chunk):
    bsz = t.shape[0]
    return t.reshape(bsz, n, chunk, t.shape[2], t.shape[3]).transpose(1, 0, 3, 2, 4)


def _from_chunks(t):
    n, bsz, heads, chunk, d = t.shape
    return t.transpose(1, 0, 3, 2, 4).reshape(bsz, n * chunk, heads, d)


def _gla(q, k, v, g, r, w_gate_up, b_gate, gain, s0):
    bsz, seq = q.shape[0], q.shape[1]
    q = q.reshape(bsz, seq, GLA_HEADS, GLA_DK) * (GLA_DK ** -0.5)
    k = k.reshape(bsz, seq, GLA_HEADS, GLA_DK)
    v = v.reshape(bsz, seq, GLA_HEADS, GLA_DV)
    log_a = jax.nn.log_sigmoid(r @ w_gate_up.astype(jnp.float32) + b_gate.astype(jnp.float32)) / GLA_TAU
    log_a = log_a.reshape(bsz, seq, GLA_HEADS, GLA_DK)
    chunk = math.gcd(seq, GLA_CHUNK)
    n = seq // chunk
    mask = jnp.tril(jnp.ones((chunk, chunk), dtype=bool))

    def step(s, inp):
        qc, kc, vc, lac = inp
        b = jnp.cumsum(lac, axis=2)
        b_last = b[:, :, -1:, :]
        q_dec = qc * jnp.exp(b)
        k_inv = kc * jnp.exp(-b)
        o_inter = jnp.einsum('bhck,bhkv->bhcv', q_dec, s)
        att = jnp.where(mask, jnp.einsum('bhik,bhjk->bhij', q_dec, k_inv), 0.0)
        o = o_inter + jnp.einsum('bhij,bhjv->bhiv', att, vc)
        k_dec = kc * jnp.exp(b_last - b)
        s_new = jnp.exp(b_last[:, :, 0, :])[..., None] * s + jnp.einsum('bhck,bhcv->bhkv', k_dec, vc)
        return s_new, o

    s_fin, o = lax.scan(step, s0, (_to_chunks(q, n, chunk), _to_chunks(k, n, chunk),
                                  _to_chunks(v, n, chunk), _to_chunks(log_a, n, chunk)))
    o = _rms(_from_chunks(o)).reshape(bsz, seq, GLA_WIDTH)
    o = o * gain.astype(jnp.float32) * jax.nn.silu(g)
    return o, s_fin


def _mlstm(q, k, v, o_pre, i_pre, f_pre, b_i, b_f, gain, c0, n0, m0):
    bsz, seq = q.shape[0], q.shape[1]
    q = q.reshape(bsz, seq, ML_HEADS, ML_DH) * (ML_DH ** -0.5)
    k = k.reshape(bsz, seq, ML_HEADS, ML_DH)
    v = v.reshape(bsz, seq, ML_HEADS, ML_DH)
    log_i = i_pre + b_i.astype(jnp.float32)
    log_f = jax.nn.log_sigmoid(f_pre + b_f.astype(jnp.float32))
    chunk = math.gcd(seq, ML_CHUNK)
    n = seq // chunk
    mask = jnp.tril(jnp.ones((chunk, chunk), dtype=bool))

    def step(carry, inp):
        c_st, n_st, m_st = carry
        qc, kc, vc, ic, fc = inp
        b = jnp.cumsum(fc, axis=-1)
        g_inter = b + m_st[..., None]
        d_log = jnp.where(mask, b[..., :, None] - b[..., None, :] + ic[..., None, :], -jnp.inf)
        m_t = jnp.maximum(g_inter, jnp.max(d_log, axis=-1))
        w_inter = jnp.exp(g_inter - m_t)
        s = jnp.einsum('bhik,bhjk->bhij', qc, kc) * jnp.exp(d_log - m_t[..., None])
        num = w_inter[..., None] * jnp.einsum('bhck,bhkv->bhcv', qc, c_st) + jnp.einsum('bhij,bhjv->bhiv', s, vc)
        den = w_inter * jnp.einsum('bhck,bhk->bhc', qc, n_st) + jnp.sum(s, axis=-1)
        h = num / jnp.maximum(jnp.abs(den), jnp.exp(-m_t))[..., None]
        m_last = m_t[..., -1]
        decay = jnp.exp(b[..., -1] + m_st - m_last)
        w_k = jnp.exp(b[..., -1:] - b + ic - m_last[..., None])
        c_new = decay[..., None, None] * c_st + jnp.einsum('bhc,bhck,bhcv->bhkv', w_k, kc, vc)
        n_new = decay[..., None] * n_st + jnp.einsum('bhc,bhck->bhk', w_k, kc)
        return (c_new, n_new, m_last), h

    gates_i = _to_chunks(log_i[..., None], n, chunk)[..., 0]
    gates_f = _to_chunks(log_f[..., None], n, chunk)[..., 0]
    (c_fin, n_fin, m_fin), h = lax.scan(step, (c0, n0, m0),
                                        (_to_chunks(q, n, chunk), _to_chunks(k, n, chunk),
                                         _to_chunks(v, n, chunk), gates_i, gates_f))
    h = _rms(_from_chunks(h)).reshape(bsz, seq, ML_WIDTH)
    h = h * gain.astype(jnp.float32) * jax.nn.sigmoid(o_pre)
    return h, c_fin, n_fin, m_fin


def _cscan_combine(e1, e2):
    a1r, a1i, b1r, b1i = e1
    a2r, a2i, b2r, b2i = e2
    return (a2r * a1r - a2i * a1i, a2r * a1i + a2i * a1r,
            a2r * b1r - a2i * b1i + b2r, a2r * b1i + a2i * b1r + b2i)


def _s5(u, lam_re, lam_im, log_dt, b_re, b_im, c_re, c_im, d_skip, w_glu, b_glu, gain, h0_re, h0_im):
    f32 = jnp.float32
    bsz, seq = u.shape[0], u.shape[1]
    lam_re = lam_re.astype(f32)
    lam_im = lam_im.astype(f32)
    ug = u.reshape(bsz, seq, S5_GROUPS, S5_CH)
    dt = jnp.exp(log_dt.astype(f32))[:, None]
    mag = jnp.exp(lam_re * dt)
    ar = mag * jnp.cos(lam_im * dt)
    ai = mag * jnp.sin(lam_im * dt)
    den = lam_re * lam_re + lam_im * lam_im
    fr = ((ar - 1.0) * lam_re + ai * lam_im) / den
    fi = (ai * lam_re - (ar - 1.0) * lam_im) / den
    b_re = b_re.astype(f32)
    b_im = b_im.astype(f32)
    bbar_re = fr[..., None] * b_re - fi[..., None] * b_im
    bbar_im = fr[..., None] * b_im + fi[..., None] * b_re
    bu_re = jnp.einsum('blgh,gph->blgp', ug, bbar_re)
    bu_im = jnp.einsum('blgh,gph->blgp', ug, bbar_im)
    bu_re = bu_re.at[:, 0].add(ar * h0_re - ai * h0_im)
    bu_im = bu_im.at[:, 0].add(ar * h0_im + ai * h0_re)
    a_re = jnp.broadcast_to(ar, bu_re.shape)
    a_im = jnp.broadcast_to(ai, bu_im.shape)
    _, _, h_re, h_im = lax.associative_scan(_cscan_combine, (a_re, a_im, bu_re, bu_im), axis=1)
    y = (jnp.einsum('blgp,ghp->blgh', h_re, c_re.astype(f32))
         - jnp.einsum('blgp,ghp->blgh', h_im, c_im.astype(f32))
         + d_skip.astype(f32) * ug)
    y = jax.nn.gelu(y.reshape(bsz, seq, S5_WIDTH))
    out = y * jax.nn.sigmoid(y @ w_glu + b_glu)
    return _rms(out, gain), h_re[:, -1], h_im[:, -1]


def _peer(h, w_q, sub_keys, u_tab, v_tab):
    bsz, seq, dm = h.shape
    n_tok = bsz * seq
    pad = (-n_tok) % PEER_BLOCK
    xt = jnp.pad(h.reshape(n_tok, dm), ((0, pad), (0, 0)))
    blocks = xt.reshape(-1, PEER_BLOCK, dm)

    def one_block(xb):
        q = (xb @ w_q).reshape(PEER_BLOCK, PEER_HEADS, 2, PEER_DQ // 2)
        scores = jnp.einsum('bhsd,hskd->bhsk', q, sub_keys).astype(jnp.float32)
        sv, si = lax.top_k(scores, PEER_TOPK)
        cand = sv[:, :, 0, :, None] + sv[:, :, 1, None, :]
        cidx = si[:, :, 0, :, None] * N_KEYS + si[:, :, 1, None, :]
        cand = cand.reshape(PEER_BLOCK, PEER_HEADS, PEER_TOPK * PEER_TOPK)
        cidx = cidx.reshape(PEER_BLOCK, PEER_HEADS, PEER_TOPK * PEER_TOPK)
        top_v, top_pos = lax.top_k(cand, PEER_TOPK)
        eidx = jnp.take_along_axis(cidx, top_pos, axis=-1)
        gate = jax.nn.softmax(top_v, axis=-1)
        u_sel = jnp.take(u_tab, eidx, axis=0)
        act = jax.nn.gelu(jnp.einsum('bhkd,bd->bhk', u_sel, xb))
        v_sel = jnp.take(v_tab, eidx, axis=0)
        return jnp.einsum('bhk,bhkd->bd', gate * act, v_sel)

    out = lax.map(one_block, blocks).reshape(-1, dm)[:n_tok]
    return out.reshape(bsz, seq, dm)


def _layer(x, c, st, p):
    (w_ada, b_ada, norm_gain, w_in, gla_w_gate_up, gla_b_gate, gla_gain, ml_b_i, ml_b_f, ml_gain,
     s5_lre, s5_lim, s5_logdt, s5_bre, s5_bim, s5_cre, s5_cim, s5_d, s5_wglu, s5_bglu, s5_gain,
     w_out, peer_wq, peer_keys, peer_u, peer_v) = p
    s_gla, m_c, m_n, m_m, h_re, h_im = st
    mod = jax.nn.silu(c.astype(jnp.float32)) @ w_ada + b_ada
    sh1, sc1, g1, sh2, sc2, g2 = jnp.split(mod, 6, axis=-1)
    h = _rms(x, norm_gain[0]) * (1.0 + sc1[:, None]) + sh1[:, None]
    z = h @ w_in
    gq, gk, gv, gg, gr, mq, mk, mv, mo, mi, mf, su = _split_cols(z)
    o_gla, s_gla = _gla(gq, gk, gv, gg, gr, gla_w_gate_up, gla_b_gate, gla_gain, s_gla)
    o_ml, m_c, m_n, m_m = _mlstm(mq, mk, mv, mo, mi, mf, ml_b_i, ml_b_f, ml_gain, m_c, m_n, m_m)
    o_s5, h_re, h_im = _s5(su, s5_lre, s5_lim, s5_logdt, s5_bre, s5_bim, s5_cre, s5_cim, s5_d,
                           s5_wglu, s5_bglu, s5_gain, h_re, h_im)
    mix = jnp.concatenate([o_gla, o_ml, o_s5], axis=-1) @ w_out
    x = x + (g1[:, None] * mix).astype(x.dtype)
    h = _rms(x, norm_gain[1]) * (1.0 + sc2[:, None]) + sh2[:, None]
    x = x + (g2[:, None] * _peer(h, peer_wq, peer_keys, peer_u, peer_v)).astype(x.dtype)
    return x, (s_gla, m_c, m_n, m_m, h_re, h_im)


def _trunk(x, c, states, params, final_gain):
    new_states = []
    for layer in range(DEPTH):
        st = tuple(s[layer].astype(jnp.float32) for s in states)
        pl = tuple(w[layer] for w in params)
        x, st = _layer(x, c, st, pl)
        new_states.append(st)
    y = _rms(x, final_gain).astype(x.dtype)
    stacked = tuple(jnp.stack([ns[i] for ns in new_states]) for i in range(6))
    return y, stacked


def setup_inputs(seed: int = 0) -> dict:
    key = jax.random.key(seed)
    ks = iter(jax.random.split(key, 48))
    f32 = jnp.float32

    def nrm(shape, scale):
        return jax.random.normal(next(ks), shape, f32) * scale

    return {
        'x_prompt': nrm((BATCH, SEQ, D_MODEL), 1.0),
        'x_sample': nrm((DEC_BATCH, DEC_SEQ, D_MODEL), 1.0),
        'state_gla': nrm((DEPTH, DEC_BATCH, GLA_HEADS, GLA_DK, GLA_DV), 0.1),
        'state_mlstm_c': nrm((DEPTH, DEC_BATCH, ML_HEADS, ML_DH, ML_DH), 0.1),
        'state_mlstm_n': nrm((DEPTH, DEC_BATCH, ML_HEADS, ML_DH), 0.3),
        'state_mlstm_m': 2.0 + nrm((DEPTH, DEC_BATCH, ML_HEADS), 0.5),
        'state_s5_re': nrm((DEPTH, DEC_BATCH, S5_GROUPS, S5_P), 0.3),
        'state_s5_im': nrm((DEPTH, DEC_BATCH, S5_GROUPS, S5_P), 0.3),
        'c_prompt': nrm((BATCH, D_MODEL), 1.0),
        'c_sample': nrm((DEC_BATCH, D_MODEL), 1.0),
        'w_ada': nrm((DEPTH, D_MODEL, 6 * D_MODEL), 0.5 * D_MODEL ** -0.5),
        'b_ada': nrm((DEPTH, 6 * D_MODEL), 0.02),
        'norm_gain': 1.0 + nrm((DEPTH, 2, D_MODEL), 0.01),
        'w_in': nrm((DEPTH, D_MODEL, IN_COLS), D_MODEL ** -0.5),
        'gla_w_gate_up': nrm((DEPTH, GLA_RANK, GLA_HEADS * GLA_DK), GLA_RANK ** -0.5),
        'gla_b_gate': 1.0 + nrm((DEPTH, GLA_HEADS * GLA_DK), 0.1),
        'gla_norm_gain': 1.0 + nrm((DEPTH, GLA_WIDTH), 0.01),
        'ml_b_igate': nrm((DEPTH, ML_HEADS), 0.1),
        'ml_b_fgate': jnp.linspace(3.0, 6.0, ML_HEADS, dtype=f32)[None, :] + nrm((DEPTH, ML_HEADS), 0.1),
        'ml_norm_gain': 1.0 + nrm((DEPTH, ML_WIDTH), 0.01),
        's5_lambda_re': -0.5 + nrm((DEPTH, S5_GROUPS, S5_P), 0.01),
        's5_lambda_im': math.pi * jnp.arange(S5_P, dtype=f32)[None, None, :] + nrm((DEPTH, S5_GROUPS, S5_P), 0.01),
        's5_log_dt': jax.random.uniform(next(ks), (DEPTH, S5_GROUPS), f32, math.log(1e-3), math.log(1e-1)),
        's5_b_re': nrm((DEPTH, S5_GROUPS, S5_P, S5_CH), (2 * S5_CH) ** -0.5),
        's5_b_im': nrm((DEPTH, S5_GROUPS, S5_P, S5_CH), (2 * S5_CH) ** -0.5),
        's5_c_re': nrm((DEPTH, S5_GROUPS, S5_CH, S5_P), 0.5),
        's5_c_im': nrm((DEPTH, S5_GROUPS, S5_CH, S5_P), 0.5),
        's5_d': nrm((DEPTH, S5_GROUPS, S5_CH), 0.5),
        's5_w_glu': nrm((DEPTH, S5_WIDTH, S5_WIDTH), S5_WIDTH ** -0.5),
        's5_b_glu': nrm((DEPTH, S5_WIDTH), 0.02),
        's5_norm_gain': 1.0 + nrm((DEPTH, S5_WIDTH), 0.01),
        'w_out': nrm((DEPTH, MIX_WIDTH, D_MODEL), MIX_WIDTH ** -0.5),
        'peer_w_q': nrm((DEPTH, D_MODEL, PEER_HEADS * PEER_DQ), D_MODEL ** -0.5),
        'peer_sub_keys': nrm((DEPTH, PEER_HEADS, 2, N_KEYS, PEER_DQ // 2), (PEER_DQ // 2) ** -0.5),
        'peer_u': nrm((DEPTH, N_EXPERTS, D_MODEL), D_MODEL ** -0.5),
        'peer_v': nrm((DEPTH, N_EXPERTS, D_MODEL), 0.5),
        'final_gain': 1.0 + nrm((D_MODEL,), 0.01),
    }


def reference(x_prompt, x_sample, state_gla, state_mlstm_c, state_mlstm_n, state_mlstm_m,
              state_s5_re, state_s5_im, c_prompt, c_sample,
              w_ada, b_ada, norm_gain, w_in, gla_w_gate_up, gla_b_gate, gla_norm_gain,
              ml_b_igate, ml_b_fgate, ml_norm_gain,
              s5_lambda_re, s5_lambda_im, s5_log_dt, s5_b_re, s5_b_im, s5_c_re, s5_c_im, s5_d,
              s5_w_glu, s5_b_glu, s5_norm_gain, w_out, peer_w_q, peer_sub_keys, peer_u, peer_v,
              final_gain):
    f32 = jnp.float32
    params = (w_ada, b_ada, norm_gain, w_in, gla_w_gate_up, gla_b_gate, gla_norm_gain,
              ml_b_igate, ml_b_fgate, ml_norm_gain,
              s5_lambda_re, s5_lambda_im, s5_log_dt, s5_b_re, s5_b_im, s5_c_re, s5_c_im, s5_d,
              s5_w_glu, s5_b_glu, s5_norm_gain, w_out, peer_w_q, peer_sub_keys, peer_u, peer_v)
    bp = x_prompt.shape[0]
    zero_states = (jnp.zeros((DEPTH, bp, GLA_HEADS, GLA_DK, GLA_DV), f32),
                   jnp.zeros((DEPTH, bp, ML_HEADS, ML_DH, ML_DH), f32),
                   jnp.zeros((DEPTH, bp, ML_HEADS, ML_DH), f32),
                   jnp.zeros((DEPTH, bp, ML_HEADS), f32),
                   jnp.zeros((DEPTH, bp, S5_GROUPS, S5_P), f32),
                   jnp.zeros((DEPTH, bp, S5_GROUPS, S5_P), f32))
    sample_states = (state_gla, state_mlstm_c, state_mlstm_n, state_mlstm_m, state_s5_re, state_s5_im)
    y_prompt, sp = _trunk(x_prompt, c_prompt, zero_states, params, final_gain)
    y_sample, ss = _trunk(x_sample, c_sample, sample_states, params, final_gain)
    return (y_prompt, y_sample,
            sp[0].astype(state_gla.dtype), ss[0].astype(state_gla.dtype),
            sp[1].astype(state_mlstm_c.dtype), ss[1].astype(state_mlstm_c.dtype),
            sp[2].astype(state_mlstm_n.dtype), ss[2].astype(state_mlstm_n.dtype),
            sp[3].astype(state_mlstm_m.dtype), ss[3].astype(state_mlstm_m.dtype),
            sp[4].astype(state_s5_re.dtype), ss[4].astype(state_s5_re.dtype),
            sp[5].astype(state_s5_im.dtype), ss[5].astype(state_s5_im.dtype))
```

```python
import functools
import math

import numpy as np
import jax
import jax.numpy as jnp
from jax import lax
from jax.experimental import pallas as pl
from jax.experimental.pallas import tpu as pltpu

F32 = jnp.float32
BF16 = jnp.bfloat16
HIGHEST = lax.Precision.HIGHEST

D_MODEL = 2048
DEPTH = 4
GLA_HEADS = 6
GLA_DK = 64
GLA_DV = 128
GLA_RANK = 16
GLA_TAU = 16.0
GLA_CHUNK = 16
ML_HEADS = 6
ML_DH = 128
ML_CHUNK = 64
S5_WIDTH = 512
S5_CH = 16
S5_GROUPS = 32
S5_P = 64
S5_STATE = S5_GROUPS * S5_P
PEER_HEADS = 8
N_KEYS = 128
N_EXPERTS = N_KEYS * N_KEYS
PEER_TOPK = 16
NORM_EPS = 1e-6
GLA_WIDTH = GLA_HEADS * GLA_DV
ML_WIDTH = ML_HEADS * ML_DH

LANE = 128
SUBLANE = 8
TOKENS_PER_GROUP = 8
VMEM_LIMIT = 56 * 1024 * 1024

ZB_GQ, ZB_GK, ZB_GV, ZB_GG, ZB_GR = 0, 6, 12, 18, 24
ZB_MQ, ZB_MK, ZB_MV, ZB_MO, ZB_MIF = 25, 31, 37, 43, 49
ZB_SU = 50
Z_BLOCKS = 54
Z_COLS = Z_BLOCKS * LANE
F_GATE_LANE = 8


def _z_source_columns():
    src = np.full((Z_COLS,), -1, np.int32)
    off_gq, off_gk, off_gv, off_gg, off_gr = 0, 384, 768, 1536, 2304
    off_mq, off_mk, off_mv, off_mo, off_mi, off_mf, off_su = 2320, 3088, 3856, 4624, 5392, 5398, 5404
    for h in range(GLA_HEADS):
        for d in range(GLA_DK):
            src[(ZB_GQ + h) * LANE + d] = off_gq + h * GLA_DK + d
            src[(ZB_GK + h) * LANE + d] = off_gk + h * GLA_DK + d
        for d in range(GLA_DV):
            src[(ZB_GV + h) * LANE + d] = off_gv + h * GLA_DV + d
            src[(ZB_GG + h) * LANE + d] = off_gg + h * GLA_DV + d
    for d in range(GLA_RANK):
        src[ZB_GR * LANE + d] = off_gr + d
    for h in range(ML_HEADS):
        for d in range(ML_DH):
            src[(ZB_MQ + h) * LANE + d] = off_mq + h * ML_DH + d
            src[(ZB_MK + h) * LANE + d] = off_mk + h * ML_DH + d
            src[(ZB_MV + h) * LANE + d] = off_mv + h * ML_DH + d
            src[(ZB_MO + h) * LANE + d] = off_mo + h * ML_DH + d
        src[ZB_MIF * LANE + h] = off_mi + h
        src[ZB_MIF * LANE + F_GATE_LANE + h] = off_mf + h
    for d in range(S5_WIDTH):
        src[ZB_SU * LANE + d] = off_su + d
    return src


_Z_SRC = _z_source_columns()


def _cparams(semantics):
    return pltpu.CompilerParams(dimension_semantics=semantics, vmem_limit_bytes=VMEM_LIMIT)


def _const_spec(block_shape, index_map):
    return pl.BlockSpec(block_shape, index_map, pipeline_mode=pl.Buffered(1))


def _rms(x):
    return x * lax.rsqrt(jnp.mean(x * x, axis=-1, keepdims=True) + NORM_EPS)


def _dot(a, b):
    return jnp.dot(a, b, preferred_element_type=F32)


def _dot_nt(a, b):
    return lax.dot_general(a, b, (((1,), (1,)), ((), ())), preferred_element_type=F32)


def _dot_exact(a, b):
    return jnp.dot(a, b, preferred_element_type=F32, precision=HIGHEST)


def _adaln_kernel(c_ref, w_ref, b_ref, o_ref):
    c = c_ref[...]
    s = (c * jax.nn.sigmoid(c)).astype(BF16)
    o_ref[0] = _dot(s, w_ref[0].astype(BF16)) + b_ref[0]


def _adaln(c_all, w_ada, b_ada):
    n_rows = c_all.shape[0]
    n_out = w_ada.shape[-1]
    tn = 1024
    return pl.pallas_call(
        _adaln_kernel,
        grid=(DEPTH, n_out // tn),
        in_specs=[
            pl.BlockSpec((n_rows, D_MODEL), lambda l, j: (0, 0)),
            pl.BlockSpec((1, D_MODEL, tn), lambda l, j: (l, 0, j)),
            pl.BlockSpec((1, 1, tn), lambda l, j: (l, 0, j)),
        ],
        out_specs=pl.BlockSpec((1, n_rows, tn), lambda l, j: (l, 0, j)),
        out_shape=jax.ShapeDtypeStruct((DEPTH, n_rows, n_out), F32),
        compiler_params=_cparams(("parallel", "parallel")),
        name="adaln",
    )(c_all, w_ada, b_ada.reshape(DEPTH, 1, n_out))


def _modulated_group(x8, gain, sc_row, sh_row):
    return _rms(x8) * gain * (1.0 + sc_row) + sh_row


def _norm_proj_kernel(x_ref, sc_ref, sh_ref, gain_ref, w_ref, o_ref, h_scr, *, groups):
    @pl.when(pl.program_id(1) == 0)
    def _():
        gain = gain_ref[...]

        def body(p, carry):
            hs = []
            for u in range(2):
                g = p * 2 + u
                r0 = pl.multiple_of(g * TOKENS_PER_GROUP, TOKENS_PER_GROUP)
                hs.append(_modulated_group(x_ref[pl.ds(r0, TOKENS_PER_GROUP), :], gain,
                                           sc_ref[pl.ds(g, 1), :], sh_ref[pl.ds(g, 1), :]))
            r = pl.multiple_of(p * 2 * TOKENS_PER_GROUP, 2 * TOKENS_PER_GROUP)
            h_scr[pl.ds(r, 2 * TOKENS_PER_GROUP), :] = jnp.concatenate(hs, axis=0).astype(BF16)
            return carry

        lax.fori_loop(0, groups // 2, body, 0)

    o_ref[...] = _dot(h_scr[...], w_ref[...])


def _norm_proj(x, sc, sh, gain, w, *, tt, tn):
    n_tok = x.shape[0]
    n_out = w.shape[1]
    groups = tt // TOKENS_PER_GROUP
    return pl.pallas_call(
        functools.partial(_norm_proj_kernel, groups=groups),
        grid=(n_tok // tt, n_out // tn),
        in_specs=[
            pl.BlockSpec((tt, D_MODEL), lambda i, j: (i, 0)),
            pl.BlockSpec((groups, D_MODEL), lambda i, j: (i, 0)),
            pl.BlockSpec((groups, D_MODEL), lambda i, j: (i, 0)),
            pl.BlockSpec((1, D_MODEL), lambda i, j: (0, 0)),
            pl.BlockSpec((D_MODEL, tn), lambda i, j: (0, j)),
        ],
        out_specs=pl.BlockSpec((tt, tn), lambda i, j: (i, j)),
        out_shape=jax.ShapeDtypeStruct((n_tok, n_out), F32),
        scratch_shapes=[pltpu.VMEM((tt, D_MODEL), BF16)],
        compiler_params=_cparams(("parallel", "arbitrary")),
        name="norm_proj",
    )(x, sc, sh, gain, w)


def _chunk_masks(rows, chunk):
    shift = int(math.log2(chunk))
    ri = lax.broadcasted_iota(jnp.int32, (rows, rows), 0)
    ci = lax.broadcasted_iota(jnp.int32, (rows, rows), 1)
    same = (ri >> shift) == (ci >> shift)
    tri = jnp.logical_and(same, ci <= ri)
    return same, tri


def _gla_kernel(q_ref, k_ref, v_ref, g_ref, r_ref, wg_ref, bg_ref, gain_ref, s0_ref,
                o_ref, sfin_ref, st_scr, *, rows, chunk, carry):
    n_chunks = rows // chunk
    shift = int(math.log2(chunk))
    same, tri = _chunk_masks(rows, chunk)
    m_tri = jnp.where(tri, 1.0, 0.0).astype(F32)
    m_all = jnp.where(same, 1.0, 0.0).astype(F32)

    q = q_ref[...] * (GLA_DK ** -0.5)
    k = k_ref[...]
    v = v_ref[...]
    la = jax.nn.log_sigmoid(_dot(r_ref[...].astype(BF16), wg_ref[0]) + bg_ref[0]) / GLA_TAU
    bl = _dot_exact(m_tri, la)
    bt = _dot_exact(m_all, la)
    qd = (q * jnp.exp(bl)).astype(BF16)
    ki = (k * jnp.exp(-bl)).astype(BF16)
    kd = (k * jnp.exp(bt - bl)).astype(BF16)
    att = jnp.where(tri, _dot_nt(qd, ki), 0.0)
    o_intra = _dot(att.astype(BF16), v.astype(BF16))

    v_t = v.T
    lane_chunk = lax.broadcasted_iota(jnp.int32, (GLA_DV, rows), 1) >> shift
    zero_pad = jnp.zeros((LANE - GLA_DK, GLA_DV), F32)

    def load_state(c):
        return jnp.concatenate([s0_ref[c, 0], zero_pad], axis=0).T

    if carry:
        @pl.when(pl.program_id(2) == 0)
        def _():
            st_scr[...] = load_state(0)
        s_t = st_scr[...]

    sliced = chunk % (2 * SUBLANE) == 0
    row_chunk = lax.broadcasted_iota(jnp.int32, (rows, GLA_DV), 0) >> shift
    o_inter = [] if sliced else jnp.zeros((rows, GLA_DV), F32)
    for c in range(n_chunks):
        lo = c * chunk
        if not carry:
            s_t = load_state(c)
        if sliced:
            o_inter.append(_dot_nt(qd[lo:lo + chunk], s_t.astype(BF16)))
        else:
            o_inter = jnp.where(row_chunk == c, _dot_nt(qd, s_t.astype(BF16)), o_inter)
        decay = jnp.exp(bt[lo:lo + 1, :])
        v_c = jnp.where(lane_chunk == c, v_t, 0.0).astype(BF16)
        s_t = s_t * decay + _dot(v_c, kd)
        if not carry:
            sfin_ref[c, 0] = s_t.T[0:GLA_DK, :]

    if carry:
        st_scr[...] = s_t

        @pl.when(pl.program_id(2) == pl.num_programs(2) - 1)
        def _():
            sfin_ref[0, 0] = s_t.T[0:GLA_DK, :]

    o = o_intra + (jnp.concatenate(o_inter, axis=0) if sliced else o_inter)
    g = g_ref[...]
    o_ref[...] = _rms(o) * gain_ref[0] * (g * jax.nn.sigmoid(g))


def _gla(z, wg, bg, gain, s0, *, n_seq, seq_blocks, rows, chunk, row_block0, carry):
    per_block = 1 if carry else rows // chunk
    n_states = s0.shape[0]
    if carry:
        grid = (n_seq, GLA_HEADS, seq_blocks)
        rb = lambda b, h, t: row_block0 + b * seq_blocks + t
        sb = lambda b, h, t: b
        sem = ("parallel", "parallel", "arbitrary")
    else:
        grid = (n_seq, GLA_HEADS)
        rb = lambda b, h: row_block0 + b
        sb = lambda b, h: b
        sem = ("parallel", "parallel")

    def zspec(block0):
        return pl.BlockSpec((rows, LANE), lambda *a: (rb(*a), block0 + a[1]))

    hspec = lambda shape: pl.BlockSpec(shape, lambda *a: (a[1],) + (0,) * (len(shape) - 1))
    state_spec = pl.BlockSpec((per_block, 1, GLA_DK, GLA_DV), lambda *a: (sb(*a), a[1], 0, 0))
    n_rows_out = n_seq * seq_blocks * rows if carry else n_seq * rows
    out_rb = (lambda *a: rb(*a) - row_block0)
    return pl.pallas_call(
        functools.partial(_gla_kernel, rows=rows, chunk=chunk, carry=carry),
        grid=grid,
        in_specs=[zspec(ZB_GQ), zspec(ZB_GK), zspec(ZB_GV), zspec(ZB_GG),
                  pl.BlockSpec((rows, LANE), lambda *a: (rb(*a), ZB_GR)),
                  hspec((1, LANE, LANE)), hspec((1, 1, LANE)), hspec((1, 1, LANE)), state_spec],
        out_specs=[pl.BlockSpec((rows, LANE), lambda *a: (out_rb(*a), a[1])), state_spec],
        out_shape=[jax.ShapeDtypeStruct((n_rows_out, GLA_WIDTH), F32),
                   jax.ShapeDtypeStruct((n_states, GLA_HEADS, GLA_DK, GLA_DV), F32)],
        scratch_shapes=[pltpu.VMEM((GLA_DV, LANE), F32)],
        compiler_params=_cparams(sem),
        name="gla_seq" if carry else "gla_step",
    )(z, z, z, z, z, wg, bg, gain, s0)


def _mlstm_kernel(q_ref, k_ref, v_ref, og_ref, gate_ref, bif_ref, gain_ref, c0_ref, n0_ref, m0_ref,
                  h_ref, cfin_ref, nfin_ref, mfin_ref, c_scr, n_scr, m_scr, *, rows, chunk, carry):
    n_chunks = rows // chunk
    shift = int(math.log2(chunk))
    head = pl.program_id(1)
    same, tri = _chunk_masks(rows, chunk)
    m_tri = jnp.where(tri, 1.0, 0.0).astype(F32)
    m_all = jnp.where(same, 1.0, 0.0).astype(F32)
    neg_inf = jnp.float32(-jnp.inf)

    x = gate_ref[...] + bif_ref[...]
    log_f = jax.nn.log_sigmoid(x)
    b_cum = _dot_exact(m_tri, log_f)
    b_tot = _dot_exact(m_all, log_f)
    lane = lax.broadcasted_iota(jnp.int32, (rows, LANE), 1)
    y = jnp.where(lane < F_GATE_LANE, x, b_cum)
    pick_i = lane == head
    pick_b = lane == head + F_GATE_LANE
    i_col = jnp.sum(jnp.where(pick_i, y, 0.0), axis=-1, keepdims=True)
    b_col = jnp.sum(jnp.where(pick_b, y, 0.0), axis=-1, keepdims=True)
    bt_col = jnp.sum(jnp.where(pick_b, b_tot, 0.0), axis=-1, keepdims=True)
    sub = lax.broadcasted_iota(jnp.int32, (LANE, rows), 0)
    y_t = y.T
    i_row = jnp.sum(jnp.where(sub == head, y_t, 0.0), axis=0, keepdims=True)
    b_row = jnp.sum(jnp.where(sub == head + F_GATE_LANE, y_t, 0.0), axis=0, keepdims=True)
    bt_row = jnp.sum(jnp.where(sub == head + F_GATE_LANE, b_tot.T, 0.0), axis=0, keepdims=True)

    a_col = bt_col - b_col + i_col
    a_row = bt_row - b_row + i_row
    mloc_col = jnp.max(jnp.where(same, a_row, neg_inf), axis=-1, keepdims=True)

    q = q_ref[...] * (ML_DH ** -0.5)
    k = k_ref[...]
    qb = q.astype(BF16)
    kb = k.astype(BF16)
    vb = v_ref[...].astype(BF16)
    kw = k * jnp.exp(a_col - mloc_col)
    kw_t = kw.T
    lane_chunk = lax.broadcasted_iota(jnp.int32, (ML_DH, rows), 1) >> shift

    def advance(c, c_st, n_st, m_st, m_last):
        hi = (c + 1) * chunk
        decay = jnp.exp(bt_col[hi - 1:hi] + m_st - m_last)
        scale = jnp.exp(mloc_col[hi - 1:hi] - m_last)
        kw_c = jnp.where(lane_chunk == c, kw_t, 0.0).astype(BF16)
        c_new = decay * c_st + scale * _dot(kw_c, vb)
        n_new = decay * n_st + scale * jnp.sum(kw[hi - chunk:hi], axis=0, keepdims=True)
        return c_new, n_new

    if carry:
        @pl.when(pl.program_id(2) == 0)
        def _():
            c_scr[...] = c0_ref[0, 0]
            n_scr[...] = n0_ref[0, 0]
            m_scr[...] = m0_ref[0, 0]
        c_st = c_scr[...]
        n_st = n_scr[...]
        m_st = m_scr[:, 0:1]
        ri = lax.broadcasted_iota(jnp.int32, (chunk, chunk), 0)
        ci = lax.broadcasted_iota(jnp.int32, (chunk, chunk), 1)
        tri_c = ci <= ri
        h_chunks = []
        for c in range(n_chunks):
            lo = c * chunk
            hi = lo + chunk
            bc = b_col[lo:hi]
            d_log = jnp.where(tri_c, bc - b_row[:, lo:hi] + i_row[:, lo:hi], neg_inf)
            g_inter = bc + m_st
            m_t = jnp.maximum(g_inter, jnp.max(d_log, axis=-1, keepdims=True))
            w_inter = jnp.exp(g_inter - m_t)
            s = _dot_nt(qb[lo:hi], kb[lo:hi]) * jnp.exp(d_log - m_t)
            num = w_inter * _dot(qb[lo:hi], c_st.astype(BF16)) + _dot(s.astype(BF16), vb[lo:hi])
            den = (w_inter * jnp.sum(q[lo:hi] * n_st, axis=-1, keepdims=True)
                   + jnp.sum(s, axis=-1, keepdims=True))
            h_chunks.append(num / jnp.maximum(jnp.abs(den), jnp.exp(-m_t)))
            m_last = m_t[chunk - 1:chunk]
            c_st, n_st = advance(c, c_st, n_st, m_st, m_last)
            m_st = m_last
        h = jnp.concatenate(h_chunks, axis=0)
        c_scr[...] = c_st
        n_scr[...] = n_st
        m_scr[...] = jnp.broadcast_to(m_st, (1, LANE))

        @pl.when(pl.program_id(2) == pl.num_programs(2) - 1)
        def _():
            cfin_ref[0, 0] = c_st
            nfin_ref[0, 0] = n_st
            mfin_ref[0, 0] = jnp.broadcast_to(m_st, (1, LANE))
    else:
        row_chunk1 = lax.broadcasted_iota(jnp.int32, (rows, 1), 0) >> shift
        row_chunk = lax.broadcasted_iota(jnp.int32, (rows, ML_DH), 0) >> shift
        m_rows = jnp.zeros((rows, 1), F32)
        n_rows = jnp.zeros((rows, ML_DH), F32)
        qc = jnp.zeros((rows, ML_DH), F32)
        for c in range(n_chunks):
            m_rows = jnp.where(row_chunk1 == c, m0_ref[c, 0][:, 0:1], m_rows)
            n_rows = jnp.where(row_chunk == c, n0_ref[c, 0], n_rows)
            qc = jnp.where(row_chunk == c, _dot(qb, c0_ref[c, 0].astype(BF16)), qc)
        d_log = jnp.where(tri, b_col - b_row + i_row, neg_inf)
        g_inter = b_col + m_rows
        m_t = jnp.maximum(g_inter, jnp.max(d_log, axis=-1, keepdims=True))
        w_inter = jnp.exp(g_inter - m_t)
        s = _dot_nt(qb, kb) * jnp.exp(d_log - m_t)
        num = w_inter * qc + _dot(s.astype(BF16), vb)
        den = w_inter * jnp.sum(q * n_rows, axis=-1, keepdims=True) + jnp.sum(s, axis=-1, keepdims=True)
        h = num / jnp.maximum(jnp.abs(den), jnp.exp(-m_t))
        for c in range(n_chunks):
            hi = (c + 1) * chunk
            m_last = m_t[hi - 1:hi]
            c_new, n_new = advance(c, c0_ref[c, 0], n0_ref[c, 0], m0_ref[c, 0][:, 0:1], m_last)
            cfin_ref[c, 0] = c_new
            nfin_ref[c, 0] = n_new
            mfin_ref[c, 0] = jnp.broadcast_to(m_last, (1, LANE))

    h_ref[...] = _rms(h) * gain_ref[0] * jax.nn.sigmoid(og_ref[...])


def _mlstm(z, bif, gain, c0, n0, m0, *, n_seq, seq_blocks, rows, chunk, row_block0, carry):
    per_block = 1 if carry else rows // chunk
    n_states = c0.shape[0]
    if carry:
        grid = (n_seq, ML_HEADS, seq_blocks)
        rb = lambda b, h, t: row_block0 + b * seq_blocks + t
        sem = ("parallel", "parallel", "arbitrary")
    else:
        grid = (n_seq, ML_HEADS)
        rb = lambda b, h: row_block0 + b
        sem = ("parallel", "parallel")

    def zspec(block0):
        return pl.BlockSpec((rows, LANE), lambda *a: (rb(*a), block0 + a[1]))

    def sspec(d2, d3):
        return pl.BlockSpec((per_block, 1, d2, d3), lambda *a: (a[0], a[1], 0, 0))

    n_rows_out = n_seq * seq_blocks * rows if carry else n_seq * rows
    return pl.pallas_call(
        functools.partial(_mlstm_kernel, rows=rows, chunk=chunk, carry=carry),
        grid=grid,
        in_specs=[zspec(ZB_MQ), zspec(ZB_MK), zspec(ZB_MV), zspec(ZB_MO),
                  pl.BlockSpec((rows, LANE), lambda *a: (rb(*a), ZB_MIF)),
                  pl.BlockSpec((1, LANE), lambda *a: (0, 0)),
                  pl.BlockSpec((1, 1, LANE), lambda *a: (a[1], 0, 0)),
                  sspec(ML_DH, ML_DH), sspec(1, ML_DH), sspec(1, LANE)],
        out_specs=[pl.BlockSpec((rows, LANE), lambda *a: (rb(*a) - row_block0, a[1])),
                   sspec(ML_DH, ML_DH), sspec(1, ML_DH), sspec(1, LANE)],
        out_shape=[jax.ShapeDtypeStruct((n_rows_out, ML_WIDTH), F32),
                   jax.ShapeDtypeStruct((n_states, ML_HEADS, ML_DH, ML_DH), F32),
                   jax.ShapeDtypeStruct((n_states, ML_HEADS, 1, ML_DH), F32),
                   jax.ShapeDtypeStruct((n_states, ML_HEADS, 1, LANE), F32)],
        scratch_shapes=[pltpu.VMEM((ML_DH, ML_DH), F32), pltpu.VMEM((1, ML_DH), F32),
                        pltpu.VMEM((1, LANE), F32)],
        compiler_params=_cparams(sem),
        name="mlstm_seq" if carry else "mlstm_step",
    )(z, z, z, z, z, bif, gain, c0, n0, m0)


def _s5_disc_kernel(lre_ref, lim_ref, ldt_ref, bre_ref, bim_ref, a_ref, bb_ref):
    lam_re = lre_ref[0]
    lam_im = lim_ref[0]
    dt = jnp.exp(ldt_ref[0])
    mag = jnp.exp(lam_re * dt)
    ar = mag * jnp.cos(lam_im * dt)
    ai = mag * jnp.sin(lam_im * dt)
    den = lam_re * lam_re + lam_im * lam_im
    fr = ((ar - 1.0) * lam_re + ai * lam_im) / den
    fi = (ai * lam_re - (ar - 1.0) * lam_im) / den
    b_re = bre_ref[0]
    b_im = bim_ref[0]
    a_ref[0, 0:1, :] = ar
    a_ref[0, 1:2, :] = ai
    bb_ref[0, 0] = fr * b_re - fi * b_im
    bb_ref[0, 1] = fr * b_im + fi * b_re


def _s5_discretise(lam_re, lam_im, log_dt, b_re, b_im):
    flat = lambda t: t.reshape(DEPTH, 1, S5_STATE)
    ldt = jnp.broadcast_to(log_dt[:, :, None], (DEPTH, S5_GROUPS, S5_P))
    chan = lambda t: t.reshape(DEPTH, S5_STATE, S5_CH).transpose(0, 2, 1)
    row = pl.BlockSpec((1, 1, S5_STATE), lambda l: (l, 0, 0))
    mat = pl.BlockSpec((1, S5_CH, S5_STATE), lambda l: (l, 0, 0))
    return pl.pallas_call(
        _s5_disc_kernel,
        grid=(DEPTH,),
        in_specs=[row, row, row, mat, mat],
        out_specs=[pl.BlockSpec((1, 2, S5_STATE), lambda l: (l, 0, 0)),
                   pl.BlockSpec((1, 2, S5_CH, S5_STATE), lambda l: (l, 0, 0, 0))],
        out_shape=[jax.ShapeDtypeStruct((DEPTH, 2, S5_STATE), F32),
                   jax.ShapeDtypeStruct((DEPTH, 2, S5_CH, S5_STATE), F32)],
        compiler_params=_cparams(("parallel",)),
        name="s5_disc",
    )(flat(lam_re), flat(lam_im), flat(ldt), chan(b_re), chan(b_im))


def _s5_kernel(u_ref, h0_ref, a_ref, wb_ref, wc_ref, d_ref, wglu_ref, bglu_ref, gain_ref,
               y_ref, hfin_ref, bu_scr, h_scr, *, steps, batch, lane_width):
    n_rows = steps * batch

    @pl.when(pl.program_id(0) == 0)
    def _():
        h_scr[...] = h0_ref[...]

    u = u_ref[...].reshape(n_rows, S5_WIDTH)
    bu_scr[...] = _dot(u.astype(BF16), wb_ref[...]).reshape(steps, batch, 2 * S5_STATE)

    for j in range(S5_STATE // lane_width):
        re = slice(j * lane_width, (j + 1) * lane_width)
        im = slice(S5_STATE + j * lane_width, S5_STATE + (j + 1) * lane_width)
        ar = a_ref[0:1, re]
        ai = a_ref[1:2, re]

        def step(t, carry):
            hr, hi = carry
            nr = ar * hr - ai * hi + bu_scr[t, :, re]
            ni = ar * hi + ai * hr + bu_scr[t, :, im]
            bu_scr[t, :, re] = nr
            bu_scr[t, :, im] = ni
            return nr, ni

        hr, hi = lax.fori_loop(0, steps, step, (h_scr[:, re], h_scr[:, im]), unroll=min(steps, 8))
        h_scr[:, re] = hr
        h_scr[:, im] = hi

    hs = bu_scr[...].reshape(n_rows, 2 * S5_STATE).astype(BF16)
    y = _dot(hs, wc_ref[...]) + d_ref[...] * u
    y = jax.nn.gelu(y)
    out = y * jax.nn.sigmoid(_dot(y.astype(BF16), wglu_ref[...]) + bglu_ref[...])
    y_ref[...] = (_rms(out) * gain_ref[...]).reshape(steps, batch, S5_WIDTH)

    @pl.when(pl.program_id(0) == pl.num_programs(0) - 1)
    def _():
        hfin_ref[...] = h_scr[...]


def _s5(u_t, h0, a, wb, wc, d, wglu, bglu, gain, *, steps, lane_width):
    seq, batch, _ = u_t.shape
    const = lambda shape: _const_spec(shape, lambda t: (0,) * len(shape))
    return pl.pallas_call(
        functools.partial(_s5_kernel, steps=steps, batch=batch, lane_width=lane_width),
        grid=(seq // steps,),
        in_specs=[pl.BlockSpec((steps, batch, S5_WIDTH), lambda t: (t, 0, 0)),
                  const((batch, 2 * S5_STATE)), const((2, S5_STATE)),
                  const((S5_WIDTH, 2 * S5_STATE)), const((2 * S5_STATE, S5_WIDTH)),
                  const((1, S5_WIDTH)), const((S5_WIDTH, S5_WIDTH)), const((1, S5_WIDTH)),
                  const((1, S5_WIDTH))],
        out_specs=[pl.BlockSpec((steps, batch, S5_WIDTH), lambda t: (t, 0, 0)),
                   pl.BlockSpec((batch, 2 * S5_STATE), lambda t: (0, 0))],
        out_shape=[jax.ShapeDtypeStruct((seq, batch, S5_WIDTH), F32),
                   jax.ShapeDtypeStruct((batch, 2 * S5_STATE), F32)],
        scratch_shapes=[pltpu.VMEM((steps, batch, 2 * S5_STATE), F32),
                        pltpu.VMEM((batch, 2 * S5_STATE), F32)],
        compiler_params=_cparams(("arbitrary",)),
        name="s5",
    )(u_t, h0, a, wb, wc, d, wglu, bglu, gain)


def _out_proj_kernel(og_ref, om_ref, os_ref, x_ref, g1_ref, sc_ref, sh_ref, gain_ref,
                     wg_ref, wm_ref, ws_ref, xo_ref, ht_ref, mix_scr, h_scr, *, groups):
    mix_scr[...] = (_dot(og_ref[...].astype(BF16), wg_ref[...])
                    + _dot(om_ref[...].astype(BF16), wm_ref[...])
                    + _dot(os_ref[...].astype(BF16), ws_ref[...]))
    gain = gain_ref[...]

    def body(g, carry):
        r0 = pl.multiple_of(g * TOKENS_PER_GROUP, TOKENS_PER_GROUP)
        rows = pl.ds(r0, TOKENS_PER_GROUP)
        xn = x_ref[rows, :] + g1_ref[pl.ds(g, 1), :] * mix_scr[rows, :]
        xo_ref[rows, :] = xn
        h_scr[rows, :] = _modulated_group(xn, gain, sc_ref[pl.ds(g, 1), :], sh_ref[pl.ds(g, 1), :])
        return carry

    lax.fori_loop(0, groups, body, 0)
    ht_ref[...] = h_scr[...].T.astype(BF16)


def _out_proj(og, om, os_, x, g1, sc, sh, gain, wg, wm, ws, *, tt):
    n_tok = x.shape[0]
    groups = tt // TOKENS_PER_GROUP
    tok = lambda width: pl.BlockSpec((tt, width), lambda i: (i, 0))
    grp = pl.BlockSpec((groups, D_MODEL), lambda i: (i, 0))
    const = lambda shape: _const_spec(shape, lambda i: (0, 0))
    return pl.pallas_call(
        functools.partial(_out_proj_kernel, groups=groups),
        grid=(n_tok // tt,),
        in_specs=[tok(GLA_WIDTH), tok(ML_WIDTH), tok(S5_WIDTH), tok(D_MODEL), grp, grp, grp,
                  const((1, D_MODEL)), const((GLA_WIDTH, D_MODEL)), const((ML_WIDTH, D_MODEL)),
                  const((S5_WIDTH, D_MODEL))],
        out_specs=[tok(D_MODEL), pl.BlockSpec((D_MODEL, tt), lambda i: (0, i))],
        out_shape=[jax.ShapeDtypeStruct((n_tok, D_MODEL), F32),
                   jax.ShapeDtypeStruct((D_MODEL, n_tok), BF16)],
        scratch_shapes=[pltpu.VMEM((tt, D_MODEL), F32), pltpu.VMEM((tt, D_MODEL), F32)],
        compiler_params=_cparams(("parallel",)),
        name="out_proj",
    )(og, om, os_, x, g1, sc, sh, gain, wg, wm, ws)


def _top_values(cur, count, out_scr=None):
    neg_inf = jnp.float32(-jnp.inf)
    vals = []
    for r in range(count):
        m = jnp.max(cur, axis=0, keepdims=True)
        vals.append(m)
        if out_scr is not None:
            out_scr[r:r + 1, :] = m
        if r + 1 < count:
            cur = jnp.where(cur == m, neg_inf, cur)
    return vals


def _peer_route_kernel(ht_ref, wqt_ref, keys_ref, s_ref, e_ref, tau_ref, qt_scr, v1_scr, v2_scr, cand_scr,
                       *, tt):
    qt_scr[...] = _dot(wqt_ref[...], ht_ref[...])
    for h in range(PEER_HEADS):
        sc = []
        for side in range(2):
            hs = 2 * h + side
            qb = qt_scr[hs * LANE:(hs + 1) * LANE, :].astype(BF16)
            sc_hs = _dot(keys_ref[hs], qb)
            s_ref[hs] = sc_hs
            sc.append(sc_hs)
        for lt in range(tt // LANE):
            ls = slice(lt * LANE, (lt + 1) * LANE)
            s1 = sc[0][:, ls]
            s2 = sc[1][:, ls]
            _top_values(s1, PEER_TOPK, v1_scr)
            _top_values(s2, PEER_TOPK, v2_scr)
            v2_all = v2_scr[...]
            v2_head = v2_scr[0:SUBLANE, :]
            cand_scr[0:PEER_TOPK, :] = v1_scr[0:1, :] + v2_all
            for a in range(1, PEER_TOPK):
                lo = PEER_TOPK + (a - 1) * SUBLANE
                cand_scr[lo:lo + SUBLANE, :] = v1_scr[a:a + 1, :] + v2_head
            top = _top_values(cand_scr[...], PEER_TOPK)
            z = jnp.zeros_like(top[0])
            for tv in top:
                z = z + jnp.exp(tv - top[0])
            tau_ref[h:h + 1, ls] = top[PEER_TOPK - 1]
            e_ref[2 * h, :, ls] = jnp.exp(s1 - v1_scr[0:1, :]) / z
            e_ref[2 * h + 1, :, ls] = jnp.exp(s2 - v2_scr[0:1, :])


def _peer_route(ht, wqt, keys, *, tt):
    n_tok = ht.shape[1]
    n_hs = 2 * PEER_HEADS
    n_cand = PEER_TOPK + (PEER_TOPK - 1) * SUBLANE
    return pl.pallas_call(
        functools.partial(_peer_route_kernel, tt=tt),
        grid=(n_tok // tt,),
        in_specs=[pl.BlockSpec((D_MODEL, tt), lambda i: (0, i)),
                  _const_spec((n_hs * LANE, D_MODEL), lambda i: (0, 0)),
                  _const_spec((n_hs, N_KEYS, LANE), lambda i: (0, 0, 0))],
        out_specs=[pl.BlockSpec((n_hs, N_KEYS, tt), lambda i: (0, 0, i)),
                   pl.BlockSpec((n_hs, N_KEYS, tt), lambda i: (0, 0, i)),
                   pl.BlockSpec((PEER_HEADS, tt), lambda i: (0, i))],
        out_shape=[jax.ShapeDtypeStruct((n_hs, N_KEYS, n_tok), F32),
                   jax.ShapeDtypeStruct((n_hs, N_KEYS, n_tok), F32),
                   jax.ShapeDtypeStruct((PEER_HEADS, n_tok), F32)],
        scratch_shapes=[pltpu.VMEM((n_hs * LANE, tt), F32), pltpu.VMEM((PEER_TOPK, LANE), F32),
                        pltpu.VMEM((PEER_TOPK, LANE), F32), pltpu.VMEM((n_cand, LANE), F32)],
        compiler_params=_cparams(("parallel",)),
        name="peer_route",
    )(ht, wqt, keys)


def _peer_dense_kernel(ht_ref, u_ref, vt_ref, s1_ref, e1_ref, s_ref, e_ref, tau_ref, x_ref, g2_ref, o_ref,
                       acc_scr, act_scr, p_scr, *, tt, ne, groups):
    e_idx = pl.program_id(1)

    @pl.when(e_idx == 0)
    def _():
        acc_scr[...] = jnp.zeros_like(acc_scr)

    act_scr[...] = jax.nn.gelu(_dot(u_ref[...], ht_ref[...]))
    n_i = ne // N_KEYS
    for ii in range(n_i):
        rs = slice(ii * N_KEYS, (ii + 1) * N_KEYS)
        for lt in range(tt // LANE):
            ls = slice(lt * LANE, (lt + 1) * LANE)
            gate = jnp.zeros((N_KEYS, LANE), F32)
            for h in range(PEER_HEADS):
                s1_row = s1_ref[2 * h, 0, ii:ii + 1, ls]
                e1_row = e1_ref[2 * h, 0, ii:ii + 1, ls]
                cand = s1_row + s_ref[2 * h + 1, :, ls]
                gate = gate + jnp.where(cand >= tau_ref[h:h + 1, ls], e1_row * e_ref[2 * h + 1, :, ls], 0.0)
            p_scr[rs, ls] = (gate * act_scr[rs, ls]).astype(BF16)
    acc_scr[...] += _dot(vt_ref[...], p_scr[...])

    @pl.when(e_idx == pl.num_programs(1) - 1)
    def _():
        o_ref[...] = acc_scr[...].T

        def body(g, carry):
            rows = pl.ds(pl.multiple_of(g * TOKENS_PER_GROUP, TOKENS_PER_GROUP), TOKENS_PER_GROUP)
            o_ref[rows, :] = x_ref[rows, :] + g2_ref[pl.ds(g, 1), :] * o_ref[rows, :]
            return carry

        lax.fori_loop(0, groups, body, 0)


def _peer_dense(ht, u_bf, vt_bf, s, e, tau, x, g2, *, tt, ne):
    n_tok = x.shape[0]
    groups = tt // TOKENS_PER_GROUP
    n_hs = 2 * PEER_HEADS
    n_i = ne // N_KEYS
    tok_const = lambda shape, imap: pl.BlockSpec(shape, imap, pipeline_mode=pl.Buffered(1))
    by_first_key = lambda t: t.reshape(n_hs, N_KEYS // n_i, n_i, n_tok)
    first_key_spec = pl.BlockSpec((n_hs, 1, n_i, tt), lambda i, j: (0, j, 0, i))
    return pl.pallas_call(
        functools.partial(_peer_dense_kernel, tt=tt, ne=ne, groups=groups),
        grid=(n_tok // tt, N_EXPERTS // ne),
        in_specs=[tok_const((D_MODEL, tt), lambda i, j: (0, i)),
                  pl.BlockSpec((ne, D_MODEL), lambda i, j: (j, 0)),
                  pl.BlockSpec((D_MODEL, ne), lambda i, j: (0, j)),
                  first_key_spec, first_key_spec,
                  tok_const((n_hs, N_KEYS, tt), lambda i, j: (0, 0, i)),
                  tok_const((n_hs, N_KEYS, tt), lambda i, j: (0, 0, i)),
                  tok_const((PEER_HEADS, tt), lambda i, j: (0, i)),
                  tok_const((tt, D_MODEL), lambda i, j: (i, 0)),
                  tok_const((groups, D_MODEL), lambda i, j: (i, 0))],
        out_specs=pl.BlockSpec((tt, D_MODEL), lambda i, j: (i, 0)),
        out_shape=jax.ShapeDtypeStruct((n_tok, D_MODEL), F32),
        scratch_shapes=[pltpu.VMEM((D_MODEL, tt), F32), pltpu.VMEM((ne, tt), F32),
                        pltpu.VMEM((ne, tt), BF16)],
        compiler_params=_cparams(("parallel", "arbitrary")),
        name="peer_dense",
    )(ht, u_bf, vt_bf, by_first_key(s), by_first_key(e), s, e, tau, x, g2)


def _final_norm_kernel(x_ref, gain_ref, o_ref):
    o_ref[...] = _rms(x_ref[...]) * gain_ref[...]


def _final_norm(x, gain, *, tt):
    n_tok = x.shape[0]
    return pl.pallas_call(
        _final_norm_kernel,
        grid=(n_tok // tt,),
        in_specs=[pl.BlockSpec((tt, D_MODEL), lambda i: (i, 0)),
                  pl.BlockSpec((1, D_MODEL), lambda i: (0, 0))],
        out_specs=pl.BlockSpec((tt, D_MODEL), lambda i: (i, 0)),
        out_shape=jax.ShapeDtypeStruct((n_tok, D_MODEL), F32),
        compiler_params=_cparams(("parallel",)),
        name="final_norm",
    )(x, gain)


PROMPT_ROWS = 256
STEP_ROWS = 128
S5_PROMPT_STEPS = 128


def _block_diag_in(bbar):
    rows_group = np.arange(S5_WIDTH) // S5_CH
    cols_group = np.arange(S5_STATE) // S5_P
    mask = jnp.asarray(rows_group[:, None] == cols_group[None, :])
    return jnp.where(mask, jnp.tile(bbar, (S5_GROUPS, 1)), 0.0)


def _block_diag_out(c):
    rows_group = np.arange(S5_STATE) // S5_P
    cols_group = np.arange(S5_WIDTH) // S5_CH
    mask = jnp.asarray(rows_group[:, None] == cols_group[None, :])
    per_state = c.transpose(0, 2, 1).reshape(S5_STATE, S5_CH)
    return jnp.where(mask, jnp.tile(per_state, (1, S5_GROUPS)), 0.0)


def kernel(x_prompt, x_sample, state_gla, state_mlstm_c, state_mlstm_n, state_mlstm_m, state_s5_re, state_s5_im, c_prompt, c_sample, w_ada, b_ada, norm_gain, w_in, gla_w_gate_up, gla_b_gate, gla_norm_gain, ml_b_igate, ml_b_fgate, ml_norm_gain, s5_lambda_re, s5_lambda_im, s5_log_dt, s5_b_re, s5_b_im, s5_c_re, s5_c_im, s5_d, s5_w_glu, s5_b_glu, s5_norm_gain, w_out, peer_w_q, peer_sub_keys, peer_u, peer_v, final_gain):
    bp, lp, _ = x_prompt.shape
    bs, ls, _ = x_sample.shape
    assert ls == TOKENS_PER_GROUP and lp % PROMPT_ROWS == 0 and (bs * ls) % STEP_ROWS == 0
    n_prompt = bp * lp
    n_sample = bs * ls
    n_tok = n_prompt + n_sample
    rep = lp // TOKENS_PER_GROUP

    x = jnp.concatenate([x_prompt.reshape(n_prompt, D_MODEL), x_sample.reshape(n_sample, D_MODEL)], axis=0)

    c_all = jnp.concatenate([c_prompt, c_sample], axis=0)
    n_c = c_all.shape[0]
    c_pad = (-n_c) % SUBLANE
    mod = _adaln(jnp.pad(c_all, ((0, c_pad), (0, 0))), w_ada, b_ada)[:, :n_c]

    def per_group(m):
        return jnp.concatenate([jnp.repeat(m[:bp], rep, axis=0), m[bp:]], axis=0)

    a_disc, bbar = _s5_discretise(s5_lambda_re, s5_lambda_im, s5_log_dt, s5_b_re, s5_b_im)

    z_src = jnp.asarray(np.maximum(_Z_SRC, 0))
    z_valid = jnp.asarray(_Z_SRC >= 0)
    zeros = lambda *shape: jnp.zeros(shape, F32)
    seq_blocks = lp // PROMPT_ROWS
    step_blocks = n_sample // STEP_ROWS
    step_block0 = n_prompt // STEP_ROWS
    prompt_pad = (-bp) % SUBLANE

    new_states = []
    for l in range(DEPTH):
        sh1, sc1, g1, sh2, sc2, g2 = [per_group(m) for m in jnp.split(mod[l], 6, axis=-1)]
        w_in_p = jnp.where(z_valid[None, :], jnp.take(w_in[l], z_src, axis=1), 0.0).astype(BF16)
        z = _norm_proj(x, sc1, sh1, norm_gain[l, 0][None, :], w_in_p, tt=512, tn=1152)

        wg = jnp.zeros((GLA_HEADS, LANE, LANE), F32).at[:, :GLA_RANK, :GLA_DK].set(
            gla_w_gate_up[l].reshape(GLA_RANK, GLA_HEADS, GLA_DK).transpose(1, 0, 2)).astype(BF16)
        bg = jnp.zeros((GLA_HEADS, 1, LANE), F32).at[:, 0, :GLA_DK].set(gla_b_gate[l].reshape(GLA_HEADS, GLA_DK))
        gla_gain = gla_norm_gain[l].reshape(GLA_HEADS, 1, GLA_DV)
        og_p, sg_p = _gla(z, wg, bg, gla_gain, zeros(bp, GLA_HEADS, GLA_DK, GLA_DV), n_seq=bp,
                          seq_blocks=seq_blocks, rows=PROMPT_ROWS, chunk=GLA_CHUNK, row_block0=0, carry=True)
        og_s, sg_s = _gla(z, wg, bg, gla_gain, state_gla[l].astype(F32), n_seq=step_blocks, seq_blocks=1,
                          rows=STEP_ROWS, chunk=ls, row_block0=step_block0, carry=False)

        bif = jnp.zeros((1, LANE), F32).at[0, :ML_HEADS].set(ml_b_igate[l]).at[
            0, F_GATE_LANE:F_GATE_LANE + ML_HEADS].set(ml_b_fgate[l])
        ml_gain = ml_norm_gain[l].reshape(ML_HEADS, 1, ML_DH)
        lanes = lambda m: jnp.broadcast_to(m[:, :, None, None], m.shape + (1, LANE))
        om_p, mc_p, mn_p, mm_p = _mlstm(z, bif, ml_gain, zeros(bp, ML_HEADS, ML_DH, ML_DH),
                                        zeros(bp, ML_HEADS, 1, ML_DH), zeros(bp, ML_HEADS, 1, LANE),
                                        n_seq=bp, seq_blocks=seq_blocks, rows=PROMPT_ROWS, chunk=ML_CHUNK,
                                        row_block0=0, carry=True)
        om_s, mc_s, mn_s, mm_s = _mlstm(z, bif, ml_gain, state_mlstm_c[l].astype(F32),
                                        state_mlstm_n[l].astype(F32)[:, :, None, :],
                                        lanes(state_mlstm_m[l].astype(F32)),
                                        n_seq=step_blocks, seq_blocks=1, rows=STEP_ROWS, chunk=ls,
                                        row_block0=step_block0, carry=False)

        su = z[:, ZB_SU * LANE:ZB_SU * LANE + S5_WIDTH]
        u_p = jnp.pad(su[:n_prompt].reshape(bp, lp, S5_WIDTH).transpose(1, 0, 2), ((0, 0), (0, prompt_pad), (0, 0)))
        u_s = su[n_prompt:].reshape(bs, ls, S5_WIDTH).transpose(1, 0, 2)
        wb = jnp.concatenate([_block_diag_in(bbar[l, 0]), _block_diag_in(bbar[l, 1])], axis=1).astype(BF16)
        wc = jnp.concatenate([_block_diag_out(s5_c_re[l]), -_block_diag_out(s5_c_im[l])], axis=0).astype(BF16)
        s5_args = (a_disc[l], wb, wc, s5_d[l].reshape(1, S5_WIDTH), s5_w_glu[l].astype(BF16),
                   s5_b_glu[l][None, :], s5_norm_gain[l][None, :])
        os_p, hs_p = _s5(u_p, zeros(bp + prompt_pad, 2 * S5_STATE), *s5_args,
                         steps=S5_PROMPT_STEPS, lane_width=S5_STATE)
        h0_s = jnp.concatenate([state_s5_re[l].reshape(bs, S5_STATE), state_s5_im[l].reshape(bs, S5_STATE)],
                               axis=1).astype(F32)
        os_s, hs_s = _s5(u_s, h0_s, *s5_args, steps=ls, lane_width=LANE)
        os_p = os_p[:, :bp].transpose(1, 0, 2).reshape(n_prompt, S5_WIDTH)
        os_s = os_s.transpose(1, 0, 2).reshape(n_sample, S5_WIDTH)

        w_out_bf = w_out[l].astype(BF16)
        x, ht = _out_proj(jnp.concatenate([og_p, og_s], axis=0), jnp.concatenate([om_p, om_s], axis=0),
                          jnp.concatenate([os_p, os_s], axis=0), x, g1, sc2, sh2, norm_gain[l, 1][None, :],
                          w_out_bf[:GLA_WIDTH], w_out_bf[GLA_WIDTH:GLA_WIDTH + ML_WIDTH],
                          w_out_bf[GLA_WIDTH + ML_WIDTH:], tt=256)

        keys = peer_sub_keys[l].reshape(2 * PEER_HEADS, N_KEYS, LANE).astype(BF16)
        s, e, tau = _peer_route(ht, peer_w_q[l].T.astype(BF16), keys, tt=256)
        x = _peer_dense(ht, peer_u[l].astype(BF16), peer_v[l].T.astype(BF16), s, e, tau, x, g2, tt=512, ne=512)

        new_states.append((
            (sg_p, sg_s),
            (mc_p, mc_s),
            (mn_p[:, :, 0, :], mn_s[:, :, 0, :]),
            (mm_p[:, :, 0, 0], mm_s[:, :, 0, 0]),
            (hs_p[:bp, :S5_STATE].reshape(bp, S5_GROUPS, S5_P), hs_s[:, :S5_STATE].reshape(bs, S5_GROUPS, S5_P)),
            (hs_p[:bp, S5_STATE:].reshape(bp, S5_GROUPS, S5_P), hs_s[:, S5_STATE:].reshape(bs, S5_GROUPS, S5_P)),
        ))

    y = _final_norm(x, final_gain[None, :], tt=512)
    y_prompt = y[:n_prompt].reshape(bp, lp, D_MODEL).astype(x_prompt.dtype)
    y_sample = y[n_prompt:].reshape(bs, ls, D_MODEL).astype(x_sample.dtype)
    outs = [y_prompt, y_sample]
    in_states = (state_gla, state_mlstm_c, state_mlstm_n, state_mlstm_m, state_s5_re, state_s5_im)
    for i in range(6):
        for grp in range(2):
            outs.append(jnp.stack([ns[i][grp] for ns in new_states]).astype(in_states[i].dtype))
    return tuple(outs)
```

```python
import functools
import math

import numpy as np
import jax
import jax.numpy as jnp
from jax import lax
from jax.experimental import pallas as pl
from jax.experimental.pallas import tpu as pltpu

F32 = jnp.float32
BF16 = jnp.bfloat16
HIGHEST = lax.Precision.HIGHEST

D_MODEL = 2048
DEPTH = 4
GLA_HEADS = 6
GLA_DK = 64
GLA_DV = 128
GLA_RANK = 16
GLA_TAU = 16.0
GLA_CHUNK = 16
ML_HEADS = 6
ML_DH = 128
ML_CHUNK = 64
S5_WIDTH = 512
S5_CH = 16
S5_GROUPS = 32
S5_P = 64
S5_STATE = S5_GROUPS * S5_P
PEER_HEADS = 8
N_KEYS = 128
N_EXPERTS = N_KEYS * N_KEYS
PEER_TOPK = 16
NORM_EPS = 1e-6
GLA_WIDTH = GLA_HEADS * GLA_DV
ML_WIDTH = ML_HEADS * ML_DH

LANE = 128
SUBLANE = 8
TOKENS_PER_GROUP = 8
VMEM_LIMIT = 56 * 1024 * 1024

ZB_GQ, ZB_GK, ZB_GV, ZB_GG, ZB_GR = 0, 6, 12, 18, 24
ZB_MQ, ZB_MK, ZB_MV, ZB_MO, ZB_MIF = 25, 31, 37, 43, 49
ZB_SU = 50
Z_BLOCKS = 54
Z_COLS = Z_BLOCKS * LANE
F_GATE_LANE = 8


def _z_source_columns():
    src = np.full((Z_COLS,), -1, np.int32)
    off_gq, off_gk, off_gv, off_gg, off_gr = 0, 384, 768, 1536, 2304
    off_mq, off_mk, off_mv, off_mo, off_mi, off_mf, off_su = 2320, 3088, 3856, 4624, 5392, 5398, 5404
    for h in range(GLA_HEADS):
        for d in range(GLA_DK):
            src[(ZB_GQ + h) * LANE + d] = off_gq + h * GLA_DK + d
            src[(ZB_GK + h) * LANE + d] = off_gk + h * GLA_DK + d
        for d in range(GLA_DV):
            src[(ZB_GV + h) * LANE + d] = off_gv + h * GLA_DV + d
            src[(ZB_GG + h) * LANE + d] = off_gg + h * GLA_DV + d
    for d in range(GLA_RANK):
        src[ZB_GR * LANE + d] = off_gr + d
    for h in range(ML_HEADS):
        for d in range(ML_DH):
            src[(ZB_MQ + h) * LANE + d] = off_mq + h * ML_DH + d
            src[(ZB_MK + h) * LANE + d] = off_mk + h * ML_DH + d
            src[(ZB_MV + h) * LANE + d] = off_mv + h * ML_DH + d
            src[(ZB_MO + h) * LANE + d] = off_mo + h * ML_DH + d
        src[ZB_MIF * LANE + h] = off_mi + h
        src[ZB_MIF * LANE + F_GATE_LANE + h] = off_mf + h
    for d in range(S5_WIDTH):
        src[ZB_SU * LANE + d] = off_su + d
    return src


_Z_SRC = _z_source_columns()


def _cparams(semantics):
    return pltpu.CompilerParams(dimension_semantics=semantics, vmem_limit_bytes=VMEM_LIMIT)


def _const_spec(block_shape, index_map):
    return pl.BlockSpec(block_shape, index_map, pipeline_mode=pl.Buffered(1))


def _rms(x):
    return x * lax.rsqrt(jnp.mean(x * x, axis=-1, keepdims=True) + NORM_EPS)


def _dot(a, b):
    return jnp.dot(a, b, preferred_element_type=F32)


def _dot_nt(a, b):
    return lax.dot_general(a, b, (((1,), (1,)), ((), ())), preferred_element_type=F32)


def _dot_exact(a, b):
    return jnp.dot(a, b, preferred_element_type=F32, precision=HIGHEST)


def _adaln_kernel(c_ref, w_ref, b_ref, o_ref):
    c = c_ref[...]
    s = (c * jax.nn.sigmoid(c)).astype(BF16)
    o_ref[0] = _dot(s, w_ref[0].astype(BF16)) + b_ref[0]


def _adaln(c_all, w_ada, b_ada):
    n_rows = c_all.shape[0]
    n_out = w_ada.shape[-1]
    tn = 1024
    return pl.pallas_call(
        _adaln_kernel,
        grid=(DEPTH, n_out // tn),
        in_specs=[
            pl.BlockSpec((n_rows, D_MODEL), lambda l, j: (0, 0)),
            pl.BlockSpec((1, D_MODEL, tn), lambda l, j: (l, 0, j)),
            pl.BlockSpec((1, 1, tn), lambda l, j: (l, 0, j)),
        ],
        out_specs=pl.BlockSpec((1, n_rows, tn), lambda l, j: (l, 0, j)),
        out_shape=jax.ShapeDtypeStruct((DEPTH, n_rows, n_out), F32),
        compiler_params=_cparams(("parallel", "parallel")),
        name="adaln",
    )(c_all, w_ada, b_ada.reshape(DEPTH, 1, n_out))


def _modulated_group(x8, gain, sc_row, sh_row):
    return _rms(x8) * gain * (1.0 + sc_row) + sh_row


def _norm_proj_kernel(x_ref, sc_ref, sh_ref, gain_ref, w_ref, o_ref, h_scr, *, groups):
    @pl.when(pl.program_id(1) == 0)
    def _():
        gain = gain_ref[...]

        def body(p, carry):
            hs = []
            for u in range(2):
                g = p * 2 + u
                r0 = pl.multiple_of(g * TOKENS_PER_GROUP, TOKENS_PER_GROUP)
                hs.append(_modulated_group(x_ref[pl.ds(r0, TOKENS_PER_GROUP), :], gain,
                                           sc_ref[pl.ds(g, 1), :], sh_ref[pl.ds(g, 1), :]))
            r = pl.multiple_of(p * 2 * TOKENS_PER_GROUP, 2 * TOKENS_PER_GROUP)
            h_scr[pl.ds(r, 2 * TOKENS_PER_GROUP), :] = jnp.concatenate(hs, axis=0).astype(BF16)
            return carry

        lax.fori_loop(0, groups // 2, body, 0)

    o_ref[...] = _dot(h_scr[...], w_ref[...])


MOD_SH1, MOD_SC1, MOD_G1, MOD_SH2, MOD_SC2, MOD_G2 = range(6)


def _mod_spec(groups, layer, kind):
    return pl.BlockSpec((None, groups, D_MODEL), lambda *a: (layer, a[0], kind))


def _gain_spec(layer, which):
    return pl.BlockSpec((None, 1, D_MODEL), lambda *a: (2 * layer + which, 0, 0))


def _norm_proj(x, mod, gains, w, *, layer, tt, tn):
    n_tok = x.shape[0]
    n_out = w.shape[-1]
    groups = tt // TOKENS_PER_GROUP
    return pl.pallas_call(
        functools.partial(_norm_proj_kernel, groups=groups),
        grid=(n_tok // tt, n_out // tn),
        in_specs=[
            pl.BlockSpec((tt, D_MODEL), lambda i, j: (i, 0)),
            _mod_spec(groups, layer, MOD_SC1),
            _mod_spec(groups, layer, MOD_SH1),
            _gain_spec(layer, 0),
            pl.BlockSpec((None, D_MODEL, tn), lambda i, j: (layer, 0, j)),
        ],
        out_specs=pl.BlockSpec((tt, tn), lambda i, j: (i, j)),
        out_shape=jax.ShapeDtypeStruct((n_tok, n_out), F32),
        scratch_shapes=[pltpu.VMEM((tt, D_MODEL), BF16)],
        compiler_params=_cparams(("parallel", "arbitrary")),
        name="norm_proj",
    )(x, mod, mod, gains, w)


def _chunk_masks(rows, chunk):
    shift = int(math.log2(chunk))
    ri = lax.broadcasted_iota(jnp.int32, (rows, rows), 0)
    ci = lax.broadcasted_iota(jnp.int32, (rows, rows), 1)
    same = (ri >> shift) == (ci >> shift)
    tri = jnp.logical_and(same, ci <= ri)
    return same, tri


def _gla_kernel(q_ref, k_ref, v_ref, g_ref, r_ref, wg_ref, bg_ref, gain_ref, s0_ref,
                o_ref, sfin_ref, st_scr, *, rows, chunk, carry):
    n_chunks = rows // chunk
    shift = int(math.log2(chunk))
    same, tri = _chunk_masks(rows, chunk)
    m_tri = jnp.where(tri, 1.0, 0.0).astype(F32)
    m_all = jnp.where(same, 1.0, 0.0).astype(F32)

    q = q_ref[...] * (GLA_DK ** -0.5)
    k = k_ref[...]
    v = v_ref[...]
    la = jax.nn.log_sigmoid(_dot(r_ref[...].astype(BF16), wg_ref[0]) + bg_ref[0]) / GLA_TAU
    bl = _dot_exact(m_tri, la)
    bt = _dot_exact(m_all, la)
    qd = (q * jnp.exp(bl)).astype(BF16)
    ki = (k * jnp.exp(-bl)).astype(BF16)
    kd = (k * jnp.exp(bt - bl)).astype(BF16)
    att = jnp.where(tri, _dot_nt(qd, ki), 0.0)
    o_intra = _dot(att.astype(BF16), v.astype(BF16))

    v_t = v.T
    lane_chunk = lax.broadcasted_iota(jnp.int32, (GLA_DV, rows), 1) >> shift
    zero_pad = jnp.zeros((LANE - GLA_DK, GLA_DV), F32)

    def load_state(c):
        return jnp.concatenate([s0_ref[c, 0], zero_pad], axis=0).T

    if carry:
        @pl.when(pl.program_id(2) == 0)
        def _():
            st_scr[...] = load_state(0)
        s_t = st_scr[...]

    sliced = chunk % (2 * SUBLANE) == 0
    row_chunk = lax.broadcasted_iota(jnp.int32, (rows, GLA_DV), 0) >> shift
    o_inter = [] if sliced else jnp.zeros((rows, GLA_DV), F32)
    for c in range(n_chunks):
        lo = c * chunk
        if not carry:
            s_t = load_state(c)
        if sliced:
            o_inter.append(_dot_nt(qd[lo:lo + chunk], s_t.astype(BF16)))
        else:
            o_inter = jnp.where(row_chunk == c, _dot_nt(qd, s_t.astype(BF16)), o_inter)
        decay = jnp.exp(bt[lo:lo + 1, :])
        v_c = jnp.where(lane_chunk == c, v_t, 0.0).astype(BF16)
        s_t = s_t * decay + _dot(v_c, kd)
        if not carry:
            sfin_ref[c, 0] = s_t.T[0:GLA_DK, :]

    if carry:
        st_scr[...] = s_t

        @pl.when(pl.program_id(2) == pl.num_programs(2) - 1)
        def _():
            sfin_ref[0, 0] = s_t.T[0:GLA_DK, :]

    o = o_intra + (jnp.concatenate(o_inter, axis=0) if sliced else o_inter)
    g = g_ref[...]
    o_ref[...] = _rms(o) * gain_ref[0] * (g * jax.nn.sigmoid(g))


def _without_refs(kernel_fn, first, count):
    def wrapped(*refs):
        return kernel_fn(*refs[:first], *refs[first + count:])
    return wrapped


def _mixer_grid(n_seq, n_heads, seq_blocks, row_block0, carry):
    if carry:
        return ((n_seq, n_heads, seq_blocks), lambda b, h, t: row_block0 + b * seq_blocks + t,
                ("parallel", "parallel", "arbitrary"))
    return (n_seq, n_heads), lambda b, h: row_block0 + b, ("parallel", "parallel")


def _state_spec(per_block, d2, d3, layer):
    if layer is None:
        return pl.BlockSpec((per_block, 1, d2, d3), lambda *a: (a[0], a[1], 0, 0))
    return pl.BlockSpec((None, per_block, 1, d2, d3), lambda *a: (layer, a[0], a[1], 0, 0))


def _gla(z, wg, bg, gain, s0, *, n_seq, seq_blocks, rows, chunk, row_block0, carry,
         layer=None, fill=()):
    per_block = 1 if carry else rows // chunk
    grid, rb, sem = _mixer_grid(n_seq, GLA_HEADS, seq_blocks, row_block0, carry)

    def zspec(block0):
        return pl.BlockSpec((rows, LANE), lambda *a: (rb(*a), block0 + a[1]))

    hspec = lambda shape: pl.BlockSpec(shape, lambda *a: (a[1],) + (0,) * (len(shape) - 1))
    state_spec = _state_spec(per_block, GLA_DK, GLA_DV, layer)
    n_in = 9
    return pl.pallas_call(
        _without_refs(functools.partial(_gla_kernel, rows=rows, chunk=chunk, carry=carry), n_in, len(fill)),
        grid=grid,
        in_specs=[zspec(ZB_GQ), zspec(ZB_GK), zspec(ZB_GV), zspec(ZB_GG),
                  pl.BlockSpec((rows, LANE), lambda *a: (rb(*a), ZB_GR)),
                  hspec((1, LANE, LANE)), hspec((1, 1, LANE)), hspec((1, 1, LANE)), state_spec]
                 + [pl.BlockSpec(memory_space=pl.ANY)] * len(fill),
        out_specs=[pl.BlockSpec((rows, LANE), lambda *a: (rb(*a), a[1])), state_spec],
        out_shape=[jax.ShapeDtypeStruct((z.shape[0], GLA_WIDTH), F32),
                   jax.ShapeDtypeStruct(s0.shape, F32)],
        input_output_aliases={n_in + i: i for i in range(len(fill))},
        scratch_shapes=[pltpu.VMEM((GLA_DV, LANE), F32)],
        compiler_params=_cparams(sem),
        name="gla_seq" if carry else "gla_step",
    )(z, z, z, z, z, wg, bg, gain, s0, *fill)


def _mlstm_kernel(q_ref, k_ref, v_ref, og_ref, gate_ref, bif_ref, gain_ref, c0_ref, n0_ref, m0_ref,
                  h_ref, cfin_ref, nfin_ref, mfin_ref, c_scr, n_scr, m_scr, *, rows, chunk, carry):
    n_chunks = rows // chunk
    shift = int(math.log2(chunk))
    head = pl.program_id(1)
    same, tri = _chunk_masks(rows, chunk)
    m_tri = jnp.where(tri, 1.0, 0.0).astype(F32)
    m_all = jnp.where(same, 1.0, 0.0).astype(F32)
    neg_inf = jnp.float32(-jnp.inf)

    x = gate_ref[...] + bif_ref[...]
    log_f = jax.nn.log_sigmoid(x)
    b_cum = _dot_exact(m_tri, log_f)
    b_tot = _dot_exact(m_all, log_f)
    lane = lax.broadcasted_iota(jnp.int32, (rows, LANE), 1)
    y = jnp.where(lane < F_GATE_LANE, x, b_cum)
    pick_i = lane == head
    pick_b = lane == head + F_GATE_LANE
    i_col = jnp.sum(jnp.where(pick_i, y, 0.0), axis=-1, keepdims=True)
    b_col = jnp.sum(jnp.where(pick_b, y, 0.0), axis=-1, keepdims=True)
    bt_col = jnp.sum(jnp.where(pick_b, b_tot, 0.0), axis=-1, keepdims=True)
    sub = lax.broadcasted_iota(jnp.int32, (LANE, rows), 0)
    y_t = y.T
    i_row = jnp.sum(jnp.where(sub == head, y_t, 0.0), axis=0, keepdims=True)
    b_row = jnp.sum(jnp.where(sub == head + F_GATE_LANE, y_t, 0.0), axis=0, keepdims=True)
    bt_row = jnp.sum(jnp.where(sub == head + F_GATE_LANE, b_tot.T, 0.0), axis=0, keepdims=True)

    a_col = bt_col - b_col + i_col
    a_row = bt_row - b_row + i_row
    mloc_col = jnp.max(jnp.where(same, a_row, neg_inf), axis=-1, keepdims=True)

    q = q_ref[...] * (ML_DH ** -0.5)
    k = k_ref[...]
    qb = q.astype(BF16)
    kb = k.astype(BF16)
    vb = v_ref[...].astype(BF16)
    kw = k * jnp.exp(a_col - mloc_col)
    kw_t = kw.T
    lane_chunk = lax.broadcasted_iota(jnp.int32, (ML_DH, rows), 1) >> shift

    def advance(c, c_st, n_st, m_st, m_last):
        hi = (c + 1) * chunk
        decay = jnp.exp(bt_col[hi - 1:hi] + m_st - m_last)
        scale = jnp.exp(mloc_col[hi - 1:hi] - m_last)
        kw_c = jnp.where(lane_chunk == c, kw_t, 0.0).astype(BF16)
        c_new = decay * c_st + scale * _dot(kw_c, vb)
        n_new = decay * n_st + scale * jnp.sum(kw[hi - chunk:hi], axis=0, keepdims=True)
        return c_new, n_new

    if carry:
        @pl.when(pl.program_id(2) == 0)
        def _():
            c_scr[...] = c0_ref[0, 0]
            n_scr[...] = n0_ref[0, 0]
            m_scr[...] = m0_ref[0, 0]
        c_st = c_scr[...]
        n_st = n_scr[...]
        m_st = m_scr[:, 0:1]
        ri = lax.broadcasted_iota(jnp.int32, (chunk, chunk), 0)
        ci = lax.broadcasted_iota(jnp.int32, (chunk, chunk), 1)
        tri_c = ci <= ri
        h_chunks = []
        for c in range(n_chunks):
            lo = c * chunk
            hi = lo + chunk
            bc = b_col[lo:hi]
            d_log = jnp.where(tri_c, bc - b_row[:, lo:hi] + i_row[:, lo:hi], neg_inf)
            g_inter = bc + m_st
            m_t = jnp.maximum(g_inter, jnp.max(d_log, axis=-1, keepdims=True))
            w_inter = jnp.exp(g_inter - m_t)
            s = _dot_nt(qb[lo:hi], kb[lo:hi]) * jnp.exp(d_log - m_t)
            num = w_inter * _dot(qb[lo:hi], c_st.astype(BF16)) + _dot(s.astype(BF16), vb[lo:hi])
            den = (w_inter * jnp.sum(q[lo:hi] * n_st, axis=-1, keepdims=True)
                   + jnp.sum(s, axis=-1, keepdims=True))
            h_chunks.append(num / jnp.maximum(jnp.abs(den), jnp.exp(-m_t)))
            m_last = m_t[chunk - 1:chunk]
            c_st, n_st = advance(c, c_st, n_st, m_st, m_last)
            m_st = m_last
        h = jnp.concatenate(h_chunks, axis=0)
        c_scr[...] = c_st
        n_scr[...] = n_st
        m_scr[...] = jnp.broadcast_to(m_st, (1, LANE))

        @pl.when(pl.program_id(2) == pl.num_programs(2) - 1)
        def _():
            cfin_ref[0, 0] = c_st
            nfin_ref[0, 0] = n_st
            mfin_ref[0, 0] = jnp.broadcast_to(m_st, (1, LANE))
    else:
        row_chunk1 = lax.broadcasted_iota(jnp.int32, (rows, 1), 0) >> shift
        row_chunk = lax.broadcasted_iota(jnp.int32, (rows, ML_DH), 0) >> shift
        m_rows = jnp.zeros((rows, 1), F32)
        n_rows = jnp.zeros((rows, ML_DH), F32)
        qc = jnp.zeros((rows, ML_DH), F32)
        for c in range(n_chunks):
            m_rows = jnp.where(row_chunk1 == c, m0_ref[c, 0][:, 0:1], m_rows)
            n_rows = jnp.where(row_chunk == c, n0_ref[c, 0], n_rows)
            qc = jnp.where(row_chunk == c, _dot(qb, c0_ref[c, 0].astype(BF16)), qc)
        d_log = jnp.where(tri, b_col - b_row + i_row, neg_inf)
        g_inter = b_col + m_rows
        m_t = jnp.maximum(g_inter, jnp.max(d_log, axis=-1, keepdims=True))
        w_inter = jnp.exp(g_inter - m_t)
        s = _dot_nt(qb, kb) * jnp.exp(d_log - m_t)
        num = w_inter * qc + _dot(s.astype(BF16), vb)
        den = w_inter * jnp.sum(q * n_rows, axis=-1, keepdims=True) + jnp.sum(s, axis=-1, keepdims=True)
        h = num / jnp.maximum(jnp.abs(den), jnp.exp(-m_t))
        for c in range(n_chunks):
            hi = (c + 1) * chunk
            m_last = m_t[hi - 1:hi]
            c_new, n_new = advance(c, c0_ref[c, 0], n0_ref[c, 0], m0_ref[c, 0][:, 0:1], m_last)
            cfin_ref[c, 0] = c_new
            nfin_ref[c, 0] = n_new
            mfin_ref[c, 0] = jnp.broadcast_to(m_last, (1, LANE))

    h_ref[...] = _rms(h) * gain_ref[0] * jax.nn.sigmoid(og_ref[...])


def _mlstm(z, bif, gain, c0, n0, m0, *, n_seq, seq_blocks, rows, chunk, row_block0, carry,
           layer=None, fill=()):
    per_block = 1 if carry else rows // chunk
    grid, rb, sem = _mixer_grid(n_seq, ML_HEADS, seq_blocks, row_block0, carry)

    def zspec(block0):
        return pl.BlockSpec((rows, LANE), lambda *a: (rb(*a), block0 + a[1]))

    state_specs = [_state_spec(per_block, ML_DH, ML_DH, layer), _state_spec(per_block, 1, ML_DH, layer),
                   _state_spec(per_block, 1, LANE, layer)]
    n_in = 10
    return pl.pallas_call(
        _without_refs(functools.partial(_mlstm_kernel, rows=rows, chunk=chunk, carry=carry), n_in, len(fill)),
        grid=grid,
        in_specs=[zspec(ZB_MQ), zspec(ZB_MK), zspec(ZB_MV), zspec(ZB_MO),
                  pl.BlockSpec((rows, LANE), lambda *a: (rb(*a), ZB_MIF)),
                  pl.BlockSpec((1, LANE), lambda *a: (0, 0)),
                  pl.BlockSpec((1, 1, LANE), lambda *a: (a[1], 0, 0))]
                 + state_specs + [pl.BlockSpec(memory_space=pl.ANY)] * len(fill),
        out_specs=[pl.BlockSpec((rows, LANE), lambda *a: (rb(*a), a[1]))] + state_specs,
        out_shape=[jax.ShapeDtypeStruct((z.shape[0], ML_WIDTH), F32),
                   jax.ShapeDtypeStruct(c0.shape, F32), jax.ShapeDtypeStruct(n0.shape, F32),
                   jax.ShapeDtypeStruct(m0.shape, F32)],
        input_output_aliases={n_in + i: i for i in range(len(fill))},
        scratch_shapes=[pltpu.VMEM((ML_DH, ML_DH), F32), pltpu.VMEM((1, ML_DH), F32),
                        pltpu.VMEM((1, LANE), F32)],
        compiler_params=_cparams(sem),
        name="mlstm_seq" if carry else "mlstm_step",
    )(z, z, z, z, z, bif, gain, c0, n0, m0, *fill)


def _s5_disc_kernel(lre_ref, lim_ref, ldt_ref, bre_ref, bim_ref, a_ref, bb_ref):
    lam_re = lre_ref[0]
    lam_im = lim_ref[0]
    dt = jnp.exp(ldt_ref[0])
    mag = jnp.exp(lam_re * dt)
    ar = mag * jnp.cos(lam_im * dt)
    ai = mag * jnp.sin(lam_im * dt)
    den = lam_re * lam_re + lam_im * lam_im
    fr = ((ar - 1.0) * lam_re + ai * lam_im) / den
    fi = (ai * lam_re - (ar - 1.0) * lam_im) / den
    b_re = bre_ref[0]
    b_im = bim_ref[0]
    a_ref[0, 0:1, :] = ar
    a_ref[0, 1:2, :] = ai
    bb_ref[0, 0] = fr * b_re - fi * b_im
    bb_ref[0, 1] = fr * b_im + fi * b_re


def _s5_discretise(lam_re, lam_im, log_dt, b_re, b_im):
    flat = lambda t: t.reshape(DEPTH, 1, S5_STATE)
    ldt = jnp.broadcast_to(log_dt[:, :, None], (DEPTH, S5_GROUPS, S5_P))
    chan = lambda t: t.reshape(DEPTH, S5_STATE, S5_CH).transpose(0, 2, 1)
    row = pl.BlockSpec((1, 1, S5_STATE), lambda l: (l, 0, 0))
    mat = pl.BlockSpec((1, S5_CH, S5_STATE), lambda l: (l, 0, 0))
    return pl.pallas_call(
        _s5_disc_kernel,
        grid=(DEPTH,),
        in_specs=[row, row, row, mat, mat],
        out_specs=[pl.BlockSpec((1, 2, S5_STATE), lambda l: (l, 0, 0)),
                   pl.BlockSpec((1, 2, S5_CH, S5_STATE), lambda l: (l, 0, 0, 0))],
        out_shape=[jax.ShapeDtypeStruct((DEPTH, 2, S5_STATE), F32),
                   jax.ShapeDtypeStruct((DEPTH, 2, S5_CH, S5_STATE), F32)],
        compiler_params=_cparams(("parallel",)),
        name="s5_disc",
    )(flat(lam_re), flat(lam_im), flat(ldt), chan(b_re), chan(b_im))


def _s5_kernel(u_ref, h0_ref, a_ref, wb_ref, wc_ref, d_ref, wglu_ref, bglu_ref, gain_ref,
               y_ref, hfin_ref, bu_scr, h_scr, *, steps, batch, lane_width):
    n_rows = steps * batch

    @pl.when(pl.program_id(0) == 0)
    def _():
        h_scr[...] = h0_ref[...]

    u = u_ref[...].reshape(n_rows, S5_WIDTH)
    bu_scr[...] = _dot(u.astype(BF16), wb_ref[...]).reshape(steps, batch, 2 * S5_STATE)

    for j in range(S5_STATE // lane_width):
        re = slice(j * lane_width, (j + 1) * lane_width)
        im = slice(S5_STATE + j * lane_width, S5_STATE + (j + 1) * lane_width)
        ar = a_ref[0:1, re]
        ai = a_ref[1:2, re]

        def step(t, carry):
            hr, hi = carry
            nr = ar * hr - ai * hi + bu_scr[t, :, re]
            ni = ar * hi + ai * hr + bu_scr[t, :, im]
            bu_scr[t, :, re] = nr
            bu_scr[t, :, im] = ni
            return nr, ni

        hr, hi = lax.fori_loop(0, steps, step, (h_scr[:, re], h_scr[:, im]), unroll=min(steps, 8))
        h_scr[:, re] = hr
        h_scr[:, im] = hi

    hs = bu_scr[...].reshape(n_rows, 2 * S5_STATE).astype(BF16)
    y = _dot(hs, wc_ref[...]) + d_ref[...] * u
    y = jax.nn.gelu(y)
    out = y * jax.nn.sigmoid(_dot(y.astype(BF16), wglu_ref[...]) + bglu_ref[...])
    y_ref[...] = (_rms(out) * gain_ref[...]).reshape(steps, batch, S5_WIDTH)

    @pl.when(pl.program_id(0) == pl.num_programs(0) - 1)
    def _():
        hfin_ref[...] = h_scr[...]


def _s5(u_t, h0, a, wb, wc, d, wglu, bglu, gain, *, steps, lane_width):
    seq, batch, _ = u_t.shape
    const = lambda shape: _const_spec(shape, lambda t: (0,) * len(shape))
    return pl.pallas_call(
        functools.partial(_s5_kernel, steps=steps, batch=batch, lane_width=lane_width),
        grid=(seq // steps,),
        in_specs=[pl.BlockSpec((steps, batch, S5_WIDTH), lambda t: (t, 0, 0)),
                  const((batch, 2 * S5_STATE)), const((2, S5_STATE)),
                  const((S5_WIDTH, 2 * S5_STATE)), const((2 * S5_STATE, S5_WIDTH)),
                  const((1, S5_WIDTH)), const((S5_WIDTH, S5_WIDTH)), const((1, S5_WIDTH)),
                  const((1, S5_WIDTH))],
        out_specs=[pl.BlockSpec((steps, batch, S5_WIDTH), lambda t: (t, 0, 0)),
                   pl.BlockSpec((batch, 2 * S5_STATE), lambda t: (0, 0))],
        out_shape=[jax.ShapeDtypeStruct((seq, batch, S5_WIDTH), F32),
                   jax.ShapeDtypeStruct((batch, 2 * S5_STATE), F32)],
        scratch_shapes=[pltpu.VMEM((steps, batch, 2 * S5_STATE), F32),
                        pltpu.VMEM((batch, 2 * S5_STATE), F32)],
        compiler_params=_cparams(("arbitrary",)),
        name="s5",
    )(u_t, h0, a, wb, wc, d, wglu, bglu, gain)


def _out_proj_kernel(og_ref, om_ref, os_ref, x_ref, g1_ref, sc_ref, sh_ref, gain_ref,
                     wg_ref, wm_ref, ws_ref, xo_ref, ht_ref, mix_scr, h_scr, *, groups):
    mix_scr[...] = (_dot(og_ref[...].astype(BF16), wg_ref[...])
                    + _dot(om_ref[...].astype(BF16), wm_ref[...])
                    + _dot(os_ref[...].astype(BF16), ws_ref[...]))
    gain = gain_ref[...]

    def body(g, carry):
        r0 = pl.multiple_of(g * TOKENS_PER_GROUP, TOKENS_PER_GROUP)
        rows = pl.ds(r0, TOKENS_PER_GROUP)
        xn = x_ref[rows, :] + g1_ref[pl.ds(g, 1), :] * mix_scr[rows, :]
        xo_ref[rows, :] = xn
        h_scr[rows, :] = _modulated_group(xn, gain, sc_ref[pl.ds(g, 1), :], sh_ref[pl.ds(g, 1), :])
        return carry

    lax.fori_loop(0, groups, body, 0)
    ht_ref[...] = h_scr[...].T.astype(BF16)


def _out_proj(og, om, os_, x, mod, gains, w_out, *, layer, tt):
    n_tok = x.shape[0]
    groups = tt // TOKENS_PER_GROUP
    tok = lambda width: pl.BlockSpec((tt, width), lambda i: (i, 0))
    w_rows = lambda height, block: _const_spec((None, height, D_MODEL), lambda i: (layer, block, 0))
    assert GLA_WIDTH == ML_WIDTH and (GLA_WIDTH + ML_WIDTH) % S5_WIDTH == 0
    return pl.pallas_call(
        functools.partial(_out_proj_kernel, groups=groups),
        grid=(n_tok // tt,),
        in_specs=[tok(GLA_WIDTH), tok(ML_WIDTH), tok(S5_WIDTH), tok(D_MODEL),
                  _mod_spec(groups, layer, MOD_G1), _mod_spec(groups, layer, MOD_SC2),
                  _mod_spec(groups, layer, MOD_SH2), _gain_spec(layer, 1),
                  w_rows(GLA_WIDTH, 0), w_rows(ML_WIDTH, 1),
                  w_rows(S5_WIDTH, (GLA_WIDTH + ML_WIDTH) // S5_WIDTH)],
        out_specs=[tok(D_MODEL), pl.BlockSpec((D_MODEL, tt), lambda i: (0, i))],
        out_shape=[jax.ShapeDtypeStruct((n_tok, D_MODEL), F32),
                   jax.ShapeDtypeStruct((D_MODEL, n_tok), BF16)],
        scratch_shapes=[pltpu.VMEM((tt, D_MODEL), F32), pltpu.VMEM((tt, D_MODEL), F32)],
        compiler_params=_cparams(("parallel",)),
        name="out_proj",
    )(og, om, os_, x, mod, mod, mod, gains, w_out, w_out, w_out)


def _top_values(cur, count, out_scr=None, with_rank=False):
    neg_inf = jnp.float32(-jnp.inf)
    vals = []
    rank = jnp.full(cur.shape, float(count), F32) if with_rank else None
    for r in range(count):
        m = jnp.max(cur, axis=0, keepdims=True)
        vals.append(m)
        if out_scr is not None:
            out_scr[r:r + 1, :] = m
        hit = cur == m
        if with_rank:
            rank = jnp.where(hit, float(r), rank)
        if r + 1 < count:
            cur = jnp.where(hit, neg_inf, cur)
    return (vals, rank) if with_rank else vals


def _peer_route_kernel(ht_ref, wqt_ref, keys_ref, first_ref, second_ref, qt_scr, v1_scr, v2_scr, cand_scr,
                       *, tt):
    qt_scr[...] = _dot(wqt_ref[...], ht_ref[...])
    for h in range(PEER_HEADS):
        sc = []
        for side in range(2):
            hs = 2 * h + side
            qb = qt_scr[hs * LANE:(hs + 1) * LANE, :].astype(BF16)
            sc.append(_dot(keys_ref[hs], qb))
        for lt in range(tt // LANE):
            ls = slice(lt * LANE, (lt + 1) * LANE)
            s1 = sc[0][:, ls]
            s2 = sc[1][:, ls]
            _top_values(s1, PEER_TOPK, v1_scr)
            _, rank2 = _top_values(s2, PEER_TOPK, v2_scr, with_rank=True)
            v2_all = v2_scr[...]
            v2_head = v2_scr[0:SUBLANE, :]
            cand_scr[0:PEER_TOPK, :] = v1_scr[0:1, :] + v2_all
            for a in range(1, PEER_TOPK):
                lo = PEER_TOPK + (a - 1) * SUBLANE
                cand_scr[lo:lo + SUBLANE, :] = v1_scr[a:a + 1, :] + v2_head
            top = _top_values(cand_scr[...], PEER_TOPK)
            z = jnp.zeros_like(top[0])
            for tv in top:
                z = z + jnp.exp(tv - top[0])
            tau = top[PEER_TOPK - 1]
            n1 = jnp.zeros_like(s1)
            for b in range(PEER_TOPK):
                n1 = n1 + jnp.where(s1 + v2_scr[b:b + 1, :] >= tau, 1.0, 0.0)
            first_ref[2 * h, :, ls] = n1
            first_ref[2 * h + 1, :, ls] = jnp.exp(s1 - v1_scr[0:1, :]) / z
            second_ref[2 * h, :, ls] = rank2.astype(BF16)
            second_ref[2 * h + 1, :, ls] = jnp.exp(s2 - v2_scr[0:1, :]).astype(BF16)


def _peer_route(ht, wqt, keys, *, tt):
    n_tok = ht.shape[1]
    n_hs = 2 * PEER_HEADS
    n_cand = PEER_TOPK + (PEER_TOPK - 1) * SUBLANE
    return pl.pallas_call(
        functools.partial(_peer_route_kernel, tt=tt),
        grid=(n_tok // tt,),
        in_specs=[pl.BlockSpec((D_MODEL, tt), lambda i: (0, i)),
                  _const_spec((n_hs * LANE, D_MODEL), lambda i: (0, 0)),
                  _const_spec((n_hs, N_KEYS, LANE), lambda i: (0, 0, 0))],
        out_specs=[pl.BlockSpec((n_hs, N_KEYS, tt), lambda i: (0, 0, i)),
                   pl.BlockSpec((n_hs, N_KEYS, tt), lambda i: (0, 0, i))],
        out_shape=[jax.ShapeDtypeStruct((n_hs, N_KEYS, n_tok), F32),
                   jax.ShapeDtypeStruct((n_hs, N_KEYS, n_tok), BF16)],
        scratch_shapes=[pltpu.VMEM((n_hs * LANE, tt), F32), pltpu.VMEM((PEER_TOPK, LANE), F32),
                        pltpu.VMEM((PEER_TOPK, LANE), F32), pltpu.VMEM((n_cand, LANE), F32)],
        compiler_params=_cparams(("parallel",)),
        name="peer_route",
    )(ht, wqt, keys)


DENSE_FIRST_KEYS = 4
DENSE_SUB = DENSE_FIRST_KEYS * N_KEYS
GATE_ROWS = 2 * SUBLANE


def _gate_times_act(first_ref, row0, second_ref, act_ref, p_ref, lane_tiles):
    n_pieces = N_KEYS // GATE_ROWS
    zero = jnp.zeros((GATE_ROWS, LANE), BF16)
    for lt in lane_tiles:
        ls = slice(lt * LANE, (lt + 1) * LANE)
        gates = [[zero for _ in range(n_pieces)] for _ in range(DENSE_FIRST_KEYS)]
        for h in range(PEER_HEADS):
            counts, weights = [], []
            for ii in range(DENSE_FIRST_KEYS):
                r = row0 + ii
                counts.append(jnp.broadcast_to(first_ref[2 * h, 0, r:r + 1, ls], (GATE_ROWS, LANE)).astype(BF16))
                weights.append(jnp.broadcast_to(first_ref[2 * h + 1, 0, r:r + 1, ls], (GATE_ROWS, LANE)).astype(BF16))
            for jp in range(n_pieces):
                js = slice(jp * GATE_ROWS, (jp + 1) * GATE_ROWS)
                rank2 = second_ref[2 * h, js, ls]
                e2 = second_ref[2 * h + 1, js, ls]
                for ii in range(DENSE_FIRST_KEYS):
                    gates[ii][jp] = gates[ii][jp] + jnp.where(rank2 < counts[ii], e2, zero) * weights[ii]
        for ii in range(DENSE_FIRST_KEYS):
            for jp in range(n_pieces):
                rs = slice(ii * N_KEYS + jp * GATE_ROWS, ii * N_KEYS + (jp + 1) * GATE_ROWS)
                p_ref[rs, ls] = gates[ii][jp] * act_ref[rs, ls].astype(BF16)


def _peer_dense_kernel(ht_ref, u_ref, vt_ref, first_a_ref, first_b_ref, second_ref,
                       x_ref, g2_ref, o_ref, acc_scr, act0, act1, p0, p1, *, tt, groups):
    k = pl.program_id(1)

    @pl.when(k == 0)
    def _():
        acc_scr[...] = jnp.zeros_like(acc_scr)
        act1[...] = jnp.zeros_like(act1)
        p0[...] = jnp.zeros_like(p0)

    tok_half = tt // 2
    tiles_half = tok_half // LANE
    for half in range(2):
        ts = slice(half * tok_half, (half + 1) * tok_half)
        lane_tiles = range(half * tiles_half, (half + 1) * tiles_half)
        act0[:, ts] = jax.nn.gelu(_dot(u_ref[0:DENSE_SUB, :], ht_ref[:, ts]))
        _gate_times_act(first_a_ref, DENSE_FIRST_KEYS, second_ref, act1, p1, lane_tiles)
        acc_scr[:, ts] += _dot(vt_ref[:, 0:DENSE_SUB], p0[:, ts])
    for half in range(2):
        ts = slice(half * tok_half, (half + 1) * tok_half)
        lane_tiles = range(half * tiles_half, (half + 1) * tiles_half)
        act1[:, ts] = jax.nn.gelu(_dot(u_ref[DENSE_SUB:, :], ht_ref[:, ts]))
        _gate_times_act(first_b_ref, 0, second_ref, act0, p0, lane_tiles)
        acc_scr[:, ts] += _dot(vt_ref[:, DENSE_SUB:], p1[:, ts])

    @pl.when(k == pl.num_programs(1) - 1)
    def _():
        o_ref[...] = acc_scr[...].T

        def body(g, carry):
            rows = pl.ds(pl.multiple_of(g * TOKENS_PER_GROUP, TOKENS_PER_GROUP), TOKENS_PER_GROUP)
            o_ref[rows, :] = x_ref[rows, :] + g2_ref[pl.ds(g, 1), :] * o_ref[rows, :]
            return carry

        lax.fori_loop(0, groups, body, 0)


def _peer_dense(ht, u_bf, vt_bf, first, second, x, mod, *, layer, tt):
    n_tok = x.shape[0]
    groups = tt // TOKENS_PER_GROUP
    n_hs = 2 * PEER_HEADS
    ne = 2 * DENSE_SUB
    n_i = ne // N_KEYS
    n_blocks = N_EXPERTS // ne
    tok_const = lambda shape, imap: pl.BlockSpec(shape, imap, pipeline_mode=pl.Buffered(1))
    by_first_key = lambda t: t.reshape(n_hs, n_blocks, n_i, n_tok)
    cur = lambda k: jnp.minimum(k, n_blocks - 1)
    prev = lambda k: jnp.maximum(k - 1, 0)
    key_spec = lambda blk: pl.BlockSpec((n_hs, 1, n_i, tt), lambda i, k: (0, blk(k), 0, i))
    return pl.pallas_call(
        functools.partial(_peer_dense_kernel, tt=tt, groups=groups),
        grid=(n_tok // tt, n_blocks + 1),
        in_specs=[tok_const((D_MODEL, tt), lambda i, k: (0, i)),
                  pl.BlockSpec((None, ne, D_MODEL), lambda i, k: (layer, cur(k), 0)),
                  pl.BlockSpec((None, D_MODEL, ne), lambda i, k: (layer, 0, prev(k))),
                  key_spec(prev), key_spec(cur),
                  tok_const((n_hs, N_KEYS, tt), lambda i, k: (0, 0, i)),
                  tok_const((tt, D_MODEL), lambda i, k: (i, 0)),
                  _mod_spec(groups, layer, MOD_G2)],
        out_specs=pl.BlockSpec((tt, D_MODEL), lambda i, k: (i, 0)),
        out_shape=jax.ShapeDtypeStruct((n_tok, D_MODEL), F32),
        scratch_shapes=[pltpu.VMEM((D_MODEL, tt), F32),
                        pltpu.VMEM((DENSE_SUB, tt), F32), pltpu.VMEM((DENSE_SUB, tt), F32),
                        pltpu.VMEM((DENSE_SUB, tt), BF16), pltpu.VMEM((DENSE_SUB, tt), BF16)],
        compiler_params=_cparams(("parallel", "arbitrary")),
        name="peer_dense",
    )(ht, u_bf, vt_bf, by_first_key(first), by_first_key(first), second, x, mod)


def _final_norm_kernel(x_ref, gain_ref, o_ref):
    o_ref[...] = _rms(x_ref[...]) * gain_ref[...]


def _final_norm(x, gain, *, tt, row0, n_rows):
    return pl.pallas_call(
        _final_norm_kernel,
        grid=(n_rows // tt,),
        in_specs=[pl.BlockSpec((tt, D_MODEL), lambda i: (row0 // tt + i, 0)),
                  pl.BlockSpec((1, D_MODEL), lambda i: (0, 0))],
        out_specs=pl.BlockSpec((tt, D_MODEL), lambda i: (i, 0)),
        out_shape=jax.ShapeDtypeStruct((n_rows, D_MODEL), F32),
        compiler_params=_cparams(("parallel",)),
        name="final_norm",
    )(x, gain)


PROMPT_ROWS = 256
STEP_ROWS = 128
S5_PROMPT_STEPS = 128


def _block_diag_in(bbar):
    rows_group = np.arange(S5_WIDTH) // S5_CH
    cols_group = np.arange(S5_STATE) // S5_P
    mask = jnp.asarray(rows_group[:, None] == cols_group[None, :])
    reps = (1,) * (bbar.ndim - 2) + (S5_GROUPS, 1)
    return jnp.where(mask, jnp.tile(bbar, reps), 0.0)


def _block_diag_out(c):
    rows_group = np.arange(S5_STATE) // S5_P
    cols_group = np.arange(S5_WIDTH) // S5_CH
    mask = jnp.asarray(rows_group[:, None] == cols_group[None, :])
    per_state = c.transpose(0, 1, 3, 2).reshape(c.shape[0], S5_STATE, S5_CH)
    return jnp.where(mask, jnp.tile(per_state, (1, 1, S5_GROUPS)), 0.0)


def kernel(x_prompt, x_sample, state_gla, state_mlstm_c, state_mlstm_n, state_mlstm_m, state_s5_re, state_s5_im, c_prompt, c_sample, w_ada, b_ada, norm_gain, w_in, gla_w_gate_up, gla_b_gate, gla_norm_gain, ml_b_igate, ml_b_fgate, ml_norm_gain, s5_lambda_re, s5_lambda_im, s5_log_dt, s5_b_re, s5_b_im, s5_c_re, s5_c_im, s5_d, s5_w_glu, s5_b_glu, s5_norm_gain, w_out, peer_w_q, peer_sub_keys, peer_u, peer_v, final_gain):
    bp, lp, _ = x_prompt.shape
    bs, ls, _ = x_sample.shape
    assert ls == TOKENS_PER_GROUP and lp % PROMPT_ROWS == 0 and (bs * ls) % STEP_ROWS == 0
    n_prompt = bp * lp
    n_sample = bs * ls
    n_tok = n_prompt + n_sample
    rep = lp // TOKENS_PER_GROUP

    x = jnp.concatenate([x_prompt.reshape(n_prompt, D_MODEL), x_sample.reshape(n_sample, D_MODEL)], axis=0)

    c_all = jnp.concatenate([c_prompt, c_sample], axis=0)
    n_c = c_all.shape[0]
    c_pad = (-n_c) % SUBLANE
    mod = _adaln(jnp.pad(c_all, ((0, c_pad), (0, 0))), w_ada, b_ada)
    mod = jnp.concatenate([jnp.repeat(mod[:, :bp], rep, axis=1), mod[:, bp:n_c]], axis=1)
    gains = norm_gain.reshape(DEPTH * 2, 1, D_MODEL)

    zeros = lambda *shape: jnp.zeros(shape, F32)
    seq_blocks = lp // PROMPT_ROWS
    step_blocks = n_sample // STEP_ROWS
    step_block0 = n_prompt // STEP_ROWS
    prompt_pad = (-bp) % SUBLANE

    z_src = jnp.asarray(np.maximum(_Z_SRC, 0))
    z_valid = jnp.asarray(_Z_SRC >= 0)
    w_in_p = jnp.where(z_valid, jnp.take(w_in.astype(BF16), z_src, axis=2), jnp.zeros((), BF16))
    wg_all = jnp.zeros((DEPTH, GLA_HEADS, LANE, LANE), F32).at[:, :, :GLA_RANK, :GLA_DK].set(
        gla_w_gate_up.reshape(DEPTH, GLA_RANK, GLA_HEADS, GLA_DK).transpose(0, 2, 1, 3)).astype(BF16)
    bg_all = jnp.zeros((DEPTH, GLA_HEADS, 1, LANE), F32).at[:, :, 0, :GLA_DK].set(
        gla_b_gate.reshape(DEPTH, GLA_HEADS, GLA_DK))
    gla_gain_all = gla_norm_gain.reshape(DEPTH, GLA_HEADS, 1, GLA_DV)
    bif_all = jnp.zeros((DEPTH, 1, LANE), F32).at[:, 0, :ML_HEADS].set(ml_b_igate).at[
        :, 0, F_GATE_LANE:F_GATE_LANE + ML_HEADS].set(ml_b_fgate)
    ml_gain_all = ml_norm_gain.reshape(DEPTH, ML_HEADS, 1, ML_DH)
    a_disc, bbar = _s5_discretise(s5_lambda_re, s5_lambda_im, s5_log_dt, s5_b_re, s5_b_im)
    wb_all = jnp.concatenate([_block_diag_in(bbar[:, 0]), _block_diag_in(bbar[:, 1])], axis=2).astype(BF16)
    wc_all = jnp.concatenate([_block_diag_out(s5_c_re), -_block_diag_out(s5_c_im)], axis=1).astype(BF16)
    wglu_all = s5_w_glu.astype(BF16)
    w_out_bf = w_out.astype(BF16)
    wqt_all = peer_w_q.astype(BF16).transpose(0, 2, 1)
    keys_all = peer_sub_keys.reshape(DEPTH, 2 * PEER_HEADS, N_KEYS, LANE).astype(BF16)
    u_bf = peer_u.astype(BF16)
    vt_bf = peer_v.astype(BF16).transpose(0, 2, 1)

    st_gla = state_gla.astype(F32)
    st_c = state_mlstm_c.astype(F32)
    st_n = state_mlstm_n.astype(F32)[:, :, :, None, :]
    st_m = jnp.broadcast_to(state_mlstm_m.astype(F32)[:, :, :, None, None], state_mlstm_m.shape + (1, LANE))
    st_s5 = jnp.concatenate([state_s5_re.reshape(DEPTH, bs, S5_STATE), state_s5_im.reshape(DEPTH, bs, S5_STATE)],
                            axis=2).astype(F32)

    sg_s = mc_s = mn_s = mm_s = None
    new_states = []
    for l in range(DEPTH):
        z = _norm_proj(x, mod, gains, w_in_p, layer=l, tt=512, tn=1152)

        seq_kw = dict(n_seq=bp, seq_blocks=seq_blocks, rows=PROMPT_ROWS, row_block0=0, carry=True)
        step_kw = dict(n_seq=step_blocks, seq_blocks=1, rows=STEP_ROWS, chunk=ls, row_block0=step_block0,
                       carry=False, layer=l)

        gla_w = (wg_all[l], bg_all[l], gla_gain_all[l])
        og, sg_p = _gla(z, *gla_w, zeros(bp, GLA_HEADS, GLA_DK, GLA_DV), chunk=GLA_CHUNK, **seq_kw)
        og, sg_s = _gla(z, *gla_w, st_gla, fill=(og,) if l == 0 else (og, sg_s), **step_kw)

        ml_w = (bif_all[l], ml_gain_all[l])
        om, mc_p, mn_p, mm_p = _mlstm(z, *ml_w, zeros(bp, ML_HEADS, ML_DH, ML_DH), zeros(bp, ML_HEADS, 1, ML_DH),
                                      zeros(bp, ML_HEADS, 1, LANE), chunk=ML_CHUNK, **seq_kw)
        om, mc_s, mn_s, mm_s = _mlstm(z, *ml_w, st_c, st_n, st_m,
                                      fill=(om,) if l == 0 else (om, mc_s, mn_s, mm_s), **step_kw)

        su = z[:, ZB_SU * LANE:ZB_SU * LANE + S5_WIDTH]
        u_p = jnp.pad(su[:n_prompt].reshape(bp, lp, S5_WIDTH).transpose(1, 0, 2), ((0, 0), (0, prompt_pad), (0, 0)))
        u_s = su[n_prompt:].reshape(bs, ls, S5_WIDTH).transpose(1, 0, 2)
        s5_args = (a_disc[l], wb_all[l], wc_all[l], s5_d[l].reshape(1, S5_WIDTH), wglu_all[l],
                   s5_b_glu[l][None, :], s5_norm_gain[l][None, :])
        os_p, hs_p = _s5(u_p, zeros(bp + prompt_pad, 2 * S5_STATE), *s5_args,
                         steps=S5_PROMPT_STEPS, lane_width=S5_STATE)
        os_s, hs_s = _s5(u_s, st_s5[l], *s5_args, steps=ls, lane_width=LANE)
        os_ = jnp.concatenate([os_p[:, :bp].transpose(1, 0, 2).reshape(n_prompt, S5_WIDTH),
                               os_s.transpose(1, 0, 2).reshape(n_sample, S5_WIDTH)], axis=0)

        x, ht = _out_proj(og, om, os_, x, mod, gains, w_out_bf, layer=l, tt=256)

        first, second = _peer_route(ht, wqt_all[l], keys_all[l], tt=256)
        x = _peer_dense(ht, u_bf, vt_bf, first, second, x, mod, layer=l, tt=512)

        new_states.append((sg_p, mc_p, mn_p[:, :, 0, :], mm_p[:, :, 0, 0], hs_p[:bp], hs_s))

    y_prompt = _final_norm(x, final_gain[None, :], tt=512, row0=0, n_rows=n_prompt)
    y_sample = _final_norm(x, final_gain[None, :], tt=512, row0=n_prompt, n_rows=n_sample)
    stack = lambda i: jnp.stack([ns[i] for ns in new_states])
    s5_p, s5_s = stack(4), stack(5)
    split_s5 = lambda h, lo: h[:, :, lo:lo + S5_STATE].reshape(DEPTH, -1, S5_GROUPS, S5_P)
    outs = (y_prompt.reshape(bp, lp, D_MODEL), y_sample.reshape(bs, ls, D_MODEL),
            stack(0), sg_s, stack(1), mc_s, stack(2), mn_s[:, :, :, 0, :], stack(3), mm_s[:, :, :, 0, 0],
            split_s5(s5_p, 0), split_s5(s5_s, 0), split_s5(s5_p, S5_STATE), split_s5(s5_s, S5_STATE))
    refs = (x_prompt, x_sample, state_gla, state_gla, state_mlstm_c, state_mlstm_c, state_mlstm_n, state_mlstm_n,
            state_mlstm_m, state_mlstm_m, state_s5_re, state_s5_re, state_s5_im, state_s5_im)
    return tuple(o.astype(r.dtype) for o, r in zip(outs, refs))
```

```python
import functools
import math

import numpy as np
import jax
import jax.numpy as jnp
from jax import lax
from jax.experimental import pallas as pl
from jax.experimental.pallas import tpu as pltpu

F32 = jnp.float32
BF16 = jnp.bfloat16
HIGHEST = lax.Precision.HIGHEST

D_MODEL = 2048
DEPTH = 4
GLA_HEADS = 6
GLA_DK = 64
GLA_DV = 128
GLA_RANK = 16
GLA_TAU = 16.0
GLA_CHUNK = 16
ML_HEADS = 6
ML_DH = 128
ML_CHUNK = 64
S5_WIDTH = 512
S5_CH = 16
S5_GROUPS = 32
S5_P = 64
S5_STATE = S5_GROUPS * S5_P
PEER_HEADS = 8
N_KEYS = 128
N_EXPERTS = N_KEYS * N_KEYS
PEER_TOPK = 16
NORM_EPS = 1e-6
GLA_WIDTH = GLA_HEADS * GLA_DV
ML_WIDTH = ML_HEADS * ML_DH

LANE = 128
SUBLANE = 8
TOKENS_PER_GROUP = 8
VMEM_LIMIT = 56 * 1024 * 1024

ZB_GQ, ZB_GK, ZB_GV, ZB_GG = 0, 6, 12, 18
ZB_MQ, ZB_MK, ZB_MV, ZB_MO = 24, 30, 36, 42
ZB_GR, ZB_MIF = 48, 49
HEADS_PER_STEP = 3
ZB_SU = 50
Z_BLOCKS = 54
Z_COLS = Z_BLOCKS * LANE
F_GATE_LANE = 8


def _z_source_columns():
    src = np.full((Z_COLS,), -1, np.int32)
    off_gq, off_gk, off_gv, off_gg, off_gr = 0, 384, 768, 1536, 2304
    off_mq, off_mk, off_mv, off_mo, off_mi, off_mf, off_su = 2320, 3088, 3856, 4624, 5392, 5398, 5404
    for h in range(GLA_HEADS):
        for d in range(GLA_DK):
            src[(ZB_GQ + h) * LANE + d] = off_gq + h * GLA_DK + d
            src[(ZB_GK + h) * LANE + d] = off_gk + h * GLA_DK + d
        for d in range(GLA_DV):
            src[(ZB_GV + h) * LANE + d] = off_gv + h * GLA_DV + d
            src[(ZB_GG + h) * LANE + d] = off_gg + h * GLA_DV + d
    for d in range(GLA_RANK):
        src[ZB_GR * LANE + d] = off_gr + d
    for h in range(ML_HEADS):
        for d in range(ML_DH):
            src[(ZB_MQ + h) * LANE + d] = off_mq + h * ML_DH + d
            src[(ZB_MK + h) * LANE + d] = off_mk + h * ML_DH + d
            src[(ZB_MV + h) * LANE + d] = off_mv + h * ML_DH + d
            src[(ZB_MO + h) * LANE + d] = off_mo + h * ML_DH + d
        src[ZB_MIF * LANE + h] = off_mi + h
        src[ZB_MIF * LANE + F_GATE_LANE + h] = off_mf + h
    for d in range(S5_WIDTH):
        src[ZB_SU * LANE + d] = off_su + d
    return src


_Z_SRC = _z_source_columns()


def _cparams(semantics):
    return pltpu.CompilerParams(dimension_semantics=semantics, vmem_limit_bytes=VMEM_LIMIT)


def _const_spec(block_shape, index_map):
    return pl.BlockSpec(block_shape, index_map, pipeline_mode=pl.Buffered(1))


def _rms(x):
    return x * lax.rsqrt(jnp.mean(x * x, axis=-1, keepdims=True) + NORM_EPS)


def _dot(a, b):
    return jnp.dot(a, b, preferred_element_type=F32)


def _dot_nt(a, b):
    return lax.dot_general(a, b, (((1,), (1,)), ((), ())), preferred_element_type=F32)


def _chunk_sums(same, tri, x):
    rows = x.shape[0]
    masks = jnp.concatenate([jnp.where(tri, 1.0, 0.0), jnp.where(same, 1.0, 0.0)], axis=0).astype(BF16)
    hi = x.astype(BF16)
    r1 = x - hi.astype(F32)
    mid = r1.astype(BF16)
    lo = (r1 - mid.astype(F32)).astype(BF16)
    sums = _dot(masks, jnp.concatenate([hi, mid, lo], axis=1))
    total = sums[:, 0:LANE] + sums[:, LANE:2 * LANE] + sums[:, 2 * LANE:3 * LANE]
    return total[0:rows], total[rows:2 * rows]


def _adaln_kernel(c_ref, w_ref, b_ref, o_ref):
    c = c_ref[...]
    s = (c * jax.nn.sigmoid(c)).astype(BF16)
    o_ref[0] = _dot(s, w_ref[0].astype(BF16)) + b_ref[0]


def _adaln(c_all, w_ada, b_ada):
    n_rows = c_all.shape[0]
    n_out = w_ada.shape[-1]
    tn = 1024
    return pl.pallas_call(
        _adaln_kernel,
        grid=(DEPTH, n_out // tn),
        in_specs=[
            pl.BlockSpec((n_rows, D_MODEL), lambda l, j: (0, 0)),
            pl.BlockSpec((1, D_MODEL, tn), lambda l, j: (l, 0, j)),
            pl.BlockSpec((1, 1, tn), lambda l, j: (l, 0, j)),
        ],
        out_specs=pl.BlockSpec((1, n_rows, tn), lambda l, j: (l, 0, j)),
        out_shape=jax.ShapeDtypeStruct((DEPTH, n_rows, n_out), F32),
        compiler_params=_cparams(("parallel", "parallel")),
        name="adaln",
    )(c_all, w_ada, b_ada.reshape(DEPTH, 1, n_out))


def _modulated_group(x8, gain, sc_row, sh_row):
    return _rms(x8) * gain * (1.0 + sc_row) + sh_row


def _norm_proj_kernel(x_ref, sc_ref, sh_ref, gain_ref, w_ref, o_ref, h_scr, *, groups):
    @pl.when(pl.program_id(1) == 0)
    def _():
        gain = gain_ref[...]

        def body(p, carry):
            hs = []
            for u in range(2):
                g = p * 2 + u
                r0 = pl.multiple_of(g * TOKENS_PER_GROUP, TOKENS_PER_GROUP)
                hs.append(_modulated_group(x_ref[pl.ds(r0, TOKENS_PER_GROUP), :], gain,
                                           sc_ref[pl.ds(g, 1), :], sh_ref[pl.ds(g, 1), :]))
            r = pl.multiple_of(p * 2 * TOKENS_PER_GROUP, 2 * TOKENS_PER_GROUP)
            h_scr[pl.ds(r, 2 * TOKENS_PER_GROUP), :] = jnp.concatenate(hs, axis=0).astype(BF16)
            return carry

        lax.fori_loop(0, groups // 2, body, 0)

    o_ref[...] = _dot(h_scr[...], w_ref[...])


MOD_SH1, MOD_SC1, MOD_G1, MOD_SH2, MOD_SC2, MOD_G2 = range(6)


def _mod_spec(groups, layer, kind):
    return pl.BlockSpec((None, groups, D_MODEL), lambda *a: (layer, a[0], kind))


def _gain_spec(layer, which):
    return pl.BlockSpec((None, 1, D_MODEL), lambda *a: (2 * layer + which, 0, 0))


def _norm_proj(x, mod, gains, w, *, layer, tt, tn):
    n_tok = x.shape[0]
    n_out = w.shape[-1]
    groups = tt // TOKENS_PER_GROUP
    return pl.pallas_call(
        functools.partial(_norm_proj_kernel, groups=groups),
        grid=(n_tok // tt, n_out // tn),
        in_specs=[
            pl.BlockSpec((tt, D_MODEL), lambda i, j: (i, 0)),
            _mod_spec(groups, layer, MOD_SC1),
            _mod_spec(groups, layer, MOD_SH1),
            _gain_spec(layer, 0),
            pl.BlockSpec((None, D_MODEL, tn), lambda i, j: (layer, 0, j)),
        ],
        out_specs=pl.BlockSpec((tt, tn), lambda i, j: (i, j)),
        out_shape=jax.ShapeDtypeStruct((n_tok, n_out), F32),
        scratch_shapes=[pltpu.VMEM((tt, D_MODEL), BF16)],
        compiler_params=_cparams(("parallel", "arbitrary")),
        name="norm_proj",
    )(x, mod, mod, gains, w)


def _chunk_masks(rows, chunk):
    shift = int(math.log2(chunk))
    ri = lax.broadcasted_iota(jnp.int32, (rows, rows), 0)
    ci = lax.broadcasted_iota(jnp.int32, (rows, rows), 1)
    same = (ri >> shift) == (ci >> shift)
    tri = jnp.logical_and(same, ci <= ri)
    return same, tri


def _gla_kernel(q_ref, k_ref, v_ref, g_ref, r_ref, wg_ref, bg_ref, gain_ref, s0_ref,
                o_ref, sfin_ref, st_scr, **kw):
    carry = kw["carry"]
    heads = range(HEADS_PER_STEP)
    if carry:
        @pl.when(pl.program_id(2) == 0)
        def _():
            for hh in heads:
                st_scr[hh] = _gla_load_state(s0_ref.at[:, hh:hh + 1], 0)
    for hh in heads:
        ln = slice(hh * LANE, (hh + 1) * LANE)
        one = slice(hh, hh + 1)
        _gla_head(q_ref.at[:, ln], k_ref.at[:, ln], v_ref.at[:, ln], g_ref.at[:, ln], r_ref,
                  wg_ref.at[one], bg_ref.at[one], gain_ref.at[one], s0_ref.at[:, one],
                  o_ref.at[:, ln], sfin_ref.at[:, one], st_scr.at[hh], **kw)
    if carry:
        @pl.when(pl.program_id(2) == pl.num_programs(2) - 1)
        def _():
            for hh in heads:
                sfin_ref[0, hh] = st_scr[hh].T[0:GLA_DK, :]


def _gla_load_state(s0_ref, c):
    zero_pad = jnp.zeros((LANE - GLA_DK, GLA_DV), F32)
    return jnp.concatenate([s0_ref[c, 0], zero_pad], axis=0).T


def _gla_head(q_ref, k_ref, v_ref, g_ref, r_ref, wg_ref, bg_ref, gain_ref, s0_ref,
              o_ref, sfin_ref, st_scr, *, rows, chunk, carry):
    n_chunks = rows // chunk
    shift = int(math.log2(chunk))
    same, tri = _chunk_masks(rows, chunk)

    q = q_ref[...] * (GLA_DK ** -0.5)
    k = k_ref[...]
    v = v_ref[...]
    la = jax.nn.log_sigmoid(_dot(r_ref[...].astype(BF16), wg_ref[0]) + bg_ref[0]) / GLA_TAU
    bl, bt = _chunk_sums(same, tri, la)
    qd = (q * jnp.exp(bl)).astype(BF16)
    ki = (k * jnp.exp(-bl)).astype(BF16)
    kd = (k * jnp.exp(bt - bl)).astype(BF16)
    att = jnp.where(tri, _dot_nt(qd, ki), 0.0)
    o_intra = _dot(att.astype(BF16), v.astype(BF16))

    v_t = v.T
    lane_chunk = lax.broadcasted_iota(jnp.int32, (GLA_DV, rows), 1) >> shift
    if carry:
        s_t = st_scr[...]

    sliced = chunk % (2 * SUBLANE) == 0
    row_chunk = lax.broadcasted_iota(jnp.int32, (rows, GLA_DV), 0) >> shift
    o_inter = [] if sliced else jnp.zeros((rows, GLA_DV), F32)
    for c in range(n_chunks):
        lo = c * chunk
        if not carry:
            s_t = _gla_load_state(s0_ref, c)
        if sliced:
            o_inter.append(_dot_nt(qd[lo:lo + chunk], s_t.astype(BF16)))
        else:
            o_inter = jnp.where(row_chunk == c, _dot_nt(qd, s_t.astype(BF16)), o_inter)
        decay = jnp.exp(bt[lo:lo + 1, :])
        v_c = jnp.where(lane_chunk == c, v_t, 0.0).astype(BF16)
        s_t = s_t * decay + _dot(v_c, kd)
        if not carry:
            sfin_ref[c, 0] = s_t.T[0:GLA_DK, :]

    if carry:
        st_scr[...] = s_t

    o = o_intra + (jnp.concatenate(o_inter, axis=0) if sliced else o_inter)
    g = g_ref[...]
    o_ref[...] = _rms(o) * gain_ref[0] * (g * jax.nn.sigmoid(g))


def _without_refs(kernel_fn, first, count):
    def wrapped(*refs):
        return kernel_fn(*refs[:first], *refs[first + count:])
    return wrapped


def _mixer_grid(n_seq, n_heads, seq_blocks, row_block0, carry):
    if carry:
        return ((n_seq, n_heads, seq_blocks), lambda b, h, t: row_block0 + b * seq_blocks + t,
                ("parallel", "parallel", "arbitrary"))
    return (n_seq, n_heads), lambda b, h: row_block0 + b, ("parallel", "parallel")


def _state_spec(per_block, d2, d3, layer):
    if layer is None:
        return pl.BlockSpec((per_block, HEADS_PER_STEP, d2, d3), lambda *a: (a[0], a[1], 0, 0))
    return pl.BlockSpec((None, per_block, HEADS_PER_STEP, d2, d3), lambda *a: (layer, a[0], a[1], 0, 0))


def _head_cols_spec(rows, rb, block0):
    assert block0 % HEADS_PER_STEP == 0
    return pl.BlockSpec((rows, HEADS_PER_STEP * LANE), lambda *a: (rb(*a), block0 // HEADS_PER_STEP + a[1]))


def _head_param_spec(shape):
    return pl.BlockSpec((HEADS_PER_STEP,) + shape, lambda *a: (a[1],) + (0,) * len(shape))


def _gla(z, wg, bg, gain, s0, *, n_seq, seq_blocks, rows, chunk, row_block0, carry,
         layer=None, fill=()):
    per_block = 1 if carry else rows // chunk
    grid, rb, sem = _mixer_grid(n_seq, GLA_HEADS // HEADS_PER_STEP, seq_blocks, row_block0, carry)
    zspec = functools.partial(_head_cols_spec, rows, rb)
    state_spec = _state_spec(per_block, GLA_DK, GLA_DV, layer)
    n_in = 9
    return pl.pallas_call(
        _without_refs(functools.partial(_gla_kernel, rows=rows, chunk=chunk, carry=carry), n_in, len(fill)),
        grid=grid,
        in_specs=[zspec(ZB_GQ), zspec(ZB_GK), zspec(ZB_GV), zspec(ZB_GG),
                  pl.BlockSpec((rows, LANE), lambda *a: (rb(*a), ZB_GR)),
                  _head_param_spec((LANE, LANE)), _head_param_spec((1, LANE)), _head_param_spec((1, LANE)),
                  state_spec]
                 + [pl.BlockSpec(memory_space=pl.ANY)] * len(fill),
        out_specs=[_head_cols_spec(rows, rb, 0), state_spec],
        out_shape=[jax.ShapeDtypeStruct((z.shape[0], GLA_WIDTH), F32),
                   jax.ShapeDtypeStruct(s0.shape, F32)],
        input_output_aliases={n_in + i: i for i in range(len(fill))},
        scratch_shapes=[pltpu.VMEM((HEADS_PER_STEP, GLA_DV, LANE), F32)],
        compiler_params=_cparams(sem),
        name="gla_seq" if carry else "gla_step",
    )(z, z, z, z, z, wg, bg, gain, s0, *fill)


def _mlstm_kernel(q_ref, k_ref, v_ref, og_ref, gate_ref, bif_ref, gain_ref, c0_ref, n0_ref, m0_ref,
                  h_ref, cfin_ref, nfin_ref, mfin_ref, c_scr, n_scr, m_scr, **kw):
    carry = kw["carry"]
    heads = range(HEADS_PER_STEP)
    if carry:
        @pl.when(pl.program_id(2) == 0)
        def _():
            for hh in heads:
                c_scr[hh] = c0_ref[0, hh]
                n_scr[hh] = n0_ref[0, hh]
                m_scr[hh] = m0_ref[0, hh]
    _mlstm_heads(q_ref, k_ref, v_ref, og_ref, gate_ref, bif_ref, gain_ref, c0_ref, n0_ref, m0_ref,
                 h_ref, cfin_ref, nfin_ref, mfin_ref, c_scr, n_scr, m_scr, **kw)
    if carry:
        @pl.when(pl.program_id(2) == pl.num_programs(2) - 1)
        def _():
            for hh in heads:
                cfin_ref[0, hh] = c_scr[hh]
                nfin_ref[0, hh] = n_scr[hh]
                mfin_ref[0, hh] = m_scr[hh]


def _mlstm_heads(q_ref, k_ref, v_ref, og_ref, gate_ref, bif_ref, gain_ref, c0_ref, n0_ref, m0_ref,
                 h_ref, cfin_ref, nfin_ref, mfin_ref, c_scr, n_scr, m_scr, **kw):
    for hh in range(HEADS_PER_STEP):
        ln = slice(hh * LANE, (hh + 1) * LANE)
        one = slice(hh, hh + 1)
        _mlstm_head(pl.program_id(1) * HEADS_PER_STEP + hh,
                    q_ref.at[:, ln], k_ref.at[:, ln], v_ref.at[:, ln], og_ref.at[:, ln], gate_ref, bif_ref,
                    gain_ref.at[one], c0_ref.at[:, one], n0_ref.at[:, one], m0_ref.at[:, one],
                    h_ref.at[:, ln], cfin_ref.at[:, one], nfin_ref.at[:, one], mfin_ref.at[:, one],
                    c_scr.at[hh], n_scr.at[hh], m_scr.at[hh], **kw)


def _mlstm_head(head, q_ref, k_ref, v_ref, og_ref, gate_ref, bif_ref, gain_ref, c0_ref, n0_ref, m0_ref,
                h_ref, cfin_ref, nfin_ref, mfin_ref, c_scr, n_scr, m_scr, *, rows, chunk, carry):
    n_chunks = rows // chunk
    shift = int(math.log2(chunk))
    same, tri = _chunk_masks(rows, chunk)
    neg_inf = jnp.float32(-jnp.inf)

    x = gate_ref[...] + bif_ref[...]
    log_f = jax.nn.log_sigmoid(x)
    b_cum, b_tot = _chunk_sums(same, tri, log_f)
    lane = lax.broadcasted_iota(jnp.int32, (rows, LANE), 1)
    y = jnp.where(lane < F_GATE_LANE, x, b_cum)
    pick_i = lane == head
    pick_b = lane == head + F_GATE_LANE
    i_col = jnp.sum(jnp.where(pick_i, y, 0.0), axis=-1, keepdims=True)
    b_col = jnp.sum(jnp.where(pick_b, y, 0.0), axis=-1, keepdims=True)
    bt_col = jnp.sum(jnp.where(pick_b, b_tot, 0.0), axis=-1, keepdims=True)
    sub = lax.broadcasted_iota(jnp.int32, (LANE, rows), 0)
    y_t = y.T
    i_row = jnp.sum(jnp.where(sub == head, y_t, 0.0), axis=0, keepdims=True)
    b_row = jnp.sum(jnp.where(sub == head + F_GATE_LANE, y_t, 0.0), axis=0, keepdims=True)

    a_col = bt_col - b_col + i_col
    mloc_col = bt_col + jnp.max(jnp.where(same, i_row - b_row, neg_inf), axis=-1, keepdims=True)

    q = q_ref[...] * (ML_DH ** -0.5)
    k = k_ref[...]
    qb = q.astype(BF16)
    kb = k.astype(BF16)
    vb = v_ref[...].astype(BF16)
    kw = k * jnp.exp(a_col - mloc_col)
    kw_t = kw.T
    lane_chunk = lax.broadcasted_iota(jnp.int32, (ML_DH, rows), 1) >> shift

    def advance(c, c_st, n_st, m_st, m_last):
        hi = (c + 1) * chunk
        decay = jnp.exp(bt_col[hi - 1:hi] + m_st - m_last)
        scale = jnp.exp(mloc_col[hi - 1:hi] - m_last)
        kw_c = jnp.where(lane_chunk == c, kw_t, 0.0).astype(BF16)
        c_new = decay * c_st + scale * _dot(kw_c, vb)
        n_new = decay * n_st + scale * jnp.sum(kw[hi - chunk:hi], axis=0, keepdims=True)
        return c_new, n_new

    if carry:
        c_st = c_scr[...]
        n_st = n_scr[...]
        m_st = m_scr[:, 0:1]
        ri = lax.broadcasted_iota(jnp.int32, (chunk, chunk), 0)
        ci = lax.broadcasted_iota(jnp.int32, (chunk, chunk), 1)
        tri_c = ci <= ri
        h_chunks = []
        for c in range(n_chunks):
            lo = c * chunk
            hi = lo + chunk
            bc = b_col[lo:hi]
            d_log = jnp.where(tri_c, bc - b_row[:, lo:hi] + i_row[:, lo:hi], neg_inf)
            g_inter = bc + m_st
            m_t = jnp.maximum(g_inter, jnp.max(d_log, axis=-1, keepdims=True))
            w_inter = jnp.exp(g_inter - m_t)
            s = _dot_nt(qb[lo:hi], kb[lo:hi]) * jnp.exp(d_log - m_t)
            num = w_inter * _dot(qb[lo:hi], c_st.astype(BF16)) + _dot(s.astype(BF16), vb[lo:hi])
            den = (w_inter * jnp.sum(q[lo:hi] * n_st, axis=-1, keepdims=True)
                   + jnp.sum(s, axis=-1, keepdims=True))
            h_chunks.append(num / jnp.maximum(jnp.abs(den), jnp.exp(-m_t)))
            m_last = m_t[chunk - 1:chunk]
            c_st, n_st = advance(c, c_st, n_st, m_st, m_last)
            m_st = m_last
        h = jnp.concatenate(h_chunks, axis=0)
        c_scr[...] = c_st
        n_scr[...] = n_st
        m_scr[...] = jnp.broadcast_to(m_st, (1, LANE))
    else:
        row_chunk1 = lax.broadcasted_iota(jnp.int32, (rows, 1), 0) >> shift
        row_chunk = lax.broadcasted_iota(jnp.int32, (rows, ML_DH), 0) >> shift
        m_rows = jnp.zeros((rows, 1), F32)
        n_rows = jnp.zeros((rows, ML_DH), F32)
        qc = jnp.zeros((rows, ML_DH), F32)
        for c in range(n_chunks):
            m_rows = jnp.where(row_chunk1 == c, m0_ref[c, 0][:, 0:1], m_rows)
            n_rows = jnp.where(row_chunk == c, n0_ref[c, 0], n_rows)
            qc = jnp.where(row_chunk == c, _dot(qb, c0_ref[c, 0].astype(BF16)), qc)
        d_log = jnp.where(tri, b_col - b_row + i_row, neg_inf)
        g_inter = b_col + m_rows
        m_t = jnp.maximum(g_inter, jnp.max(d_log, axis=-1, keepdims=True))
        w_inter = jnp.exp(g_inter - m_t)
        s = _dot_nt(qb, kb) * jnp.exp(d_log - m_t)
        num = w_inter * qc + _dot(s.astype(BF16), vb)
        den = w_inter * jnp.sum(q * n_rows, axis=-1, keepdims=True) + jnp.sum(s, axis=-1, keepdims=True)
        h = num / jnp.maximum(jnp.abs(den), jnp.exp(-m_t))
        for c in range(n_chunks):
            hi = (c + 1) * chunk
            m_last = m_t[hi - 1:hi]
            c_new, n_new = advance(c, c0_ref[c, 0], n0_ref[c, 0], m0_ref[c, 0][:, 0:1], m_last)
            cfin_ref[c, 0] = c_new
            nfin_ref[c, 0] = n_new
            mfin_ref[c, 0] = jnp.broadcast_to(m_last, (1, LANE))

    h_ref[...] = _rms(h) * gain_ref[0] * jax.nn.sigmoid(og_ref[...])


def _mlstm(z, bif, gain, c0, n0, m0, *, n_seq, seq_blocks, rows, chunk, row_block0, carry,
           layer=None, fill=()):
    per_block = 1 if carry else rows // chunk
    grid, rb, sem = _mixer_grid(n_seq, ML_HEADS // HEADS_PER_STEP, seq_blocks, row_block0, carry)
    zspec = functools.partial(_head_cols_spec, rows, rb)
    state_specs = [_state_spec(per_block, ML_DH, ML_DH, layer), _state_spec(per_block, 1, ML_DH, layer),
                   _state_spec(per_block, 1, LANE, layer)]
    n_in = 10
    return pl.pallas_call(
        _without_refs(functools.partial(_mlstm_kernel, rows=rows, chunk=chunk, carry=carry), n_in, len(fill)),
        grid=grid,
        in_specs=[zspec(ZB_MQ), zspec(ZB_MK), zspec(ZB_MV), zspec(ZB_MO),
                  pl.BlockSpec((rows, LANE), lambda *a: (rb(*a), ZB_MIF)),
                  pl.BlockSpec((1, LANE), lambda *a: (0, 0)),
                  _head_param_spec((1, LANE))]
                 + state_specs + [pl.BlockSpec(memory_space=pl.ANY)] * len(fill),
        out_specs=[_head_cols_spec(rows, rb, 0)] + state_specs,
        out_shape=[jax.ShapeDtypeStruct((z.shape[0], ML_WIDTH), F32),
                   jax.ShapeDtypeStruct(c0.shape, F32), jax.ShapeDtypeStruct(n0.shape, F32),
                   jax.ShapeDtypeStruct(m0.shape, F32)],
        input_output_aliases={n_in + i: i for i in range(len(fill))},
        scratch_shapes=[pltpu.VMEM((HEADS_PER_STEP, ML_DH, ML_DH), F32), pltpu.VMEM((HEADS_PER_STEP, 1, ML_DH), F32),
                        pltpu.VMEM((HEADS_PER_STEP, 1, LANE), F32)],
        compiler_params=_cparams(sem),
        name="mlstm_seq" if carry else "mlstm_step",
    )(z, z, z, z, z, bif, gain, c0, n0, m0, *fill)


def _s5_disc_kernel(lre_ref, lim_ref, ldt_ref, bre_ref, bim_ref, a_ref, bb_ref):
    lam_re = lre_ref[0]
    lam_im = lim_ref[0]
    dt = jnp.exp(ldt_ref[0])
    mag = jnp.exp(lam_re * dt)
    ar = mag * jnp.cos(lam_im * dt)
    ai = mag * jnp.sin(lam_im * dt)
    den = lam_re * lam_re + lam_im * lam_im
    fr = ((ar - 1.0) * lam_re + ai * lam_im) / den
    fi = (ai * lam_re - (ar - 1.0) * lam_im) / den
    b_re = bre_ref[0]
    b_im = bim_ref[0]
    a_ref[0, 0:1, :] = ar
    a_ref[0, 1:2, :] = ai
    bb_ref[0, 0] = fr * b_re - fi * b_im
    bb_ref[0, 1] = fr * b_im + fi * b_re


def _s5_discretise(lam_re, lam_im, log_dt, b_re, b_im):
    flat = lambda t: t.reshape(DEPTH, 1, S5_STATE)
    ldt = jnp.broadcast_to(log_dt[:, :, None], (DEPTH, S5_GROUPS, S5_P))
    chan = lambda t: t.reshape(DEPTH, S5_STATE, S5_CH).transpose(0, 2, 1)
    row = pl.BlockSpec((1, 1, S5_STATE), lambda l: (l, 0, 0))
    mat = pl.BlockSpec((1, S5_CH, S5_STATE), lambda l: (l, 0, 0))
    return pl.pallas_call(
        _s5_disc_kernel,
        grid=(DEPTH,),
        in_specs=[row, row, row, mat, mat],
        out_specs=[pl.BlockSpec((1, 2, S5_STATE), lambda l: (l, 0, 0)),
                   pl.BlockSpec((1, 2, S5_CH, S5_STATE), lambda l: (l, 0, 0, 0))],
        out_shape=[jax.ShapeDtypeStruct((DEPTH, 2, S5_STATE), F32),
                   jax.ShapeDtypeStruct((DEPTH, 2, S5_CH, S5_STATE), F32)],
        compiler_params=_cparams(("parallel",)),
        name="s5_disc",
    )(flat(lam_re), flat(lam_im), flat(ldt), chan(b_re), chan(b_im))


def _s5_kernel(u_ref, h0_ref, a_ref, wb_ref, wc_ref, d_ref, wglu_ref, bglu_ref, gain_ref,
               y_ref, hfin_ref, bu_scr, h_scr, *, steps, batch, lane_width):
    n_rows = steps * batch

    @pl.when(pl.program_id(0) == 0)
    def _():
        h_scr[...] = h0_ref[...]

    u = u_ref[...].reshape(n_rows, S5_WIDTH)
    bu_scr[...] = _dot(u.astype(BF16), wb_ref[...]).reshape(steps, batch, 2 * S5_STATE)

    for j in range(S5_STATE // lane_width):
        re = slice(j * lane_width, (j + 1) * lane_width)
        im = slice(S5_STATE + j * lane_width, S5_STATE + (j + 1) * lane_width)
        ar = a_ref[0:1, re]
        ai = a_ref[1:2, re]

        def step(t, carry):
            hr, hi = carry
            nr = ar * hr - ai * hi + bu_scr[t, :, re]
            ni = ar * hi + ai * hr + bu_scr[t, :, im]
            bu_scr[t, :, re] = nr
            bu_scr[t, :, im] = ni
            return nr, ni

        hr, hi = lax.fori_loop(0, steps, step, (h_scr[:, re], h_scr[:, im]), unroll=min(steps, 8))
        h_scr[:, re] = hr
        h_scr[:, im] = hi

    hs = bu_scr[...].reshape(n_rows, 2 * S5_STATE).astype(BF16)
    y = _dot(hs, wc_ref[...]) + d_ref[...] * u
    y = jax.nn.gelu(y)
    out = y * jax.nn.sigmoid(_dot(y.astype(BF16), wglu_ref[...]) + bglu_ref[...])
    y_ref[...] = (_rms(out) * gain_ref[...]).reshape(steps, batch, S5_WIDTH)

    @pl.when(pl.program_id(0) == pl.num_programs(0) - 1)
    def _():
        hfin_ref[...] = h_scr[...]


def _s5(u_t, h0, a, wb, wc, d, wglu, bglu, gain, *, steps, lane_width):
    seq, batch, _ = u_t.shape
    const = lambda shape: _const_spec(shape, lambda t: (0,) * len(shape))
    return pl.pallas_call(
        functools.partial(_s5_kernel, steps=steps, batch=batch, lane_width=lane_width),
        grid=(seq // steps,),
        in_specs=[pl.BlockSpec((steps, batch, S5_WIDTH), lambda t: (t, 0, 0)),
                  const((batch, 2 * S5_STATE)), const((2, S5_STATE)),
                  const((S5_WIDTH, 2 * S5_STATE)), const((2 * S5_STATE, S5_WIDTH)),
                  const((1, S5_WIDTH)), const((S5_WIDTH, S5_WIDTH)), const((1, S5_WIDTH)),
                  const((1, S5_WIDTH))],
        out_specs=[pl.BlockSpec((steps, batch, S5_WIDTH), lambda t: (t, 0, 0)),
                   pl.BlockSpec((batch, 2 * S5_STATE), lambda t: (0, 0))],
        out_shape=[jax.ShapeDtypeStruct((seq, batch, S5_WIDTH), F32),
                   jax.ShapeDtypeStruct((batch, 2 * S5_STATE), F32)],
        scratch_shapes=[pltpu.VMEM((steps, batch, 2 * S5_STATE), F32),
                        pltpu.VMEM((batch, 2 * S5_STATE), F32)],
        compiler_params=_cparams(("arbitrary",)),
        name="s5",
    )(u_t, h0, a, wb, wc, d, wglu, bglu, gain)


def _out_proj_kernel(og_ref, om_ref, os_ref, x_ref, g1_ref, sc_ref, sh_ref, gain_ref,
                     wg_ref, wm_ref, ws_ref, xo_ref, ht_ref, mix_scr, h_scr, *, groups):
    mix_scr[...] = (_dot(og_ref[...].astype(BF16), wg_ref[...])
                    + _dot(om_ref[...].astype(BF16), wm_ref[...])
                    + _dot(os_ref[...].astype(BF16), ws_ref[...]))
    gain = gain_ref[...]

    def body(g, carry):
        r0 = pl.multiple_of(g * TOKENS_PER_GROUP, TOKENS_PER_GROUP)
        rows = pl.ds(r0, TOKENS_PER_GROUP)
        xn = x_ref[rows, :] + g1_ref[pl.ds(g, 1), :] * mix_scr[rows, :]
        xo_ref[rows, :] = xn
        h_scr[rows, :] = _modulated_group(xn, gain, sc_ref[pl.ds(g, 1), :], sh_ref[pl.ds(g, 1), :])
        return carry

    lax.fori_loop(0, groups, body, 0)
    ht_ref[...] = h_scr[...].T.astype(BF16)


def _out_proj(og, om, os_, x, mod, gains, w_out, *, layer, tt):
    n_tok = x.shape[0]
    groups = tt // TOKENS_PER_GROUP
    tok = lambda width: pl.BlockSpec((tt, width), lambda i: (i, 0))
    w_rows = lambda height, block: _const_spec((None, height, D_MODEL), lambda i: (layer, block, 0))
    assert GLA_WIDTH == ML_WIDTH and (GLA_WIDTH + ML_WIDTH) % S5_WIDTH == 0
    return pl.pallas_call(
        functools.partial(_out_proj_kernel, groups=groups),
        grid=(n_tok // tt,),
        in_specs=[tok(GLA_WIDTH), tok(ML_WIDTH), tok(S5_WIDTH), tok(D_MODEL),
                  _mod_spec(groups, layer, MOD_G1), _mod_spec(groups, layer, MOD_SC2),
                  _mod_spec(groups, layer, MOD_SH2), _gain_spec(layer, 1),
                  w_rows(GLA_WIDTH, 0), w_rows(ML_WIDTH, 1),
                  w_rows(S5_WIDTH, (GLA_WIDTH + ML_WIDTH) // S5_WIDTH)],
        out_specs=[tok(D_MODEL), pl.BlockSpec((D_MODEL, tt), lambda i: (0, i))],
        out_shape=[jax.ShapeDtypeStruct((n_tok, D_MODEL), F32),
                   jax.ShapeDtypeStruct((D_MODEL, n_tok), BF16)],
        scratch_shapes=[pltpu.VMEM((tt, D_MODEL), F32), pltpu.VMEM((tt, D_MODEL), F32)],
        compiler_params=_cparams(("parallel",)),
        name="out_proj",
    )(og, om, os_, x, mod, mod, mod, gains, w_out, w_out, w_out)


def _top_values(cur, count, out_scr=None, with_rank=False):
    neg_inf = jnp.float32(-jnp.inf)
    vals = []
    rank = jnp.full(cur.shape, float(count), F32) if with_rank else None
    for r in range(count):
        m = jnp.max(cur, axis=0, keepdims=True)
        vals.append(m)
        if out_scr is not None:
            out_scr[r:r + 1, :] = m
        hit = cur == m
        if with_rank:
            rank = jnp.where(hit, float(r), rank)
        if r + 1 < count:
            cur = jnp.where(hit, neg_inf, cur)
    return (vals, rank) if with_rank else vals


def _peer_route_kernel(ht_ref, wqt_ref, keys_ref, first_ref, second_ref, qt_scr, v1_scr, v2_scr, cand_scr,
                       *, tt):
    qt_scr[...] = _dot(wqt_ref[...], ht_ref[...])
    for h in range(PEER_HEADS):
        sc = []
        for side in range(2):
            hs = 2 * h + side
            qb = qt_scr[hs * LANE:(hs + 1) * LANE, :].astype(BF16)
            sc.append(_dot(keys_ref[hs], qb))
        for lt in range(tt // LANE):
            ls = slice(lt * LANE, (lt + 1) * LANE)
            s1 = sc[0][:, ls]
            s2 = sc[1][:, ls]
            _top_values(s1, PEER_TOPK, v1_scr)
            _, rank2 = _top_values(s2, PEER_TOPK, v2_scr, with_rank=True)
            v2_all = v2_scr[...]
            v2_head = v2_scr[0:SUBLANE, :]
            cand_scr[0:PEER_TOPK, :] = v1_scr[0:1, :] + v2_all
            for a in range(1, PEER_TOPK):
                lo = PEER_TOPK + (a - 1) * SUBLANE
                cand_scr[lo:lo + SUBLANE, :] = v1_scr[a:a + 1, :] + v2_head
            top = _top_values(cand_scr[...], PEER_TOPK)
            z = jnp.zeros_like(top[0])
            for tv in top:
                z = z + jnp.exp(tv - top[0])
            tau = top[PEER_TOPK - 1]
            n1 = jnp.zeros_like(s1)
            for b in range(PEER_TOPK):
                n1 = n1 + jnp.where(s1 + v2_scr[b:b + 1, :] >= tau, 1.0, 0.0)
            first_ref[2 * h, :, ls] = n1
            first_ref[2 * h + 1, :, ls] = jnp.exp(s1 - v1_scr[0:1, :]) / z
            second_ref[2 * h, :, ls] = rank2.astype(BF16)
            second_ref[2 * h + 1, :, ls] = jnp.exp(s2 - v2_scr[0:1, :]).astype(BF16)


def _peer_route(ht, wqt, keys, *, tt):
    n_tok = ht.shape[1]
    n_hs = 2 * PEER_HEADS
    n_cand = PEER_TOPK + (PEER_TOPK - 1) * SUBLANE
    return pl.pallas_call(
        functools.partial(_peer_route_kernel, tt=tt),
        grid=(n_tok // tt,),
        in_specs=[pl.BlockSpec((D_MODEL, tt), lambda i: (0, i)),
                  _const_spec((n_hs * LANE, D_MODEL), lambda i: (0, 0)),
                  _const_spec((n_hs, N_KEYS, LANE), lambda i: (0, 0, 0))],
        out_specs=[pl.BlockSpec((n_hs, N_KEYS, tt), lambda i: (0, 0, i)),
                   pl.BlockSpec((n_hs, N_KEYS, tt), lambda i: (0, 0, i))],
        out_shape=[jax.ShapeDtypeStruct((n_hs, N_KEYS, n_tok), F32),
                   jax.ShapeDtypeStruct((n_hs, N_KEYS, n_tok), BF16)],
        scratch_shapes=[pltpu.VMEM((n_hs * LANE, tt), F32), pltpu.VMEM((PEER_TOPK, LANE), F32),
                        pltpu.VMEM((PEER_TOPK, LANE), F32), pltpu.VMEM((n_cand, LANE), F32)],
        compiler_params=_cparams(("parallel",)),
        name="peer_route",
    )(ht, wqt, keys)


DENSE_FIRST_KEYS = 4
DENSE_SUB = DENSE_FIRST_KEYS * N_KEYS
GATE_ROWS = 4 * SUBLANE


def _gate_times_act(first_ref, row0, second_ref, act_ref, p_ref, lane_tiles):
    n_pieces = N_KEYS // GATE_ROWS
    zero = jnp.zeros((GATE_ROWS, LANE), BF16)
    for lt in lane_tiles:
        ls = slice(lt * LANE, (lt + 1) * LANE)
        gates = [[zero for _ in range(n_pieces)] for _ in range(DENSE_FIRST_KEYS)]
        for h in range(PEER_HEADS):
            counts, weights = [], []
            for ii in range(DENSE_FIRST_KEYS):
                r = row0 + ii
                counts.append(jnp.broadcast_to(first_ref[2 * h, 0, r:r + 1, ls], (GATE_ROWS, LANE)).astype(BF16))
                weights.append(jnp.broadcast_to(first_ref[2 * h + 1, 0, r:r + 1, ls], (GATE_ROWS, LANE)).astype(BF16))
            for jp in range(n_pieces):
                js = slice(jp * GATE_ROWS, (jp + 1) * GATE_ROWS)
                rank2 = second_ref[2 * h, js, ls]
                e2 = second_ref[2 * h + 1, js, ls]
                for ii in range(DENSE_FIRST_KEYS):
                    gates[ii][jp] = gates[ii][jp] + jnp.where(rank2 < counts[ii], e2, zero) * weights[ii]
        for ii in range(DENSE_FIRST_KEYS):
            for jp in range(n_pieces):
                rs = slice(ii * N_KEYS + jp * GATE_ROWS, ii * N_KEYS + (jp + 1) * GATE_ROWS)
                p_ref[rs, ls] = gates[ii][jp] * act_ref[rs, ls].astype(BF16)


def _peer_dense_kernel(ht_ref, u_ref, vt_ref, first_a_ref, first_b_ref, second_in_ref,
                       x_ref, g2_ref, o_ref, acc_scr, act0, act1, p0, p1, second_ref, *, tt, groups):
    k = pl.program_id(1)

    @pl.when(k == 0)
    def _():
        acc_scr[...] = jnp.zeros_like(acc_scr)
        act1[...] = jnp.zeros_like(act1)
        p0[...] = jnp.zeros_like(p0)
        second_ref[...] = second_in_ref[...]

    tok_half = tt // 2
    tiles_half = tok_half // LANE
    for half in range(2):
        ts = slice(half * tok_half, (half + 1) * tok_half)
        lane_tiles = range(half * tiles_half, (half + 1) * tiles_half)
        act0[:, ts] = jax.nn.gelu(_dot(u_ref[0:DENSE_SUB, :], ht_ref[:, ts]))
        _gate_times_act(first_a_ref, DENSE_FIRST_KEYS, second_ref, act1, p1, lane_tiles)
        acc_scr[:, ts] += _dot(vt_ref[:, 0:DENSE_SUB], p0[:, ts])
    for half in range(2):
        ts = slice(half * tok_half, (half + 1) * tok_half)
        lane_tiles = range(half * tiles_half, (half + 1) * tiles_half)
        act1[:, ts] = jax.nn.gelu(_dot(u_ref[DENSE_SUB:, :], ht_ref[:, ts]))
        _gate_times_act(first_b_ref, 0, second_ref, act0, p0, lane_tiles)
        acc_scr[:, ts] += _dot(vt_ref[:, DENSE_SUB:], p1[:, ts])

    @pl.when(k == pl.num_programs(1) - 1)
    def _():
        o_ref[...] = acc_scr[...].T

        def body(g, carry):
            rows = pl.ds(pl.multiple_of(g * TOKENS_PER_GROUP, TOKENS_PER_GROUP), TOKENS_PER_GROUP)
            o_ref[rows, :] = x_ref[rows, :] + g2_ref[pl.ds(g, 1), :] * o_ref[rows, :]
            return carry

        lax.fori_loop(0, groups, body, 0)


def _peer_dense(ht, u_bf, vt_bf, first, second, x, mod, *, layer, tt):
    n_tok = x.shape[0]
    groups = tt // TOKENS_PER_GROUP
    n_hs = 2 * PEER_HEADS
    ne = 2 * DENSE_SUB
    n_i = ne // N_KEYS
    n_blocks = N_EXPERTS // ne
    tok_const = lambda shape, imap: pl.BlockSpec(shape, imap, pipeline_mode=pl.Buffered(1))
    by_first_key = lambda t: t.reshape(n_hs, n_blocks, n_i, n_tok)
    cur = lambda k: jnp.minimum(k, n_blocks - 1)
    prev = lambda k: jnp.maximum(k - 1, 0)
    key_spec = lambda blk: pl.BlockSpec((n_hs, 1, n_i, tt), lambda i, k: (0, blk(k), 0, i))
    return pl.pallas_call(
        functools.partial(_peer_dense_kernel, tt=tt, groups=groups),
        grid=(n_tok // tt, n_blocks + 1),
        in_specs=[tok_const((D_MODEL, tt), lambda i, k: (0, i)),
                  pl.BlockSpec((None, ne, D_MODEL), lambda i, k: (layer, cur(k), 0)),
                  pl.BlockSpec((None, D_MODEL, ne), lambda i, k: (layer, 0, prev(k))),
                  key_spec(prev), key_spec(cur),
                  tok_const((n_hs, N_KEYS, tt), lambda i, k: (0, 0, i)),
                  tok_const((tt, D_MODEL), lambda i, k: (i, 0)),
                  _mod_spec(groups, layer, MOD_G2)],
        out_specs=pl.BlockSpec((tt, D_MODEL), lambda i, k: (i, 0)),
        out_shape=jax.ShapeDtypeStruct((n_tok, D_MODEL), F32),
        scratch_shapes=[pltpu.VMEM((D_MODEL, tt), F32),
                        pltpu.VMEM((DENSE_SUB, tt), F32), pltpu.VMEM((DENSE_SUB, tt), F32),
                        pltpu.VMEM((DENSE_SUB, tt), BF16), pltpu.VMEM((DENSE_SUB, tt), BF16),
                        pltpu.VMEM((n_hs, N_KEYS, tt), BF16)],
        compiler_params=_cparams(("parallel", "arbitrary")),
        name="peer_dense",
    )(ht, u_bf, vt_bf, by_first_key(first), by_first_key(first), second, x, mod)


def _final_norm_kernel(x_ref, gain_ref, o_ref):
    o_ref[...] = _rms(x_ref[...]) * gain_ref[...]


def _final_norm(x, gain, *, tt, row0, n_rows):
    return pl.pallas_call(
        _final_norm_kernel,
        grid=(n_rows // tt,),
        in_specs=[pl.BlockSpec((tt, D_MODEL), lambda i: (row0 // tt + i, 0)),
                  pl.BlockSpec((1, D_MODEL), lambda i: (0, 0))],
        out_specs=pl.BlockSpec((tt, D_MODEL), lambda i: (i, 0)),
        out_shape=jax.ShapeDtypeStruct((n_rows, D_MODEL), F32),
        compiler_params=_cparams(("parallel",)),
        name="final_norm",
    )(x, gain)


PROMPT_ROWS = 256
STEP_ROWS = 128
S5_PROMPT_STEPS = 128


def _block_diag_in(bbar):
    rows_group = np.arange(S5_WIDTH) // S5_CH
    cols_group = np.arange(S5_STATE) // S5_P
    mask = jnp.asarray(rows_group[:, None] == cols_group[None, :])
    reps = (1,) * (bbar.ndim - 2) + (S5_GROUPS, 1)
    return jnp.where(mask, jnp.tile(bbar, reps), 0.0)


def _block_diag_out(c):
    rows_group = np.arange(S5_STATE) // S5_P
    cols_group = np.arange(S5_WIDTH) // S5_CH
    mask = jnp.asarray(rows_group[:, None] == cols_group[None, :])
    per_state = c.transpose(0, 1, 3, 2).reshape(c.shape[0], S5_STATE, S5_CH)
    return jnp.where(mask, jnp.tile(per_state, (1, 1, S5_GROUPS)), 0.0)


def kernel(x_prompt, x_sample, state_gla, state_mlstm_c, state_mlstm_n, state_mlstm_m, state_s5_re, state_s5_im, c_prompt, c_sample, w_ada, b_ada, norm_gain, w_in, gla_w_gate_up, gla_b_gate, gla_norm_gain, ml_b_igate, ml_b_fgate, ml_norm_gain, s5_lambda_re, s5_lambda_im, s5_log_dt, s5_b_re, s5_b_im, s5_c_re, s5_c_im, s5_d, s5_w_glu, s5_b_glu, s5_norm_gain, w_out, peer_w_q, peer_sub_keys, peer_u, peer_v, final_gain):
    bp, lp, _ = x_prompt.shape
    bs, ls, _ = x_sample.shape
    assert ls == TOKENS_PER_GROUP and lp % PROMPT_ROWS == 0 and (bs * ls) % STEP_ROWS == 0
    n_prompt = bp * lp
    n_sample = bs * ls
    n_tok = n_prompt + n_sample
    rep = lp // TOKENS_PER_GROUP

    x = jnp.concatenate([x_prompt.reshape(n_prompt, D_MODEL), x_sample.reshape(n_sample, D_MODEL)], axis=0)

    c_all = jnp.concatenate([c_prompt, c_sample], axis=0)
    n_c = c_all.shape[0]
    c_pad = (-n_c) % SUBLANE
    mod = _adaln(jnp.pad(c_all, ((0, c_pad), (0, 0))), w_ada, b_ada)
    mod = jnp.concatenate([jnp.repeat(mod[:, :bp], rep, axis=1), mod[:, bp:n_c]], axis=1)
    gains = norm_gain.reshape(DEPTH * 2, 1, D_MODEL)

    zeros = lambda *shape: jnp.zeros(shape, F32)
    seq_blocks = lp // PROMPT_ROWS
    step_blocks = n_sample // STEP_ROWS
    step_block0 = n_prompt // STEP_ROWS
    prompt_pad = (-bp) % SUBLANE

    z_src = jnp.asarray(np.maximum(_Z_SRC, 0))
    z_valid = jnp.asarray(_Z_SRC >= 0)
    w_in_p = jnp.where(z_valid, jnp.take(w_in.astype(BF16), z_src, axis=2), jnp.zeros((), BF16))
    wg_all = jnp.zeros((DEPTH, GLA_HEADS, LANE, LANE), F32).at[:, :, :GLA_RANK, :GLA_DK].set(
        gla_w_gate_up.reshape(DEPTH, GLA_RANK, GLA_HEADS, GLA_DK).transpose(0, 2, 1, 3)).astype(BF16)
    bg_all = jnp.zeros((DEPTH, GLA_HEADS, 1, LANE), F32).at[:, :, 0, :GLA_DK].set(
        gla_b_gate.reshape(DEPTH, GLA_HEADS, GLA_DK))
    gla_gain_all = gla_norm_gain.reshape(DEPTH, GLA_HEADS, 1, GLA_DV)
    bif_all = jnp.zeros((DEPTH, 1, LANE), F32).at[:, 0, :ML_HEADS].set(ml_b_igate).at[
        :, 0, F_GATE_LANE:F_GATE_LANE + ML_HEADS].set(ml_b_fgate)
    ml_gain_all = ml_norm_gain.reshape(DEPTH, ML_HEADS, 1, ML_DH)
    a_disc, bbar = _s5_discretise(s5_lambda_re, s5_lambda_im, s5_log_dt, s5_b_re, s5_b_im)
    wb_all = jnp.concatenate([_block_diag_in(bbar[:, 0]), _block_diag_in(bbar[:, 1])], axis=2).astype(BF16)
    wc_all = jnp.concatenate([_block_diag_out(s5_c_re), -_block_diag_out(s5_c_im)], axis=1).astype(BF16)
    wglu_all = s5_w_glu.astype(BF16)
    w_out_bf = w_out.astype(BF16)
    wqt_all = peer_w_q.astype(BF16).transpose(0, 2, 1)
    keys_all = peer_sub_keys.reshape(DEPTH, 2 * PEER_HEADS, N_KEYS, LANE).astype(BF16)
    u_bf = peer_u.astype(BF16)
    vt_bf = peer_v.astype(BF16).transpose(0, 2, 1)

    st_gla = state_gla.astype(F32)
    st_c = state_mlstm_c.astype(F32)
    st_n = state_mlstm_n.astype(F32)[:, :, :, None, :]
    st_m = jnp.broadcast_to(state_mlstm_m.astype(F32)[:, :, :, None, None], state_mlstm_m.shape + (1, LANE))
    st_s5 = jnp.concatenate([state_s5_re.reshape(DEPTH, bs, S5_STATE), state_s5_im.reshape(DEPTH, bs, S5_STATE)],
                            axis=2).astype(F32)

    sg_s = mc_s = mn_s = mm_s = None
    new_states = []
    for l in range(DEPTH):
        z = _norm_proj(x, mod, gains, w_in_p, layer=l, tt=512, tn=1152)

        seq_kw = dict(n_seq=bp, seq_blocks=seq_blocks, rows=PROMPT_ROWS, row_block0=0, carry=True)
        step_kw = dict(n_seq=step_blocks, seq_blocks=1, rows=STEP_ROWS, chunk=ls, row_block0=step_block0,
                       carry=False, layer=l)

        gla_w = (wg_all[l], bg_all[l], gla_gain_all[l])
        og, sg_p = _gla(z, *gla_w, zeros(bp, GLA_HEADS, GLA_DK, GLA_DV), chunk=GLA_CHUNK, **seq_kw)
        og, sg_s = _gla(z, *gla_w, st_gla, fill=(og,) if l == 0 else (og, sg_s), **step_kw)

        ml_w = (bif_all[l], ml_gain_all[l])
        om, mc_p, mn_p, mm_p = _mlstm(z, *ml_w, zeros(bp, ML_HEADS, ML_DH, ML_DH), zeros(bp, ML_HEADS, 1, ML_DH),
                                      zeros(bp, ML_HEADS, 1, LANE), chunk=ML_CHUNK, **seq_kw)
        om, mc_s, mn_s, mm_s = _mlstm(z, *ml_w, st_c, st_n, st_m,
                                      fill=(om,) if l == 0 else (om, mc_s, mn_s, mm_s), **step_kw)

        su = z[:, ZB_SU * LANE:ZB_SU * LANE + S5_WIDTH]
        u_p = jnp.pad(su[:n_prompt].reshape(bp, lp, S5_WIDTH).transpose(1, 0, 2), ((0, 0), (0, prompt_pad), (0, 0)))
        u_s = su[n_prompt:].reshape(bs, ls, S5_WIDTH).transpose(1, 0, 2)
        s5_args = (a_disc[l], wb_all[l], wc_all[l], s5_d[l].reshape(1, S5_WIDTH), wglu_all[l],
                   s5_b_glu[l][None, :], s5_norm_gain[l][None, :])
        os_p, hs_p = _s5(u_p, zeros(bp + prompt_pad, 2 * S5_STATE), *s5_args,
                         steps=S5_PROMPT_STEPS, lane_width=S5_STATE)
        os_s, hs_s = _s5(u_s, st_s5[l], *s5_args, steps=ls, lane_width=LANE)
        os_ = jnp.concatenate([os_p[:, :bp].transpose(1, 0, 2).reshape(n_prompt, S5_WIDTH),
                               os_s.transpose(1, 0, 2).reshape(n_sample, S5_WIDTH)], axis=0)

        x, ht = _out_proj(og, om, os_, x, mod, gains, w_out_bf, layer=l, tt=256)

        first, second = _peer_route(ht, wqt_all[l], keys_all[l], tt=256)
        x = _peer_dense(ht, u_bf, vt_bf, first, second, x, mod, layer=l, tt=512)

        new_states.append((sg_p, mc_p, mn_p[:, :, 0, :], mm_p[:, :, 0, 0], hs_p[:bp], hs_s))

    y_prompt = _final_norm(x, final_gain[None, :], tt=512, row0=0, n_rows=n_prompt)
    y_sample = _final_norm(x, final_gain[None, :], tt=512, row0=n_prompt, n_rows=n_sample)
    stack = lambda i: jnp.stack([ns[i] for ns in new_states])
    s5_p, s5_s = stack(4), stack(5)
    split_s5 = lambda h, lo: h[:, :, lo:lo + S5_STATE].reshape(DEPTH, -1, S5_GROUPS, S5_P)
    outs = (y_prompt.reshape(bp, lp, D_MODEL), y_sample.reshape(bs, ls, D_MODEL),
            stack(0), sg_s, stack(1), mc_s, stack(2), mn_s[:, :, :, 0, :], stack(3), mm_s[:, :, :, 0, 0],
            split_s5(s5_p, 0), split_s5(s5_s, 0), split_s5(s5_p, S5_STATE), split_s5(s5_s, S5_STATE))
    refs = (x_prompt, x_sample, state_gla, state_gla, state_mlstm_c, state_mlstm_c, state_mlstm_n, state_mlstm_n,
            state_mlstm_m, state_mlstm_m, state_s5_re, state_s5_re, state_s5_im, state_s5_im)
    return tuple(o.astype(r.dtype) for o, r in zip(outs, refs))
```

```python
import functools
import math

import numpy as np
import jax
import jax.numpy as jnp
from jax import lax
from jax.experimental import pallas as pl
from jax.experimental.pallas import tpu as pltpu

F32 = jnp.float32
BF16 = jnp.bfloat16
HIGHEST = lax.Precision.HIGHEST

D_MODEL = 2048
DEPTH = 4
GLA_HEADS = 6
GLA_DK = 64
GLA_DV = 128
GLA_RANK = 16
GLA_TAU = 16.0
GLA_CHUNK = 16
ML_HEADS = 6
ML_DH = 128
ML_CHUNK = 64
S5_WIDTH = 512
S5_CH = 16
S5_GROUPS = 32
S5_P = 64
S5_STATE = S5_GROUPS * S5_P
PEER_HEADS = 8
N_KEYS = 128
N_EXPERTS = N_KEYS * N_KEYS
PEER_TOPK = 16
NORM_EPS = 1e-6
GLA_WIDTH = GLA_HEADS * GLA_DV
ML_WIDTH = ML_HEADS * ML_DH

LANE = 128
SUBLANE = 8
TOKENS_PER_GROUP = 8
GROUP_UNROLL = 4
VMEM_LIMIT = 56 * 1024 * 1024

ZB_GQ, ZB_GK, ZB_GV, ZB_GG = 0, 6, 12, 18
ZB_MQ, ZB_MK, ZB_MV, ZB_MO = 24, 30, 36, 42
ZB_GR, ZB_MIF = 48, 49
HEADS_PER_STEP = 3
ZB_SU = 50
Z_BLOCKS = 54
Z_COLS = Z_BLOCKS * LANE
F_GATE_LANE = 8


def _z_source_columns():
    src = np.full((Z_COLS,), -1, np.int32)
    off_gq, off_gk, off_gv, off_gg, off_gr = 0, 384, 768, 1536, 2304
    off_mq, off_mk, off_mv, off_mo, off_mi, off_mf, off_su = 2320, 3088, 3856, 4624, 5392, 5398, 5404
    for h in range(GLA_HEADS):
        for d in range(GLA_DK):
            src[(ZB_GQ + h) * LANE + d] = off_gq + h * GLA_DK + d
            src[(ZB_GK + h) * LANE + d] = off_gk + h * GLA_DK + d
        for d in range(GLA_DV):
            src[(ZB_GV + h) * LANE + d] = off_gv + h * GLA_DV + d
            src[(ZB_GG + h) * LANE + d] = off_gg + h * GLA_DV + d
    for d in range(GLA_RANK):
        src[ZB_GR * LANE + d] = off_gr + d
    for h in range(ML_HEADS):
        for d in range(ML_DH):
            src[(ZB_MQ + h) * LANE + d] = off_mq + h * ML_DH + d
            src[(ZB_MK + h) * LANE + d] = off_mk + h * ML_DH + d
            src[(ZB_MV + h) * LANE + d] = off_mv + h * ML_DH + d
            src[(ZB_MO + h) * LANE + d] = off_mo + h * ML_DH + d
        src[ZB_MIF * LANE + h] = off_mi + h
        src[ZB_MIF * LANE + F_GATE_LANE + h] = off_mf + h
    for d in range(S5_WIDTH):
        src[ZB_SU * LANE + d] = off_su + d
    return src


_Z_SRC = _z_source_columns()


def _cparams(semantics):
    return pltpu.CompilerParams(dimension_semantics=semantics, vmem_limit_bytes=VMEM_LIMIT)


def _const_spec(block_shape, index_map):
    return pl.BlockSpec(block_shape, index_map, pipeline_mode=pl.Buffered(1))


def _rms(x):
    return x * lax.rsqrt(jnp.mean(x * x, axis=-1, keepdims=True) + NORM_EPS)


def _pack_rows(x):
    *lead, m2, n = x.shape
    pairs = jnp.swapaxes(x.reshape(*lead, m2 // 2, 2, n), -1, -2)
    return lax.bitcast_convert_type(pairs, jnp.uint32)


def _unpack_rows(x):
    return pltpu.bitcast(x, BF16)


def _dot(a, b):
    return jnp.dot(a, b, preferred_element_type=F32)


def _dot_nt(a, b):
    return lax.dot_general(a, b, (((1,), (1,)), ((), ())), preferred_element_type=F32)


def _chunk_sums(same, tri, x):
    rows = x.shape[0]
    masks = jnp.concatenate([jnp.where(tri, 1.0, 0.0), jnp.where(same, 1.0, 0.0)], axis=0).astype(BF16)
    hi = x.astype(BF16)
    r1 = x - hi.astype(F32)
    mid = r1.astype(BF16)
    lo = (r1 - mid.astype(F32)).astype(BF16)
    sums = _dot(masks, jnp.concatenate([hi, mid, lo], axis=1))
    total = sums[:, 0:LANE] + sums[:, LANE:2 * LANE] + sums[:, 2 * LANE:3 * LANE]
    return total[0:rows], total[rows:2 * rows]


def _adaln_kernel(c_ref, w_ref, b_ref, o_ref):
    c = c_ref[...]
    s = (c * jax.nn.sigmoid(c)).astype(BF16)
    o_ref[0] = _dot(s, w_ref[0].astype(BF16)) + b_ref[0]


def _adaln(c_all, w_ada, b_ada):
    n_rows = c_all.shape[0]
    n_out = w_ada.shape[-1]
    tn = 1024
    return pl.pallas_call(
        _adaln_kernel,
        grid=(DEPTH, n_out // tn),
        in_specs=[
            pl.BlockSpec((n_rows, D_MODEL), lambda l, j: (0, 0)),
            pl.BlockSpec((1, D_MODEL, tn), lambda l, j: (l, 0, j)),
            pl.BlockSpec((1, 1, tn), lambda l, j: (l, 0, j)),
        ],
        out_specs=pl.BlockSpec((1, n_rows, tn), lambda l, j: (l, 0, j)),
        out_shape=jax.ShapeDtypeStruct((DEPTH, n_rows, n_out), F32),
        compiler_params=_cparams(("parallel", "parallel")),
        name="adaln",
    )(c_all, w_ada, b_ada.reshape(DEPTH, 1, n_out))


def _modulated_group(x8, gain, sc_row, sh_row):
    return _rms(x8) * gain * (1.0 + sc_row) + sh_row


def _norm_proj_kernel(x_ref, sc_ref, sh_ref, gain_ref, w_ref, o_ref, h_scr, *, groups):
    @pl.when(pl.program_id(1) == 0)
    def _():
        gain = gain_ref[...]

        def body(p, carry):
            hs = []
            for u in range(2):
                g = p * 2 + u
                r0 = pl.multiple_of(g * TOKENS_PER_GROUP, TOKENS_PER_GROUP)
                hs.append(_modulated_group(x_ref[pl.ds(r0, TOKENS_PER_GROUP), :], gain,
                                           sc_ref[pl.ds(g, 1), :], sh_ref[pl.ds(g, 1), :]))
            r = pl.multiple_of(p * 2 * TOKENS_PER_GROUP, 2 * TOKENS_PER_GROUP)
            h_scr[pl.ds(r, 2 * TOKENS_PER_GROUP), :] = jnp.concatenate(hs, axis=0).astype(BF16)
            return carry

        lax.fori_loop(0, groups // 2, body, 0, unroll=GROUP_UNROLL)

    o_ref[...] = _dot(h_scr[...], _unpack_rows(w_ref[...]))


MOD_SH1, MOD_SC1, MOD_G1, MOD_SH2, MOD_SC2, MOD_G2 = range(6)


def _mod_spec(groups, layer, kind):
    return pl.BlockSpec((None, groups, D_MODEL), lambda *a: (layer, a[0], kind))


def _gain_spec(layer, which):
    return pl.BlockSpec((None, 1, D_MODEL), lambda *a: (2 * layer + which, 0, 0))


def _norm_proj(x, mod, gains, w, *, layer, tt, tn):
    n_tok = x.shape[0]
    n_out = w.shape[-1]
    groups = tt // TOKENS_PER_GROUP
    return pl.pallas_call(
        functools.partial(_norm_proj_kernel, groups=groups),
        grid=(n_tok // tt, n_out // tn),
        in_specs=[
            pl.BlockSpec((tt, D_MODEL), lambda i, j: (i, 0)),
            _mod_spec(groups, layer, MOD_SC1),
            _mod_spec(groups, layer, MOD_SH1),
            _gain_spec(layer, 0),
            pl.BlockSpec((None, D_MODEL // 2, tn), lambda i, j: (layer, 0, j)),
        ],
        out_specs=pl.BlockSpec((tt, tn), lambda i, j: (i, j)),
        out_shape=jax.ShapeDtypeStruct((n_tok, n_out), F32),
        scratch_shapes=[pltpu.VMEM((tt, D_MODEL), BF16)],
        compiler_params=_cparams(("parallel", "arbitrary")),
        name="norm_proj",
    )(x, mod, mod, gains, w)


def _chunk_masks(rows, chunk):
    shift = int(math.log2(chunk))
    ri = lax.broadcasted_iota(jnp.int32, (rows, rows), 0)
    ci = lax.broadcasted_iota(jnp.int32, (rows, rows), 1)
    same = (ri >> shift) == (ci >> shift)
    tri = jnp.logical_and(same, ci <= ri)
    return same, tri


def _gla_kernel(q_ref, k_ref, v_ref, g_ref, r_ref, wg_ref, bg_ref, gain_ref, s0_ref,
                o_ref, sfin_ref, st_scr, **kw):
    carry = kw["carry"]
    heads = range(HEADS_PER_STEP)
    if carry:
        @pl.when(pl.program_id(2) == 0)
        def _():
            for hh in heads:
                st_scr[hh] = _gla_load_state(s0_ref.at[:, hh:hh + 1], 0)
    for hh in heads:
        ln = slice(hh * LANE, (hh + 1) * LANE)
        one = slice(hh, hh + 1)
        _gla_head(q_ref.at[:, ln], k_ref.at[:, ln], v_ref.at[:, ln], g_ref.at[:, ln], r_ref,
                  wg_ref.at[one], bg_ref.at[one], gain_ref.at[one], s0_ref.at[:, one],
                  o_ref.at[:, ln], sfin_ref.at[:, one], st_scr.at[hh], **kw)
    if carry:
        @pl.when(pl.program_id(2) == pl.num_programs(2) - 1)
        def _():
            for hh in heads:
                sfin_ref[0, hh] = st_scr[hh].T[0:GLA_DK, :]


def _gla_load_state(s0_ref, c):
    zero_pad = jnp.zeros((LANE - GLA_DK, GLA_DV), F32)
    return jnp.concatenate([s0_ref[c, 0], zero_pad], axis=0).T


def _gla_head(q_ref, k_ref, v_ref, g_ref, r_ref, wg_ref, bg_ref, gain_ref, s0_ref,
              o_ref, sfin_ref, st_scr, *, rows, chunk, carry):
    n_chunks = rows // chunk
    shift = int(math.log2(chunk))
    same, tri = _chunk_masks(rows, chunk)

    q = q_ref[...] * (GLA_DK ** -0.5)
    k = k_ref[...]
    v = v_ref[...]
    la = jax.nn.log_sigmoid(_dot(r_ref[...].astype(BF16), wg_ref[0]) + bg_ref[0]) / GLA_TAU
    bl, bt = _chunk_sums(same, tri, la)
    qd = (q * jnp.exp(bl)).astype(BF16)
    ki = (k * jnp.exp(-bl)).astype(BF16)
    kd = (k * jnp.exp(bt - bl)).astype(BF16)
    att = jnp.where(tri, _dot_nt(qd, ki), 0.0)
    o_intra = _dot(att.astype(BF16), v.astype(BF16))

    v_t = v.T
    lane_chunk = lax.broadcasted_iota(jnp.int32, (GLA_DV, rows), 1) >> shift
    if carry:
        s_t = st_scr[...]

    sliced = chunk % (2 * SUBLANE) == 0
    row_chunk = lax.broadcasted_iota(jnp.int32, (rows, GLA_DV), 0) >> shift
    o_inter = [] if sliced else jnp.zeros((rows, GLA_DV), F32)
    for c in range(n_chunks):
        lo = c * chunk
        if not carry:
            s_t = _gla_load_state(s0_ref, c)
        if sliced:
            o_inter.append(_dot_nt(qd[lo:lo + chunk], s_t.astype(BF16)))
        else:
            o_inter = jnp.where(row_chunk == c, _dot_nt(qd, s_t.astype(BF16)), o_inter)
        decay = jnp.exp(bt[lo:lo + 1, :])
        v_c = jnp.where(lane_chunk == c, v_t, 0.0).astype(BF16)
        s_t = s_t * decay + _dot(v_c, kd)
        if not carry:
            sfin_ref[c, 0] = s_t.T[0:GLA_DK, :]

    if carry:
        st_scr[...] = s_t

    o = o_intra + (jnp.concatenate(o_inter, axis=0) if sliced else o_inter)
    g = g_ref[...]
    o_ref[...] = _rms(o) * gain_ref[0] * (g * jax.nn.sigmoid(g))


def _without_refs(kernel_fn, first, count):
    def wrapped(*refs):
        return kernel_fn(*refs[:first], *refs[first + count:])
    return wrapped


def _mixer_grid(n_seq, n_heads, seq_blocks, row_block0, carry):
    if carry:
        return ((n_seq, n_heads, seq_blocks), lambda b, h, t: row_block0 + b * seq_blocks + t,
                ("parallel", "parallel", "arbitrary"))
    return (n_seq, n_heads), lambda b, h: row_block0 + b, ("parallel", "parallel")


def _state_spec(per_block, d2, d3, layer):
    if layer is None:
        return pl.BlockSpec((per_block, HEADS_PER_STEP, d2, d3), lambda *a: (a[0], a[1], 0, 0))
    return pl.BlockSpec((None, per_block, HEADS_PER_STEP, d2, d3), lambda *a: (layer, a[0], a[1], 0, 0))


def _head_cols_spec(rows, rb, block0):
    assert block0 % HEADS_PER_STEP == 0
    return pl.BlockSpec((rows, HEADS_PER_STEP * LANE), lambda *a: (rb(*a), block0 // HEADS_PER_STEP + a[1]))


def _head_param_spec(shape):
    return pl.BlockSpec((HEADS_PER_STEP,) + shape, lambda *a: (a[1],) + (0,) * len(shape))


def _gla(z, wg, bg, gain, s0, *, n_seq, seq_blocks, rows, chunk, row_block0, carry,
         layer=None, fill=()):
    per_block = 1 if carry else rows // chunk
    grid, rb, sem = _mixer_grid(n_seq, GLA_HEADS // HEADS_PER_STEP, seq_blocks, row_block0, carry)
    zspec = functools.partial(_head_cols_spec, rows, rb)
    state_spec = _state_spec(per_block, GLA_DK, GLA_DV, layer)
    n_in = 9
    return pl.pallas_call(
        _without_refs(functools.partial(_gla_kernel, rows=rows, chunk=chunk, carry=carry), n_in, len(fill)),
        grid=grid,
        in_specs=[zspec(ZB_GQ), zspec(ZB_GK), zspec(ZB_GV), zspec(ZB_GG),
                  pl.BlockSpec((rows, LANE), lambda *a: (rb(*a), ZB_GR)),
                  _head_param_spec((LANE, LANE)), _head_param_spec((1, LANE)), _head_param_spec((1, LANE)),
                  state_spec]
                 + [pl.BlockSpec(memory_space=pl.ANY)] * len(fill),
        out_specs=[_head_cols_spec(rows, rb, 0), state_spec],
        out_shape=[jax.ShapeDtypeStruct((z.shape[0], GLA_WIDTH), F32),
                   jax.ShapeDtypeStruct(s0.shape, F32)],
        input_output_aliases={n_in + i: i for i in range(len(fill))},
        scratch_shapes=[pltpu.VMEM((HEADS_PER_STEP, GLA_DV, LANE), F32)],
        compiler_params=_cparams(sem),
        name="gla_seq" if carry else "gla_step",
    )(z, z, z, z, z, wg, bg, gain, s0, *fill)


def _mlstm_kernel(q_ref, k_ref, v_ref, og_ref, gate_ref, bif_ref, gain_ref, c0_ref, n0_ref, m0_ref,
                  h_ref, cfin_ref, nfin_ref, mfin_ref, c_scr, n_scr, m_scr, **kw):
    carry = kw["carry"]
    heads = range(HEADS_PER_STEP)
    if carry:
        @pl.when(pl.program_id(2) == 0)
        def _():
            for hh in heads:
                c_scr[hh] = c0_ref[0, hh]
                n_scr[hh] = n0_ref[0, hh]
                m_scr[hh] = m0_ref[0, hh]
    _mlstm_heads(q_ref, k_ref, v_ref, og_ref, gate_ref, bif_ref, gain_ref, c0_ref, n0_ref, m0_ref,
                 h_ref, cfin_ref, nfin_ref, mfin_ref, c_scr, n_scr, m_scr, **kw)
    if carry:
        @pl.when(pl.program_id(2) == pl.num_programs(2) - 1)
        def _():
            for hh in heads:
                cfin_ref[0, hh] = c_scr[hh]
                nfin_ref[0, hh] = n_scr[hh]
                mfin_ref[0, hh] = m_scr[hh]


def _mlstm_heads(q_ref, k_ref, v_ref, og_ref, gate_ref, bif_ref, gain_ref, c0_ref, n0_ref, m0_ref,
                 h_ref, cfin_ref, nfin_ref, mfin_ref, c_scr, n_scr, m_scr, **kw):
    for hh in range(HEADS_PER_STEP):
        ln = slice(hh * LANE, (hh + 1) * LANE)
        one = slice(hh, hh + 1)
        _mlstm_head(pl.program_id(1) * HEADS_PER_STEP + hh,
                    q_ref.at[:, ln], k_ref.at[:, ln], v_ref.at[:, ln], og_ref.at[:, ln], gate_ref, bif_ref,
                    gain_ref.at[one], c0_ref.at[:, one], n0_ref.at[:, one], m0_ref.at[:, one],
                    h_ref.at[:, ln], cfin_ref.at[:, one], nfin_ref.at[:, one], mfin_ref.at[:, one],
                    c_scr.at[hh], n_scr.at[hh], m_scr.at[hh], **kw)


def _mlstm_head(head, q_ref, k_ref, v_ref, og_ref, gate_ref, bif_ref, gain_ref, c0_ref, n0_ref, m0_ref,
                h_ref, cfin_ref, nfin_ref, mfin_ref, c_scr, n_scr, m_scr, *, rows, chunk, carry):
    n_chunks = rows // chunk
    shift = int(math.log2(chunk))
    same, tri = _chunk_masks(rows, chunk)
    neg_inf = jnp.float32(-jnp.inf)

    x = gate_ref[...] + bif_ref[...]
    log_f = jax.nn.log_sigmoid(x)
    b_cum, b_tot = _chunk_sums(same, tri, log_f)
    lane = lax.broadcasted_iota(jnp.int32, (rows, LANE), 1)
    y = jnp.where(lane < F_GATE_LANE, x, b_cum)
    pick_i = lane == head
    pick_b = lane == head + F_GATE_LANE
    i_col = jnp.sum(jnp.where(pick_i, y, 0.0), axis=-1, keepdims=True)
    b_col = jnp.sum(jnp.where(pick_b, y, 0.0), axis=-1, keepdims=True)
    bt_col = jnp.sum(jnp.where(pick_b, b_tot, 0.0), axis=-1, keepdims=True)
    sub = lax.broadcasted_iota(jnp.int32, (LANE, rows), 0)
    y_t = y.T
    i_row = jnp.sum(jnp.where(sub == head, y_t, 0.0), axis=0, keepdims=True)
    b_row = jnp.sum(jnp.where(sub == head + F_GATE_LANE, y_t, 0.0), axis=0, keepdims=True)

    a_col = bt_col - b_col + i_col
    mloc_col = bt_col + jnp.max(jnp.where(same, i_row - b_row, neg_inf), axis=-1, keepdims=True)

    q = q_ref[...] * (ML_DH ** -0.5)
    k = k_ref[...]
    qb = q.astype(BF16)
    kb = k.astype(BF16)
    vb = v_ref[...].astype(BF16)
    kw = k * jnp.exp(a_col - mloc_col)
    kw_t = kw.T
    lane_chunk = lax.broadcasted_iota(jnp.int32, (ML_DH, rows), 1) >> shift

    def advance(c, c_st, n_st, m_st, m_last):
        hi = (c + 1) * chunk
        decay = jnp.exp(bt_col[hi - 1:hi] + m_st - m_last)
        scale = jnp.exp(mloc_col[hi - 1:hi] - m_last)
        kw_c = jnp.where(lane_chunk == c, kw_t, 0.0).astype(BF16)
        c_new = decay * c_st + scale * _dot(kw_c, vb)
        n_new = decay * n_st + scale * jnp.sum(kw[hi - chunk:hi], axis=0, keepdims=True)
        return c_new, n_new

    if carry:
        c_st = c_scr[...]
        n_st = n_scr[...]
        m_st = m_scr[:, 0:1]
        ri = lax.broadcasted_iota(jnp.int32, (chunk, chunk), 0)
        ci = lax.broadcasted_iota(jnp.int32, (chunk, chunk), 1)
        tri_c = ci <= ri
        h_chunks = []
        for c in range(n_chunks):
            lo = c * chunk
            hi = lo + chunk
            bc = b_col[lo:hi]
            d_log = jnp.where(tri_c, bc - b_row[:, lo:hi] + i_row[:, lo:hi], neg_inf)
            g_inter = bc + m_st
            m_t = jnp.maximum(g_inter, jnp.max(d_log, axis=-1, keepdims=True))
            w_inter = jnp.exp(g_inter - m_t)
            s = _dot_nt(qb[lo:hi], kb[lo:hi]) * jnp.exp(d_log - m_t)
            num = w_inter * _dot(qb[lo:hi], c_st.astype(BF16)) + _dot(s.astype(BF16), vb[lo:hi])
            den = (w_inter * jnp.sum(q[lo:hi] * n_st, axis=-1, keepdims=True)
                   + jnp.sum(s, axis=-1, keepdims=True))
            h_chunks.append(num / jnp.maximum(jnp.abs(den), jnp.exp(-m_t)))
            m_last = m_t[chunk - 1:chunk]
            c_st, n_st = advance(c, c_st, n_st, m_st, m_last)
            m_st = m_last
        h = jnp.concatenate(h_chunks, axis=0)
        c_scr[...] = c_st
        n_scr[...] = n_st
        m_scr[...] = jnp.broadcast_to(m_st, (1, LANE))
    else:
        row_chunk1 = lax.broadcasted_iota(jnp.int32, (rows, 1), 0) >> shift
        row_chunk = lax.broadcasted_iota(jnp.int32, (rows, ML_DH), 0) >> shift
        m_rows = jnp.zeros((rows, 1), F32)
        n_rows = jnp.zeros((rows, ML_DH), F32)
        qc = jnp.zeros((rows, ML_DH), F32)
        for c in range(n_chunks):
            m_rows = jnp.where(row_chunk1 == c, m0_ref[c, 0][:, 0:1], m_rows)
            n_rows = jnp.where(row_chunk == c, n0_ref[c, 0], n_rows)
            qc = jnp.where(row_chunk == c, _dot(qb, c0_ref[c, 0].astype(BF16)), qc)
        d_log = jnp.where(tri, b_col - b_row + i_row, neg_inf)
        g_inter = b_col + m_rows
        m_t = jnp.maximum(g_inter, jnp.max(d_log, axis=-1, keepdims=True))
        w_inter = jnp.exp(g_inter - m_t)
        s = _dot_nt(qb, kb) * jnp.exp(d_log - m_t)
        num = w_inter * qc + _dot(s.astype(BF16), vb)
        den = w_inter * jnp.sum(q * n_rows, axis=-1, keepdims=True) + jnp.sum(s, axis=-1, keepdims=True)
        h = num / jnp.maximum(jnp.abs(den), jnp.exp(-m_t))
        for c in range(n_chunks):
            hi = (c + 1) * chunk
            m_last = m_t[hi - 1:hi]
            c_new, n_new = advance(c, c0_ref[c, 0], n0_ref[c, 0], m0_ref[c, 0][:, 0:1], m_last)
            cfin_ref[c, 0] = c_new
            nfin_ref[c, 0] = n_new
            mfin_ref[c, 0] = jnp.broadcast_to(m_last, (1, LANE))

    h_ref[...] = _rms(h) * gain_ref[0] * jax.nn.sigmoid(og_ref[...])


def _mlstm(z, bif, gain, c0, n0, m0, *, n_seq, seq_blocks, rows, chunk, row_block0, carry,
           layer=None, fill=()):
    per_block = 1 if carry else rows // chunk
    grid, rb, sem = _mixer_grid(n_seq, ML_HEADS // HEADS_PER_STEP, seq_blocks, row_block0, carry)
    zspec = functools.partial(_head_cols_spec, rows, rb)
    state_specs = [_state_spec(per_block, ML_DH, ML_DH, layer), _state_spec(per_block, 1, ML_DH, layer),
                   _state_spec(per_block, 1, LANE, layer)]
    n_in = 10
    return pl.pallas_call(
        _without_refs(functools.partial(_mlstm_kernel, rows=rows, chunk=chunk, carry=carry), n_in, len(fill)),
        grid=grid,
        in_specs=[zspec(ZB_MQ), zspec(ZB_MK), zspec(ZB_MV), zspec(ZB_MO),
                  pl.BlockSpec((rows, LANE), lambda *a: (rb(*a), ZB_MIF)),
                  pl.BlockSpec((1, LANE), lambda *a: (0, 0)),
                  _head_param_spec((1, LANE))]
                 + state_specs + [pl.BlockSpec(memory_space=pl.ANY)] * len(fill),
        out_specs=[_head_cols_spec(rows, rb, 0)] + state_specs,
        out_shape=[jax.ShapeDtypeStruct((z.shape[0], ML_WIDTH), F32),
                   jax.ShapeDtypeStruct(c0.shape, F32), jax.ShapeDtypeStruct(n0.shape, F32),
                   jax.ShapeDtypeStruct(m0.shape, F32)],
        input_output_aliases={n_in + i: i for i in range(len(fill))},
        scratch_shapes=[pltpu.VMEM((HEADS_PER_STEP, ML_DH, ML_DH), F32), pltpu.VMEM((HEADS_PER_STEP, 1, ML_DH), F32),
                        pltpu.VMEM((HEADS_PER_STEP, 1, LANE), F32)],
        compiler_params=_cparams(sem),
        name="mlstm_seq" if carry else "mlstm_step",
    )(z, z, z, z, z, bif, gain, c0, n0, m0, *fill)


def _s5_disc_kernel(lre_ref, lim_ref, ldt_ref, bre_ref, bim_ref, a_ref, bb_ref):
    lam_re = lre_ref[0]
    lam_im = lim_ref[0]
    dt = jnp.exp(ldt_ref[0])
    mag = jnp.exp(lam_re * dt)
    ar = mag * jnp.cos(lam_im * dt)
    ai = mag * jnp.sin(lam_im * dt)
    den = lam_re * lam_re + lam_im * lam_im
    fr = ((ar - 1.0) * lam_re + ai * lam_im) / den
    fi = (ai * lam_re - (ar - 1.0) * lam_im) / den
    b_re = bre_ref[0]
    b_im = bim_ref[0]
    a_ref[0, 0:1, :] = ar
    a_ref[0, 1:2, :] = ai
    bb_ref[0, 0] = fr * b_re - fi * b_im
    bb_ref[0, 1] = fr * b_im + fi * b_re


def _s5_discretise(lam_re, lam_im, log_dt, b_re, b_im):
    flat = lambda t: t.reshape(DEPTH, 1, S5_STATE)
    ldt = jnp.broadcast_to(log_dt[:, :, None], (DEPTH, S5_GROUPS, S5_P))
    chan = lambda t: t.reshape(DEPTH, S5_STATE, S5_CH).transpose(0, 2, 1)
    row = pl.BlockSpec((1, 1, S5_STATE), lambda l: (l, 0, 0))
    mat = pl.BlockSpec((1, S5_CH, S5_STATE), lambda l: (l, 0, 0))
    return pl.pallas_call(
        _s5_disc_kernel,
        grid=(DEPTH,),
        in_specs=[row, row, row, mat, mat],
        out_specs=[pl.BlockSpec((1, 2, S5_STATE), lambda l: (l, 0, 0)),
                   pl.BlockSpec((1, 2, S5_CH, S5_STATE), lambda l: (l, 0, 0, 0))],
        out_shape=[jax.ShapeDtypeStruct((DEPTH, 2, S5_STATE), F32),
                   jax.ShapeDtypeStruct((DEPTH, 2, S5_CH, S5_STATE), F32)],
        compiler_params=_cparams(("parallel",)),
        name="s5_disc",
    )(flat(lam_re), flat(lam_im), flat(ldt), chan(b_re), chan(b_im))


def _s5_kernel(u_ref, h0_ref, a_ref, wb_ref, wc_ref, d_ref, wglu_ref, bglu_ref, gain_ref,
               y_ref, hfin_ref, bu_scr, h_scr, *, steps, batch, lane_width):
    n_rows = steps * batch

    @pl.when(pl.program_id(0) == 0)
    def _():
        h_scr[...] = h0_ref[...]

    u = u_ref[...].reshape(n_rows, S5_WIDTH)
    bu_scr[...] = _dot(u.astype(BF16), wb_ref[...]).reshape(steps, batch, 2 * S5_STATE)

    for j in range(S5_STATE // lane_width):
        re = slice(j * lane_width, (j + 1) * lane_width)
        im = slice(S5_STATE + j * lane_width, S5_STATE + (j + 1) * lane_width)
        ar = a_ref[0:1, re]
        ai = a_ref[1:2, re]

        def step(t, carry):
            hr, hi = carry
            nr = ar * hr - ai * hi + bu_scr[t, :, re]
            ni = ar * hi + ai * hr + bu_scr[t, :, im]
            bu_scr[t, :, re] = nr
            bu_scr[t, :, im] = ni
            return nr, ni

        hr, hi = lax.fori_loop(0, steps, step, (h_scr[:, re], h_scr[:, im]), unroll=min(steps, 8))
        h_scr[:, re] = hr
        h_scr[:, im] = hi

    hs = bu_scr[...].reshape(n_rows, 2 * S5_STATE).astype(BF16)
    y = _dot(hs, wc_ref[...]) + d_ref[...] * u
    y = jax.nn.gelu(y)
    out = y * jax.nn.sigmoid(_dot(y.astype(BF16), wglu_ref[...]) + bglu_ref[...])
    y_ref[...] = (_rms(out) * gain_ref[...]).reshape(steps, batch, S5_WIDTH)

    @pl.when(pl.program_id(0) == pl.num_programs(0) - 1)
    def _():
        hfin_ref[...] = h_scr[...]


def _s5(u_t, h0, a, wb, wc, d, wglu, bglu, gain, *, steps, lane_width):
    seq, batch, _ = u_t.shape
    const = lambda shape: _const_spec(shape, lambda t: (0,) * len(shape))
    return pl.pallas_call(
        functools.partial(_s5_kernel, steps=steps, batch=batch, lane_width=lane_width),
        grid=(seq // steps,),
        in_specs=[pl.BlockSpec((steps, batch, S5_WIDTH), lambda t: (t, 0, 0)),
                  const((batch, 2 * S5_STATE)), const((2, S5_STATE)),
                  const((S5_WIDTH, 2 * S5_STATE)), const((2 * S5_STATE, S5_WIDTH)),
                  const((1, S5_WIDTH)), const((S5_WIDTH, S5_WIDTH)), const((1, S5_WIDTH)),
                  const((1, S5_WIDTH))],
        out_specs=[pl.BlockSpec((steps, batch, S5_WIDTH), lambda t: (t, 0, 0)),
                   pl.BlockSpec((batch, 2 * S5_STATE), lambda t: (0, 0))],
        out_shape=[jax.ShapeDtypeStruct((seq, batch, S5_WIDTH), F32),
                   jax.ShapeDtypeStruct((batch, 2 * S5_STATE), F32)],
        scratch_shapes=[pltpu.VMEM((steps, batch, 2 * S5_STATE), F32),
                        pltpu.VMEM((batch, 2 * S5_STATE), F32)],
        compiler_params=_cparams(("arbitrary",)),
        name="s5",
    )(u_t, h0, a, wb, wc, d, wglu, bglu, gain)


def _out_proj_kernel(og_ref, om_ref, os_ref, x_ref, g1_ref, sc_ref, sh_ref, gain_ref,
                     wg_ref, wm_ref, ws_ref, xo_ref, ht_ref, mix_scr, h_scr, *, groups):
    mix_scr[...] = (_dot(og_ref[...].astype(BF16), _unpack_rows(wg_ref[...]))
                    + _dot(om_ref[...].astype(BF16), _unpack_rows(wm_ref[...]))
                    + _dot(os_ref[...].astype(BF16), _unpack_rows(ws_ref[...])))
    gain = gain_ref[...]

    def body(g, carry):
        r0 = pl.multiple_of(g * TOKENS_PER_GROUP, TOKENS_PER_GROUP)
        rows = pl.ds(r0, TOKENS_PER_GROUP)
        xn = x_ref[rows, :] + g1_ref[pl.ds(g, 1), :] * mix_scr[rows, :]
        xo_ref[rows, :] = xn
        h_scr[rows, :] = _modulated_group(xn, gain, sc_ref[pl.ds(g, 1), :], sh_ref[pl.ds(g, 1), :])
        return carry

    lax.fori_loop(0, groups, body, 0, unroll=GROUP_UNROLL)
    ht_ref[...] = pltpu.bitcast(h_scr[...].T.astype(BF16), jnp.uint32)


def _out_proj(og, om, os_, x, mod, gains, w_out, *, layer, tt):
    n_tok = x.shape[0]
    groups = tt // TOKENS_PER_GROUP
    tok = lambda width: pl.BlockSpec((tt, width), lambda i: (i, 0))
    w_rows = lambda height, block: _const_spec((None, height // 2, D_MODEL), lambda i: (layer, block, 0))
    assert GLA_WIDTH == ML_WIDTH and (GLA_WIDTH + ML_WIDTH) % S5_WIDTH == 0
    return pl.pallas_call(
        functools.partial(_out_proj_kernel, groups=groups),
        grid=(n_tok // tt,),
        in_specs=[tok(GLA_WIDTH), tok(ML_WIDTH), tok(S5_WIDTH), tok(D_MODEL),
                  _mod_spec(groups, layer, MOD_G1), _mod_spec(groups, layer, MOD_SC2),
                  _mod_spec(groups, layer, MOD_SH2), _gain_spec(layer, 1),
                  w_rows(GLA_WIDTH, 0), w_rows(ML_WIDTH, 1),
                  w_rows(S5_WIDTH, (GLA_WIDTH + ML_WIDTH) // S5_WIDTH)],
        out_specs=[tok(D_MODEL), pl.BlockSpec((D_MODEL // 2, tt), lambda i: (0, i))],
        out_shape=[jax.ShapeDtypeStruct((n_tok, D_MODEL), F32),
                   jax.ShapeDtypeStruct((D_MODEL // 2, n_tok), jnp.uint32)],
        scratch_shapes=[pltpu.VMEM((tt, D_MODEL), F32), pltpu.VMEM((tt, D_MODEL), F32)],
        compiler_params=_cparams(("parallel",)),
        name="out_proj",
    )(og, om, os_, x, mod, mod, mod, gains, w_out, w_out, w_out)


def _top_values(cur, count, out_scr=None, with_rank=False):
    neg_inf = jnp.float32(-jnp.inf)
    vals = []
    rank = jnp.full(cur.shape, float(count), F32) if with_rank else None
    for r in range(count):
        m = jnp.max(cur, axis=0, keepdims=True)
        vals.append(m)
        if out_scr is not None:
            out_scr[r:r + 1, :] = m
        hit = cur == m
        if with_rank:
            rank = jnp.where(hit, float(r), rank)
        if r + 1 < count:
            cur = jnp.where(hit, neg_inf, cur)
    return (vals, rank) if with_rank else vals


def _peer_route_kernel(ht_ref, wqt_ref, keys_ref, first_ref, second_ref, qt_scr, v1_scr, v2_scr, cand_scr,
                       *, tt):
    qt_scr[...] = _dot(_unpack_rows(wqt_ref[...]), _unpack_rows(ht_ref[...]))
    for h in range(PEER_HEADS):
        sc = []
        for side in range(2):
            hs = 2 * h + side
            qb = qt_scr[hs * LANE:(hs + 1) * LANE, :].astype(BF16)
            sc.append(_dot(keys_ref[hs], qb))
        for lt in range(tt // LANE):
            ls = slice(lt * LANE, (lt + 1) * LANE)
            s1 = sc[0][:, ls]
            s2 = sc[1][:, ls]
            _top_values(s1, PEER_TOPK, v1_scr)
            _, rank2 = _top_values(s2, PEER_TOPK, v2_scr, with_rank=True)
            v2_all = v2_scr[...]
            v2_head = v2_scr[0:SUBLANE, :]
            cand_scr[0:PEER_TOPK, :] = v1_scr[0:1, :] + v2_all
            for a in range(1, PEER_TOPK):
                lo = PEER_TOPK + (a - 1) * SUBLANE
                cand_scr[lo:lo + SUBLANE, :] = v1_scr[a:a + 1, :] + v2_head
            top = _top_values(cand_scr[...], PEER_TOPK)
            z = jnp.zeros_like(top[0])
            for tv in top:
                z = z + jnp.exp(tv - top[0])
            tau = top[PEER_TOPK - 1]
            n1 = jnp.zeros_like(s1)
            for b in range(SUBLANE):
                n1 = n1 + jnp.where(s1 + v2_scr[b:b + 1, :] >= tau, 1.0, 0.0)
            best = v1_scr[0:1, :]
            tail = jnp.zeros_like(best)
            for b in range(SUBLANE, PEER_TOPK):
                tail = tail + jnp.where(best + v2_scr[b:b + 1, :] >= tau, 1.0, 0.0)
            n1 = n1 + jnp.where(s1 == best, tail, 0.0)
            first_ref[2 * h, :, ls] = n1
            first_ref[2 * h + 1, :, ls] = jnp.exp(s1 - v1_scr[0:1, :]) / z
            second_ref[2 * h, :, ls] = rank2.astype(BF16)
            second_ref[2 * h + 1, :, ls] = jnp.exp(s2 - v2_scr[0:1, :]).astype(BF16)


def _peer_route(ht, wqt, keys, *, tt):
    n_tok = ht.shape[1]
    n_hs = 2 * PEER_HEADS
    n_cand = PEER_TOPK + (PEER_TOPK - 1) * SUBLANE
    return pl.pallas_call(
        functools.partial(_peer_route_kernel, tt=tt),
        grid=(n_tok // tt,),
        in_specs=[pl.BlockSpec((D_MODEL // 2, tt), lambda i: (0, i)),
                  _const_spec((n_hs * LANE // 2, D_MODEL), lambda i: (0, 0)),
                  _const_spec((n_hs, N_KEYS, LANE), lambda i: (0, 0, 0))],
        out_specs=[pl.BlockSpec((n_hs, N_KEYS, tt), lambda i: (0, 0, i)),
                   pl.BlockSpec((n_hs, N_KEYS, tt), lambda i: (0, 0, i))],
        out_shape=[jax.ShapeDtypeStruct((n_hs, N_KEYS, n_tok), F32),
                   jax.ShapeDtypeStruct((n_hs, N_KEYS, n_tok), BF16)],
        scratch_shapes=[pltpu.VMEM((n_hs * LANE, tt), F32), pltpu.VMEM((PEER_TOPK, LANE), F32),
                        pltpu.VMEM((PEER_TOPK, LANE), F32), pltpu.VMEM((n_cand, LANE), F32)],
        compiler_params=_cparams(("parallel",)),
        name="peer_route",
    )(ht, wqt, keys)


DENSE_FIRST_KEYS = 4
DENSE_SUB = DENSE_FIRST_KEYS * N_KEYS
GATE_ROWS = 4 * SUBLANE


def _gate_times_act(first_ref, row0, second_ref, act_ref, p_ref, lane_tiles):
    n_pieces = N_KEYS // GATE_ROWS
    zero = jnp.zeros((GATE_ROWS, LANE), BF16)
    for lt in lane_tiles:
        ls = slice(lt * LANE, (lt + 1) * LANE)
        gates = [[zero for _ in range(n_pieces)] for _ in range(DENSE_FIRST_KEYS)]
        for h in range(PEER_HEADS):
            counts, weights = [], []
            for ii in range(DENSE_FIRST_KEYS):
                r = row0 + ii
                counts.append(jnp.broadcast_to(first_ref[2 * h, 0, r:r + 1, ls], (GATE_ROWS, LANE)).astype(BF16))
                weights.append(jnp.broadcast_to(first_ref[2 * h + 1, 0, r:r + 1, ls], (GATE_ROWS, LANE)).astype(BF16))
            for jp in range(n_pieces):
                js = slice(jp * GATE_ROWS, (jp + 1) * GATE_ROWS)
                rank2 = second_ref[2 * h, js, ls]
                e2 = second_ref[2 * h + 1, js, ls]
                for ii in range(DENSE_FIRST_KEYS):
                    gates[ii][jp] = gates[ii][jp] + jnp.where(rank2 < counts[ii], e2, zero) * weights[ii]
        for ii in range(DENSE_FIRST_KEYS):
            for jp in range(n_pieces):
                rs = slice(ii * N_KEYS + jp * GATE_ROWS, ii * N_KEYS + (jp + 1) * GATE_ROWS)
                p_ref[rs, ls] = gates[ii][jp] * act_ref[rs, ls].astype(BF16)


def _peer_dense_kernel(ht_ref, u_ref, vt_ref, first_a_ref, first_b_ref, second_in_ref,
                       x_ref, g2_ref, o_ref, acc_scr, act0, act1, p0, p1, second_ref, *, tt, groups):
    k = pl.program_id(1)

    @pl.when(k == 0)
    def _():
        acc_scr[...] = jnp.zeros_like(acc_scr)
        act1[...] = jnp.zeros_like(act1)
        p0[...] = jnp.zeros_like(p0)
        second_ref[...] = second_in_ref[...]

    tok_half = tt // 2
    tiles_half = tok_half // LANE
    for half in range(2):
        ts = slice(half * tok_half, (half + 1) * tok_half)
        lane_tiles = range(half * tiles_half, (half + 1) * tiles_half)
        act0[:, ts] = jax.nn.gelu(_dot(_unpack_rows(u_ref[0:DENSE_SUB // 2, :]), _unpack_rows(ht_ref[:, ts])))
        _gate_times_act(first_a_ref, DENSE_FIRST_KEYS, second_ref, act1, p1, lane_tiles)
        acc_scr[:, ts] += _dot(_unpack_rows(vt_ref[:, 0:DENSE_SUB]), p0[:, ts])
    for half in range(2):
        ts = slice(half * tok_half, (half + 1) * tok_half)
        lane_tiles = range(half * tiles_half, (half + 1) * tiles_half)
        act1[:, ts] = jax.nn.gelu(_dot(_unpack_rows(u_ref[DENSE_SUB // 2:, :]), _unpack_rows(ht_ref[:, ts])))
        _gate_times_act(first_b_ref, 0, second_ref, act0, p0, lane_tiles)
        acc_scr[:, ts] += _dot(_unpack_rows(vt_ref[:, DENSE_SUB:]), p1[:, ts])

    @pl.when(k == pl.num_programs(1) - 1)
    def _():
        o_ref[...] = acc_scr[...].T

        def body(g, carry):
            rows = pl.ds(pl.multiple_of(g * TOKENS_PER_GROUP, TOKENS_PER_GROUP), TOKENS_PER_GROUP)
            o_ref[rows, :] = x_ref[rows, :] + g2_ref[pl.ds(g, 1), :] * o_ref[rows, :]
            return carry

        lax.fori_loop(0, groups, body, 0, unroll=GROUP_UNROLL)


def _peer_dense(ht, u_bf, vt_bf, first, second, x, mod, *, layer, tt):
    n_tok = x.shape[0]
    groups = tt // TOKENS_PER_GROUP
    n_hs = 2 * PEER_HEADS
    ne = 2 * DENSE_SUB
    n_i = ne // N_KEYS
    n_blocks = N_EXPERTS // ne
    tok_const = lambda shape, imap: pl.BlockSpec(shape, imap, pipeline_mode=pl.Buffered(1))
    by_first_key = lambda t: t.reshape(n_hs, n_blocks, n_i, n_tok)
    cur = lambda k: jnp.minimum(k, n_blocks - 1)
    prev = lambda k: jnp.maximum(k - 1, 0)
    key_spec = lambda blk: pl.BlockSpec((n_hs, 1, n_i, tt), lambda i, k: (0, blk(k), 0, i))
    return pl.pallas_call(
        functools.partial(_peer_dense_kernel, tt=tt, groups=groups),
        grid=(n_tok // tt, n_blocks + 1),
        in_specs=[tok_const((D_MODEL // 2, tt), lambda i, k: (0, i)),
                  pl.BlockSpec((None, ne // 2, D_MODEL), lambda i, k: (layer, cur(k), 0)),
                  pl.BlockSpec((None, D_MODEL // 2, ne), lambda i, k: (layer, 0, prev(k))),
                  key_spec(prev), key_spec(cur),
                  tok_const((n_hs, N_KEYS, tt), lambda i, k: (0, 0, i)),
                  tok_const((tt, D_MODEL), lambda i, k: (i, 0)),
                  _mod_spec(groups, layer, MOD_G2)],
        out_specs=pl.BlockSpec((tt, D_MODEL), lambda i, k: (i, 0)),
        out_shape=jax.ShapeDtypeStruct((n_tok, D_MODEL), F32),
        scratch_shapes=[pltpu.VMEM((D_MODEL, tt), F32),
                        pltpu.VMEM((DENSE_SUB, tt), F32), pltpu.VMEM((DENSE_SUB, tt), F32),
                        pltpu.VMEM((DENSE_SUB, tt), BF16), pltpu.VMEM((DENSE_SUB, tt), BF16),
                        pltpu.VMEM((n_hs, N_KEYS, tt), BF16)],
        compiler_params=_cparams(("parallel", "arbitrary")),
        name="peer_dense",
    )(ht, u_bf, vt_bf, by_first_key(first), by_first_key(first), second, x, mod)


def _final_norm_kernel(x_ref, gain_ref, o_ref):
    o_ref[...] = _rms(x_ref[...]) * gain_ref[...]


def _final_norm(x, gain, *, tt, row0, n_rows):
    return pl.pallas_call(
        _final_norm_kernel,
        grid=(n_rows // tt,),
        in_specs=[pl.BlockSpec((tt, D_MODEL), lambda i: (row0 // tt + i, 0)),
                  pl.BlockSpec((1, D_MODEL), lambda i: (0, 0))],
        out_specs=pl.BlockSpec((tt, D_MODEL), lambda i: (i, 0)),
        out_shape=jax.ShapeDtypeStruct((n_rows, D_MODEL), F32),
        compiler_params=_cparams(("parallel",)),
        name="final_norm",
    )(x, gain)


PROMPT_ROWS = 256
STEP_ROWS = 128
S5_PROMPT_STEPS = 128


def _block_diag_in(bbar):
    rows_group = np.arange(S5_WIDTH) // S5_CH
    cols_group = np.arange(S5_STATE) // S5_P
    mask = jnp.asarray(rows_group[:, None] == cols_group[None, :])
    reps = (1,) * (bbar.ndim - 2) + (S5_GROUPS, 1)
    return jnp.where(mask, jnp.tile(bbar, reps), 0.0)


def _block_diag_out(c):
    rows_group = np.arange(S5_STATE) // S5_P
    cols_group = np.arange(S5_WIDTH) // S5_CH
    mask = jnp.asarray(rows_group[:, None] == cols_group[None, :])
    per_state = c.transpose(0, 1, 3, 2).reshape(c.shape[0], S5_STATE, S5_CH)
    return jnp.where(mask, jnp.tile(per_state, (1, 1, S5_GROUPS)), 0.0)


def kernel(x_prompt, x_sample, state_gla, state_mlstm_c, state_mlstm_n, state_mlstm_m, state_s5_re, state_s5_im, c_prompt, c_sample, w_ada, b_ada, norm_gain, w_in, gla_w_gate_up, gla_b_gate, gla_norm_gain, ml_b_igate, ml_b_fgate, ml_norm_gain, s5_lambda_re, s5_lambda_im, s5_log_dt, s5_b_re, s5_b_im, s5_c_re, s5_c_im, s5_d, s5_w_glu, s5_b_glu, s5_norm_gain, w_out, peer_w_q, peer_sub_keys, peer_u, peer_v, final_gain):
    bp, lp, _ = x_prompt.shape
    bs, ls, _ = x_sample.shape
    assert ls == TOKENS_PER_GROUP and lp % PROMPT_ROWS == 0 and (bs * ls) % STEP_ROWS == 0
    n_prompt = bp * lp
    n_sample = bs * ls
    n_tok = n_prompt + n_sample
    rep = lp // TOKENS_PER_GROUP

    x = jnp.concatenate([x_prompt.reshape(n_prompt, D_MODEL), x_sample.reshape(n_sample, D_MODEL)], axis=0)

    c_all = jnp.concatenate([c_prompt, c_sample], axis=0)
    n_c = c_all.shape[0]
    c_pad = (-n_c) % SUBLANE
    mod = _adaln(jnp.pad(c_all, ((0, c_pad), (0, 0))), w_ada, b_ada)
    mod = jnp.concatenate([jnp.repeat(mod[:, :bp], rep, axis=1), mod[:, bp:n_c]], axis=1)
    gains = norm_gain.reshape(DEPTH * 2, 1, D_MODEL)

    zeros = lambda *shape: jnp.zeros(shape, F32)
    seq_blocks = lp // PROMPT_ROWS
    step_blocks = n_sample // STEP_ROWS
    step_block0 = n_prompt // STEP_ROWS
    prompt_pad = (-bp) % SUBLANE

    z_src = jnp.asarray(np.maximum(_Z_SRC, 0))
    z_valid = jnp.asarray(_Z_SRC >= 0)
    w_in_p = _pack_rows(jnp.where(z_valid, jnp.take(w_in.astype(BF16), z_src, axis=2), jnp.zeros((), BF16)))
    wg_all = jnp.zeros((DEPTH, GLA_HEADS, LANE, LANE), F32).at[:, :, :GLA_RANK, :GLA_DK].set(
        gla_w_gate_up.reshape(DEPTH, GLA_RANK, GLA_HEADS, GLA_DK).transpose(0, 2, 1, 3)).astype(BF16)
    bg_all = jnp.zeros((DEPTH, GLA_HEADS, 1, LANE), F32).at[:, :, 0, :GLA_DK].set(
        gla_b_gate.reshape(DEPTH, GLA_HEADS, GLA_DK))
    gla_gain_all = gla_norm_gain.reshape(DEPTH, GLA_HEADS, 1, GLA_DV)
    bif_all = jnp.zeros((DEPTH, 1, LANE), F32).at[:, 0, :ML_HEADS].set(ml_b_igate).at[
        :, 0, F_GATE_LANE:F_GATE_LANE + ML_HEADS].set(ml_b_fgate)
    ml_gain_all = ml_norm_gain.reshape(DEPTH, ML_HEADS, 1, ML_DH)
    a_disc, bbar = _s5_discretise(s5_lambda_re, s5_lambda_im, s5_log_dt, s5_b_re, s5_b_im)
    wb_all = jnp.concatenate([_block_diag_in(bbar[:, 0]), _block_diag_in(bbar[:, 1])], axis=2).astype(BF16)
    wc_all = jnp.concatenate([_block_diag_out(s5_c_re), -_block_diag_out(s5_c_im)], axis=1).astype(BF16)
    wglu_all = s5_w_glu.astype(BF16)
    w_out_bf = _pack_rows(w_out.astype(BF16))
    wqt_all = _pack_rows(peer_w_q.astype(BF16).transpose(0, 2, 1))
    keys_all = peer_sub_keys.reshape(DEPTH, 2 * PEER_HEADS, N_KEYS, LANE).astype(BF16)
    u_bf = _pack_rows(peer_u.astype(BF16))
    vt_bf = _pack_rows(peer_v.astype(BF16).transpose(0, 2, 1))

    st_gla = state_gla.astype(F32)
    st_c = state_mlstm_c.astype(F32)
    st_n = state_mlstm_n.astype(F32)[:, :, :, None, :]
    st_m = jnp.broadcast_to(state_mlstm_m.astype(F32)[:, :, :, None, None], state_mlstm_m.shape + (1, LANE))
    st_s5 = jnp.concatenate([state_s5_re.reshape(DEPTH, bs, S5_STATE), state_s5_im.reshape(DEPTH, bs, S5_STATE)],
                            axis=2).astype(F32)

    sg_s = mc_s = mn_s = mm_s = None
    new_states = []
    for l in range(DEPTH):
        z = _norm_proj(x, mod, gains, w_in_p, layer=l, tt=1024, tn=1152)

        seq_kw = dict(n_seq=bp, seq_blocks=seq_blocks, rows=PROMPT_ROWS, row_block0=0, carry=True)
        step_kw = dict(n_seq=step_blocks, seq_blocks=1, rows=STEP_ROWS, chunk=ls, row_block0=step_block0,
                       carry=False, layer=l)

        gla_w = (wg_all[l], bg_all[l], gla_gain_all[l])
        og, sg_p = _gla(z, *gla_w, zeros(bp, GLA_HEADS, GLA_DK, GLA_DV), chunk=GLA_CHUNK, **seq_kw)
        og, sg_s = _gla(z, *gla_w, st_gla, fill=(og,) if l == 0 else (og, sg_s), **step_kw)

        ml_w = (bif_all[l], ml_gain_all[l])
        om, mc_p, mn_p, mm_p = _mlstm(z, *ml_w, zeros(bp, ML_HEADS, ML_DH, ML_DH), zeros(bp, ML_HEADS, 1, ML_DH),
                                      zeros(bp, ML_HEADS, 1, LANE), chunk=ML_CHUNK, **seq_kw)
        om, mc_s, mn_s, mm_s = _mlstm(z, *ml_w, st_c, st_n, st_m,
                                      fill=(om,) if l == 0 else (om, mc_s, mn_s, mm_s), **step_kw)

        su = z[:, ZB_SU * LANE:ZB_SU * LANE + S5_WIDTH]
        u_p = jnp.pad(su[:n_prompt].reshape(bp, lp, S5_WIDTH).transpose(1, 0, 2), ((0, 0), (0, prompt_pad), (0, 0)))
        u_s = su[n_prompt:].reshape(bs, ls, S5_WIDTH).transpose(1, 0, 2)
        s5_args = (a_disc[l], wb_all[l], wc_all[l], s5_d[l].reshape(1, S5_WIDTH), wglu_all[l],
                   s5_b_glu[l][None, :], s5_norm_gain[l][None, :])
        os_p, hs_p = _s5(u_p, zeros(bp + prompt_pad, 2 * S5_STATE), *s5_args,
                         steps=S5_PROMPT_STEPS, lane_width=S5_STATE)
        os_s, hs_s = _s5(u_s, st_s5[l], *s5_args, steps=ls, lane_width=LANE)
        os_ = jnp.concatenate([os_p[:, :bp].transpose(1, 0, 2).reshape(n_prompt, S5_WIDTH),
                               os_s.transpose(1, 0, 2).reshape(n_sample, S5_WIDTH)], axis=0)

        x, ht = _out_proj(og, om, os_, x, mod, gains, w_out_bf, layer=l, tt=256)

        first, second = _peer_route(ht, wqt_all[l], keys_all[l], tt=256)
        x = _peer_dense(ht, u_bf, vt_bf, first, second, x, mod, layer=l, tt=512)

        new_states.append((sg_p, mc_p, mn_p[:, :, 0, :], mm_p[:, :, 0, 0], hs_p[:bp], hs_s))

    y_prompt = _final_norm(x, final_gain[None, :], tt=512, row0=0, n_rows=n_prompt)
    y_sample = _final_norm(x, final_gain[None, :], tt=512, row0=n_prompt, n_rows=n_sample)
    stack = lambda i: jnp.stack([ns[i] for ns in new_states])
    s5_p, s5_s = stack(4), stack(5)
    split_s5 = lambda h, lo: h[:, :, lo:lo + S5_STATE].reshape(DEPTH, -1, S5_GROUPS, S5_P)
    outs = (y_prompt.reshape(bp, lp, D_MODEL), y_sample.reshape(bs, ls, D_MODEL),
            stack(0), sg_s, stack(1), mc_s, stack(2), mn_s[:, :, :, 0, :], stack(3), mm_s[:, :, :, 0, 0],
            split_s5(s5_p, 0), split_s5(s5_s, 0), split_s5(s5_p, S5_STATE), split_s5(s5_s, S5_STATE))
    refs = (x_prompt, x_sample, state_gla, state_gla, state_mlstm_c, state_mlstm_c, state_mlstm_n, state_mlstm_n,
            state_mlstm_m, state_mlstm_m, state_s5_re, state_s5_re, state_s5_im, state_s5_im)
    return tuple(o.astype(r.dtype) for o, r in zip(outs, refs))
```

```python
import functools
import math

import numpy as np
import jax
import jax.numpy as jnp
from jax import lax
from jax.experimental import pallas as pl
from jax.experimental.pallas import tpu as pltpu

F32 = jnp.float32
BF16 = jnp.bfloat16
HIGHEST = lax.Precision.HIGHEST

D_MODEL = 2048
DEPTH = 4
GLA_HEADS = 6
GLA_DK = 64
GLA_DV = 128
GLA_RANK = 16
GLA_TAU = 16.0
GLA_CHUNK = 16
ML_HEADS = 6
ML_DH = 128
ML_CHUNK = 64
S5_WIDTH = 512
S5_CH = 16
S5_GROUPS = 32
S5_P = 64
S5_STATE = S5_GROUPS * S5_P
S5_TILES = S5_WIDTH // 128
S5_TILE_STATE = S5_STATE // S5_TILES
PEER_HEADS = 8
N_KEYS = 128
N_EXPERTS = N_KEYS * N_KEYS
PEER_TOPK = 16
NORM_EPS = 1e-6
GLA_WIDTH = GLA_HEADS * GLA_DV
ML_WIDTH = ML_HEADS * ML_DH

LANE = 128
SUBLANE = 8
TOKENS_PER_GROUP = 8
GROUP_UNROLL = 4
VMEM_LIMIT = 56 * 1024 * 1024

ZB_GQ, ZB_GK, ZB_GV, ZB_GG = 0, 6, 12, 18
ZB_MQ, ZB_MK, ZB_MV, ZB_MO = 24, 30, 36, 42
ZB_GR, ZB_MIF = 48, 49
HEADS_PER_STEP = 3
ZB_SU = 50
Z_BLOCKS = 54
Z_COLS = Z_BLOCKS * LANE
F_GATE_LANE = 8


def _z_source_columns():
    src = np.full((Z_COLS,), -1, np.int32)
    off_gq, off_gk, off_gv, off_gg, off_gr = 0, 384, 768, 1536, 2304
    off_mq, off_mk, off_mv, off_mo, off_mi, off_mf, off_su = 2320, 3088, 3856, 4624, 5392, 5398, 5404
    for h in range(GLA_HEADS):
        for d in range(GLA_DK):
            src[(ZB_GQ + h) * LANE + d] = off_gq + h * GLA_DK + d
            src[(ZB_GK + h) * LANE + d] = off_gk + h * GLA_DK + d
        for d in range(GLA_DV):
            src[(ZB_GV + h) * LANE + d] = off_gv + h * GLA_DV + d
            src[(ZB_GG + h) * LANE + d] = off_gg + h * GLA_DV + d
    for d in range(GLA_RANK):
        src[ZB_GR * LANE + d] = off_gr + d
    for h in range(ML_HEADS):
        for d in range(ML_DH):
            src[(ZB_MQ + h) * LANE + d] = off_mq + h * ML_DH + d
            src[(ZB_MK + h) * LANE + d] = off_mk + h * ML_DH + d
            src[(ZB_MV + h) * LANE + d] = off_mv + h * ML_DH + d
            src[(ZB_MO + h) * LANE + d] = off_mo + h * ML_DH + d
        src[ZB_MIF * LANE + h] = off_mi + h
        src[ZB_MIF * LANE + F_GATE_LANE + h] = off_mf + h
    for d in range(S5_WIDTH):
        src[ZB_SU * LANE + d] = off_su + d
    return src


_Z_SRC = _z_source_columns()


def _cparams(semantics):
    return pltpu.CompilerParams(dimension_semantics=semantics, vmem_limit_bytes=VMEM_LIMIT)


def _const_spec(block_shape, index_map):
    return pl.BlockSpec(block_shape, index_map, pipeline_mode=pl.Buffered(1))


def _rms(x):
    return x * lax.rsqrt(jnp.mean(x * x, axis=-1, keepdims=True) + NORM_EPS)


def _unpack_rows(x):
    return pltpu.bitcast(x, BF16)


def _dot(a, b):
    return jnp.dot(a, b, preferred_element_type=F32)


def _dot_nt(a, b):
    return lax.dot_general(a, b, (((1,), (1,)), ((), ())), preferred_element_type=F32)


def _chunk_sums(same, tri, x):
    rows = x.shape[0]
    masks = jnp.concatenate([jnp.where(tri, 1.0, 0.0), jnp.where(same, 1.0, 0.0)], axis=0).astype(BF16)
    hi = x.astype(BF16)
    r1 = x - hi.astype(F32)
    mid = r1.astype(BF16)
    lo = (r1 - mid.astype(F32)).astype(BF16)
    sums = _dot(masks, jnp.concatenate([hi, mid, lo], axis=1))
    total = sums[:, 0:LANE] + sums[:, LANE:2 * LANE] + sums[:, 2 * LANE:3 * LANE]
    return total[0:rows], total[rows:2 * rows]


def _adaln_kernel(c_ref, w_lo_ref, w_hi_ref, b_ref, o_ref):
    c = c_ref[...]
    s = (c * jax.nn.sigmoid(c)).astype(BF16)
    half = D_MODEL // 2
    o_ref[0] = (_dot(s[:, :half], w_lo_ref[0].astype(BF16)) + _dot(s[:, half:], w_hi_ref[0].astype(BF16))
                + b_ref[0])


def _adaln(c_all, w_ada, b_ada):
    n_rows = c_all.shape[0]
    n_out = w_ada.shape[-1]
    tn = 1024
    w_half = lambda which: pl.BlockSpec((1, D_MODEL // 2, tn), lambda l, j: (l, which, j))
    return pl.pallas_call(
        _adaln_kernel,
        grid=(DEPTH, n_out // tn),
        in_specs=[
            pl.BlockSpec((n_rows, D_MODEL), lambda l, j: (0, 0)),
            w_half(0), w_half(1),
            pl.BlockSpec((1, 1, tn), lambda l, j: (l, 0, j)),
        ],
        out_specs=pl.BlockSpec((1, n_rows, tn), lambda l, j: (l, 0, j)),
        out_shape=jax.ShapeDtypeStruct((DEPTH, n_rows, n_out), F32),
        compiler_params=_cparams(("parallel", "parallel")),
        name="adaln",
    )(c_all, w_ada, w_ada, b_ada.reshape(DEPTH, 1, n_out))


def _modulated_group(x8, gain, sc_row, sh_row):
    return _rms(x8) * gain * (1.0 + sc_row) + sh_row


def _norm_proj_kernel(x_ref, sc_ref, sh_ref, gain_ref, w_ref, o_ref, h_scr, *, groups):
    @pl.when(pl.program_id(1) == 0)
    def _():
        gain = gain_ref[...]

        def body(p, carry):
            hs = []
            for u in range(2):
                g = p * 2 + u
                r0 = pl.multiple_of(g * TOKENS_PER_GROUP, TOKENS_PER_GROUP)
                hs.append(_modulated_group(x_ref[pl.ds(r0, TOKENS_PER_GROUP), :], gain,
                                           sc_ref[pl.ds(g, 1), :], sh_ref[pl.ds(g, 1), :]))
            r = pl.multiple_of(p * 2 * TOKENS_PER_GROUP, 2 * TOKENS_PER_GROUP)
            h_scr[pl.ds(r, 2 * TOKENS_PER_GROUP), :] = jnp.concatenate(hs, axis=0).astype(BF16)
            return carry

        lax.fori_loop(0, groups // 2, body, 0, unroll=GROUP_UNROLL)

    o_ref[...] = _dot(h_scr[...], w_ref[...])


MOD_SH1, MOD_SC1, MOD_G1, MOD_SH2, MOD_SC2, MOD_G2 = range(6)


def _mod_spec(groups, layer, kind):
    return pl.BlockSpec((None, groups, D_MODEL), lambda *a: (layer, a[0], kind))


def _gain_spec(layer, which):
    return pl.BlockSpec((None, 1, D_MODEL), lambda *a: (2 * layer + which, 0, 0))


def _norm_proj(x, mod, gains, w, *, layer, tt, tn):
    n_tok = x.shape[0]
    n_out = w.shape[-1]
    groups = tt // TOKENS_PER_GROUP
    return pl.pallas_call(
        functools.partial(_norm_proj_kernel, groups=groups),
        grid=(n_tok // tt, n_out // tn),
        in_specs=[
            pl.BlockSpec((tt, D_MODEL), lambda i, j: (i, 0)),
            _mod_spec(groups, layer, MOD_SC1),
            _mod_spec(groups, layer, MOD_SH1),
            _gain_spec(layer, 0),
            pl.BlockSpec((None, D_MODEL, tn), lambda i, j: (layer, 0, j)),
        ],
        out_specs=pl.BlockSpec((tt, tn), lambda i, j: (i, j)),
        out_shape=jax.ShapeDtypeStruct((n_tok, n_out), F32),
        scratch_shapes=[pltpu.VMEM((tt, D_MODEL), BF16)],
        compiler_params=_cparams(("parallel", "arbitrary")),
        name="norm_proj",
    )(x, mod, mod, gains, w)


def _chunk_masks(rows, chunk):
    shift = int(math.log2(chunk))
    ri = lax.broadcasted_iota(jnp.int32, (rows, rows), 0)
    ci = lax.broadcasted_iota(jnp.int32, (rows, rows), 1)
    same = (ri >> shift) == (ci >> shift)
    tri = jnp.logical_and(same, ci <= ri)
    return same, tri


def _gla_kernel(q_ref, k_ref, v_ref, g_ref, r_ref, wg_ref, bg_ref, gain_ref, s0_ref,
                o_ref, sfin_ref, st_scr, **kw):
    carry = kw["carry"]
    heads = range(HEADS_PER_STEP)
    if carry:
        @pl.when(pl.program_id(2) == 0)
        def _():
            for hh in heads:
                st_scr[hh] = _gla_load_state(s0_ref.at[:, hh:hh + 1], 0)
    for hh in heads:
        ln = slice(hh * LANE, (hh + 1) * LANE)
        one = slice(hh, hh + 1)
        _gla_head(q_ref.at[:, ln], k_ref.at[:, ln], v_ref.at[:, ln], g_ref.at[:, ln], r_ref,
                  wg_ref.at[one], bg_ref.at[one], gain_ref.at[one], s0_ref.at[:, one],
                  o_ref.at[:, ln], sfin_ref.at[:, one], st_scr.at[hh], **kw)
    if carry:
        @pl.when(pl.program_id(2) == pl.num_programs(2) - 1)
        def _():
            for hh in heads:
                sfin_ref[0, hh] = st_scr[hh].T[0:GLA_DK, :]


def _gla_load_state(s0_ref, c):
    zero_pad = jnp.zeros((LANE - GLA_DK, GLA_DV), F32)
    return jnp.concatenate([s0_ref[c, 0], zero_pad], axis=0).T


def _gla_head(q_ref, k_ref, v_ref, g_ref, r_ref, wg_ref, bg_ref, gain_ref, s0_ref,
              o_ref, sfin_ref, st_scr, *, rows, chunk, carry):
    n_chunks = rows // chunk
    shift = int(math.log2(chunk))
    same, tri = _chunk_masks(rows, chunk)

    q = q_ref[...] * (GLA_DK ** -0.5)
    k = k_ref[...]
    v = v_ref[...]
    la = jax.nn.log_sigmoid(_dot(r_ref[...].astype(BF16), wg_ref[0]) + bg_ref[0]) / GLA_TAU
    bl, bt = _chunk_sums(same, tri, la)
    qd = (q * jnp.exp(bl)).astype(BF16)
    ki = (k * jnp.exp(-bl)).astype(BF16)
    kd = (k * jnp.exp(bt - bl)).astype(BF16)
    att = jnp.where(tri, _dot_nt(qd, ki), 0.0)
    o_intra = _dot(att.astype(BF16), v.astype(BF16))

    v_t = v.T
    lane_chunk = lax.broadcasted_iota(jnp.int32, (GLA_DV, rows), 1) >> shift
    if carry:
        s_t = st_scr[...]

    sliced = chunk % (2 * SUBLANE) == 0
    row_chunk = lax.broadcasted_iota(jnp.int32, (rows, GLA_DV), 0) >> shift
    o_inter = [] if sliced else jnp.zeros((rows, GLA_DV), F32)
    for c in range(n_chunks):
        lo = c * chunk
        if not carry:
            s_t = _gla_load_state(s0_ref, c)
        if sliced:
            o_inter.append(_dot_nt(qd[lo:lo + chunk], s_t.astype(BF16)))
        else:
            o_inter = jnp.where(row_chunk == c, _dot_nt(qd, s_t.astype(BF16)), o_inter)
        decay = jnp.exp(bt[lo:lo + 1, :])
        v_c = jnp.where(lane_chunk == c, v_t, 0.0).astype(BF16)
        s_t = s_t * decay + _dot(v_c, kd)
        if not carry:
            sfin_ref[c, 0] = s_t.T[0:GLA_DK, :]

    if carry:
        st_scr[...] = s_t

    o = o_intra + (jnp.concatenate(o_inter, axis=0) if sliced else o_inter)
    g = g_ref[...]
    o_ref[...] = _rms(o) * gain_ref[0] * (g * jax.nn.sigmoid(g))


def _without_refs(kernel_fn, first, count):
    def wrapped(*refs):
        return kernel_fn(*refs[:first], *refs[first + count:])
    return wrapped


def _mixer_grid(n_seq, n_heads, seq_blocks, row_block0, carry):
    if carry:
        return ((n_seq, n_heads, seq_blocks), lambda b, h, t: row_block0 + b * seq_blocks + t,
                ("parallel", "parallel", "arbitrary"))
    return (n_seq, n_heads), lambda b, h: row_block0 + b, ("parallel", "parallel")


def _state_spec(per_block, d2, d3, layer):
    if layer is None:
        return pl.BlockSpec((per_block, HEADS_PER_STEP, d2, d3), lambda *a: (a[0], a[1], 0, 0))
    return pl.BlockSpec((None, per_block, HEADS_PER_STEP, d2, d3), lambda *a: (layer, a[0], a[1], 0, 0))


def _head_cols_spec(rows, rb, block0):
    assert block0 % HEADS_PER_STEP == 0
    return pl.BlockSpec((rows, HEADS_PER_STEP * LANE), lambda *a: (rb(*a), block0 // HEADS_PER_STEP + a[1]))


def _head_param_spec(shape):
    return pl.BlockSpec((HEADS_PER_STEP,) + shape, lambda *a: (a[1],) + (0,) * len(shape))


def _gla(z, wg, bg, gain, s0, *, n_seq, seq_blocks, rows, chunk, row_block0, carry,
         layer=None, fill=()):
    per_block = 1 if carry else rows // chunk
    grid, rb, sem = _mixer_grid(n_seq, GLA_HEADS // HEADS_PER_STEP, seq_blocks, row_block0, carry)
    zspec = functools.partial(_head_cols_spec, rows, rb)
    state_spec = _state_spec(per_block, GLA_DK, GLA_DV, layer)
    n_in = 9
    return pl.pallas_call(
        _without_refs(functools.partial(_gla_kernel, rows=rows, chunk=chunk, carry=carry), n_in, len(fill)),
        grid=grid,
        in_specs=[zspec(ZB_GQ), zspec(ZB_GK), zspec(ZB_GV), zspec(ZB_GG),
                  pl.BlockSpec((rows, LANE), lambda *a: (rb(*a), ZB_GR)),
                  _head_param_spec((LANE, LANE)), _head_param_spec((1, LANE)), _head_param_spec((1, LANE)),
                  state_spec]
                 + [pl.BlockSpec(memory_space=pl.ANY)] * len(fill),
        out_specs=[_head_cols_spec(rows, rb, 0), state_spec],
        out_shape=[jax.ShapeDtypeStruct((z.shape[0], GLA_WIDTH), F32),
                   jax.ShapeDtypeStruct(s0.shape, F32)],
        input_output_aliases={n_in + i: i for i in range(len(fill))},
        scratch_shapes=[pltpu.VMEM((HEADS_PER_STEP, GLA_DV, LANE), F32)],
        compiler_params=_cparams(sem),
        name="gla_seq" if carry else "gla_step",
    )(z, z, z, z, z, wg, bg, gain, s0, *fill)


def _mlstm_kernel(q_ref, k_ref, v_ref, og_ref, gate_ref, bif_ref, gain_ref, c0_ref, n0_ref, m0_ref,
                  h_ref, cfin_ref, nfin_ref, mfin_ref, c_scr, n_scr, m_scr, **kw):
    carry = kw["carry"]
    heads = range(HEADS_PER_STEP)
    if carry:
        @pl.when(pl.program_id(2) == 0)
        def _():
            for hh in heads:
                c_scr[hh] = c0_ref[0, hh]
                n_scr[hh] = n0_ref[0, hh]
                m_scr[hh] = m0_ref[0, hh]
    _mlstm_heads(q_ref, k_ref, v_ref, og_ref, gate_ref, bif_ref, gain_ref, c0_ref, n0_ref, m0_ref,
                 h_ref, cfin_ref, nfin_ref, mfin_ref, c_scr, n_scr, m_scr, **kw)
    if carry:
        @pl.when(pl.program_id(2) == pl.num_programs(2) - 1)
        def _():
            for hh in heads:
                cfin_ref[0, hh] = c_scr[hh]
                nfin_ref[0, hh] = n_scr[hh]
                mfin_ref[0, hh] = m_scr[hh]


def _mlstm_heads(q_ref, k_ref, v_ref, og_ref, gate_ref, bif_ref, gain_ref, c0_ref, n0_ref, m0_ref,
                 h_ref, cfin_ref, nfin_ref, mfin_ref, c_scr, n_scr, m_scr, **kw):
    for hh in range(HEADS_PER_STEP):
        ln = slice(hh * LANE, (hh + 1) * LANE)
        one = slice(hh, hh + 1)
        _mlstm_head(pl.program_id(1) * HEADS_PER_STEP + hh,
                    q_ref.at[:, ln], k_ref.at[:, ln], v_ref.at[:, ln], og_ref.at[:, ln], gate_ref, bif_ref,
                    gain_ref.at[one], c0_ref.at[:, one], n0_ref.at[:, one], m0_ref.at[:, one],
                    h_ref.at[:, ln], cfin_ref.at[:, one], nfin_ref.at[:, one], mfin_ref.at[:, one],
                    c_scr.at[hh], n_scr.at[hh], m_scr.at[hh], **kw)


def _mlstm_head(head, q_ref, k_ref, v_ref, og_ref, gate_ref, bif_ref, gain_ref, c0_ref, n0_ref, m0_ref,
                h_ref, cfin_ref, nfin_ref, mfin_ref, c_scr, n_scr, m_scr, *, rows, chunk, carry):
    n_chunks = rows // chunk
    shift = int(math.log2(chunk))
    same, tri = _chunk_masks(rows, chunk)
    neg_inf = jnp.float32(-jnp.inf)

    x = gate_ref[...] + bif_ref[...]
    log_f = jax.nn.log_sigmoid(x)
    b_cum, b_tot = _chunk_sums(same, tri, log_f)
    lane = lax.broadcasted_iota(jnp.int32, (rows, LANE), 1)
    y = jnp.where(lane < F_GATE_LANE, x, b_cum)
    pick_i = lane == head
    pick_b = lane == head + F_GATE_LANE
    i_col = jnp.sum(jnp.where(pick_i, y, 0.0), axis=-1, keepdims=True)
    b_col = jnp.sum(jnp.where(pick_b, y, 0.0), axis=-1, keepdims=True)
    bt_col = jnp.sum(jnp.where(pick_b, b_tot, 0.0), axis=-1, keepdims=True)
    sub = lax.broadcasted_iota(jnp.int32, (LANE, rows), 0)
    y_t = y.T
    i_row = jnp.sum(jnp.where(sub == head, y_t, 0.0), axis=0, keepdims=True)
    b_row = jnp.sum(jnp.where(sub == head + F_GATE_LANE, y_t, 0.0), axis=0, keepdims=True)

    a_col = bt_col - b_col + i_col
    mloc_col = bt_col + jnp.max(jnp.where(same, i_row - b_row, neg_inf), axis=-1, keepdims=True)

    q = q_ref[...] * (ML_DH ** -0.5)
    k = k_ref[...]
    qb = q.astype(BF16)
    kb = k.astype(BF16)
    vb = v_ref[...].astype(BF16)
    kw = k * jnp.exp(a_col - mloc_col)
    kw_t = kw.T
    lane_chunk = lax.broadcasted_iota(jnp.int32, (ML_DH, rows), 1) >> shift

    def advance(c, c_st, n_st, m_st, m_last):
        hi = (c + 1) * chunk
        decay = jnp.exp(bt_col[hi - 1:hi] + m_st - m_last)
        scale = jnp.exp(mloc_col[hi - 1:hi] - m_last)
        kw_c = jnp.where(lane_chunk == c, kw_t, 0.0).astype(BF16)
        c_new = decay * c_st + scale * _dot(kw_c, vb)
        n_new = decay * n_st + scale * jnp.sum(kw[hi - chunk:hi], axis=0, keepdims=True)
        return c_new, n_new

    if carry:
        c_st = c_scr[...]
        n_st = n_scr[...]
        m_st = m_scr[:, 0:1]
        ri = lax.broadcasted_iota(jnp.int32, (chunk, chunk), 0)
        ci = lax.broadcasted_iota(jnp.int32, (chunk, chunk), 1)
        tri_c = ci <= ri
        h_chunks = []
        for c in range(n_chunks):
            lo = c * chunk
            hi = lo + chunk
            bc = b_col[lo:hi]
            d_log = jnp.where(tri_c, bc - b_row[:, lo:hi] + i_row[:, lo:hi], neg_inf)
            g_inter = bc + m_st
            m_t = jnp.maximum(g_inter, jnp.max(d_log, axis=-1, keepdims=True))
            w_inter = jnp.exp(g_inter - m_t)
            s = _dot_nt(qb[lo:hi], kb[lo:hi]) * jnp.exp(d_log - m_t)
            num = w_inter * _dot(qb[lo:hi], c_st.astype(BF16)) + _dot(s.astype(BF16), vb[lo:hi])
            den = (w_inter * jnp.sum(q[lo:hi] * n_st, axis=-1, keepdims=True)
                   + jnp.sum(s, axis=-1, keepdims=True))
            h_chunks.append(num / jnp.maximum(jnp.abs(den), jnp.exp(-m_t)))
            m_last = m_t[chunk - 1:chunk]
            c_st, n_st = advance(c, c_st, n_st, m_st, m_last)
            m_st = m_last
        h = jnp.concatenate(h_chunks, axis=0)
        c_scr[...] = c_st
        n_scr[...] = n_st
        m_scr[...] = jnp.broadcast_to(m_st, (1, LANE))
    else:
        row_chunk1 = lax.broadcasted_iota(jnp.int32, (rows, 1), 0) >> shift
        row_chunk = lax.broadcasted_iota(jnp.int32, (rows, ML_DH), 0) >> shift
        m_rows = jnp.zeros((rows, 1), F32)
        n_rows = jnp.zeros((rows, ML_DH), F32)
        qc = jnp.zeros((rows, ML_DH), F32)
        for c in range(n_chunks):
            m_rows = jnp.where(row_chunk1 == c, m0_ref[c, 0][:, 0:1], m_rows)
            n_rows = jnp.where(row_chunk == c, n0_ref[c, 0], n_rows)
            qc = jnp.where(row_chunk == c, _dot(qb, c0_ref[c, 0].astype(BF16)), qc)
        d_log = jnp.where(tri, b_col - b_row + i_row, neg_inf)
        g_inter = b_col + m_rows
        m_t = jnp.maximum(g_inter, jnp.max(d_log, axis=-1, keepdims=True))
        w_inter = jnp.exp(g_inter - m_t)
        s = _dot_nt(qb, kb) * jnp.exp(d_log - m_t)
        num = w_inter * qc + _dot(s.astype(BF16), vb)
        den = w_inter * jnp.sum(q * n_rows, axis=-1, keepdims=True) + jnp.sum(s, axis=-1, keepdims=True)
        h = num / jnp.maximum(jnp.abs(den), jnp.exp(-m_t))
        for c in range(n_chunks):
            hi = (c + 1) * chunk
            m_last = m_t[hi - 1:hi]
            c_new, n_new = advance(c, c0_ref[c, 0], n0_ref[c, 0], m0_ref[c, 0][:, 0:1], m_last)
            cfin_ref[c, 0] = c_new
            nfin_ref[c, 0] = n_new
            mfin_ref[c, 0] = jnp.broadcast_to(m_last, (1, LANE))

    h_ref[...] = _rms(h) * gain_ref[0] * jax.nn.sigmoid(og_ref[...])


def _mlstm(z, bif, gain, c0, n0, m0, *, n_seq, seq_blocks, rows, chunk, row_block0, carry,
           layer=None, fill=()):
    per_block = 1 if carry else rows // chunk
    grid, rb, sem = _mixer_grid(n_seq, ML_HEADS // HEADS_PER_STEP, seq_blocks, row_block0, carry)
    zspec = functools.partial(_head_cols_spec, rows, rb)
    state_specs = [_state_spec(per_block, ML_DH, ML_DH, layer), _state_spec(per_block, 1, ML_DH, layer),
                   _state_spec(per_block, 1, LANE, layer)]
    n_in = 10
    return pl.pallas_call(
        _without_refs(functools.partial(_mlstm_kernel, rows=rows, chunk=chunk, carry=carry), n_in, len(fill)),
        grid=grid,
        in_specs=[zspec(ZB_MQ), zspec(ZB_MK), zspec(ZB_MV), zspec(ZB_MO),
                  pl.BlockSpec((rows, LANE), lambda *a: (rb(*a), ZB_MIF)),
                  pl.BlockSpec((1, LANE), lambda *a: (0, 0)),
                  _head_param_spec((1, LANE))]
                 + state_specs + [pl.BlockSpec(memory_space=pl.ANY)] * len(fill),
        out_specs=[_head_cols_spec(rows, rb, 0)] + state_specs,
        out_shape=[jax.ShapeDtypeStruct((z.shape[0], ML_WIDTH), F32),
                   jax.ShapeDtypeStruct(c0.shape, F32), jax.ShapeDtypeStruct(n0.shape, F32),
                   jax.ShapeDtypeStruct(m0.shape, F32)],
        input_output_aliases={n_in + i: i for i in range(len(fill))},
        scratch_shapes=[pltpu.VMEM((HEADS_PER_STEP, ML_DH, ML_DH), F32), pltpu.VMEM((HEADS_PER_STEP, 1, ML_DH), F32),
                        pltpu.VMEM((HEADS_PER_STEP, 1, LANE), F32)],
        compiler_params=_cparams(sem),
        name="mlstm_seq" if carry else "mlstm_step",
    )(z, z, z, z, z, bif, gain, c0, n0, m0, *fill)


def _s5_disc_kernel(lre_ref, lim_ref, ldt_ref, bre_ref, bim_ref, a_ref, bb_ref):
    lam_re = lre_ref[0]
    lam_im = lim_ref[0]
    dt = jnp.exp(ldt_ref[0])
    mag = jnp.exp(lam_re * dt)
    ar = mag * jnp.cos(lam_im * dt)
    ai = mag * jnp.sin(lam_im * dt)
    den = lam_re * lam_re + lam_im * lam_im
    fr = ((ar - 1.0) * lam_re + ai * lam_im) / den
    fi = (ai * lam_re - (ar - 1.0) * lam_im) / den
    b_re = bre_ref[0]
    b_im = bim_ref[0]
    a_ref[0, 0:1, :] = ar
    a_ref[0, 1:2, :] = ai
    bb_ref[0, 0] = fr * b_re - fi * b_im
    bb_ref[0, 1] = fr * b_im + fi * b_re


def _s5_discretise(lam_re, lam_im, log_dt, b_re, b_im):
    flat = lambda t: t.reshape(DEPTH, 1, S5_STATE)
    ldt = jnp.broadcast_to(log_dt[:, :, None], (DEPTH, S5_GROUPS, S5_P))
    chan = lambda t: t.reshape(DEPTH, S5_STATE, S5_CH).transpose(0, 2, 1)
    row = pl.BlockSpec((1, 1, S5_STATE), lambda l: (l, 0, 0))
    mat = pl.BlockSpec((1, S5_CH, S5_STATE), lambda l: (l, 0, 0))
    return pl.pallas_call(
        _s5_disc_kernel,
        grid=(DEPTH,),
        in_specs=[row, row, row, mat, mat],
        out_specs=[pl.BlockSpec((1, 2, S5_STATE), lambda l: (l, 0, 0)),
                   pl.BlockSpec((1, 2, S5_CH, S5_STATE), lambda l: (l, 0, 0, 0))],
        out_shape=[jax.ShapeDtypeStruct((DEPTH, 2, S5_STATE), F32),
                   jax.ShapeDtypeStruct((DEPTH, 2, S5_CH, S5_STATE), F32)],
        compiler_params=_cparams(("parallel",)),
        name="s5_disc",
    )(flat(lam_re), flat(lam_im), flat(ldt), chan(b_re), chan(b_im))


def _s5_kernel(u_ref, h0_ref, a_ref, wb_ref, wc_ref, d_ref, wglu_ref, bglu_ref, gain_ref,
               y_ref, hfin_ref, bu_scr, h_scr, *, steps, batch, lane_width):
    n_rows = steps * batch

    @pl.when(pl.program_id(0) == 0)
    def _():
        h_scr[...] = h0_ref[...]

    u = u_ref[...].reshape(n_rows, S5_WIDTH)
    ub = u.astype(BF16)
    for t in range(S5_TILES):
        bu = _dot(ub[:, t * LANE:(t + 1) * LANE], wb_ref[t])
        re0 = t * S5_TILE_STATE
        bu_scr[:, :, re0:re0 + S5_TILE_STATE] = bu[:, :S5_TILE_STATE].reshape(steps, batch, S5_TILE_STATE)
        bu_scr[:, :, S5_STATE + re0:S5_STATE + re0 + S5_TILE_STATE] = (
            bu[:, S5_TILE_STATE:].reshape(steps, batch, S5_TILE_STATE))

    for j in range(S5_STATE // lane_width):
        re = slice(j * lane_width, (j + 1) * lane_width)
        im = slice(S5_STATE + j * lane_width, S5_STATE + (j + 1) * lane_width)
        ar = a_ref[0:1, re]
        ai = a_ref[1:2, re]

        def step(t, carry):
            hr, hi = carry
            nr = ar * hr - ai * hi + bu_scr[t, :, re]
            ni = ar * hi + ai * hr + bu_scr[t, :, im]
            bu_scr[t, :, re] = nr
            bu_scr[t, :, im] = ni
            return nr, ni

        hr, hi = lax.fori_loop(0, steps, step, (h_scr[:, re], h_scr[:, im]), unroll=min(steps, 8))
        h_scr[:, re] = hr
        h_scr[:, im] = hi

    ys = []
    for t in range(S5_TILES):
        re0 = t * S5_TILE_STATE
        h_re = bu_scr[:, :, re0:re0 + S5_TILE_STATE].reshape(n_rows, S5_TILE_STATE).astype(BF16)
        h_im = bu_scr[:, :, S5_STATE + re0:S5_STATE + re0 + S5_TILE_STATE].reshape(n_rows, S5_TILE_STATE).astype(BF16)
        ys.append(_dot(h_re, wc_ref[t, :S5_TILE_STATE]) + _dot(h_im, wc_ref[t, S5_TILE_STATE:]))
    y = jnp.concatenate(ys, axis=1) + d_ref[...] * u
    y = jax.nn.gelu(y)
    out = y * jax.nn.sigmoid(_dot(y.astype(BF16), wglu_ref[...]) + bglu_ref[...])
    y_ref[...] = (_rms(out) * gain_ref[...]).reshape(steps, batch, S5_WIDTH)

    @pl.when(pl.program_id(0) == pl.num_programs(0) - 1)
    def _():
        hfin_ref[...] = h_scr[...]


def _s5(u_t, h0, a, wb, wc, d, wglu, bglu, gain, *, steps, lane_width):
    seq, batch, _ = u_t.shape
    const = lambda shape: _const_spec(shape, lambda t: (0,) * len(shape))
    return pl.pallas_call(
        functools.partial(_s5_kernel, steps=steps, batch=batch, lane_width=lane_width),
        grid=(seq // steps,),
        in_specs=[pl.BlockSpec((steps, batch, S5_WIDTH), lambda t: (t, 0, 0)),
                  const((batch, 2 * S5_STATE)), const((2, S5_STATE)),
                  const((S5_TILES, LANE, 2 * S5_TILE_STATE)), const((S5_TILES, 2 * S5_TILE_STATE, LANE)),
                  const((1, S5_WIDTH)), const((S5_WIDTH, S5_WIDTH)), const((1, S5_WIDTH)),
                  const((1, S5_WIDTH))],
        out_specs=[pl.BlockSpec((steps, batch, S5_WIDTH), lambda t: (t, 0, 0)),
                   pl.BlockSpec((batch, 2 * S5_STATE), lambda t: (0, 0))],
        out_shape=[jax.ShapeDtypeStruct((seq, batch, S5_WIDTH), F32),
                   jax.ShapeDtypeStruct((batch, 2 * S5_STATE), F32)],
        scratch_shapes=[pltpu.VMEM((steps, batch, 2 * S5_STATE), F32),
                        pltpu.VMEM((batch, 2 * S5_STATE), F32)],
        compiler_params=_cparams(("arbitrary",)),
        name="s5",
    )(u_t, h0, a, wb, wc, d, wglu, bglu, gain)


def _out_proj_kernel(og_ref, om_ref, os_ref, x_ref, g1_ref, sc_ref, sh_ref, gain_ref,
                     wg_ref, wm_ref, ws_ref, xo_ref, ht_ref, mix_scr, h_scr, *, groups):
    mix_scr[...] = (_dot(og_ref[...].astype(BF16), wg_ref[...])
                    + _dot(om_ref[...].astype(BF16), wm_ref[...])
                    + _dot(os_ref[...].astype(BF16), ws_ref[...]))
    gain = gain_ref[...]

    def body(g, carry):
        r0 = pl.multiple_of(g * TOKENS_PER_GROUP, TOKENS_PER_GROUP)
        rows = pl.ds(r0, TOKENS_PER_GROUP)
        xn = x_ref[rows, :] + g1_ref[pl.ds(g, 1), :] * mix_scr[rows, :]
        xo_ref[rows, :] = xn
        h_scr[rows, :] = _modulated_group(xn, gain, sc_ref[pl.ds(g, 1), :], sh_ref[pl.ds(g, 1), :])
        return carry

    lax.fori_loop(0, groups, body, 0, unroll=GROUP_UNROLL)
    ht_ref[...] = pltpu.bitcast(h_scr[...].T.astype(BF16), jnp.uint32)


def _out_proj(og, om, os_, x, mod, gains, w_out, *, layer, tt):
    n_tok = x.shape[0]
    groups = tt // TOKENS_PER_GROUP
    tok = lambda width: pl.BlockSpec((tt, width), lambda i: (i, 0))
    w_rows = lambda height, block: _const_spec((None, height, D_MODEL), lambda i: (layer, block, 0))
    assert GLA_WIDTH == ML_WIDTH and (GLA_WIDTH + ML_WIDTH) % S5_WIDTH == 0
    return pl.pallas_call(
        functools.partial(_out_proj_kernel, groups=groups),
        grid=(n_tok // tt,),
        in_specs=[tok(GLA_WIDTH), tok(ML_WIDTH), tok(S5_WIDTH), tok(D_MODEL),
                  _mod_spec(groups, layer, MOD_G1), _mod_spec(groups, layer, MOD_SC2),
                  _mod_spec(groups, layer, MOD_SH2), _gain_spec(layer, 1),
                  w_rows(GLA_WIDTH, 0), w_rows(ML_WIDTH, 1),
                  w_rows(S5_WIDTH, (GLA_WIDTH + ML_WIDTH) // S5_WIDTH)],
        out_specs=[tok(D_MODEL), pl.BlockSpec((D_MODEL // 2, tt), lambda i: (0, i))],
        out_shape=[jax.ShapeDtypeStruct((n_tok, D_MODEL), F32),
                   jax.ShapeDtypeStruct((D_MODEL // 2, n_tok), jnp.uint32)],
        scratch_shapes=[pltpu.VMEM((tt, D_MODEL), F32), pltpu.VMEM((tt, D_MODEL), F32)],
        compiler_params=_cparams(("parallel",)),
        name="out_proj",
    )(og, om, os_, x, mod, mod, mod, gains, w_out, w_out, w_out)


def _top_values(cur, count, out_scr=None, with_rank=False):
    neg_inf = jnp.float32(-jnp.inf)
    vals = []
    rank = jnp.full(cur.shape, float(count), F32) if with_rank else None
    for r in range(count):
        m = jnp.max(cur, axis=0, keepdims=True)
        vals.append(m)
        if out_scr is not None:
            out_scr[r:r + 1, :] = m
        hit = cur == m
        if with_rank:
            rank = jnp.where(hit, float(r), rank)
        if r + 1 < count:
            cur = jnp.where(hit, neg_inf, cur)
    return (vals, rank) if with_rank else vals


def _peer_route_kernel(ht_ref, wqt_ref, keys_ref, first_ref, second_ref, qt_scr, v1_scr, v2_scr, cand_scr,
                       *, tt):
    qt_scr[...] = _dot(wqt_ref[...], _unpack_rows(ht_ref[...]))
    for h in range(PEER_HEADS):
        sc = []
        for side in range(2):
            hs = 2 * h + side
            qb = qt_scr[hs * LANE:(hs + 1) * LANE, :].astype(BF16)
            sc.append(_dot(keys_ref[hs], qb))
        for lt in range(tt // LANE):
            ls = slice(lt * LANE, (lt + 1) * LANE)
            s1 = sc[0][:, ls]
            s2 = sc[1][:, ls]
            _top_values(s1, PEER_TOPK, v1_scr)
            _, rank2 = _top_values(s2, PEER_TOPK, v2_scr, with_rank=True)
            v2_all = v2_scr[...]
            v2_head = v2_scr[0:SUBLANE, :]
            cand_scr[0:PEER_TOPK, :] = v1_scr[0:1, :] + v2_all
            for a in range(1, PEER_TOPK):
                lo = PEER_TOPK + (a - 1) * SUBLANE
                cand_scr[lo:lo + SUBLANE, :] = v1_scr[a:a + 1, :] + v2_head
            top = _top_values(cand_scr[...], PEER_TOPK)
            z = jnp.zeros_like(top[0])
            for tv in top:
                z = z + jnp.exp(tv - top[0])
            tau = top[PEER_TOPK - 1]
            n1 = jnp.zeros_like(s1)
            for b in range(SUBLANE):
                n1 = n1 + jnp.where(s1 + v2_scr[b:b + 1, :] >= tau, 1.0, 0.0)
            best = v1_scr[0:1, :]
            tail = jnp.zeros_like(best)
            for b in range(SUBLANE, PEER_TOPK):
                tail = tail + jnp.where(best + v2_scr[b:b + 1, :] >= tau, 1.0, 0.0)
            n1 = n1 + jnp.where(s1 == best, tail, 0.0)
            first_ref[2 * h, :, ls] = n1
            first_ref[2 * h + 1, :, ls] = jnp.exp(s1 - v1_scr[0:1, :]) / z
            second_ref[2 * h, :, ls] = rank2.astype(BF16)
            second_ref[2 * h + 1, :, ls] = jnp.exp(s2 - v2_scr[0:1, :]).astype(BF16)


def _peer_route(ht, wqt, keys, *, tt):
    n_tok = ht.shape[1]
    n_hs = 2 * PEER_HEADS
    n_cand = PEER_TOPK + (PEER_TOPK - 1) * SUBLANE
    return pl.pallas_call(
        functools.partial(_peer_route_kernel, tt=tt),
        grid=(n_tok // tt,),
        in_specs=[pl.BlockSpec((D_MODEL // 2, tt), lambda i: (0, i)),
                  _const_spec((n_hs * LANE, D_MODEL), lambda i: (0, 0)),
                  _const_spec((n_hs, N_KEYS, LANE), lambda i: (0, 0, 0))],
        out_specs=[pl.BlockSpec((n_hs, N_KEYS, tt), lambda i: (0, 0, i)),
                   pl.BlockSpec((n_hs, N_KEYS, tt), lambda i: (0, 0, i))],
        out_shape=[jax.ShapeDtypeStruct((n_hs, N_KEYS, n_tok), F32),
                   jax.ShapeDtypeStruct((n_hs, N_KEYS, n_tok), BF16)],
        scratch_shapes=[pltpu.VMEM((n_hs * LANE, tt), F32), pltpu.VMEM((PEER_TOPK, LANE), F32),
                        pltpu.VMEM((PEER_TOPK, LANE), F32), pltpu.VMEM((n_cand, LANE), F32)],
        compiler_params=_cparams(("parallel",)),
        name="peer_route",
    )(ht, wqt, keys)


DENSE_FIRST_KEYS = 4
DENSE_SUB = DENSE_FIRST_KEYS * N_KEYS
GATE_ROWS = 4 * SUBLANE
GATE_PIECES = 4


def _gate_times_act(first_ref, row0, second_ref, act_ref, p_ref, lane_tiles):
    n_pieces = N_KEYS // GATE_ROWS
    zero = jnp.zeros((GATE_ROWS, LANE), BF16)
    for lt in lane_tiles:
        ls = slice(lt * LANE, (lt + 1) * LANE)
        for jp0 in range(0, n_pieces, GATE_PIECES):
            pieces = range(jp0, jp0 + GATE_PIECES)
            gates = [[zero for _ in pieces] for _ in range(DENSE_FIRST_KEYS)]
            for h in range(PEER_HEADS):
                counts, weights = [], []
                for ii in range(DENSE_FIRST_KEYS):
                    r = row0 + ii
                    counts.append(jnp.broadcast_to(first_ref[2 * h, 0, r:r + 1, ls], (GATE_ROWS, LANE)).astype(BF16))
                    weights.append(jnp.broadcast_to(first_ref[2 * h + 1, 0, r:r + 1, ls], (GATE_ROWS, LANE)).astype(BF16))
                for q, jp in enumerate(pieces):
                    js = slice(jp * GATE_ROWS, (jp + 1) * GATE_ROWS)
                    rank2 = second_ref[2 * h, js, ls]
                    e2 = second_ref[2 * h + 1, js, ls]
                    for ii in range(DENSE_FIRST_KEYS):
                        gates[ii][q] = gates[ii][q] + jnp.where(rank2 < counts[ii], e2, zero) * weights[ii]
            for ii in range(DENSE_FIRST_KEYS):
                for q, jp in enumerate(pieces):
                    rs = slice(ii * N_KEYS + jp * GATE_ROWS, ii * N_KEYS + (jp + 1) * GATE_ROWS)
                    p_ref[rs, ls] = gates[ii][q] * act_ref[rs, ls].astype(BF16)


def _peer_dense_kernel(ht_ref, u_ref, vt_ref, first_a_ref, first_b_ref, second_in_ref,
                       x_ref, g2_ref, o_ref, acc_scr, act0, act1, p0, p1, second_ref, *, tt, groups):
    k = pl.program_id(1)

    @pl.when(k == 0)
    def _():
        acc_scr[...] = jnp.zeros_like(acc_scr)
        act1[...] = jnp.zeros_like(act1)
        p0[...] = jnp.zeros_like(p0)
        second_ref[...] = second_in_ref[...]

    tok_half = tt // 2
    tiles_half = tok_half // LANE
    for half in range(2):
        ts = slice(half * tok_half, (half + 1) * tok_half)
        lane_tiles = range(half * tiles_half, (half + 1) * tiles_half)
        act0[:, ts] = jax.nn.gelu(_dot(u_ref[0:DENSE_SUB, :], _unpack_rows(ht_ref[:, ts])))
        _gate_times_act(first_a_ref, DENSE_FIRST_KEYS, second_ref, act1, p1, lane_tiles)
        acc_scr[:, ts] += _dot(vt_ref[:, 0:DENSE_SUB], p0[:, ts])
    for half in range(2):
        ts = slice(half * tok_half, (half + 1) * tok_half)
        lane_tiles = range(half * tiles_half, (half + 1) * tiles_half)
        act1[:, ts] = jax.nn.gelu(_dot(u_ref[DENSE_SUB:, :], _unpack_rows(ht_ref[:, ts])))
        _gate_times_act(first_b_ref, 0, second_ref, act0, p0, lane_tiles)
        acc_scr[:, ts] += _dot(vt_ref[:, DENSE_SUB:], p1[:, ts])

    @pl.when(k == pl.num_programs(1) - 1)
    def _():
        o_ref[...] = acc_scr[...].T

        def body(g, carry):
            rows = pl.ds(pl.multiple_of(g * TOKENS_PER_GROUP, TOKENS_PER_GROUP), TOKENS_PER_GROUP)
            o_ref[rows, :] = x_ref[rows, :] + g2_ref[pl.ds(g, 1), :] * o_ref[rows, :]
            return carry

        lax.fori_loop(0, groups, body, 0, unroll=GROUP_UNROLL)


def _peer_dense(ht, u_bf, vt_bf, first, second, x, mod, *, layer, tt):
    n_tok = x.shape[0]
    groups = tt // TOKENS_PER_GROUP
    n_hs = 2 * PEER_HEADS
    ne = 2 * DENSE_SUB
    n_i = ne // N_KEYS
    n_blocks = N_EXPERTS // ne
    tok_const = lambda shape, imap: pl.BlockSpec(shape, imap, pipeline_mode=pl.Buffered(1))
    by_first_key = lambda t: t.reshape(n_hs, n_blocks, n_i, n_tok)
    cur = lambda k: jnp.minimum(k, n_blocks - 1)
    prev = lambda k: jnp.maximum(k - 1, 0)
    key_spec = lambda blk: pl.BlockSpec((n_hs, 1, n_i, tt), lambda i, k: (0, blk(k), 0, i))
    return pl.pallas_call(
        functools.partial(_peer_dense_kernel, tt=tt, groups=groups),
        grid=(n_tok // tt, n_blocks + 1),
        in_specs=[tok_const((D_MODEL // 2, tt), lambda i, k: (0, i)),
                  pl.BlockSpec((None, ne, D_MODEL), lambda i, k: (layer, cur(k), 0)),
                  pl.BlockSpec((None, D_MODEL, ne), lambda i, k: (layer, 0, prev(k))),
                  key_spec(prev), key_spec(cur),
                  tok_const((n_hs, N_KEYS, tt), lambda i, k: (0, 0, i)),
                  tok_const((tt, D_MODEL), lambda i, k: (i, 0)),
                  _mod_spec(groups, layer, MOD_G2)],
        out_specs=pl.BlockSpec((tt, D_MODEL), lambda i, k: (i, 0)),
        out_shape=jax.ShapeDtypeStruct((n_tok, D_MODEL), F32),
        scratch_shapes=[pltpu.VMEM((D_MODEL, tt), F32),
                        pltpu.VMEM((DENSE_SUB, tt), F32), pltpu.VMEM((DENSE_SUB, tt), F32),
                        pltpu.VMEM((DENSE_SUB, tt), BF16), pltpu.VMEM((DENSE_SUB, tt), BF16),
                        pltpu.VMEM((n_hs, N_KEYS, tt), BF16)],
        compiler_params=_cparams(("parallel", "arbitrary")),
        name="peer_dense",
    )(ht, u_bf, vt_bf, by_first_key(first), by_first_key(first), second, x, mod)


def _final_norm_kernel(x_ref, gain_ref, o_ref):
    o_ref[...] = _rms(x_ref[...]) * gain_ref[...]


def _final_norm(x, gain, *, tt, row0, n_rows):
    return pl.pallas_call(
        _final_norm_kernel,
        grid=(n_rows // tt,),
        in_specs=[pl.BlockSpec((tt, D_MODEL), lambda i: (row0 // tt + i, 0)),
                  pl.BlockSpec((1, D_MODEL), lambda i: (0, 0))],
        out_specs=pl.BlockSpec((tt, D_MODEL), lambda i: (i, 0)),
        out_shape=jax.ShapeDtypeStruct((n_rows, D_MODEL), F32),
        compiler_params=_cparams(("parallel",)),
        name="final_norm",
    )(x, gain)


PROMPT_ROWS = 256
STEP_ROWS = 128
S5_PROMPT_STEPS = 128


def _block_diag_in(bbar):
    rows_group = np.arange(S5_WIDTH) // S5_CH
    cols_group = np.arange(S5_STATE) // S5_P
    mask = jnp.asarray(rows_group[:, None] == cols_group[None, :])
    reps = (1,) * (bbar.ndim - 2) + (S5_GROUPS, 1)
    return jnp.where(mask, jnp.tile(bbar, reps), 0.0)


def _diag_tiles(w, rows, cols):
    return jnp.stack([w[:, t * rows:(t + 1) * rows, t * cols:(t + 1) * cols]
                      for t in range(w.shape[1] // rows)], axis=1)


def _block_diag_out(c):
    rows_group = np.arange(S5_STATE) // S5_P
    cols_group = np.arange(S5_WIDTH) // S5_CH
    mask = jnp.asarray(rows_group[:, None] == cols_group[None, :])
    per_state = c.transpose(0, 1, 3, 2).reshape(c.shape[0], S5_STATE, S5_CH)
    return jnp.where(mask, jnp.tile(per_state, (1, 1, S5_GROUPS)), 0.0)


def kernel(x_prompt, x_sample, state_gla, state_mlstm_c, state_mlstm_n, state_mlstm_m, state_s5_re, state_s5_im, c_prompt, c_sample, w_ada, b_ada, norm_gain, w_in, gla_w_gate_up, gla_b_gate, gla_norm_gain, ml_b_igate, ml_b_fgate, ml_norm_gain, s5_lambda_re, s5_lambda_im, s5_log_dt, s5_b_re, s5_b_im, s5_c_re, s5_c_im, s5_d, s5_w_glu, s5_b_glu, s5_norm_gain, w_out, peer_w_q, peer_sub_keys, peer_u, peer_v, final_gain):
    bp, lp, _ = x_prompt.shape
    bs, ls, _ = x_sample.shape
    assert ls == TOKENS_PER_GROUP and lp % PROMPT_ROWS == 0 and (bs * ls) % STEP_ROWS == 0
    n_prompt = bp * lp
    n_sample = bs * ls
    n_tok = n_prompt + n_sample
    rep = lp // TOKENS_PER_GROUP

    x = jnp.concatenate([x_prompt.reshape(n_prompt, D_MODEL), x_sample.reshape(n_sample, D_MODEL)], axis=0)

    c_all = jnp.concatenate([c_prompt, c_sample], axis=0)
    n_c = c_all.shape[0]
    c_pad = (-n_c) % SUBLANE
    mod = _adaln(jnp.pad(c_all, ((0, c_pad), (0, 0))), w_ada, b_ada)
    mod = jnp.concatenate([jnp.repeat(mod[:, :bp], rep, axis=1), mod[:, bp:n_c]], axis=1)
    gains = norm_gain.reshape(DEPTH * 2, 1, D_MODEL)

    zeros = lambda *shape: jnp.zeros(shape, F32)
    seq_blocks = lp // PROMPT_ROWS
    step_blocks = n_sample // STEP_ROWS
    step_block0 = n_prompt // STEP_ROWS
    prompt_pad = (-bp) % SUBLANE

    z_src = jnp.asarray(np.maximum(_Z_SRC, 0))
    z_valid = jnp.asarray(_Z_SRC >= 0)
    w_in_p = jnp.where(z_valid, jnp.take(w_in.astype(BF16), z_src, axis=2), jnp.zeros((), BF16))
    wg_all = jnp.zeros((DEPTH, GLA_HEADS, LANE, LANE), F32).at[:, :, :GLA_RANK, :GLA_DK].set(
        gla_w_gate_up.reshape(DEPTH, GLA_RANK, GLA_HEADS, GLA_DK).transpose(0, 2, 1, 3)).astype(BF16)
    bg_all = jnp.zeros((DEPTH, GLA_HEADS, 1, LANE), F32).at[:, :, 0, :GLA_DK].set(
        gla_b_gate.reshape(DEPTH, GLA_HEADS, GLA_DK))
    gla_gain_all = gla_norm_gain.reshape(DEPTH, GLA_HEADS, 1, GLA_DV)
    bif_all = jnp.zeros((DEPTH, 1, LANE), F32).at[:, 0, :ML_HEADS].set(ml_b_igate).at[
        :, 0, F_GATE_LANE:F_GATE_LANE + ML_HEADS].set(ml_b_fgate)
    ml_gain_all = ml_norm_gain.reshape(DEPTH, ML_HEADS, 1, ML_DH)
    a_disc, bbar = _s5_discretise(s5_lambda_re, s5_lambda_im, s5_log_dt, s5_b_re, s5_b_im)
    wb_all = jnp.concatenate([_diag_tiles(_block_diag_in(bbar[:, 0]), LANE, S5_TILE_STATE),
                              _diag_tiles(_block_diag_in(bbar[:, 1]), LANE, S5_TILE_STATE)], axis=3).astype(BF16)
    wc_all = jnp.concatenate([_diag_tiles(_block_diag_out(s5_c_re), S5_TILE_STATE, LANE),
                              -_diag_tiles(_block_diag_out(s5_c_im), S5_TILE_STATE, LANE)], axis=2).astype(BF16)
    wglu_all = s5_w_glu.astype(BF16)
    w_out_bf = w_out.astype(BF16)
    wqt_all = peer_w_q.astype(BF16).transpose(0, 2, 1)
    keys_all = peer_sub_keys.reshape(DEPTH, 2 * PEER_HEADS, N_KEYS, LANE).astype(BF16)
    u_bf = peer_u.astype(BF16)
    vt_bf = peer_v.astype(BF16).transpose(0, 2, 1)

    st_gla = state_gla.astype(F32)
    st_c = state_mlstm_c.astype(F32)
    st_n = state_mlstm_n.astype(F32)[:, :, :, None, :]
    st_m = jnp.broadcast_to(state_mlstm_m.astype(F32)[:, :, :, None, None], state_mlstm_m.shape + (1, LANE))
    st_s5 = jnp.concatenate([state_s5_re.reshape(DEPTH, bs, S5_STATE), state_s5_im.reshape(DEPTH, bs, S5_STATE)],
                            axis=2).astype(F32)

    sg_s = mc_s = mn_s = mm_s = None
    new_states = []
    for l in range(DEPTH):
        z = _norm_proj(x, mod, gains, w_in_p, layer=l, tt=1024, tn=1152)

        seq_kw = dict(n_seq=bp, seq_blocks=seq_blocks, rows=PROMPT_ROWS, row_block0=0, carry=True)
        step_kw = dict(n_seq=step_blocks, seq_blocks=1, rows=STEP_ROWS, chunk=ls, row_block0=step_block0,
                       carry=False, layer=l)

        gla_w = (wg_all[l], bg_all[l], gla_gain_all[l])
        og, sg_p = _gla(z, *gla_w, zeros(bp, GLA_HEADS, GLA_DK, GLA_DV), chunk=GLA_CHUNK, **seq_kw)
        og, sg_s = _gla(z, *gla_w, st_gla, fill=(og,) if l == 0 else (og, sg_s), **step_kw)

        ml_w = (bif_all[l], ml_gain_all[l])
        om, mc_p, mn_p, mm_p = _mlstm(z, *ml_w, zeros(bp, ML_HEADS, ML_DH, ML_DH), zeros(bp, ML_HEADS, 1, ML_DH),
                                      zeros(bp, ML_HEADS, 1, LANE), chunk=ML_CHUNK, **seq_kw)
        om, mc_s, mn_s, mm_s = _mlstm(z, *ml_w, st_c, st_n, st_m,
                                      fill=(om,) if l == 0 else (om, mc_s, mn_s, mm_s), **step_kw)

        su = z[:, ZB_SU * LANE:ZB_SU * LANE + S5_WIDTH]
        u_p = jnp.pad(su[:n_prompt].reshape(bp, lp, S5_WIDTH).transpose(1, 0, 2), ((0, 0), (0, prompt_pad), (0, 0)))
        u_s = su[n_prompt:].reshape(bs, ls, S5_WIDTH).transpose(1, 0, 2)
        s5_args = (a_disc[l], wb_all[l], wc_all[l], s5_d[l].reshape(1, S5_WIDTH), wglu_all[l],
                   s5_b_glu[l][None, :], s5_norm_gain[l][None, :])
        os_p, hs_p = _s5(u_p, zeros(bp + prompt_pad, 2 * S5_STATE), *s5_args,
                         steps=S5_PROMPT_STEPS, lane_width=S5_STATE)
        os_s, hs_s = _s5(u_s, st_s5[l], *s5_args, steps=ls, lane_width=LANE)
        os_ = jnp.concatenate([os_p[:, :bp].transpose(1, 0, 2).reshape(n_prompt, S5_WIDTH),
                               os_s.transpose(1, 0, 2).reshape(n_sample, S5_WIDTH)], axis=0)

        x, ht = _out_proj(og, om, os_, x, mod, gains, w_out_bf, layer=l, tt=256)

        first, second = _peer_route(ht, wqt_all[l], keys_all[l], tt=256)
        x = _peer_dense(ht, u_bf, vt_bf, first, second, x, mod, layer=l, tt=512)

        new_states.append((sg_p, mc_p, mn_p[:, :, 0, :], mm_p[:, :, 0, 0], hs_p[:bp], hs_s))

    y_prompt = _final_norm(x, final_gain[None, :], tt=512, row0=0, n_rows=n_prompt)
    y_sample = _final_norm(x, final_gain[None, :], tt=512, row0=n_prompt, n_rows=n_sample)
    stack = lambda i: jnp.stack([ns[i] for ns in new_states])
    s5_p, s5_s = stack(4), stack(5)
    split_s5 = lambda h, lo: h[:, :, lo:lo + S5_STATE].reshape(DEPTH, -1, S5_GROUPS, S5_P)
    outs = (y_prompt.reshape(bp, lp, D_MODEL), y_sample.reshape(bs, ls, D_MODEL),
            stack(0), sg_s, stack(1), mc_s, stack(2), mn_s[:, :, :, 0, :], stack(3), mm_s[:, :, :, 0, 0],
            split_s5(s5_p, 0), split_s5(s5_s, 0), split_s5(s5_p, S5_STATE), split_s5(s5_s, S5_STATE))
    refs = (x_prompt, x_sample, state_gla, state_gla, state_mlstm_c, state_mlstm_c, state_mlstm_n, state_mlstm_n,
            state_mlstm_m, state_mlstm_m, state_s5_re, state_s5_re, state_s5_im, state_s5_im)
    return tuple(o.astype(r.dtype) for o, r in zip(outs, refs))
```

```python
import functools
import math

import numpy as np
import jax
import jax.numpy as jnp
from jax import lax
from jax.experimental import pallas as pl
from jax.experimental.pallas import tpu as pltpu

F32 = jnp.float32
BF16 = jnp.bfloat16
HIGHEST = lax.Precision.HIGHEST

D_MODEL = 2048
DEPTH = 4
GLA_HEADS = 6
GLA_DK = 64
GLA_DV = 128
GLA_RANK = 16
GLA_TAU = 16.0
GLA_CHUNK = 16
ML_HEADS = 6
ML_DH = 128
ML_CHUNK = 64
S5_WIDTH = 512
S5_CH = 16
S5_GROUPS = 32
S5_P = 64
S5_STATE = S5_GROUPS * S5_P
S5_TILES = S5_WIDTH // 128
S5_TILE_STATE = S5_STATE // S5_TILES
PEER_HEADS = 8
N_KEYS = 128
N_EXPERTS = N_KEYS * N_KEYS
PEER_TOPK = 16
NORM_EPS = 1e-6
GLA_WIDTH = GLA_HEADS * GLA_DV
ML_WIDTH = ML_HEADS * ML_DH

LANE = 128
SUBLANE = 8
TOKENS_PER_GROUP = 8
GROUP_UNROLL = 4
VMEM_LIMIT = 56 * 1024 * 1024

ZB_GQ, ZB_GK, ZB_GV, ZB_GG = 0, 6, 12, 18
ZB_MQ, ZB_MK, ZB_MV, ZB_MO = 24, 30, 36, 42
ZB_GR, ZB_MIF = 48, 49
HEADS_PER_STEP = 3
ZB_SU = 50
Z_BLOCKS = 54
Z_COLS = Z_BLOCKS * LANE
F_GATE_LANE = 8


def _z_source_columns():
    src = np.full((Z_COLS,), -1, np.int32)
    off_gq, off_gk, off_gv, off_gg, off_gr = 0, 384, 768, 1536, 2304
    off_mq, off_mk, off_mv, off_mo, off_mi, off_mf, off_su = 2320, 3088, 3856, 4624, 5392, 5398, 5404
    for h in range(GLA_HEADS):
        for d in range(GLA_DK):
            src[(ZB_GQ + h) * LANE + d] = off_gq + h * GLA_DK + d
            src[(ZB_GK + h) * LANE + d] = off_gk + h * GLA_DK + d
        for d in range(GLA_DV):
            src[(ZB_GV + h) * LANE + d] = off_gv + h * GLA_DV + d
            src[(ZB_GG + h) * LANE + d] = off_gg + h * GLA_DV + d
    for d in range(GLA_RANK):
        src[ZB_GR * LANE + d] = off_gr + d
    for h in range(ML_HEADS):
        for d in range(ML_DH):
            src[(ZB_MQ + h) * LANE + d] = off_mq + h * ML_DH + d
            src[(ZB_MK + h) * LANE + d] = off_mk + h * ML_DH + d
            src[(ZB_MV + h) * LANE + d] = off_mv + h * ML_DH + d
            src[(ZB_MO + h) * LANE + d] = off_mo + h * ML_DH + d
        src[ZB_MIF * LANE + h] = off_mi + h
        src[ZB_MIF * LANE + F_GATE_LANE + h] = off_mf + h
    for d in range(S5_WIDTH):
        src[ZB_SU * LANE + d] = off_su + d
    return src


_Z_SRC = _z_source_columns()


def _z_runs():
    runs, i = [], 0
    while i < Z_COLS:
        j = i + 1
        if _Z_SRC[i] < 0:
            while j < Z_COLS and _Z_SRC[j] < 0:
                j += 1
            runs.append((-1, j - i))
        else:
            while j < Z_COLS and _Z_SRC[j] == _Z_SRC[j - 1] + 1:
                j += 1
            runs.append((int(_Z_SRC[i]), j - i))
        i = j
    return runs


_Z_RUNS = _z_runs()


def _pad_z_columns(w):
    parts = [jnp.zeros(w.shape[:-1] + (n,), w.dtype) if s < 0 else w[..., s:s + n] for s, n in _Z_RUNS]
    return jnp.concatenate(parts, axis=-1)


def _cparams(semantics):
    return pltpu.CompilerParams(dimension_semantics=semantics, vmem_limit_bytes=VMEM_LIMIT)


def _const_spec(block_shape, index_map):
    return pl.BlockSpec(block_shape, index_map, pipeline_mode=pl.Buffered(1))


def _rms(x):
    return x * lax.rsqrt(jnp.mean(x * x, axis=-1, keepdims=True) + NORM_EPS)


def _unpack_rows(x):
    return pltpu.bitcast(x, BF16)


def _dot(a, b):
    return jnp.dot(a, b, preferred_element_type=F32)


def _dot_tn(a, b):
    return lax.dot_general(a, b, (((0,), (0,)), ((), ())), preferred_element_type=F32)


def _dot_nt(a, b):
    return lax.dot_general(a, b, (((1,), (1,)), ((), ())), preferred_element_type=F32)


def _chunk_sums(same, tri, x):
    rows = x.shape[0]
    masks = jnp.concatenate([jnp.where(tri, 1.0, 0.0), jnp.where(same, 1.0, 0.0)], axis=0).astype(BF16)
    hi = x.astype(BF16)
    r1 = x - hi.astype(F32)
    mid = r1.astype(BF16)
    lo = (r1 - mid.astype(F32)).astype(BF16)
    sums = _dot(masks, jnp.concatenate([hi, mid, lo], axis=1))
    total = sums[:, 0:LANE] + sums[:, LANE:2 * LANE] + sums[:, 2 * LANE:3 * LANE]
    return total[0:rows], total[rows:2 * rows]


def _adaln_kernel(c_ref, w_lo_ref, w_hi_ref, b_ref, o_ref):
    c = c_ref[...]
    s = (c * jax.nn.sigmoid(c)).astype(BF16)
    half = D_MODEL // 2
    o_ref[0] = (_dot(s[:, :half], w_lo_ref[0].astype(BF16)) + _dot(s[:, half:], w_hi_ref[0].astype(BF16))
                + b_ref[0])


def _adaln(c_all, w_ada, b_ada):
    n_rows = c_all.shape[0]
    n_out = w_ada.shape[-1]
    tn = 1024
    w_half = lambda which: pl.BlockSpec((1, D_MODEL // 2, tn), lambda l, j: (l, which, j))
    return pl.pallas_call(
        _adaln_kernel,
        grid=(DEPTH, n_out // tn),
        in_specs=[
            pl.BlockSpec((n_rows, D_MODEL), lambda l, j: (0, 0)),
            w_half(0), w_half(1),
            pl.BlockSpec((1, 1, tn), lambda l, j: (l, 0, j)),
        ],
        out_specs=pl.BlockSpec((1, n_rows, tn), lambda l, j: (l, 0, j)),
        out_shape=jax.ShapeDtypeStruct((DEPTH, n_rows, n_out), F32),
        compiler_params=_cparams(("parallel", "parallel")),
        name="adaln",
    )(c_all, w_ada, w_ada, b_ada.reshape(DEPTH, 1, n_out))


def _modulated_group(x8, gain, sc_row, sh_row):
    return _rms(x8) * gain * (1.0 + sc_row) + sh_row


def _norm_proj_kernel(x_ref, sc_ref, sh_ref, gain_ref, w_ref, o_ref, h_scr, *, groups):
    @pl.when(pl.program_id(1) == 0)
    def _():
        gain = gain_ref[...]

        def body(p, carry):
            hs = []
            for u in range(2):
                g = p * 2 + u
                r0 = pl.multiple_of(g * TOKENS_PER_GROUP, TOKENS_PER_GROUP)
                hs.append(_modulated_group(x_ref[pl.ds(r0, TOKENS_PER_GROUP), :], gain,
                                           sc_ref[pl.ds(g, 1), :], sh_ref[pl.ds(g, 1), :]))
            r = pl.multiple_of(p * 2 * TOKENS_PER_GROUP, 2 * TOKENS_PER_GROUP)
            h_scr[pl.ds(r, 2 * TOKENS_PER_GROUP), :] = jnp.concatenate(hs, axis=0).astype(BF16)
            return carry

        lax.fori_loop(0, groups // 2, body, 0, unroll=GROUP_UNROLL)

    o_ref[...] = _dot(h_scr[...], w_ref[...])


MOD_SH1, MOD_SC1, MOD_G1, MOD_SH2, MOD_SC2, MOD_G2 = range(6)


def _mod_spec(groups, layer, kind):
    return pl.BlockSpec((None, groups, D_MODEL), lambda *a: (layer, a[0], kind))


def _gain_spec(layer, which):
    return pl.BlockSpec((None, 1, D_MODEL), lambda *a: (2 * layer + which, 0, 0))


def _norm_proj(x, mod, gains, w, *, layer, tt, tn):
    n_tok = x.shape[0]
    n_out = w.shape[-1]
    groups = tt // TOKENS_PER_GROUP
    return pl.pallas_call(
        functools.partial(_norm_proj_kernel, groups=groups),
        grid=(n_tok // tt, n_out // tn),
        in_specs=[
            pl.BlockSpec((tt, D_MODEL), lambda i, j: (i, 0)),
            _mod_spec(groups, layer, MOD_SC1),
            _mod_spec(groups, layer, MOD_SH1),
            _gain_spec(layer, 0),
            pl.BlockSpec((None, D_MODEL, tn), lambda i, j: (layer, 0, j)),
        ],
        out_specs=pl.BlockSpec((tt, tn), lambda i, j: (i, j)),
        out_shape=jax.ShapeDtypeStruct((n_tok, n_out), F32),
        scratch_shapes=[pltpu.VMEM((tt, D_MODEL), BF16)],
        compiler_params=_cparams(("parallel", "arbitrary")),
        name="norm_proj",
    )(x, mod, mod, gains, w)


def _chunk_masks(rows, chunk):
    shift = int(math.log2(chunk))
    ri = lax.broadcasted_iota(jnp.int32, (rows, rows), 0)
    ci = lax.broadcasted_iota(jnp.int32, (rows, rows), 1)
    same = (ri >> shift) == (ci >> shift)
    tri = jnp.logical_and(same, ci <= ri)
    return same, tri


def _gla_kernel(q_ref, k_ref, v_ref, g_ref, r_ref, wg_ref, bg_ref, gain_ref, s0_ref,
                o_ref, sfin_ref, st_scr, **kw):
    carry = kw["carry"]
    heads = range(HEADS_PER_STEP)
    if carry:
        @pl.when(pl.program_id(2) == 0)
        def _():
            for hh in heads:
                st_scr[hh] = _gla_load_state(s0_ref.at[:, hh:hh + 1], 0)
    for hh in heads:
        ln = slice(hh * LANE, (hh + 1) * LANE)
        one = slice(hh, hh + 1)
        _gla_head(q_ref.at[:, ln], k_ref.at[:, ln], v_ref.at[:, ln], g_ref.at[:, ln], r_ref,
                  wg_ref.at[one], bg_ref.at[one], gain_ref.at[one], s0_ref.at[:, one],
                  o_ref.at[:, ln], sfin_ref.at[:, one], st_scr.at[hh], **kw)
    if carry:
        @pl.when(pl.program_id(2) == pl.num_programs(2) - 1)
        def _():
            for hh in heads:
                sfin_ref[0, hh] = st_scr[hh].T[0:GLA_DK, :]


def _gla_load_state(s0_ref, c):
    zero_pad = jnp.zeros((LANE - GLA_DK, GLA_DV), F32)
    return jnp.concatenate([s0_ref[c, 0], zero_pad], axis=0).T


def _gla_head(q_ref, k_ref, v_ref, g_ref, r_ref, wg_ref, bg_ref, gain_ref, s0_ref,
              o_ref, sfin_ref, st_scr, *, rows, chunk, carry):
    n_chunks = rows // chunk
    shift = int(math.log2(chunk))
    same, tri = _chunk_masks(rows, chunk)

    q = q_ref[...] * (GLA_DK ** -0.5)
    k = k_ref[...]
    v = v_ref[...]
    la = jax.nn.log_sigmoid(_dot(r_ref[...].astype(BF16), wg_ref[0]) + bg_ref[0]) / GLA_TAU
    bl, bt = _chunk_sums(same, tri, la)
    qd = (q * jnp.exp(bl)).astype(BF16)
    ki = (k * jnp.exp(-bl)).astype(BF16)
    kd = (k * jnp.exp(bt - bl)).astype(BF16)
    att = jnp.where(tri, _dot_nt(qd, ki), 0.0)
    o_intra = _dot(att.astype(BF16), v.astype(BF16))

    v_t = v.T
    lane_chunk = lax.broadcasted_iota(jnp.int32, (GLA_DV, rows), 1) >> shift
    if carry:
        s_t = st_scr[...]

    sliced = chunk % (2 * SUBLANE) == 0
    row_chunk = lax.broadcasted_iota(jnp.int32, (rows, GLA_DV), 0) >> shift
    o_inter = [] if sliced else jnp.zeros((rows, GLA_DV), F32)
    for c in range(n_chunks):
        lo = c * chunk
        if not carry:
            s_t = _gla_load_state(s0_ref, c)
        if sliced:
            o_inter.append(_dot_nt(qd[lo:lo + chunk], s_t.astype(BF16)))
        else:
            o_inter = jnp.where(row_chunk == c, _dot_nt(qd, s_t.astype(BF16)), o_inter)
        decay = jnp.exp(bt[lo:lo + 1, :])
        v_c = jnp.where(lane_chunk == c, v_t, 0.0).astype(BF16)
        s_t = s_t * decay + _dot(v_c, kd)
        if not carry:
            sfin_ref[c, 0] = s_t.T[0:GLA_DK, :]

    if carry:
        st_scr[...] = s_t

    o = o_intra + (jnp.concatenate(o_inter, axis=0) if sliced else o_inter)
    g = g_ref[...]
    o_ref[...] = _rms(o) * gain_ref[0] * (g * jax.nn.sigmoid(g))


def _without_refs(kernel_fn, first, count):
    def wrapped(*refs):
        return kernel_fn(*refs[:first], *refs[first + count:])
    return wrapped


def _mixer_grid(n_seq, n_heads, seq_blocks, row_block0, carry):
    if carry:
        return ((n_seq, n_heads, seq_blocks), lambda b, h, t: row_block0 + b * seq_blocks + t,
                ("parallel", "parallel", "arbitrary"))
    return (n_seq, n_heads), lambda b, h: row_block0 + b, ("parallel", "parallel")


def _state_spec(per_block, d2, d3, layer):
    if layer is None:
        return pl.BlockSpec((per_block, HEADS_PER_STEP, d2, d3), lambda *a: (a[0], a[1], 0, 0))
    return pl.BlockSpec((None, per_block, HEADS_PER_STEP, d2, d3), lambda *a: (layer, a[0], a[1], 0, 0))


def _head_cols_spec(rows, rb, block0):
    assert block0 % HEADS_PER_STEP == 0
    return pl.BlockSpec((rows, HEADS_PER_STEP * LANE), lambda *a: (rb(*a), block0 // HEADS_PER_STEP + a[1]))


def _head_param_spec(shape):
    return pl.BlockSpec((HEADS_PER_STEP,) + shape, lambda *a: (a[1],) + (0,) * len(shape))


def _gla(z, wg, bg, gain, s0, *, n_seq, seq_blocks, rows, chunk, row_block0, carry,
         layer=None, fill=()):
    per_block = 1 if carry else rows // chunk
    grid, rb, sem = _mixer_grid(n_seq, GLA_HEADS // HEADS_PER_STEP, seq_blocks, row_block0, carry)
    zspec = functools.partial(_head_cols_spec, rows, rb)
    state_spec = _state_spec(per_block, GLA_DK, GLA_DV, layer)
    n_in = 9
    return pl.pallas_call(
        _without_refs(functools.partial(_gla_kernel, rows=rows, chunk=chunk, carry=carry), n_in, len(fill)),
        grid=grid,
        in_specs=[zspec(ZB_GQ), zspec(ZB_GK), zspec(ZB_GV), zspec(ZB_GG),
                  pl.BlockSpec((rows, LANE), lambda *a: (rb(*a), ZB_GR)),
                  _head_param_spec((LANE, LANE)), _head_param_spec((1, LANE)), _head_param_spec((1, LANE)),
                  state_spec]
                 + [pl.BlockSpec(memory_space=pl.ANY)] * len(fill),
        out_specs=[_head_cols_spec(rows, rb, 0), state_spec],
        out_shape=[jax.ShapeDtypeStruct((z.shape[0], GLA_WIDTH), F32),
                   jax.ShapeDtypeStruct(s0.shape, F32)],
        input_output_aliases={n_in + i: i for i in range(len(fill))},
        scratch_shapes=[pltpu.VMEM((HEADS_PER_STEP, GLA_DV, LANE), F32)],
        compiler_params=_cparams(sem),
        name="gla_seq" if carry else "gla_step",
    )(z, z, z, z, z, wg, bg, gain, s0, *fill)


def _mlstm_kernel(q_ref, k_ref, v_ref, og_ref, gate_ref, bif_ref, gain_ref, c0_ref, n0_ref, m0_ref,
                  h_ref, cfin_ref, nfin_ref, mfin_ref, c_scr, n_scr, m_scr, **kw):
    carry = kw["carry"]
    heads = range(HEADS_PER_STEP)
    if carry:
        @pl.when(pl.program_id(2) == 0)
        def _():
            for hh in heads:
                c_scr[hh] = c0_ref[0, hh]
                n_scr[hh] = n0_ref[0, hh]
                m_scr[hh] = m0_ref[0, hh]
    _mlstm_heads(q_ref, k_ref, v_ref, og_ref, gate_ref, bif_ref, gain_ref, c0_ref, n0_ref, m0_ref,
                 h_ref, cfin_ref, nfin_ref, mfin_ref, c_scr, n_scr, m_scr, **kw)
    if carry:
        @pl.when(pl.program_id(2) == pl.num_programs(2) - 1)
        def _():
            for hh in heads:
                cfin_ref[0, hh] = c_scr[hh]
                nfin_ref[0, hh] = n_scr[hh]
                mfin_ref[0, hh] = m_scr[hh]


def _mlstm_heads(q_ref, k_ref, v_ref, og_ref, gate_ref, bif_ref, gain_ref, c0_ref, n0_ref, m0_ref,
                 h_ref, cfin_ref, nfin_ref, mfin_ref, c_scr, n_scr, m_scr, **kw):
    for hh in range(HEADS_PER_STEP):
        ln = slice(hh * LANE, (hh + 1) * LANE)
        one = slice(hh, hh + 1)
        _mlstm_head(pl.program_id(1) * HEADS_PER_STEP + hh,
                    q_ref.at[:, ln], k_ref.at[:, ln], v_ref.at[:, ln], og_ref.at[:, ln], gate_ref, bif_ref,
                    gain_ref.at[one], c0_ref.at[:, one], n0_ref.at[:, one], m0_ref.at[:, one],
                    h_ref.at[:, ln], cfin_ref.at[:, one], nfin_ref.at[:, one], mfin_ref.at[:, one],
                    c_scr.at[hh], n_scr.at[hh], m_scr.at[hh], **kw)


def _mlstm_head(head, q_ref, k_ref, v_ref, og_ref, gate_ref, bif_ref, gain_ref, c0_ref, n0_ref, m0_ref,
                h_ref, cfin_ref, nfin_ref, mfin_ref, c_scr, n_scr, m_scr, *, rows, chunk, carry):
    n_chunks = rows // chunk
    shift = int(math.log2(chunk))
    same, tri = _chunk_masks(rows, chunk)
    neg_inf = jnp.float32(-jnp.inf)

    x = gate_ref[...] + bif_ref[...]
    log_f = jax.nn.log_sigmoid(x)
    b_cum, b_tot = _chunk_sums(same, tri, log_f)
    lane = lax.broadcasted_iota(jnp.int32, (rows, LANE), 1)
    y = jnp.where(lane < F_GATE_LANE, x, b_cum)
    pick_i = lane == head
    pick_b = lane == head + F_GATE_LANE
    i_col = jnp.sum(jnp.where(pick_i, y, 0.0), axis=-1, keepdims=True)
    b_col = jnp.sum(jnp.where(pick_b, y, 0.0), axis=-1, keepdims=True)
    bt_col = jnp.sum(jnp.where(pick_b, b_tot, 0.0), axis=-1, keepdims=True)
    sub = lax.broadcasted_iota(jnp.int32, (LANE, rows), 0)
    y_t = y.T
    i_row = jnp.sum(jnp.where(sub == head, y_t, 0.0), axis=0, keepdims=True)
    b_row = jnp.sum(jnp.where(sub == head + F_GATE_LANE, y_t, 0.0), axis=0, keepdims=True)

    a_col = bt_col - b_col + i_col
    mloc_col = bt_col + jnp.max(jnp.where(same, i_row - b_row, neg_inf), axis=-1, keepdims=True)

    q = q_ref[...] * (ML_DH ** -0.5)
    k = k_ref[...]
    qb = q.astype(BF16)
    kb = k.astype(BF16)
    vb = v_ref[...].astype(BF16)
    kw = k * jnp.exp(a_col - mloc_col)
    kw_t = kw.T
    lane_chunk = lax.broadcasted_iota(jnp.int32, (ML_DH, rows), 1) >> shift

    def advance(c, c_st, n_st, m_st, m_last):
        hi = (c + 1) * chunk
        decay = jnp.exp(bt_col[hi - 1:hi] + m_st - m_last)
        scale = jnp.exp(mloc_col[hi - 1:hi] - m_last)
        kw_c = jnp.where(lane_chunk == c, kw_t, 0.0).astype(BF16)
        c_new = decay * c_st + scale * _dot(kw_c, vb)
        n_new = decay * n_st + scale * jnp.sum(kw[hi - chunk:hi], axis=0, keepdims=True)
        return c_new, n_new

    if carry:
        c_st = c_scr[...]
        n_st = n_scr[...]
        m_st = m_scr[:, 0:1]
        ri = lax.broadcasted_iota(jnp.int32, (chunk, chunk), 0)
        ci = lax.broadcasted_iota(jnp.int32, (chunk, chunk), 1)
        tri_c = ci <= ri
        h_chunks = []
        for c in range(n_chunks):
            lo = c * chunk
            hi = lo + chunk
            bc = b_col[lo:hi]
            d_log = jnp.where(tri_c, bc - b_row[:, lo:hi] + i_row[:, lo:hi], neg_inf)
            g_inter = bc + m_st
            m_t = jnp.maximum(g_inter, jnp.max(d_log, axis=-1, keepdims=True))
            w_inter = jnp.exp(g_inter - m_t)
            s = _dot_nt(qb[lo:hi], kb[lo:hi]) * jnp.exp(d_log - m_t)
            num = w_inter * _dot(qb[lo:hi], c_st.astype(BF16)) + _dot(s.astype(BF16), vb[lo:hi])
            den = (w_inter * jnp.sum(q[lo:hi] * n_st, axis=-1, keepdims=True)
                   + jnp.sum(s, axis=-1, keepdims=True))
            h_chunks.append(num / jnp.maximum(jnp.abs(den), jnp.exp(-m_t)))
            m_last = m_t[chunk - 1:chunk]
            c_st, n_st = advance(c, c_st, n_st, m_st, m_last)
            m_st = m_last
        h = jnp.concatenate(h_chunks, axis=0)
        c_scr[...] = c_st
        n_scr[...] = n_st
        m_scr[...] = jnp.broadcast_to(m_st, (1, LANE))
    else:
        row_chunk1 = lax.broadcasted_iota(jnp.int32, (rows, 1), 0) >> shift
        row_chunk = lax.broadcasted_iota(jnp.int32, (rows, ML_DH), 0) >> shift
        m_rows = jnp.zeros((rows, 1), F32)
        n_rows = jnp.zeros((rows, ML_DH), F32)
        qc = jnp.zeros((rows, ML_DH), F32)
        for c in range(n_chunks):
            m_rows = jnp.where(row_chunk1 == c, m0_ref[c, 0][:, 0:1], m_rows)
            n_rows = jnp.where(row_chunk == c, n0_ref[c, 0], n_rows)
            qc = jnp.where(row_chunk == c, _dot(qb, c0_ref[c, 0].astype(BF16)), qc)
        d_log = jnp.where(tri, b_col - b_row + i_row, neg_inf)
        g_inter = b_col + m_rows
        m_t = jnp.maximum(g_inter, jnp.max(d_log, axis=-1, keepdims=True))
        w_inter = jnp.exp(g_inter - m_t)
        s = _dot_nt(qb, kb) * jnp.exp(d_log - m_t)
        num = w_inter * qc + _dot(s.astype(BF16), vb)
        den = w_inter * jnp.sum(q * n_rows, axis=-1, keepdims=True) + jnp.sum(s, axis=-1, keepdims=True)
        h = num / jnp.maximum(jnp.abs(den), jnp.exp(-m_t))
        for c in range(n_chunks):
            hi = (c + 1) * chunk
            m_last = m_t[hi - 1:hi]
            c_new, n_new = advance(c, c0_ref[c, 0], n0_ref[c, 0], m0_ref[c, 0][:, 0:1], m_last)
            cfin_ref[c, 0] = c_new
            nfin_ref[c, 0] = n_new
            mfin_ref[c, 0] = jnp.broadcast_to(m_last, (1, LANE))

    h_ref[...] = _rms(h) * gain_ref[0] * jax.nn.sigmoid(og_ref[...])


def _mlstm(z, bif, gain, c0, n0, m0, *, n_seq, seq_blocks, rows, chunk, row_block0, carry,
           layer=None, fill=()):
    per_block = 1 if carry else rows // chunk
    grid, rb, sem = _mixer_grid(n_seq, ML_HEADS // HEADS_PER_STEP, seq_blocks, row_block0, carry)
    zspec = functools.partial(_head_cols_spec, rows, rb)
    state_specs = [_state_spec(per_block, ML_DH, ML_DH, layer), _state_spec(per_block, 1, ML_DH, layer),
                   _state_spec(per_block, 1, LANE, layer)]
    n_in = 10
    return pl.pallas_call(
        _without_refs(functools.partial(_mlstm_kernel, rows=rows, chunk=chunk, carry=carry), n_in, len(fill)),
        grid=grid,
        in_specs=[zspec(ZB_MQ), zspec(ZB_MK), zspec(ZB_MV), zspec(ZB_MO),
                  pl.BlockSpec((rows, LANE), lambda *a: (rb(*a), ZB_MIF)),
                  pl.BlockSpec((1, LANE), lambda *a: (0, 0)),
                  _head_param_spec((1, LANE))]
                 + state_specs + [pl.BlockSpec(memory_space=pl.ANY)] * len(fill),
        out_specs=[_head_cols_spec(rows, rb, 0)] + state_specs,
        out_shape=[jax.ShapeDtypeStruct((z.shape[0], ML_WIDTH), F32),
                   jax.ShapeDtypeStruct(c0.shape, F32), jax.ShapeDtypeStruct(n0.shape, F32),
                   jax.ShapeDtypeStruct(m0.shape, F32)],
        input_output_aliases={n_in + i: i for i in range(len(fill))},
        scratch_shapes=[pltpu.VMEM((HEADS_PER_STEP, ML_DH, ML_DH), F32), pltpu.VMEM((HEADS_PER_STEP, 1, ML_DH), F32),
                        pltpu.VMEM((HEADS_PER_STEP, 1, LANE), F32)],
        compiler_params=_cparams(sem),
        name="mlstm_seq" if carry else "mlstm_step",
    )(z, z, z, z, z, bif, gain, c0, n0, m0, *fill)


def _s5_disc_kernel(lre_ref, lim_ref, ldt_ref, bre_ref, bim_ref, a_ref, bb_ref):
    lam_re = lre_ref[0]
    lam_im = lim_ref[0]
    dt = jnp.exp(ldt_ref[0])
    mag = jnp.exp(lam_re * dt)
    ar = mag * jnp.cos(lam_im * dt)
    ai = mag * jnp.sin(lam_im * dt)
    den = lam_re * lam_re + lam_im * lam_im
    fr = ((ar - 1.0) * lam_re + ai * lam_im) / den
    fi = (ai * lam_re - (ar - 1.0) * lam_im) / den
    b_re = bre_ref[0]
    b_im = bim_ref[0]
    a_ref[0, 0:1, :] = ar
    a_ref[0, 1:2, :] = ai
    bb_ref[0, 0] = fr * b_re - fi * b_im
    bb_ref[0, 1] = fr * b_im + fi * b_re


def _s5_discretise(lam_re, lam_im, log_dt, b_re, b_im):
    flat = lambda t: t.reshape(DEPTH, 1, S5_STATE)
    ldt = jnp.broadcast_to(log_dt[:, :, None], (DEPTH, S5_GROUPS, S5_P))
    chan = lambda t: t.reshape(DEPTH, S5_STATE, S5_CH).transpose(0, 2, 1)
    row = pl.BlockSpec((1, 1, S5_STATE), lambda l: (l, 0, 0))
    mat = pl.BlockSpec((1, S5_CH, S5_STATE), lambda l: (l, 0, 0))
    return pl.pallas_call(
        _s5_disc_kernel,
        grid=(DEPTH,),
        in_specs=[row, row, row, mat, mat],
        out_specs=[pl.BlockSpec((1, 2, S5_STATE), lambda l: (l, 0, 0)),
                   pl.BlockSpec((1, 2, S5_CH, S5_STATE), lambda l: (l, 0, 0, 0))],
        out_shape=[jax.ShapeDtypeStruct((DEPTH, 2, S5_STATE), F32),
                   jax.ShapeDtypeStruct((DEPTH, 2, S5_CH, S5_STATE), F32)],
        compiler_params=_cparams(("parallel",)),
        name="s5_disc",
    )(flat(lam_re), flat(lam_im), flat(ldt), chan(b_re), chan(b_im))


def _s5_kernel(u_ref, h0_ref, a_ref, wb_ref, wc_ref, d_ref, wglu_ref, bglu_ref, gain_ref,
               y_ref, hfin_ref, bu_scr, h_scr, *, steps, batch, lane_width):
    n_rows = steps * batch

    @pl.when(pl.program_id(0) == 0)
    def _():
        h_scr[...] = h0_ref[...]

    u = u_ref[...].reshape(n_rows, S5_WIDTH)
    ub = u.astype(BF16)
    for t in range(S5_TILES):
        bu = _dot(ub[:, t * LANE:(t + 1) * LANE], wb_ref[t])
        re0 = t * S5_TILE_STATE
        bu_scr[:, :, re0:re0 + S5_TILE_STATE] = bu[:, :S5_TILE_STATE].reshape(steps, batch, S5_TILE_STATE)
        bu_scr[:, :, S5_STATE + re0:S5_STATE + re0 + S5_TILE_STATE] = (
            bu[:, S5_TILE_STATE:].reshape(steps, batch, S5_TILE_STATE))

    for j in range(S5_STATE // lane_width):
        re = slice(j * lane_width, (j + 1) * lane_width)
        im = slice(S5_STATE + j * lane_width, S5_STATE + (j + 1) * lane_width)
        ar = a_ref[0:1, re]
        ai = a_ref[1:2, re]

        def step(t, carry):
            hr, hi = carry
            nr = ar * hr - ai * hi + bu_scr[t, :, re]
            ni = ar * hi + ai * hr + bu_scr[t, :, im]
            bu_scr[t, :, re] = nr
            bu_scr[t, :, im] = ni
            return nr, ni

        hr, hi = lax.fori_loop(0, steps, step, (h_scr[:, re], h_scr[:, im]), unroll=min(steps, 8))
        h_scr[:, re] = hr
        h_scr[:, im] = hi

    ys = []
    for t in range(S5_TILES):
        re0 = t * S5_TILE_STATE
        h_re = bu_scr[:, :, re0:re0 + S5_TILE_STATE].reshape(n_rows, S5_TILE_STATE).astype(BF16)
        h_im = bu_scr[:, :, S5_STATE + re0:S5_STATE + re0 + S5_TILE_STATE].reshape(n_rows, S5_TILE_STATE).astype(BF16)
        ys.append(_dot(h_re, wc_ref[t, :S5_TILE_STATE]) + _dot(h_im, wc_ref[t, S5_TILE_STATE:]))
    y = jnp.concatenate(ys, axis=1) + d_ref[...] * u
    y = jax.nn.gelu(y)
    out = y * jax.nn.sigmoid(_dot(y.astype(BF16), wglu_ref[...]) + bglu_ref[...])
    y_ref[...] = (_rms(out) * gain_ref[...]).reshape(steps, batch, S5_WIDTH)

    @pl.when(pl.program_id(0) == pl.num_programs(0) - 1)
    def _():
        hfin_ref[...] = h_scr[...]


def _s5(u_t, h0, a, wb, wc, d, wglu, bglu, gain, *, steps, lane_width):
    seq, batch, _ = u_t.shape
    const = lambda shape: _const_spec(shape, lambda t: (0,) * len(shape))
    return pl.pallas_call(
        functools.partial(_s5_kernel, steps=steps, batch=batch, lane_width=lane_width),
        grid=(seq // steps,),
        in_specs=[pl.BlockSpec((steps, batch, S5_WIDTH), lambda t: (t, 0, 0)),
                  const((batch, 2 * S5_STATE)), const((2, S5_STATE)),
                  const((S5_TILES, LANE, 2 * S5_TILE_STATE)), const((S5_TILES, 2 * S5_TILE_STATE, LANE)),
                  const((1, S5_WIDTH)), const((S5_WIDTH, S5_WIDTH)), const((1, S5_WIDTH)),
                  const((1, S5_WIDTH))],
        out_specs=[pl.BlockSpec((steps, batch, S5_WIDTH), lambda t: (t, 0, 0)),
                   pl.BlockSpec((batch, 2 * S5_STATE), lambda t: (0, 0))],
        out_shape=[jax.ShapeDtypeStruct((seq, batch, S5_WIDTH), F32),
                   jax.ShapeDtypeStruct((batch, 2 * S5_STATE), F32)],
        scratch_shapes=[pltpu.VMEM((steps, batch, 2 * S5_STATE), F32),
                        pltpu.VMEM((batch, 2 * S5_STATE), F32)],
        compiler_params=_cparams(("arbitrary",)),
        name="s5",
    )(u_t, h0, a, wb, wc, d, wglu, bglu, gain)


def _out_proj_kernel(og_ref, om_ref, os_ref, x_ref, g1_ref, sc_ref, sh_ref, gain_ref,
                     wg_ref, wm_ref, ws_ref, xo_ref, ht_ref, mix_scr, h_scr, *, groups):
    mix_scr[...] = (_dot(og_ref[...].astype(BF16), wg_ref[...])
                    + _dot(om_ref[...].astype(BF16), wm_ref[...])
                    + _dot(os_ref[...].astype(BF16), ws_ref[...]))
    gain = gain_ref[...]

    def body(g, carry):
        r0 = pl.multiple_of(g * TOKENS_PER_GROUP, TOKENS_PER_GROUP)
        rows = pl.ds(r0, TOKENS_PER_GROUP)
        xn = x_ref[rows, :] + g1_ref[pl.ds(g, 1), :] * mix_scr[rows, :]
        xo_ref[rows, :] = xn
        h_scr[rows, :] = _modulated_group(xn, gain, sc_ref[pl.ds(g, 1), :], sh_ref[pl.ds(g, 1), :])
        return carry

    lax.fori_loop(0, groups, body, 0, unroll=GROUP_UNROLL)
    ht_ref[...] = pltpu.bitcast(h_scr[...].T.astype(BF16), jnp.uint32)


def _out_proj(og, om, os_, x, mod, gains, w_out, *, layer, tt):
    n_tok = x.shape[0]
    groups = tt // TOKENS_PER_GROUP
    tok = lambda width: pl.BlockSpec((tt, width), lambda i: (i, 0))
    w_rows = lambda height, block: _const_spec((None, height, D_MODEL), lambda i: (layer, block, 0))
    assert GLA_WIDTH == ML_WIDTH and (GLA_WIDTH + ML_WIDTH) % S5_WIDTH == 0
    return pl.pallas_call(
        functools.partial(_out_proj_kernel, groups=groups),
        grid=(n_tok // tt,),
        in_specs=[tok(GLA_WIDTH), tok(ML_WIDTH), tok(S5_WIDTH), tok(D_MODEL),
                  _mod_spec(groups, layer, MOD_G1), _mod_spec(groups, layer, MOD_SC2),
                  _mod_spec(groups, layer, MOD_SH2), _gain_spec(layer, 1),
                  w_rows(GLA_WIDTH, 0), w_rows(ML_WIDTH, 1),
                  w_rows(S5_WIDTH, (GLA_WIDTH + ML_WIDTH) // S5_WIDTH)],
        out_specs=[tok(D_MODEL), pl.BlockSpec((D_MODEL // 2, tt), lambda i: (0, i))],
        out_shape=[jax.ShapeDtypeStruct((n_tok, D_MODEL), F32),
                   jax.ShapeDtypeStruct((D_MODEL // 2, n_tok), jnp.uint32)],
        scratch_shapes=[pltpu.VMEM((tt, D_MODEL), F32), pltpu.VMEM((tt, D_MODEL), F32)],
        compiler_params=_cparams(("parallel",)),
        name="out_proj",
    )(og, om, os_, x, mod, mod, mod, gains, w_out, w_out, w_out)


def _top_values(cur, count, out_scr=None, with_rank=False):
    neg_inf = jnp.float32(-jnp.inf)
    vals = []
    rank = jnp.full(cur.shape, float(count), F32) if with_rank else None
    for r in range(count):
        m = jnp.max(cur, axis=0, keepdims=True)
        vals.append(m)
        if out_scr is not None:
            out_scr[r:r + 1, :] = m
        hit = cur == m
        if with_rank:
            rank = jnp.where(hit, float(r), rank)
        if r + 1 < count:
            cur = jnp.where(hit, neg_inf, cur)
    return (vals, rank) if with_rank else vals


def _peer_route_kernel(ht_ref, wqt_ref, keys_ref, first_ref, second_ref, qt_scr, v1_scr, v2_scr, cand_scr,
                       *, tt):
    qt_scr[...] = _dot(wqt_ref[...], _unpack_rows(ht_ref[...]))
    for h in range(PEER_HEADS):
        sc = []
        for side in range(2):
            hs = 2 * h + side
            qb = qt_scr[hs * LANE:(hs + 1) * LANE, :].astype(BF16)
            sc.append(_dot(keys_ref[hs], qb))
        for lt in range(tt // LANE):
            ls = slice(lt * LANE, (lt + 1) * LANE)
            s1 = sc[0][:, ls]
            s2 = sc[1][:, ls]
            _top_values(s1, PEER_TOPK, v1_scr)
            _, rank2 = _top_values(s2, PEER_TOPK, v2_scr, with_rank=True)
            v2_all = v2_scr[...]
            v2_head = v2_scr[0:SUBLANE, :]
            cand_scr[0:PEER_TOPK, :] = v1_scr[0:1, :] + v2_all
            for a in range(1, PEER_TOPK):
                lo = PEER_TOPK + (a - 1) * SUBLANE
                cand_scr[lo:lo + SUBLANE, :] = v1_scr[a:a + 1, :] + v2_head
            top = _top_values(cand_scr[...], PEER_TOPK)
            z = jnp.zeros_like(top[0])
            for tv in top:
                z = z + jnp.exp(tv - top[0])
            tau = top[PEER_TOPK - 1]
            n1 = jnp.zeros_like(s1)
            for b in range(SUBLANE):
                n1 = n1 + jnp.where(s1 + v2_scr[b:b + 1, :] >= tau, 1.0, 0.0)
            best = v1_scr[0:1, :]
            tail = jnp.zeros_like(best)
            for b in range(SUBLANE, PEER_TOPK):
                tail = tail + jnp.where(best + v2_scr[b:b + 1, :] >= tau, 1.0, 0.0)
            n1 = n1 + jnp.where(s1 == best, tail, 0.0)
            first_ref[2 * h, :, ls] = n1
            first_ref[2 * h + 1, :, ls] = jnp.exp(s1 - v1_scr[0:1, :]) / z
            second_ref[2 * h, :, ls] = rank2.astype(BF16)
            second_ref[2 * h + 1, :, ls] = jnp.exp(s2 - v2_scr[0:1, :]).astype(BF16)


def _peer_route(ht, wqt, keys, *, tt):
    n_tok = ht.shape[1]
    n_hs = 2 * PEER_HEADS
    n_cand = PEER_TOPK + (PEER_TOPK - 1) * SUBLANE
    return pl.pallas_call(
        functools.partial(_peer_route_kernel, tt=tt),
        grid=(n_tok // tt,),
        in_specs=[pl.BlockSpec((D_MODEL // 2, tt), lambda i: (0, i)),
                  _const_spec((n_hs * LANE, D_MODEL), lambda i: (0, 0)),
                  _const_spec((n_hs, N_KEYS, LANE), lambda i: (0, 0, 0))],
        out_specs=[pl.BlockSpec((n_hs, N_KEYS, tt), lambda i: (0, 0, i)),
                   pl.BlockSpec((n_hs, N_KEYS, tt), lambda i: (0, 0, i))],
        out_shape=[jax.ShapeDtypeStruct((n_hs, N_KEYS, n_tok), F32),
                   jax.ShapeDtypeStruct((n_hs, N_KEYS, n_tok), BF16)],
        scratch_shapes=[pltpu.VMEM((n_hs * LANE, tt), F32), pltpu.VMEM((PEER_TOPK, LANE), F32),
                        pltpu.VMEM((PEER_TOPK, LANE), F32), pltpu.VMEM((n_cand, LANE), F32)],
        compiler_params=_cparams(("parallel",)),
        name="peer_route",
    )(ht, wqt, keys)


DENSE_FIRST_KEYS = 4
DENSE_SUB = DENSE_FIRST_KEYS * N_KEYS
GATE_ROWS = 4 * SUBLANE
GATE_PIECES = 4


def _gate_times_act(first_ref, row0, second_ref, act_ref, p_ref, lane_tiles):
    n_pieces = N_KEYS // GATE_ROWS
    zero = jnp.zeros((GATE_ROWS, LANE), BF16)
    for lt in lane_tiles:
        ls = slice(lt * LANE, (lt + 1) * LANE)
        for jp0 in range(0, n_pieces, GATE_PIECES):
            pieces = range(jp0, jp0 + GATE_PIECES)
            gates = [[zero for _ in pieces] for _ in range(DENSE_FIRST_KEYS)]
            for h in range(PEER_HEADS):
                counts, weights = [], []
                for ii in range(DENSE_FIRST_KEYS):
                    r = row0 + ii
                    counts.append(jnp.broadcast_to(first_ref[2 * h, 0, r:r + 1, ls], (GATE_ROWS, LANE)).astype(BF16))
                    weights.append(jnp.broadcast_to(first_ref[2 * h + 1, 0, r:r + 1, ls], (GATE_ROWS, LANE)).astype(BF16))
                for q, jp in enumerate(pieces):
                    js = slice(jp * GATE_ROWS, (jp + 1) * GATE_ROWS)
                    rank2 = second_ref[2 * h, js, ls]
                    e2 = second_ref[2 * h + 1, js, ls]
                    for ii in range(DENSE_FIRST_KEYS):
                        gates[ii][q] = gates[ii][q] + jnp.where(rank2 < counts[ii], e2, zero) * weights[ii]
            for ii in range(DENSE_FIRST_KEYS):
                for q, jp in enumerate(pieces):
                    rs = slice(ii * N_KEYS + jp * GATE_ROWS, ii * N_KEYS + (jp + 1) * GATE_ROWS)
                    p_ref[rs, ls] = gates[ii][q] * act_ref[rs, ls]


def _peer_dense_kernel(ht_ref, u_ref, v_ref, first_a_ref, first_b_ref, second_in_ref,
                       x_ref, g2_ref, o_ref, acc_scr, act0, act1, p0, p1, second_ref, *, tt, groups):
    k = pl.program_id(1)

    @pl.when(k == 0)
    def _():
        acc_scr[...] = jnp.zeros_like(acc_scr)
        act1[...] = jnp.zeros_like(act1)
        p0[...] = jnp.zeros_like(p0)
        second_ref[...] = second_in_ref[...]

    tok_half = tt // 2
    tiles_half = tok_half // LANE
    for half in range(2):
        ts = slice(half * tok_half, (half + 1) * tok_half)
        lane_tiles = range(half * tiles_half, (half + 1) * tiles_half)
        act0[:, ts] = jax.nn.gelu(_dot(u_ref[0:DENSE_SUB, :], _unpack_rows(ht_ref[:, ts]))).astype(BF16)
        _gate_times_act(first_a_ref, DENSE_FIRST_KEYS, second_ref, act1, p1, lane_tiles)
        acc_scr[:, ts] += _dot_tn(v_ref[0:DENSE_SUB, :], p0[:, ts])
    for half in range(2):
        ts = slice(half * tok_half, (half + 1) * tok_half)
        lane_tiles = range(half * tiles_half, (half + 1) * tiles_half)
        act1[:, ts] = jax.nn.gelu(_dot(u_ref[DENSE_SUB:, :], _unpack_rows(ht_ref[:, ts]))).astype(BF16)
        _gate_times_act(first_b_ref, 0, second_ref, act0, p0, lane_tiles)
        acc_scr[:, ts] += _dot_tn(v_ref[DENSE_SUB:, :], p1[:, ts])

    @pl.when(k == pl.num_programs(1) - 1)
    def _():
        o_ref[...] = acc_scr[...].T

        def body(g, carry):
            rows = pl.ds(pl.multiple_of(g * TOKENS_PER_GROUP, TOKENS_PER_GROUP), TOKENS_PER_GROUP)
            o_ref[rows, :] = x_ref[rows, :] + g2_ref[pl.ds(g, 1), :] * o_ref[rows, :]
            return carry

        lax.fori_loop(0, groups, body, 0, unroll=GROUP_UNROLL)


def _peer_dense(ht, u_bf, v_bf, first, second, x, mod, *, layer, tt):
    n_tok = x.shape[0]
    groups = tt // TOKENS_PER_GROUP
    n_hs = 2 * PEER_HEADS
    ne = 2 * DENSE_SUB
    n_i = ne // N_KEYS
    n_blocks = N_EXPERTS // ne
    tok_const = lambda shape, imap: pl.BlockSpec(shape, imap, pipeline_mode=pl.Buffered(1))
    by_first_key = lambda t: t.reshape(n_hs, n_blocks, n_i, n_tok)
    cur = lambda k: jnp.minimum(k, n_blocks - 1)
    prev = lambda k: jnp.maximum(k - 1, 0)
    key_spec = lambda blk: pl.BlockSpec((n_hs, 1, n_i, tt), lambda i, k: (0, blk(k), 0, i))
    return pl.pallas_call(
        functools.partial(_peer_dense_kernel, tt=tt, groups=groups),
        grid=(n_tok // tt, n_blocks + 1),
        in_specs=[tok_const((D_MODEL // 2, tt), lambda i, k: (0, i)),
                  pl.BlockSpec((None, ne, D_MODEL), lambda i, k: (layer, cur(k), 0)),
                  pl.BlockSpec((None, ne, D_MODEL), lambda i, k: (layer, prev(k), 0)),
                  key_spec(prev), key_spec(cur),
                  tok_const((n_hs, N_KEYS, tt), lambda i, k: (0, 0, i)),
                  tok_const((tt, D_MODEL), lambda i, k: (i, 0)),
                  _mod_spec(groups, layer, MOD_G2)],
        out_specs=pl.BlockSpec((tt, D_MODEL), lambda i, k: (i, 0)),
        out_shape=jax.ShapeDtypeStruct((n_tok, D_MODEL), F32),
        scratch_shapes=[pltpu.VMEM((D_MODEL, tt), F32),
                        pltpu.VMEM((DENSE_SUB, tt), BF16), pltpu.VMEM((DENSE_SUB, tt), BF16),
                        pltpu.VMEM((DENSE_SUB, tt), BF16), pltpu.VMEM((DENSE_SUB, tt), BF16),
                        pltpu.VMEM((n_hs, N_KEYS, tt), BF16)],
        compiler_params=_cparams(("parallel", "arbitrary")),
        name="peer_dense",
    )(ht, u_bf, v_bf, by_first_key(first), by_first_key(first), second, x, mod)


def _final_norm_kernel(x_ref, gain_ref, o_ref):
    o_ref[...] = _rms(x_ref[...]) * gain_ref[...]


def _final_norm(x, gain, *, tt, row0, n_rows):
    return pl.pallas_call(
        _final_norm_kernel,
        grid=(n_rows // tt,),
        in_specs=[pl.BlockSpec((tt, D_MODEL), lambda i: (row0 // tt + i, 0)),
                  pl.BlockSpec((1, D_MODEL), lambda i: (0, 0))],
        out_specs=pl.BlockSpec((tt, D_MODEL), lambda i: (i, 0)),
        out_shape=jax.ShapeDtypeStruct((n_rows, D_MODEL), F32),
        compiler_params=_cparams(("parallel",)),
        name="final_norm",
    )(x, gain)


PROMPT_ROWS = 256
STEP_ROWS = 128
S5_PROMPT_STEPS = 128


def _block_diag_in(bbar):
    rows_group = np.arange(S5_WIDTH) // S5_CH
    cols_group = np.arange(S5_STATE) // S5_P
    mask = jnp.asarray(rows_group[:, None] == cols_group[None, :])
    reps = (1,) * (bbar.ndim - 2) + (S5_GROUPS, 1)
    return jnp.where(mask, jnp.tile(bbar, reps), 0.0)


def _diag_tiles(w, rows, cols):
    return jnp.stack([w[:, t * rows:(t + 1) * rows, t * cols:(t + 1) * cols]
                      for t in range(w.shape[1] // rows)], axis=1)


def _block_diag_out(c):
    rows_group = np.arange(S5_STATE) // S5_P
    cols_group = np.arange(S5_WIDTH) // S5_CH
    mask = jnp.asarray(rows_group[:, None] == cols_group[None, :])
    per_state = c.transpose(0, 1, 3, 2).reshape(c.shape[0], S5_STATE, S5_CH)
    return jnp.where(mask, jnp.tile(per_state, (1, 1, S5_GROUPS)), 0.0)


def kernel(x_prompt, x_sample, state_gla, state_mlstm_c, state_mlstm_n, state_mlstm_m, state_s5_re, state_s5_im, c_prompt, c_sample, w_ada, b_ada, norm_gain, w_in, gla_w_gate_up, gla_b_gate, gla_norm_gain, ml_b_igate, ml_b_fgate, ml_norm_gain, s5_lambda_re, s5_lambda_im, s5_log_dt, s5_b_re, s5_b_im, s5_c_re, s5_c_im, s5_d, s5_w_glu, s5_b_glu, s5_norm_gain, w_out, peer_w_q, peer_sub_keys, peer_u, peer_v, final_gain):
    bp, lp, _ = x_prompt.shape
    bs, ls, _ = x_sample.shape
    assert ls == TOKENS_PER_GROUP and lp % PROMPT_ROWS == 0 and (bs * ls) % STEP_ROWS == 0
    n_prompt = bp * lp
    n_sample = bs * ls
    n_tok = n_prompt + n_sample
    rep = lp // TOKENS_PER_GROUP

    x = jnp.concatenate([x_prompt.reshape(n_prompt, D_MODEL), x_sample.reshape(n_sample, D_MODEL)], axis=0)

    c_all = jnp.concatenate([c_prompt, c_sample], axis=0)
    n_c = c_all.shape[0]
    c_pad = (-n_c) % SUBLANE
    mod = _adaln(jnp.pad(c_all, ((0, c_pad), (0, 0))), w_ada, b_ada)
    mod = jnp.concatenate([jnp.repeat(mod[:, :bp], rep, axis=1), mod[:, bp:n_c]], axis=1)
    gains = norm_gain.reshape(DEPTH * 2, 1, D_MODEL)

    zeros = lambda *shape: jnp.zeros(shape, F32)
    seq_blocks = lp // PROMPT_ROWS
    step_blocks = n_sample // STEP_ROWS
    step_block0 = n_prompt // STEP_ROWS
    prompt_pad = (-bp) % SUBLANE

    w_in_p = _pad_z_columns(w_in.astype(BF16))
    wg_all = jnp.zeros((DEPTH, GLA_HEADS, LANE, LANE), F32).at[:, :, :GLA_RANK, :GLA_DK].set(
        gla_w_gate_up.reshape(DEPTH, GLA_RANK, GLA_HEADS, GLA_DK).transpose(0, 2, 1, 3)).astype(BF16)
    bg_all = jnp.zeros((DEPTH, GLA_HEADS, 1, LANE), F32).at[:, :, 0, :GLA_DK].set(
        gla_b_gate.reshape(DEPTH, GLA_HEADS, GLA_DK))
    gla_gain_all = gla_norm_gain.reshape(DEPTH, GLA_HEADS, 1, GLA_DV)
    bif_all = jnp.zeros((DEPTH, 1, LANE), F32).at[:, 0, :ML_HEADS].set(ml_b_igate).at[
        :, 0, F_GATE_LANE:F_GATE_LANE + ML_HEADS].set(ml_b_fgate)
    ml_gain_all = ml_norm_gain.reshape(DEPTH, ML_HEADS, 1, ML_DH)
    a_disc, bbar = _s5_discretise(s5_lambda_re, s5_lambda_im, s5_log_dt, s5_b_re, s5_b_im)
    wb_all = jnp.concatenate([_diag_tiles(_block_diag_in(bbar[:, 0]), LANE, S5_TILE_STATE),
                              _diag_tiles(_block_diag_in(bbar[:, 1]), LANE, S5_TILE_STATE)], axis=3).astype(BF16)
    wc_all = jnp.concatenate([_diag_tiles(_block_diag_out(s5_c_re), S5_TILE_STATE, LANE),
                              -_diag_tiles(_block_diag_out(s5_c_im), S5_TILE_STATE, LANE)], axis=2).astype(BF16)
    wglu_all = s5_w_glu.astype(BF16)
    w_out_bf = w_out.astype(BF16)
    wqt_all = peer_w_q.astype(BF16).transpose(0, 2, 1)
    keys_all = peer_sub_keys.reshape(DEPTH, 2 * PEER_HEADS, N_KEYS, LANE).astype(BF16)
    u_bf = peer_u.astype(BF16)
    v_bf = peer_v.astype(BF16)

    st_gla = state_gla.astype(F32)
    st_c = state_mlstm_c.astype(F32)
    st_n = state_mlstm_n.astype(F32)[:, :, :, None, :]
    st_m = jnp.broadcast_to(state_mlstm_m.astype(F32)[:, :, :, None, None], state_mlstm_m.shape + (1, LANE))
    st_s5 = jnp.concatenate([state_s5_re.reshape(DEPTH, bs, S5_STATE), state_s5_im.reshape(DEPTH, bs, S5_STATE)],
                            axis=2).astype(F32)

    sg_s = mc_s = mn_s = mm_s = None
    new_states = []
    for l in range(DEPTH):
        z = _norm_proj(x, mod, gains, w_in_p, layer=l, tt=1024, tn=1152)

        seq_kw = dict(n_seq=bp, seq_blocks=seq_blocks, rows=PROMPT_ROWS, row_block0=0, carry=True)
        step_kw = dict(n_seq=step_blocks, seq_blocks=1, rows=STEP_ROWS, chunk=ls, row_block0=step_block0,
                       carry=False, layer=l)

        gla_w = (wg_all[l], bg_all[l], gla_gain_all[l])
        og, sg_p = _gla(z, *gla_w, zeros(bp, GLA_HEADS, GLA_DK, GLA_DV), chunk=GLA_CHUNK, **seq_kw)
        og, sg_s = _gla(z, *gla_w, st_gla, fill=(og,) if l == 0 else (og, sg_s), **step_kw)

        ml_w = (bif_all[l], ml_gain_all[l])
        om, mc_p, mn_p, mm_p = _mlstm(z, *ml_w, zeros(bp, ML_HEADS, ML_DH, ML_DH), zeros(bp, ML_HEADS, 1, ML_DH),
                                      zeros(bp, ML_HEADS, 1, LANE), chunk=ML_CHUNK, **seq_kw)
        om, mc_s, mn_s, mm_s = _mlstm(z, *ml_w, st_c, st_n, st_m,
                                      fill=(om,) if l == 0 else (om, mc_s, mn_s, mm_s), **step_kw)

        su = z[:, ZB_SU * LANE:ZB_SU * LANE + S5_WIDTH]
        u_p = jnp.pad(su[:n_prompt].reshape(bp, lp, S5_WIDTH).transpose(1, 0, 2), ((0, 0), (0, prompt_pad), (0, 0)))
        u_s = su[n_prompt:].reshape(bs, ls, S5_WIDTH).transpose(1, 0, 2)
        s5_args = (a_disc[l], wb_all[l], wc_all[l], s5_d[l].reshape(1, S5_WIDTH), wglu_all[l],
                   s5_b_glu[l][None, :], s5_norm_gain[l][None, :])
        os_p, hs_p = _s5(u_p, zeros(bp + prompt_pad, 2 * S5_STATE), *s5_args,
                         steps=S5_PROMPT_STEPS, lane_width=S5_STATE)
        os_s, hs_s = _s5(u_s, st_s5[l], *s5_args, steps=ls, lane_width=LANE)
        os_ = jnp.concatenate([os_p[:, :bp].transpose(1, 0, 2).reshape(n_prompt, S5_WIDTH),
                               os_s.transpose(1, 0, 2).reshape(n_sample, S5_WIDTH)], axis=0)

        x, ht = _out_proj(og, om, os_, x, mod, gains, w_out_bf, layer=l, tt=256)

        first, second = _peer_route(ht, wqt_all[l], keys_all[l], tt=256)
        x = _peer_dense(ht, u_bf, v_bf, first, second, x, mod, layer=l, tt=512)

        new_states.append((sg_p, mc_p, mn_p[:, :, 0, :], mm_p[:, :, 0, 0], hs_p[:bp], hs_s))

    y_prompt = _final_norm(x, final_gain[None, :], tt=512, row0=0, n_rows=n_prompt)
    y_sample = _final_norm(x, final_gain[None, :], tt=512, row0=n_prompt, n_rows=n_sample)
    stack = lambda i: jnp.stack([ns[i] for ns in new_states])
    s5_p, s5_s = stack(4), stack(5)
    split_s5 = lambda h, lo: h[:, :, lo:lo + S5_STATE].reshape(DEPTH, -1, S5_GROUPS, S5_P)
    outs = (y_prompt.reshape(bp, lp, D_MODEL), y_sample.reshape(bs, ls, D_MODEL),
            stack(0), sg_s, stack(1), mc_s, stack(2), mn_s[:, :, :, 0, :], stack(3), mm_s[:, :, :, 0, 0],
            split_s5(s5_p, 0), split_s5(s5_s, 0), split_s5(s5_p, S5_STATE), split_s5(s5_s, S5_STATE))
    refs = (x_prompt, x_sample, state_gla, state_gla, state_mlstm_c, state_mlstm_c, state_mlstm_n, state_mlstm_n,
            state_mlstm_m, state_mlstm_m, state_s5_re, state_s5_re, state_s5_im, state_s5_im)
    return tuple(o.astype(r.dtype) for o, r in zip(outs, refs))
```

```python
import functools
import math

import numpy as np
import jax
import jax.numpy as jnp
from jax import lax
from jax.experimental import pallas as pl
from jax.experimental.pallas import tpu as pltpu

F32 = jnp.float32
BF16 = jnp.bfloat16
HIGHEST = lax.Precision.HIGHEST

D_MODEL = 2048
DEPTH = 4
GLA_HEADS = 6
GLA_DK = 64
GLA_DV = 128
GLA_RANK = 16
GLA_TAU = 16.0
GLA_CHUNK = 16
ML_HEADS = 6
ML_DH = 128
ML_CHUNK = 64
S5_WIDTH = 512
S5_CH = 16
S5_GROUPS = 32
S5_P = 64
S5_STATE = S5_GROUPS * S5_P
S5_TILES = S5_WIDTH // 128
S5_TILE_STATE = S5_STATE // S5_TILES
PEER_HEADS = 8
N_KEYS = 128
N_EXPERTS = N_KEYS * N_KEYS
PEER_TOPK = 16
NORM_EPS = 1e-6
GLA_WIDTH = GLA_HEADS * GLA_DV
ML_WIDTH = ML_HEADS * ML_DH

LANE = 128
SUBLANE = 8
TOKENS_PER_GROUP = 8
GROUP_UNROLL = 4
VMEM_LIMIT = 56 * 1024 * 1024

ZB_GQ, ZB_GK, ZB_GV, ZB_GG = 0, 6, 12, 18
ZB_MQ, ZB_MK, ZB_MV, ZB_MO = 24, 30, 36, 42
ZB_GR, ZB_MIF = 48, 49
HEADS_PER_STEP = 6
ZB_SU = 50
Z_BLOCKS = 54
Z_COLS = Z_BLOCKS * LANE
F_GATE_LANE = 8


def _z_source_columns():
    src = np.full((Z_COLS,), -1, np.int32)
    off_gq, off_gk, off_gv, off_gg, off_gr = 0, 384, 768, 1536, 2304
    off_mq, off_mk, off_mv, off_mo, off_mi, off_mf, off_su = 2320, 3088, 3856, 4624, 5392, 5398, 5404
    for h in range(GLA_HEADS):
        for d in range(GLA_DK):
            src[(ZB_GQ + h) * LANE + d] = off_gq + h * GLA_DK + d
            src[(ZB_GK + h) * LANE + d] = off_gk + h * GLA_DK + d
        for d in range(GLA_DV):
            src[(ZB_GV + h) * LANE + d] = off_gv + h * GLA_DV + d
            src[(ZB_GG + h) * LANE + d] = off_gg + h * GLA_DV + d
    for d in range(GLA_RANK):
        src[ZB_GR * LANE + d] = off_gr + d
    for h in range(ML_HEADS):
        for d in range(ML_DH):
            src[(ZB_MQ + h) * LANE + d] = off_mq + h * ML_DH + d
            src[(ZB_MK + h) * LANE + d] = off_mk + h * ML_DH + d
            src[(ZB_MV + h) * LANE + d] = off_mv + h * ML_DH + d
            src[(ZB_MO + h) * LANE + d] = off_mo + h * ML_DH + d
        src[ZB_MIF * LANE + h] = off_mi + h
        src[ZB_MIF * LANE + F_GATE_LANE + h] = off_mf + h
    for d in range(S5_WIDTH):
        src[ZB_SU * LANE + d] = off_su + d
    return src


_Z_SRC = _z_source_columns()


def _z_runs():
    runs, i = [], 0
    while i < Z_COLS:
        j = i + 1
        if _Z_SRC[i] < 0:
            while j < Z_COLS and _Z_SRC[j] < 0:
                j += 1
            runs.append((-1, j - i))
        else:
            while j < Z_COLS and _Z_SRC[j] == _Z_SRC[j - 1] + 1:
                j += 1
            runs.append((int(_Z_SRC[i]), j - i))
        i = j
    return runs


_Z_RUNS = _z_runs()


def _pad_z_columns(w):
    parts = [jnp.zeros(w.shape[:-1] + (n,), w.dtype) if s < 0 else w[..., s:s + n] for s, n in _Z_RUNS]
    return jnp.concatenate(parts, axis=-1)


def _cparams(semantics):
    return pltpu.CompilerParams(dimension_semantics=semantics, vmem_limit_bytes=VMEM_LIMIT)


def _const_spec(block_shape, index_map):
    return pl.BlockSpec(block_shape, index_map, pipeline_mode=pl.Buffered(1))


def _rms(x):
    return x * lax.rsqrt(jnp.mean(x * x, axis=-1, keepdims=True) + NORM_EPS)


def _unpack_rows(x):
    return pltpu.bitcast(x, BF16)


def _dot(a, b):
    return jnp.dot(a, b, preferred_element_type=F32)


def _dot_nt(a, b):
    return lax.dot_general(a, b, (((1,), (1,)), ((), ())), preferred_element_type=F32)


def _chunk_sums(same, tri, x):
    rows = x.shape[0]
    masks = jnp.concatenate([jnp.where(tri, 1.0, 0.0), jnp.where(same, 1.0, 0.0)], axis=0).astype(BF16)
    hi = x.astype(BF16)
    r1 = x - hi.astype(F32)
    mid = r1.astype(BF16)
    lo = (r1 - mid.astype(F32)).astype(BF16)
    sums = _dot(masks, jnp.concatenate([hi, mid, lo], axis=1))
    total = sums[:, 0:LANE] + sums[:, LANE:2 * LANE] + sums[:, 2 * LANE:3 * LANE]
    return total[0:rows], total[rows:2 * rows]


def _adaln_kernel(c_ref, w_lo_ref, w_hi_ref, b_ref, o_ref):
    c = c_ref[...]
    s = (c * jax.nn.sigmoid(c)).astype(BF16)
    half = D_MODEL // 2
    o_ref[0] = (_dot(s[:, :half], w_lo_ref[0].astype(BF16)) + _dot(s[:, half:], w_hi_ref[0].astype(BF16))
                + b_ref[0])


def _adaln(c_all, w_ada, b_ada):
    n_rows = c_all.shape[0]
    n_out = w_ada.shape[-1]
    tn = 1024
    w_half = lambda which: pl.BlockSpec((1, D_MODEL // 2, tn), lambda l, j: (l, which, j))
    return pl.pallas_call(
        _adaln_kernel,
        grid=(DEPTH, n_out // tn),
        in_specs=[
            pl.BlockSpec((n_rows, D_MODEL), lambda l, j: (0, 0)),
            w_half(0), w_half(1),
            pl.BlockSpec((1, 1, tn), lambda l, j: (l, 0, j)),
        ],
        out_specs=pl.BlockSpec((1, n_rows, tn), lambda l, j: (l, 0, j)),
        out_shape=jax.ShapeDtypeStruct((DEPTH, n_rows, n_out), F32),
        compiler_params=_cparams(("parallel", "parallel")),
        name="adaln",
    )(c_all, w_ada, w_ada, b_ada.reshape(DEPTH, 1, n_out))


def _modulated_group(x8, gain, sc_row, sh_row):
    return _rms(x8) * gain * (1.0 + sc_row) + sh_row


def _norm_proj_kernel(x_ref, sc_ref, sh_ref, gain_ref, w_ref, o_ref, h_scr, *, groups):
    @pl.when(pl.program_id(1) == 0)
    def _():
        gain = gain_ref[...]

        def body(p, carry):
            hs = []
            for u in range(2):
                g = p * 2 + u
                r0 = pl.multiple_of(g * TOKENS_PER_GROUP, TOKENS_PER_GROUP)
                hs.append(_modulated_group(x_ref[pl.ds(r0, TOKENS_PER_GROUP), :], gain,
                                           sc_ref[pl.ds(g, 1), :], sh_ref[pl.ds(g, 1), :]))
            r = pl.multiple_of(p * 2 * TOKENS_PER_GROUP, 2 * TOKENS_PER_GROUP)
            h_scr[pl.ds(r, 2 * TOKENS_PER_GROUP), :] = jnp.concatenate(hs, axis=0).astype(BF16)
            return carry

        lax.fori_loop(0, groups // 2, body, 0, unroll=GROUP_UNROLL)

    o_ref[...] = _dot(h_scr[...], w_ref[...])


MOD_SH1, MOD_SC1, MOD_G1, MOD_SH2, MOD_SC2, MOD_G2 = range(6)


def _mod_spec(groups, layer, kind):
    return pl.BlockSpec((None, groups, D_MODEL), lambda *a: (layer, a[0], kind))


def _gain_spec(layer, which):
    return pl.BlockSpec((None, 1, D_MODEL), lambda *a: (2 * layer + which, 0, 0))


def _norm_proj(x, mod, gains, w, *, layer, tt, tn):
    n_tok = x.shape[0]
    n_out = w.shape[-1]
    groups = tt // TOKENS_PER_GROUP
    return pl.pallas_call(
        functools.partial(_norm_proj_kernel, groups=groups),
        grid=(n_tok // tt, n_out // tn),
        in_specs=[
            pl.BlockSpec((tt, D_MODEL), lambda i, j: (i, 0)),
            _mod_spec(groups, layer, MOD_SC1),
            _mod_spec(groups, layer, MOD_SH1),
            _gain_spec(layer, 0),
            pl.BlockSpec((None, D_MODEL, tn), lambda i, j: (layer, 0, j)),
        ],
        out_specs=pl.BlockSpec((tt, tn), lambda i, j: (i, j)),
        out_shape=jax.ShapeDtypeStruct((n_tok, n_out), F32),
        scratch_shapes=[pltpu.VMEM((tt, D_MODEL), BF16)],
        compiler_params=_cparams(("parallel", "arbitrary")),
        name="norm_proj",
    )(x, mod, mod, gains, w)


def _chunk_masks(rows, chunk):
    shift = int(math.log2(chunk))
    ri = lax.broadcasted_iota(jnp.int32, (rows, rows), 0)
    ci = lax.broadcasted_iota(jnp.int32, (rows, rows), 1)
    same = (ri >> shift) == (ci >> shift)
    tri = jnp.logical_and(same, ci <= ri)
    return same, tri


def _gla_kernel(q_ref, k_ref, v_ref, g_ref, r_ref, wg_ref, bg_ref, gain_ref, s0_ref,
                o_ref, sfin_ref, st_scr, **kw):
    carry = kw["carry"]
    heads = range(HEADS_PER_STEP)
    if carry:
        @pl.when(pl.program_id(2) == 0)
        def _():
            for hh in heads:
                st_scr[hh] = _gla_load_state(s0_ref.at[:, hh:hh + 1], 0)
    for hh in heads:
        ln = slice(hh * LANE, (hh + 1) * LANE)
        one = slice(hh, hh + 1)
        _gla_head(q_ref.at[:, ln], k_ref.at[:, ln], v_ref.at[:, ln], g_ref.at[:, ln], r_ref,
                  wg_ref.at[one], bg_ref.at[one], gain_ref.at[one], s0_ref.at[:, one],
                  o_ref.at[:, ln], sfin_ref.at[:, one], st_scr.at[hh], **kw)
    if carry:
        @pl.when(pl.program_id(2) == pl.num_programs(2) - 1)
        def _():
            for hh in heads:
                sfin_ref[0, hh] = st_scr[hh].T[0:GLA_DK, :]


def _gla_load_state(s0_ref, c):
    zero_pad = jnp.zeros((LANE - GLA_DK, GLA_DV), F32)
    return jnp.concatenate([s0_ref[c, 0], zero_pad], axis=0).T


def _gla_head(q_ref, k_ref, v_ref, g_ref, r_ref, wg_ref, bg_ref, gain_ref, s0_ref,
              o_ref, sfin_ref, st_scr, *, rows, chunk, carry):
    n_chunks = rows // chunk
    shift = int(math.log2(chunk))
    same, tri = _chunk_masks(rows, chunk)

    q = q_ref[...] * (GLA_DK ** -0.5)
    k = k_ref[...]
    v = v_ref[...]
    la = jax.nn.log_sigmoid(_dot(r_ref[...].astype(BF16), wg_ref[0]) + bg_ref[0]) / GLA_TAU
    bl, bt = _chunk_sums(same, tri, la)
    qd = (q * jnp.exp(bl)).astype(BF16)
    ki = (k * jnp.exp(-bl)).astype(BF16)
    kd = (k * jnp.exp(bt - bl)).astype(BF16)
    att = jnp.where(tri, _dot_nt(qd, ki), 0.0)
    o_intra = _dot(att.astype(BF16), v.astype(BF16))

    v_t = v.T
    lane_chunk = lax.broadcasted_iota(jnp.int32, (GLA_DV, rows), 1) >> shift
    if carry:
        s_t = st_scr[...]

    sliced = chunk % (2 * SUBLANE) == 0
    row_chunk = lax.broadcasted_iota(jnp.int32, (rows, GLA_DV), 0) >> shift
    o_inter = [] if sliced else jnp.zeros((rows, GLA_DV), F32)
    for c in range(n_chunks):
        lo = c * chunk
        if not carry:
            s_t = _gla_load_state(s0_ref, c)
        if sliced:
            o_inter.append(_dot_nt(qd[lo:lo + chunk], s_t.astype(BF16)))
        else:
            o_inter = jnp.where(row_chunk == c, _dot_nt(qd, s_t.astype(BF16)), o_inter)
        decay = jnp.exp(bt[lo:lo + 1, :])
        v_c = jnp.where(lane_chunk == c, v_t, 0.0).astype(BF16)
        s_t = s_t * decay + _dot(v_c, kd)
        if not carry:
            sfin_ref[c, 0] = s_t.T[0:GLA_DK, :]

    if carry:
        st_scr[...] = s_t

    o = o_intra + (jnp.concatenate(o_inter, axis=0) if sliced else o_inter)
    g = g_ref[...]
    o_ref[...] = _rms(o) * gain_ref[0] * (g * jax.nn.sigmoid(g))


def _without_refs(kernel_fn, first, count):
    def wrapped(*refs):
        return kernel_fn(*refs[:first], *refs[first + count:])
    return wrapped


def _mixer_grid(n_seq, n_heads, seq_blocks, row_block0, carry):
    if carry:
        return ((n_seq, n_heads, seq_blocks), lambda b, h, t: row_block0 + b * seq_blocks + t,
                ("parallel", "parallel", "arbitrary"))
    return (n_seq, n_heads), lambda b, h: row_block0 + b, ("parallel", "parallel")


def _state_spec(per_block, d2, d3, layer):
    if layer is None:
        return pl.BlockSpec((per_block, HEADS_PER_STEP, d2, d3), lambda *a: (a[0], a[1], 0, 0))
    return pl.BlockSpec((None, per_block, HEADS_PER_STEP, d2, d3), lambda *a: (layer, a[0], a[1], 0, 0))


def _head_cols_spec(rows, rb, block0):
    assert block0 % HEADS_PER_STEP == 0
    return pl.BlockSpec((rows, HEADS_PER_STEP * LANE), lambda *a: (rb(*a), block0 // HEADS_PER_STEP + a[1]))


def _head_param_spec(shape):
    return pl.BlockSpec((HEADS_PER_STEP,) + shape, lambda *a: (a[1],) + (0,) * len(shape))


def _gla(z, wg, bg, gain, s0, *, n_seq, seq_blocks, rows, chunk, row_block0, carry,
         layer=None, fill=()):
    per_block = 1 if carry else rows // chunk
    grid, rb, sem = _mixer_grid(n_seq, GLA_HEADS // HEADS_PER_STEP, seq_blocks, row_block0, carry)
    zspec = functools.partial(_head_cols_spec, rows, rb)
    state_spec = _state_spec(per_block, GLA_DK, GLA_DV, layer)
    n_in = 9
    return pl.pallas_call(
        _without_refs(functools.partial(_gla_kernel, rows=rows, chunk=chunk, carry=carry), n_in, len(fill)),
        grid=grid,
        in_specs=[zspec(ZB_GQ), zspec(ZB_GK), zspec(ZB_GV), zspec(ZB_GG),
                  pl.BlockSpec((rows, LANE), lambda *a: (rb(*a), ZB_GR)),
                  _head_param_spec((LANE, LANE)), _head_param_spec((1, LANE)), _head_param_spec((1, LANE)),
                  state_spec]
                 + [pl.BlockSpec(memory_space=pl.ANY)] * len(fill),
        out_specs=[_head_cols_spec(rows, rb, 0), state_spec],
        out_shape=[jax.ShapeDtypeStruct((z.shape[0], GLA_WIDTH), F32),
                   jax.ShapeDtypeStruct(s0.shape, F32)],
        input_output_aliases={n_in + i: i for i in range(len(fill))},
        scratch_shapes=[pltpu.VMEM((HEADS_PER_STEP, GLA_DV, LANE), F32)],
        compiler_params=_cparams(sem),
        name="gla_seq" if carry else "gla_step",
    )(z, z, z, z, z, wg, bg, gain, s0, *fill)


def _mlstm_kernel(q_ref, k_ref, v_ref, og_ref, gate_ref, bif_ref, gain_ref, c0_ref, n0_ref, m0_ref,
                  h_ref, cfin_ref, nfin_ref, mfin_ref, c_scr, n_scr, m_scr, **kw):
    carry = kw["carry"]
    heads = range(HEADS_PER_STEP)
    if carry:
        @pl.when(pl.program_id(2) == 0)
        def _():
            for hh in heads:
                c_scr[hh] = c0_ref[0, hh]
                n_scr[hh] = n0_ref[0, hh]
                m_scr[hh] = m0_ref[0, hh]
    _mlstm_heads(q_ref, k_ref, v_ref, og_ref, gate_ref, bif_ref, gain_ref, c0_ref, n0_ref, m0_ref,
                 h_ref, cfin_ref, nfin_ref, mfin_ref, c_scr, n_scr, m_scr, **kw)
    if carry:
        @pl.when(pl.program_id(2) == pl.num_programs(2) - 1)
        def _():
            for hh in heads:
                cfin_ref[0, hh] = c_scr[hh]
                nfin_ref[0, hh] = n_scr[hh]
                mfin_ref[0, hh] = m_scr[hh]


def _mlstm_heads(q_ref, k_ref, v_ref, og_ref, gate_ref, bif_ref, gain_ref, c0_ref, n0_ref, m0_ref,
                 h_ref, cfin_ref, nfin_ref, mfin_ref, c_scr, n_scr, m_scr, **kw):
    for hh in range(HEADS_PER_STEP):
        ln = slice(hh * LANE, (hh + 1) * LANE)
        one = slice(hh, hh + 1)
        _mlstm_head(pl.program_id(1) * HEADS_PER_STEP + hh,
                    q_ref.at[:, ln], k_ref.at[:, ln], v_ref.at[:, ln], og_ref.at[:, ln], gate_ref, bif_ref,
                    gain_ref.at[one], c0_ref.at[:, one], n0_ref.at[:, one], m0_ref.at[:, one],
                    h_ref.at[:, ln], cfin_ref.at[:, one], nfin_ref.at[:, one], mfin_ref.at[:, one],
                    c_scr.at[hh], n_scr.at[hh], m_scr.at[hh], **kw)


def _mlstm_head(head, q_ref, k_ref, v_ref, og_ref, gate_ref, bif_ref, gain_ref, c0_ref, n0_ref, m0_ref,
                h_ref, cfin_ref, nfin_ref, mfin_ref, c_scr, n_scr, m_scr, *, rows, chunk, carry):
    n_chunks = rows // chunk
    shift = int(math.log2(chunk))
    same, tri = _chunk_masks(rows, chunk)
    neg_inf = jnp.float32(-jnp.inf)

    x = gate_ref[...] + bif_ref[...]
    log_f = jax.nn.log_sigmoid(x)
    b_cum, b_tot = _chunk_sums(same, tri, log_f)
    lane = lax.broadcasted_iota(jnp.int32, (rows, LANE), 1)
    y = jnp.where(lane < F_GATE_LANE, x, b_cum)
    pick_i = lane == head
    pick_b = lane == head + F_GATE_LANE
    i_col = jnp.sum(jnp.where(pick_i, y, 0.0), axis=-1, keepdims=True)
    b_col = jnp.sum(jnp.where(pick_b, y, 0.0), axis=-1, keepdims=True)
    bt_col = jnp.sum(jnp.where(pick_b, b_tot, 0.0), axis=-1, keepdims=True)
    sub = lax.broadcasted_iota(jnp.int32, (LANE, rows), 0)
    y_t = y.T
    i_row = jnp.sum(jnp.where(sub == head, y_t, 0.0), axis=0, keepdims=True)
    b_row = jnp.sum(jnp.where(sub == head + F_GATE_LANE, y_t, 0.0), axis=0, keepdims=True)

    a_col = bt_col - b_col + i_col
    mloc_col = bt_col + jnp.max(jnp.where(same, i_row - b_row, neg_inf), axis=-1, keepdims=True)

    q = q_ref[...] * (ML_DH ** -0.5)
    k = k_ref[...]
    qb = q.astype(BF16)
    kb = k.astype(BF16)
    vb = v_ref[...].astype(BF16)
    kw = k * jnp.exp(a_col - mloc_col)
    kw_t = kw.T
    lane_chunk = lax.broadcasted_iota(jnp.int32, (ML_DH, rows), 1) >> shift

    def advance(c, c_st, n_st, m_st, m_last):
        hi = (c + 1) * chunk
        decay = jnp.exp(bt_col[hi - 1:hi] + m_st - m_last)
        scale = jnp.exp(mloc_col[hi - 1:hi] - m_last)
        kw_c = jnp.where(lane_chunk == c, kw_t, 0.0).astype(BF16)
        c_new = decay * c_st + scale * _dot(kw_c, vb)
        n_new = decay * n_st + scale * jnp.sum(kw[hi - chunk:hi], axis=0, keepdims=True)
        return c_new, n_new

    if carry:
        c_st = c_scr[...]
        n_st = n_scr[...]
        m_st = m_scr[:, 0:1]
        ri = lax.broadcasted_iota(jnp.int32, (chunk, chunk), 0)
        ci = lax.broadcasted_iota(jnp.int32, (chunk, chunk), 1)
        tri_c = ci <= ri
        h_chunks = []
        for c in range(n_chunks):
            lo = c * chunk
            hi = lo + chunk
            bc = b_col[lo:hi]
            d_log = jnp.where(tri_c, bc - b_row[:, lo:hi] + i_row[:, lo:hi], neg_inf)
            g_inter = bc + m_st
            m_t = jnp.maximum(g_inter, jnp.max(d_log, axis=-1, keepdims=True))
            w_inter = jnp.exp(g_inter - m_t)
            s = _dot_nt(qb[lo:hi], kb[lo:hi]) * jnp.exp(d_log - m_t)
            num = w_inter * _dot(qb[lo:hi], c_st.astype(BF16)) + _dot(s.astype(BF16), vb[lo:hi])
            den = (w_inter * jnp.sum(q[lo:hi] * n_st, axis=-1, keepdims=True)
                   + jnp.sum(s, axis=-1, keepdims=True))
            h_chunks.append(num / jnp.maximum(jnp.abs(den), jnp.exp(-m_t)))
            m_last = m_t[chunk - 1:chunk]
            c_st, n_st = advance(c, c_st, n_st, m_st, m_last)
            m_st = m_last
        h = jnp.concatenate(h_chunks, axis=0)
        c_scr[...] = c_st
        n_scr[...] = n_st
        m_scr[...] = jnp.broadcast_to(m_st, (1, LANE))
    else:
        row_chunk1 = lax.broadcasted_iota(jnp.int32, (rows, 1), 0) >> shift
        row_chunk = lax.broadcasted_iota(jnp.int32, (rows, ML_DH), 0) >> shift
        m_rows = jnp.zeros((rows, 1), F32)
        n_rows = jnp.zeros((rows, ML_DH), F32)
        qc = jnp.zeros((rows, ML_DH), F32)
        for c in range(n_chunks):
            m_rows = jnp.where(row_chunk1 == c, m0_ref[c, 0][:, 0:1], m_rows)
            n_rows = jnp.where(row_chunk == c, n0_ref[c, 0], n_rows)
            qc = jnp.where(row_chunk == c, _dot(qb, c0_ref[c, 0].astype(BF16)), qc)
        d_log = jnp.where(tri, b_col - b_row + i_row, neg_inf)
        g_inter = b_col + m_rows
        m_t = jnp.maximum(g_inter, jnp.max(d_log, axis=-1, keepdims=True))
        w_inter = jnp.exp(g_inter - m_t)
        s = _dot_nt(qb, kb) * jnp.exp(d_log - m_t)
        num = w_inter * qc + _dot(s.astype(BF16), vb)
        den = w_inter * jnp.sum(q * n_rows, axis=-1, keepdims=True) + jnp.sum(s, axis=-1, keepdims=True)
        h = num / jnp.maximum(jnp.abs(den), jnp.exp(-m_t))
        for c in range(n_chunks):
            hi = (c + 1) * chunk
            m_last = m_t[hi - 1:hi]
            c_new, n_new = advance(c, c0_ref[c, 0], n0_ref[c, 0], m0_ref[c, 0][:, 0:1], m_last)
            cfin_ref[c, 0] = c_new
            nfin_ref[c, 0] = n_new
            mfin_ref[c, 0] = jnp.broadcast_to(m_last, (1, LANE))

    h_ref[...] = _rms(h) * gain_ref[0] * jax.nn.sigmoid(og_ref[...])


def _mlstm(z, bif, gain, c0, n0, m0, *, n_seq, seq_blocks, rows, chunk, row_block0, carry,
           layer=None, fill=()):
    per_block = 1 if carry else rows // chunk
    grid, rb, sem = _mixer_grid(n_seq, ML_HEADS // HEADS_PER_STEP, seq_blocks, row_block0, carry)
    zspec = functools.partial(_head_cols_spec, rows, rb)
    state_specs = [_state_spec(per_block, ML_DH, ML_DH, layer), _state_spec(per_block, 1, ML_DH, layer),
                   _state_spec(per_block, 1, LANE, layer)]
    n_in = 10
    return pl.pallas_call(
        _without_refs(functools.partial(_mlstm_kernel, rows=rows, chunk=chunk, carry=carry), n_in, len(fill)),
        grid=grid,
        in_specs=[zspec(ZB_MQ), zspec(ZB_MK), zspec(ZB_MV), zspec(ZB_MO),
                  pl.BlockSpec((rows, LANE), lambda *a: (rb(*a), ZB_MIF)),
                  pl.BlockSpec((1, LANE), lambda *a: (0, 0)),
                  _head_param_spec((1, LANE))]
                 + state_specs + [pl.BlockSpec(memory_space=pl.ANY)] * len(fill),
        out_specs=[_head_cols_spec(rows, rb, 0)] + state_specs,
        out_shape=[jax.ShapeDtypeStruct((z.shape[0], ML_WIDTH), F32),
                   jax.ShapeDtypeStruct(c0.shape, F32), jax.ShapeDtypeStruct(n0.shape, F32),
                   jax.ShapeDtypeStruct(m0.shape, F32)],
        input_output_aliases={n_in + i: i for i in range(len(fill))},
        scratch_shapes=[pltpu.VMEM((HEADS_PER_STEP, ML_DH, ML_DH), F32), pltpu.VMEM((HEADS_PER_STEP, 1, ML_DH), F32),
                        pltpu.VMEM((HEADS_PER_STEP, 1, LANE), F32)],
        compiler_params=_cparams(sem),
        name="mlstm_seq" if carry else "mlstm_step",
    )(z, z, z, z, z, bif, gain, c0, n0, m0, *fill)


def _s5_disc_kernel(lre_ref, lim_ref, ldt_ref, bre_ref, bim_ref, a_ref, bb_ref):
    lam_re = lre_ref[0]
    lam_im = lim_ref[0]
    dt = jnp.exp(ldt_ref[0])
    mag = jnp.exp(lam_re * dt)
    ar = mag * jnp.cos(lam_im * dt)
    ai = mag * jnp.sin(lam_im * dt)
    den = lam_re * lam_re + lam_im * lam_im
    fr = ((ar - 1.0) * lam_re + ai * lam_im) / den
    fi = (ai * lam_re - (ar - 1.0) * lam_im) / den
    b_re = bre_ref[0]
    b_im = bim_ref[0]
    a_ref[0, 0:1, :] = ar
    a_ref[0, 1:2, :] = ai
    bb_ref[0, 0] = fr * b_re - fi * b_im
    bb_ref[0, 1] = fr * b_im + fi * b_re


def _s5_discretise(lam_re, lam_im, log_dt, b_re, b_im):
    flat = lambda t: t.reshape(DEPTH, 1, S5_STATE)
    ldt = jnp.broadcast_to(log_dt[:, :, None], (DEPTH, S5_GROUPS, S5_P))
    chan = lambda t: t.reshape(DEPTH, S5_STATE, S5_CH).transpose(0, 2, 1)
    row = pl.BlockSpec((1, 1, S5_STATE), lambda l: (l, 0, 0))
    mat = pl.BlockSpec((1, S5_CH, S5_STATE), lambda l: (l, 0, 0))
    return pl.pallas_call(
        _s5_disc_kernel,
        grid=(DEPTH,),
        in_specs=[row, row, row, mat, mat],
        out_specs=[pl.BlockSpec((1, 2, S5_STATE), lambda l: (l, 0, 0)),
                   pl.BlockSpec((1, 2, S5_CH, S5_STATE), lambda l: (l, 0, 0, 0))],
        out_shape=[jax.ShapeDtypeStruct((DEPTH, 2, S5_STATE), F32),
                   jax.ShapeDtypeStruct((DEPTH, 2, S5_CH, S5_STATE), F32)],
        compiler_params=_cparams(("parallel",)),
        name="s5_disc",
    )(flat(lam_re), flat(lam_im), flat(ldt), chan(b_re), chan(b_im))


def _s5_kernel(u_ref, h0_ref, a_ref, wb_ref, wc_ref, d_ref, wglu_ref, bglu_ref, gain_ref,
               y_ref, hfin_ref, bu_scr, h_scr, *, steps, batch, lane_width):
    n_rows = steps * batch

    @pl.when(pl.program_id(0) == 0)
    def _():
        h_scr[...] = h0_ref[...]

    u = u_ref[...].reshape(n_rows, S5_WIDTH)
    ub = u.astype(BF16)
    for t in range(S5_TILES):
        bu = _dot(ub[:, t * LANE:(t + 1) * LANE], wb_ref[t])
        re0 = t * S5_TILE_STATE
        bu_scr[:, :, re0:re0 + S5_TILE_STATE] = bu[:, :S5_TILE_STATE].reshape(steps, batch, S5_TILE_STATE)
        bu_scr[:, :, S5_STATE + re0:S5_STATE + re0 + S5_TILE_STATE] = (
            bu[:, S5_TILE_STATE:].reshape(steps, batch, S5_TILE_STATE))

    for j in range(S5_STATE // lane_width):
        re = slice(j * lane_width, (j + 1) * lane_width)
        im = slice(S5_STATE + j * lane_width, S5_STATE + (j + 1) * lane_width)
        ar = a_ref[0:1, re]
        ai = a_ref[1:2, re]

        def step(t, carry):
            hr, hi = carry
            nr = ar * hr - ai * hi + bu_scr[t, :, re]
            ni = ar * hi + ai * hr + bu_scr[t, :, im]
            bu_scr[t, :, re] = nr
            bu_scr[t, :, im] = ni
            return nr, ni

        hr, hi = lax.fori_loop(0, steps, step, (h_scr[:, re], h_scr[:, im]), unroll=min(steps, 8))
        h_scr[:, re] = hr
        h_scr[:, im] = hi

    ys = []
    for t in range(S5_TILES):
        re0 = t * S5_TILE_STATE
        h_re = bu_scr[:, :, re0:re0 + S5_TILE_STATE].reshape(n_rows, S5_TILE_STATE).astype(BF16)
        h_im = bu_scr[:, :, S5_STATE + re0:S5_STATE + re0 + S5_TILE_STATE].reshape(n_rows, S5_TILE_STATE).astype(BF16)
        ys.append(_dot(h_re, wc_ref[t, :S5_TILE_STATE]) + _dot(h_im, wc_ref[t, S5_TILE_STATE:]))
    y = jnp.concatenate(ys, axis=1) + d_ref[...] * u
    y = jax.nn.gelu(y)
    out = y * jax.nn.sigmoid(_dot(y.astype(BF16), wglu_ref[...]) + bglu_ref[...])
    y_ref[...] = (_rms(out) * gain_ref[...]).reshape(steps, batch, S5_WIDTH)

    @pl.when(pl.program_id(0) == pl.num_programs(0) - 1)
    def _():
        hfin_ref[...] = h_scr[...]


def _s5(u_t, h0, a, wb, wc, d, wglu, bglu, gain, *, steps, lane_width):
    seq, batch, _ = u_t.shape
    const = lambda shape: _const_spec(shape, lambda t: (0,) * len(shape))
    return pl.pallas_call(
        functools.partial(_s5_kernel, steps=steps, batch=batch, lane_width=lane_width),
        grid=(seq // steps,),
        in_specs=[pl.BlockSpec((steps, batch, S5_WIDTH), lambda t: (t, 0, 0)),
                  const((batch, 2 * S5_STATE)), const((2, S5_STATE)),
                  const((S5_TILES, LANE, 2 * S5_TILE_STATE)), const((S5_TILES, 2 * S5_TILE_STATE, LANE)),
                  const((1, S5_WIDTH)), const((S5_WIDTH, S5_WIDTH)), const((1, S5_WIDTH)),
                  const((1, S5_WIDTH))],
        out_specs=[pl.BlockSpec((steps, batch, S5_WIDTH), lambda t: (t, 0, 0)),
                   pl.BlockSpec((batch, 2 * S5_STATE), lambda t: (0, 0))],
        out_shape=[jax.ShapeDtypeStruct((seq, batch, S5_WIDTH), F32),
                   jax.ShapeDtypeStruct((batch, 2 * S5_STATE), F32)],
        scratch_shapes=[pltpu.VMEM((steps, batch, 2 * S5_STATE), F32),
                        pltpu.VMEM((batch, 2 * S5_STATE), F32)],
        compiler_params=_cparams(("arbitrary",)),
        name="s5",
    )(u_t, h0, a, wb, wc, d, wglu, bglu, gain)


def _out_proj_kernel(og_ref, om_ref, os_ref, x_ref, g1_ref, sc_ref, sh_ref, gain_ref,
                     wg_ref, wm_ref, ws_ref, xo_ref, ht_ref, mix_scr, h_scr, *, groups):
    mix_scr[...] = (_dot(og_ref[...].astype(BF16), wg_ref[...])
                    + _dot(om_ref[...].astype(BF16), wm_ref[...])
                    + _dot(os_ref[...].astype(BF16), ws_ref[...]))
    gain = gain_ref[...]

    def body(g, carry):
        r0 = pl.multiple_of(g * TOKENS_PER_GROUP, TOKENS_PER_GROUP)
        rows = pl.ds(r0, TOKENS_PER_GROUP)
        xn = x_ref[rows, :] + g1_ref[pl.ds(g, 1), :] * mix_scr[rows, :]
        xo_ref[rows, :] = xn
        h_scr[rows, :] = _modulated_group(xn, gain, sc_ref[pl.ds(g, 1), :], sh_ref[pl.ds(g, 1), :])
        return carry

    lax.fori_loop(0, groups, body, 0, unroll=GROUP_UNROLL)
    ht_ref[...] = pltpu.bitcast(h_scr[...].T.astype(BF16), jnp.uint32)


def _out_proj(og, om, os_, x, mod, gains, w_out, *, layer, tt):
    n_tok = x.shape[0]
    groups = tt // TOKENS_PER_GROUP
    tok = lambda width: pl.BlockSpec((tt, width), lambda i: (i, 0))
    w_rows = lambda height, block: _const_spec((None, height, D_MODEL), lambda i: (layer, block, 0))
    assert GLA_WIDTH == ML_WIDTH and (GLA_WIDTH + ML_WIDTH) % S5_WIDTH == 0
    return pl.pallas_call(
        functools.partial(_out_proj_kernel, groups=groups),
        grid=(n_tok // tt,),
        in_specs=[tok(GLA_WIDTH), tok(ML_WIDTH), tok(S5_WIDTH), tok(D_MODEL),
                  _mod_spec(groups, layer, MOD_G1), _mod_spec(groups, layer, MOD_SC2),
                  _mod_spec(groups, layer, MOD_SH2), _gain_spec(layer, 1),
                  w_rows(GLA_WIDTH, 0), w_rows(ML_WIDTH, 1),
                  w_rows(S5_WIDTH, (GLA_WIDTH + ML_WIDTH) // S5_WIDTH)],
        out_specs=[tok(D_MODEL), pl.BlockSpec((D_MODEL // 2, tt), lambda i: (0, i))],
        out_shape=[jax.ShapeDtypeStruct((n_tok, D_MODEL), F32),
                   jax.ShapeDtypeStruct((D_MODEL // 2, n_tok), jnp.uint32)],
        scratch_shapes=[pltpu.VMEM((tt, D_MODEL), F32), pltpu.VMEM((tt, D_MODEL), F32)],
        compiler_params=_cparams(("parallel",)),
        name="out_proj",
    )(og, om, os_, x, mod, mod, mod, gains, w_out, w_out, w_out)


def _top_values(cur, count, out_scr=None, with_rank=False):
    neg_inf = jnp.float32(-jnp.inf)
    vals = []
    rank = jnp.full(cur.shape, float(count), F32) if with_rank else None
    for r in range(count):
        m = jnp.max(cur, axis=0, keepdims=True)
        vals.append(m)
        if out_scr is not None:
            out_scr[r:r + 1, :] = m
        hit = cur == m
        if with_rank:
            rank = jnp.where(hit, float(r), rank)
        if r + 1 < count:
            cur = jnp.where(hit, neg_inf, cur)
    return (vals, rank) if with_rank else vals


def _peer_route_kernel(ht_ref, wqt_ref, keys_ref, first_ref, second_ref, qt_scr, v1_scr, v2_scr, cand_scr,
                       *, tt):
    qt_scr[...] = _dot(wqt_ref[...], _unpack_rows(ht_ref[...]))
    for h in range(PEER_HEADS):
        sc = []
        for side in range(2):
            hs = 2 * h + side
            qb = qt_scr[hs * LANE:(hs + 1) * LANE, :].astype(BF16)
            sc.append(_dot(keys_ref[hs], qb))
        for lt in range(tt // LANE):
            ls = slice(lt * LANE, (lt + 1) * LANE)
            s1 = sc[0][:, ls]
            s2 = sc[1][:, ls]
            _top_values(s1, PEER_TOPK, v1_scr)
            _, rank2 = _top_values(s2, PEER_TOPK, v2_scr, with_rank=True)
            v1_head = v1_scr[0:SUBLANE, :]
            cand_scr[0:PEER_TOPK, :] = v1_scr[0:1, :] + v2_scr[...]
            cand_scr[PEER_TOPK:2 * PEER_TOPK, :] = v1_scr[...] + v2_scr[0:1, :]
            cand_scr[2 * PEER_TOPK:2 * PEER_TOPK + SUBLANE, :] = v1_scr[1:2, :] + v2_scr[0:SUBLANE, :]
            for b in range(1, 5):
                lo = 2 * PEER_TOPK + b * SUBLANE
                cand_scr[lo:lo + SUBLANE, :] = v1_head + v2_scr[b:b + 1, :]
            top = _top_values(cand_scr[...], PEER_TOPK)
            z = jnp.zeros_like(top[0])
            for tv in top:
                z = z + jnp.exp(tv - top[0])
            tau = top[PEER_TOPK - 1]
            n1 = jnp.zeros_like(s1)
            for b in range(SUBLANE):
                n1 = n1 + jnp.where(s1 + v2_scr[b:b + 1, :] >= tau, 1.0, 0.0)
            best = v1_scr[0:1, :]
            tail = jnp.zeros_like(best)
            for b in range(SUBLANE, PEER_TOPK):
                tail = tail + jnp.where(best + v2_scr[b:b + 1, :] >= tau, 1.0, 0.0)
            n1 = n1 + jnp.where(s1 == best, tail, 0.0)
            first_ref[2 * h, :, ls] = n1
            first_ref[2 * h + 1, :, ls] = jnp.exp(s1 - v1_scr[0:1, :]) / z
            second_ref[2 * h, :, ls] = rank2.astype(BF16)
            second_ref[2 * h + 1, :, ls] = jnp.exp(s2 - v2_scr[0:1, :]).astype(BF16)


def _peer_route(ht, wqt, keys, *, tt):
    n_tok = ht.shape[1]
    n_hs = 2 * PEER_HEADS
    assert PEER_TOPK == 2 * SUBLANE
    n_cand = 2 * PEER_TOPK + 5 * SUBLANE
    return pl.pallas_call(
        functools.partial(_peer_route_kernel, tt=tt),
        grid=(n_tok // tt,),
        in_specs=[pl.BlockSpec((D_MODEL // 2, tt), lambda i: (0, i)),
                  _const_spec((n_hs * LANE, D_MODEL), lambda i: (0, 0)),
                  _const_spec((n_hs, N_KEYS, LANE), lambda i: (0, 0, 0))],
        out_specs=[pl.BlockSpec((n_hs, N_KEYS, tt), lambda i: (0, 0, i)),
                   pl.BlockSpec((n_hs, N_KEYS, tt), lambda i: (0, 0, i))],
        out_shape=[jax.ShapeDtypeStruct((n_hs, N_KEYS, n_tok), F32),
                   jax.ShapeDtypeStruct((n_hs, N_KEYS, n_tok), BF16)],
        scratch_shapes=[pltpu.VMEM((n_hs * LANE, tt), F32), pltpu.VMEM((PEER_TOPK, LANE), F32),
                        pltpu.VMEM((PEER_TOPK, LANE), F32), pltpu.VMEM((n_cand, LANE), F32)],
        compiler_params=_cparams(("parallel",)),
        name="peer_route",
    )(ht, wqt, keys)


DENSE_FIRST_KEYS = 4
DENSE_SUB = DENSE_FIRST_KEYS * N_KEYS
GATE_ROWS = 4 * SUBLANE
GATE_PIECES = 4


def _gate_times_act(first_ref, row0, second_ref, act_ref, p_ref, lane_tiles):
    n_pieces = N_KEYS // GATE_ROWS
    zero = jnp.zeros((GATE_ROWS, LANE), BF16)
    for lt in lane_tiles:
        ls = slice(lt * LANE, (lt + 1) * LANE)
        for jp0 in range(0, n_pieces, GATE_PIECES):
            pieces = range(jp0, jp0 + GATE_PIECES)
            gates = [[zero for _ in pieces] for _ in range(DENSE_FIRST_KEYS)]
            for h in range(PEER_HEADS):
                counts, weights = [], []
                for ii in range(DENSE_FIRST_KEYS):
                    r = row0 + ii
                    counts.append(jnp.broadcast_to(first_ref[2 * h, 0, r:r + 1, ls], (GATE_ROWS, LANE)).astype(BF16))
                    weights.append(jnp.broadcast_to(first_ref[2 * h + 1, 0, r:r + 1, ls], (GATE_ROWS, LANE)).astype(BF16))
                for q, jp in enumerate(pieces):
                    js = slice(jp * GATE_ROWS, (jp + 1) * GATE_ROWS)
                    rank2 = second_ref[2 * h, js, ls]
                    e2 = second_ref[2 * h + 1, js, ls]
                    for ii in range(DENSE_FIRST_KEYS):
                        gates[ii][q] = gates[ii][q] + jnp.where(rank2 < counts[ii], e2, zero) * weights[ii]
            for ii in range(DENSE_FIRST_KEYS):
                for q, jp in enumerate(pieces):
                    rs = slice(ii * N_KEYS + jp * GATE_ROWS, ii * N_KEYS + (jp + 1) * GATE_ROWS)
                    p_ref[rs, ls] = gates[ii][q] * act_ref[rs, ls]


def _peer_dense_kernel(ht_ref, u_ref, vt_ref, first_a_ref, first_b_ref, second_in_ref,
                       x_ref, g2_ref, o_ref, acc_scr, act0, act1, p0, p1, second_ref, *, tt, groups):
    k = pl.program_id(1)

    @pl.when(k == 0)
    def _():
        acc_scr[...] = jnp.zeros_like(acc_scr)
        act1[...] = jnp.zeros_like(act1)
        p0[...] = jnp.zeros_like(p0)
        second_ref[...] = second_in_ref[...]

    tok_half = tt // 2
    tiles_half = tok_half // LANE
    for half in range(2):
        ts = slice(half * tok_half, (half + 1) * tok_half)
        lane_tiles = range(half * tiles_half, (half + 1) * tiles_half)
        act0[:, ts] = jax.nn.gelu(_dot(u_ref[0:DENSE_SUB, :], _unpack_rows(ht_ref[:, ts]))).astype(BF16)
        _gate_times_act(first_a_ref, DENSE_FIRST_KEYS, second_ref, act1, p1, lane_tiles)
        acc_scr[:, ts] += _dot(vt_ref[:, 0:DENSE_SUB], p0[:, ts])
    for half in range(2):
        ts = slice(half * tok_half, (half + 1) * tok_half)
        lane_tiles = range(half * tiles_half, (half + 1) * tiles_half)
        act1[:, ts] = jax.nn.gelu(_dot(u_ref[DENSE_SUB:, :], _unpack_rows(ht_ref[:, ts]))).astype(BF16)
        _gate_times_act(first_b_ref, 0, second_ref, act0, p0, lane_tiles)
        acc_scr[:, ts] += _dot(vt_ref[:, DENSE_SUB:], p1[:, ts])

    @pl.when(k == pl.num_programs(1) - 1)
    def _():
        o_ref[...] = acc_scr[...].T

        def body(g, carry):
            rows = pl.ds(pl.multiple_of(g * TOKENS_PER_GROUP, TOKENS_PER_GROUP), TOKENS_PER_GROUP)
            o_ref[rows, :] = x_ref[rows, :] + g2_ref[pl.ds(g, 1), :] * o_ref[rows, :]
            return carry

        lax.fori_loop(0, groups, body, 0, unroll=GROUP_UNROLL)


def _peer_dense(ht, u_bf, vt_bf, first, second, x, mod, *, layer, tt):
    n_tok = x.shape[0]
    groups = tt // TOKENS_PER_GROUP
    n_hs = 2 * PEER_HEADS
    ne = 2 * DENSE_SUB
    n_i = ne // N_KEYS
    n_blocks = N_EXPERTS // ne
    tok_const = lambda shape, imap: pl.BlockSpec(shape, imap, pipeline_mode=pl.Buffered(1))
    by_first_key = lambda t: t.reshape(n_hs, n_blocks, n_i, n_tok)
    cur = lambda k: jnp.minimum(k, n_blocks - 1)
    prev = lambda k: jnp.maximum(k - 1, 0)
    key_spec = lambda blk: pl.BlockSpec((n_hs, 1, n_i, tt), lambda i, k: (0, blk(k), 0, i))
    return pl.pallas_call(
        functools.partial(_peer_dense_kernel, tt=tt, groups=groups),
        grid=(n_tok // tt, n_blocks + 1),
        in_specs=[tok_const((D_MODEL // 2, tt), lambda i, k: (0, i)),
                  pl.BlockSpec((None, ne, D_MODEL), lambda i, k: (layer, cur(k), 0)),
                  pl.BlockSpec((None, D_MODEL, ne), lambda i, k: (layer, 0, prev(k))),
                  key_spec(prev), key_spec(cur),
                  tok_const((n_hs, N_KEYS, tt), lambda i, k: (0, 0, i)),
                  tok_const((tt, D_MODEL), lambda i, k: (i, 0)),
                  _mod_spec(groups, layer, MOD_G2)],
        out_specs=pl.BlockSpec((tt, D_MODEL), lambda i, k: (i, 0)),
        out_shape=jax.ShapeDtypeStruct((n_tok, D_MODEL), F32),
        scratch_shapes=[pltpu.VMEM((D_MODEL, tt), F32),
                        pltpu.VMEM((DENSE_SUB, tt), BF16), pltpu.VMEM((DENSE_SUB, tt), BF16),
                        pltpu.VMEM((DENSE_SUB, tt), BF16), pltpu.VMEM((DENSE_SUB, tt), BF16),
                        pltpu.VMEM((n_hs, N_KEYS, tt), BF16)],
        compiler_params=_cparams(("parallel", "arbitrary")),
        name="peer_dense",
    )(ht, u_bf, vt_bf, by_first_key(first), by_first_key(first), second, x, mod)


def _final_norm_kernel(x_ref, gain_ref, o_ref):
    o_ref[...] = _rms(x_ref[...]) * gain_ref[...]


def _final_norm(x, gain, *, tt, row0, n_rows):
    return pl.pallas_call(
        _final_norm_kernel,
        grid=(n_rows // tt,),
        in_specs=[pl.BlockSpec((tt, D_MODEL), lambda i: (row0 // tt + i, 0)),
                  pl.BlockSpec((1, D_MODEL), lambda i: (0, 0))],
        out_specs=pl.BlockSpec((tt, D_MODEL), lambda i: (i, 0)),
        out_shape=jax.ShapeDtypeStruct((n_rows, D_MODEL), F32),
        compiler_params=_cparams(("parallel",)),
        name="final_norm",
    )(x, gain)


PROMPT_ROWS = 256
STEP_ROWS = 128
S5_PROMPT_STEPS = 128


def _block_diag_in(bbar):
    rows_group = np.arange(S5_WIDTH) // S5_CH
    cols_group = np.arange(S5_STATE) // S5_P
    mask = jnp.asarray(rows_group[:, None] == cols_group[None, :])
    reps = (1,) * (bbar.ndim - 2) + (S5_GROUPS, 1)
    return jnp.where(mask, jnp.tile(bbar, reps), 0.0)


def _diag_tiles(w, rows, cols):
    return jnp.stack([w[:, t * rows:(t + 1) * rows, t * cols:(t + 1) * cols]
                      for t in range(w.shape[1] // rows)], axis=1)


def _block_diag_out(c):
    rows_group = np.arange(S5_STATE) // S5_P
    cols_group = np.arange(S5_WIDTH) // S5_CH
    mask = jnp.asarray(rows_group[:, None] == cols_group[None, :])
    per_state = c.transpose(0, 1, 3, 2).reshape(c.shape[0], S5_STATE, S5_CH)
    return jnp.where(mask, jnp.tile(per_state, (1, 1, S5_GROUPS)), 0.0)


def kernel(x_prompt, x_sample, state_gla, state_mlstm_c, state_mlstm_n, state_mlstm_m, state_s5_re, state_s5_im, c_prompt, c_sample, w_ada, b_ada, norm_gain, w_in, gla_w_gate_up, gla_b_gate, gla_norm_gain, ml_b_igate, ml_b_fgate, ml_norm_gain, s5_lambda_re, s5_lambda_im, s5_log_dt, s5_b_re, s5_b_im, s5_c_re, s5_c_im, s5_d, s5_w_glu, s5_b_glu, s5_norm_gain, w_out, peer_w_q, peer_sub_keys, peer_u, peer_v, final_gain):
    bp, lp, _ = x_prompt.shape
    bs, ls, _ = x_sample.shape
    assert ls == TOKENS_PER_GROUP and lp % PROMPT_ROWS == 0 and (bs * ls) % STEP_ROWS == 0
    n_prompt = bp * lp
    n_sample = bs * ls
    n_tok = n_prompt + n_sample
    rep = lp // TOKENS_PER_GROUP

    x = jnp.concatenate([x_prompt.reshape(n_prompt, D_MODEL), x_sample.reshape(n_sample, D_MODEL)], axis=0)

    c_all = jnp.concatenate([c_prompt, c_sample], axis=0)
    n_c = c_all.shape[0]
    c_pad = (-n_c) % SUBLANE
    mod = _adaln(jnp.pad(c_all, ((0, c_pad), (0, 0))), w_ada, b_ada)
    mod = jnp.concatenate([jnp.repeat(mod[:, :bp], rep, axis=1), mod[:, bp:n_c]], axis=1)
    gains = norm_gain.reshape(DEPTH * 2, 1, D_MODEL)

    zeros = lambda *shape: jnp.zeros(shape, F32)
    seq_blocks = lp // PROMPT_ROWS
    step_blocks = n_sample // STEP_ROWS
    step_block0 = n_prompt // STEP_ROWS
    prompt_pad = (-bp) % SUBLANE

    w_in_p = _pad_z_columns(w_in.astype(BF16))
    wg_all = jnp.zeros((DEPTH, GLA_HEADS, LANE, LANE), F32).at[:, :, :GLA_RANK, :GLA_DK].set(
        gla_w_gate_up.reshape(DEPTH, GLA_RANK, GLA_HEADS, GLA_DK).transpose(0, 2, 1, 3)).astype(BF16)
    bg_all = jnp.zeros((DEPTH, GLA_HEADS, 1, LANE), F32).at[:, :, 0, :GLA_DK].set(
        gla_b_gate.reshape(DEPTH, GLA_HEADS, GLA_DK))
    gla_gain_all = gla_norm_gain.reshape(DEPTH, GLA_HEADS, 1, GLA_DV)
    bif_all = jnp.zeros((DEPTH, 1, LANE), F32).at[:, 0, :ML_HEADS].set(ml_b_igate).at[
        :, 0, F_GATE_LANE:F_GATE_LANE + ML_HEADS].set(ml_b_fgate)
    ml_gain_all = ml_norm_gain.reshape(DEPTH, ML_HEADS, 1, ML_DH)
    a_disc, bbar = _s5_discretise(s5_lambda_re, s5_lambda_im, s5_log_dt, s5_b_re, s5_b_im)
    wb_all = jnp.concatenate([_diag_tiles(_block_diag_in(bbar[:, 0]), LANE, S5_TILE_STATE),
                              _diag_tiles(_block_diag_in(bbar[:, 1]), LANE, S5_TILE_STATE)], axis=3).astype(BF16)
    wc_all = jnp.concatenate([_diag_tiles(_block_diag_out(s5_c_re), S5_TILE_STATE, LANE),
                              -_diag_tiles(_block_diag_out(s5_c_im), S5_TILE_STATE, LANE)], axis=2).astype(BF16)
    wglu_all = s5_w_glu.astype(BF16)
    w_out_bf = w_out.astype(BF16)
    wqt_all = peer_w_q.astype(BF16).transpose(0, 2, 1)
    keys_all = peer_sub_keys.reshape(DEPTH, 2 * PEER_HEADS, N_KEYS, LANE).astype(BF16)
    u_bf = peer_u.astype(BF16)
    vt_bf = peer_v.astype(BF16).transpose(0, 2, 1)

    st_gla = state_gla.astype(F32)
    st_c = state_mlstm_c.astype(F32)
    st_n = state_mlstm_n.astype(F32)[:, :, :, None, :]
    st_m = jnp.broadcast_to(state_mlstm_m.astype(F32)[:, :, :, None, None], state_mlstm_m.shape + (1, LANE))
    st_s5 = jnp.concatenate([state_s5_re.reshape(DEPTH, bs, S5_STATE), state_s5_im.reshape(DEPTH, bs, S5_STATE)],
                            axis=2).astype(F32)

    sg_s = mc_s = mn_s = mm_s = None
    new_states = []
    for l in range(DEPTH):
        z = _norm_proj(x, mod, gains, w_in_p, layer=l, tt=1024, tn=1152)

        seq_kw = dict(n_seq=bp, seq_blocks=seq_blocks, rows=PROMPT_ROWS, row_block0=0, carry=True)
        step_kw = dict(n_seq=step_blocks, seq_blocks=1, rows=STEP_ROWS, chunk=ls, row_block0=step_block0,
                       carry=False, layer=l)

        gla_w = (wg_all[l], bg_all[l], gla_gain_all[l])
        og, sg_p = _gla(z, *gla_w, zeros(bp, GLA_HEADS, GLA_DK, GLA_DV), chunk=GLA_CHUNK, **seq_kw)
        og, sg_s = _gla(z, *gla_w, st_gla, fill=(og,) if l == 0 else (og, sg_s), **step_kw)

        ml_w = (bif_all[l], ml_gain_all[l])
        om, mc_p, mn_p, mm_p = _mlstm(z, *ml_w, zeros(bp, ML_HEADS, ML_DH, ML_DH), zeros(bp, ML_HEADS, 1, ML_DH),
                                      zeros(bp, ML_HEADS, 1, LANE), chunk=ML_CHUNK, **seq_kw)
        om, mc_s, mn_s, mm_s = _mlstm(z, *ml_w, st_c, st_n, st_m,
                                      fill=(om,) if l == 0 else (om, mc_s, mn_s, mm_s), **step_kw)

        su = z[:, ZB_SU * LANE:ZB_SU * LANE + S5_WIDTH]
        u_p = jnp.pad(su[:n_prompt].reshape(bp, lp, S5_WIDTH).transpose(1, 0, 2), ((0, 0), (0, prompt_pad), (0, 0)))
        u_s = su[n_prompt:].reshape(bs, ls, S5_WIDTH).transpose(1, 0, 2)
        s5_args = (a_disc[l], wb_all[l], wc_all[l], s5_d[l].reshape(1, S5_WIDTH), wglu_all[l],
                   s5_b_glu[l][None, :], s5_norm_gain[l][None, :])
        os_p, hs_p = _s5(u_p, zeros(bp + prompt_pad, 2 * S5_STATE), *s5_args,
                         steps=S5_PROMPT_STEPS, lane_width=S5_STATE)
        os_s, hs_s = _s5(u_s, st_s5[l], *s5_args, steps=ls, lane_width=LANE)
        os_ = jnp.concatenate([os_p[:, :bp].transpose(1, 0, 2).reshape(n_prompt, S5_WIDTH),
                               os_s.transpose(1, 0, 2).reshape(n_sample, S5_WIDTH)], axis=0)

        x, ht = _out_proj(og, om, os_, x, mod, gains, w_out_bf, layer=l, tt=256)

        first, second = _peer_route(ht, wqt_all[l], keys_all[l], tt=256)
        x = _peer_dense(ht, u_bf, vt_bf, first, second, x, mod, layer=l, tt=512)

        new_states.append((sg_p, mc_p, mn_p[:, :, 0, :], mm_p[:, :, 0, 0], hs_p[:bp], hs_s))

    y_prompt = _final_norm(x, final_gain[None, :], tt=512, row0=0, n_rows=n_prompt)
    y_sample = _final_norm(x, final_gain[None, :], tt=512, row0=n_prompt, n_rows=n_sample)
    stack = lambda i: jnp.stack([ns[i] for ns in new_states])
    s5_p, s5_s = stack(4), stack(5)
    split_s5 = lambda h, lo: h[:, :, lo:lo + S5_STATE].reshape(DEPTH, -1, S5_GROUPS, S5_P)
    outs = (y_prompt.reshape(bp, lp, D_MODEL), y_sample.reshape(bs, ls, D_MODEL),
            stack(0), sg_s, stack(1), mc_s, stack(2), mn_s[:, :, :, 0, :], stack(3), mm_s[:, :, :, 0, 0],
            split_s5(s5_p, 0), split_s5(s5_s, 0), split_s5(s5_p, S5_STATE), split_s5(s5_s, S5_STATE))
    refs = (x_prompt, x_sample, state_gla, state_gla, state_mlstm_c, state_mlstm_c, state_mlstm_n, state_mlstm_n,
            state_mlstm_m, state_mlstm_m, state_s5_re, state_s5_re, state_s5_im, state_s5_im)
    return tuple(o.astype(r.dtype) for o, r in zip(outs, refs))
```

```python
import functools
import math

import numpy as np
import jax
import jax.numpy as jnp
from jax import lax
from jax.experimental import pallas as pl
from jax.experimental.pallas import tpu as pltpu

F32 = jnp.float32
BF16 = jnp.bfloat16
HIGHEST = lax.Precision.HIGHEST

D_MODEL = 2048
DEPTH = 4
GLA_HEADS = 6
GLA_DK = 64
GLA_DV = 128
GLA_RANK = 16
GLA_TAU = 16.0
GLA_CHUNK = 16
ML_HEADS = 6
ML_DH = 128
ML_CHUNK = 64
S5_WIDTH = 512
S5_CH = 16
S5_GROUPS = 32
S5_P = 64
S5_STATE = S5_GROUPS * S5_P
S5_TILES = S5_WIDTH // 128
S5_TILE_STATE = S5_STATE // S5_TILES
PEER_HEADS = 8
N_KEYS = 128
N_EXPERTS = N_KEYS * N_KEYS
PEER_TOPK = 16
NORM_EPS = 1e-6
GLA_WIDTH = GLA_HEADS * GLA_DV
ML_WIDTH = ML_HEADS * ML_DH

LANE = 128
SUBLANE = 8
TOKENS_PER_GROUP = 8
GROUP_UNROLL = 4
VMEM_LIMIT = 56 * 1024 * 1024

ZB_GQ, ZB_GK, ZB_GV, ZB_GG = 0, 6, 12, 18
ZB_MQ, ZB_MK, ZB_MV, ZB_MO = 24, 30, 36, 42
ZB_GR, ZB_MIF = 48, 49
HEADS_PER_STEP = 6
ZB_SU = 50
Z_BLOCKS = 54
Z_COLS = Z_BLOCKS * LANE
F_GATE_LANE = 8


def _z_source_columns():
    src = np.full((Z_COLS,), -1, np.int32)
    off_gq, off_gk, off_gv, off_gg, off_gr = 0, 384, 768, 1536, 2304
    off_mq, off_mk, off_mv, off_mo, off_mi, off_mf, off_su = 2320, 3088, 3856, 4624, 5392, 5398, 5404
    for h in range(GLA_HEADS):
        for d in range(GLA_DK):
            src[(ZB_GQ + h) * LANE + d] = off_gq + h * GLA_DK + d
            src[(ZB_GK + h) * LANE + d] = off_gk + h * GLA_DK + d
        for d in range(GLA_DV):
            src[(ZB_GV + h) * LANE + d] = off_gv + h * GLA_DV + d
            src[(ZB_GG + h) * LANE + d] = off_gg + h * GLA_DV + d
    for d in range(GLA_RANK):
        src[ZB_GR * LANE + d] = off_gr + d
    for h in range(ML_HEADS):
        for d in range(ML_DH):
            src[(ZB_MQ + h) * LANE + d] = off_mq + h * ML_DH + d
            src[(ZB_MK + h) * LANE + d] = off_mk + h * ML_DH + d
            src[(ZB_MV + h) * LANE + d] = off_mv + h * ML_DH + d
            src[(ZB_MO + h) * LANE + d] = off_mo + h * ML_DH + d
        src[ZB_MIF * LANE + h] = off_mi + h
        src[ZB_MIF * LANE + F_GATE_LANE + h] = off_mf + h
    for d in range(S5_WIDTH):
        src[ZB_SU * LANE + d] = off_su + d
    return src


_Z_SRC = _z_source_columns()


def _z_runs():
    runs, i = [], 0
    while i < Z_COLS:
        j = i + 1
        if _Z_SRC[i] < 0:
            while j < Z_COLS and _Z_SRC[j] < 0:
                j += 1
            runs.append((-1, j - i))
        else:
            while j < Z_COLS and _Z_SRC[j] == _Z_SRC[j - 1] + 1:
                j += 1
            runs.append((int(_Z_SRC[i]), j - i))
        i = j
    return runs


_Z_RUNS = _z_runs()


def _pad_z_columns(w):
    parts = [jnp.zeros(w.shape[:-1] + (n,), w.dtype) if s < 0 else w[..., s:s + n] for s, n in _Z_RUNS]
    return jnp.concatenate(parts, axis=-1)


def _cparams(semantics):
    return pltpu.CompilerParams(dimension_semantics=semantics, vmem_limit_bytes=VMEM_LIMIT)


def _const_spec(block_shape, index_map):
    return pl.BlockSpec(block_shape, index_map, pipeline_mode=pl.Buffered(1))


def _rms(x):
    return x * lax.rsqrt(jnp.mean(x * x, axis=-1, keepdims=True) + NORM_EPS)


def _unpack_rows(x):
    return pltpu.bitcast(x, BF16)


def _dot(a, b):
    return jnp.dot(a, b, preferred_element_type=F32)


def _dot_nt(a, b):
    return lax.dot_general(a, b, (((1,), (1,)), ((), ())), preferred_element_type=F32)


def _chunk_sums(same, tri, x):
    rows = x.shape[0]
    masks = jnp.concatenate([jnp.where(tri, 1.0, 0.0), jnp.where(same, 1.0, 0.0)], axis=0).astype(BF16)
    hi = x.astype(BF16)
    r1 = x - hi.astype(F32)
    mid = r1.astype(BF16)
    lo = (r1 - mid.astype(F32)).astype(BF16)
    sums = _dot(masks, jnp.concatenate([hi, mid, lo], axis=1))
    total = sums[:, 0:LANE] + sums[:, LANE:2 * LANE] + sums[:, 2 * LANE:3 * LANE]
    return total[0:rows], total[rows:2 * rows]


def _adaln_kernel(c_ref, w_lo_ref, w_hi_ref, b_ref, o_ref):
    c = c_ref[...]
    s = (c * jax.nn.sigmoid(c)).astype(BF16)
    half = D_MODEL // 2
    o_ref[0] = (_dot(s[:, :half], w_lo_ref[0].astype(BF16)) + _dot(s[:, half:], w_hi_ref[0].astype(BF16))
                + b_ref[0])


def _adaln(c_all, w_ada, b_ada):
    n_rows = c_all.shape[0]
    n_out = w_ada.shape[-1]
    tn = 1024
    w_half = lambda which: pl.BlockSpec((1, D_MODEL // 2, tn), lambda l, j: (l, which, j))
    return pl.pallas_call(
        _adaln_kernel,
        grid=(DEPTH, n_out // tn),
        in_specs=[
            pl.BlockSpec((n_rows, D_MODEL), lambda l, j: (0, 0)),
            w_half(0), w_half(1),
            pl.BlockSpec((1, 1, tn), lambda l, j: (l, 0, j)),
        ],
        out_specs=pl.BlockSpec((1, n_rows, tn), lambda l, j: (l, 0, j)),
        out_shape=jax.ShapeDtypeStruct((DEPTH, n_rows, n_out), F32),
        compiler_params=_cparams(("parallel", "parallel")),
        name="adaln",
    )(c_all, w_ada, w_ada, b_ada.reshape(DEPTH, 1, n_out))


def _modulated_group(x8, gain, sc_row, sh_row):
    return _rms(x8) * gain * (1.0 + sc_row) + sh_row


def _norm_proj_kernel(x_ref, sc_ref, sh_ref, gain_ref, w_ref, o_ref, h_scr, *, groups):
    @pl.when(pl.program_id(1) == 0)
    def _():
        gain = gain_ref[...]

        def body(p, carry):
            hs = []
            for u in range(2):
                g = p * 2 + u
                r0 = pl.multiple_of(g * TOKENS_PER_GROUP, TOKENS_PER_GROUP)
                hs.append(_modulated_group(x_ref[pl.ds(r0, TOKENS_PER_GROUP), :], gain,
                                           sc_ref[pl.ds(g, 1), :], sh_ref[pl.ds(g, 1), :]))
            r = pl.multiple_of(p * 2 * TOKENS_PER_GROUP, 2 * TOKENS_PER_GROUP)
            h_scr[pl.ds(r, 2 * TOKENS_PER_GROUP), :] = jnp.concatenate(hs, axis=0).astype(BF16)
            return carry

        lax.fori_loop(0, groups // 2, body, 0, unroll=GROUP_UNROLL)

    o_ref[...] = _dot(h_scr[...], w_ref[...])


MOD_SH1, MOD_SC1, MOD_G1, MOD_SH2, MOD_SC2, MOD_G2 = range(6)


def _mod_spec(groups, layer, kind):
    return pl.BlockSpec((None, groups, D_MODEL), lambda *a: (layer, a[0], kind))


def _gain_spec(layer, which):
    return pl.BlockSpec((None, 1, D_MODEL), lambda *a: (2 * layer + which, 0, 0))


def _norm_proj(x, mod, gains, w, *, layer, tt, tn):
    n_tok = x.shape[0]
    n_out = w.shape[-1]
    groups = tt // TOKENS_PER_GROUP
    return pl.pallas_call(
        functools.partial(_norm_proj_kernel, groups=groups),
        grid=(n_tok // tt, n_out // tn),
        in_specs=[
            pl.BlockSpec((tt, D_MODEL), lambda i, j: (i, 0)),
            _mod_spec(groups, layer, MOD_SC1),
            _mod_spec(groups, layer, MOD_SH1),
            _gain_spec(layer, 0),
            pl.BlockSpec((None, D_MODEL, tn), lambda i, j: (layer, 0, j)),
        ],
        out_specs=pl.BlockSpec((tt, tn), lambda i, j: (i, j)),
        out_shape=jax.ShapeDtypeStruct((n_tok, n_out), F32),
        scratch_shapes=[pltpu.VMEM((tt, D_MODEL), BF16)],
        compiler_params=_cparams(("parallel", "arbitrary")),
        name="norm_proj",
    )(x, mod, mod, gains, w)


def _chunk_masks(rows, chunk):
    shift = int(math.log2(chunk))
    ri = lax.broadcasted_iota(jnp.int32, (rows, rows), 0)
    ci = lax.broadcasted_iota(jnp.int32, (rows, rows), 1)
    same = (ri >> shift) == (ci >> shift)
    tri = jnp.logical_and(same, ci <= ri)
    return same, tri


def _gla_kernel(q_ref, k_ref, v_ref, g_ref, r_ref, wg_ref, bg_ref, gain_ref, s0_ref,
                o_ref, sfin_ref, st_scr, **kw):
    carry = kw["carry"]
    heads = range(HEADS_PER_STEP)
    if carry:
        @pl.when(pl.program_id(2) == 0)
        def _():
            for hh in heads:
                st_scr[hh] = _gla_load_state(s0_ref.at[:, hh:hh + 1], 0)
    for hh in heads:
        ln = slice(hh * LANE, (hh + 1) * LANE)
        one = slice(hh, hh + 1)
        _gla_head(q_ref.at[:, ln], k_ref.at[:, ln], v_ref.at[:, ln], g_ref.at[:, ln], r_ref,
                  wg_ref.at[one], bg_ref.at[one], gain_ref.at[one], s0_ref.at[:, one],
                  o_ref.at[:, ln], sfin_ref.at[:, one], st_scr.at[hh], **kw)
    if carry:
        @pl.when(pl.program_id(2) == pl.num_programs(2) - 1)
        def _():
            for hh in heads:
                sfin_ref[0, hh] = st_scr[hh].T[0:GLA_DK, :]


def _gla_load_state(s0_ref, c):
    zero_pad = jnp.zeros((LANE - GLA_DK, GLA_DV), F32)
    return jnp.concatenate([s0_ref[c, 0], zero_pad], axis=0).T


def _gla_head(q_ref, k_ref, v_ref, g_ref, r_ref, wg_ref, bg_ref, gain_ref, s0_ref,
              o_ref, sfin_ref, st_scr, *, rows, chunk, carry):
    n_chunks = rows // chunk
    shift = int(math.log2(chunk))
    same, tri = _chunk_masks(rows, chunk)

    q = q_ref[...] * (GLA_DK ** -0.5)
    k = k_ref[...]
    v = v_ref[...]
    la = jax.nn.log_sigmoid(_dot(r_ref[...].astype(BF16), wg_ref[0]) + bg_ref[0]) / GLA_TAU
    bl, bt = _chunk_sums(same, tri, la)
    qd = (q * jnp.exp(bl)).astype(BF16)
    ki = (k * jnp.exp(-bl)).astype(BF16)
    kd = (k * jnp.exp(bt - bl)).astype(BF16)
    att = jnp.where(tri, _dot_nt(qd, ki), 0.0)
    o_intra = _dot(att.astype(BF16), v.astype(BF16))

    v_t = v.T
    lane_chunk = lax.broadcasted_iota(jnp.int32, (GLA_DV, rows), 1) >> shift
    if carry:
        s_t = st_scr[...]

    sliced = chunk % (2 * SUBLANE) == 0
    row_chunk = lax.broadcasted_iota(jnp.int32, (rows, GLA_DV), 0) >> shift
    o_inter = [] if sliced else jnp.zeros((rows, GLA_DV), F32)
    for c in range(n_chunks):
        lo = c * chunk
        if not carry:
            s_t = _gla_load_state(s0_ref, c)
        if sliced:
            o_inter.append(_dot_nt(qd[lo:lo + chunk], s_t.astype(BF16)))
        else:
            o_inter = jnp.where(row_chunk == c, _dot_nt(qd, s_t.astype(BF16)), o_inter)
        decay = jnp.exp(bt[lo:lo + 1, :])
        v_c = jnp.where(lane_chunk == c, v_t, 0.0).astype(BF16)
        s_t = s_t * decay + _dot(v_c, kd)
        if not carry:
            sfin_ref[c, 0] = s_t.T[0:GLA_DK, :]

    if carry:
        st_scr[...] = s_t

    o = o_intra + (jnp.concatenate(o_inter, axis=0) if sliced else o_inter)
    g = g_ref[...]
    o_ref[...] = _rms(o) * gain_ref[0] * (g * jax.nn.sigmoid(g))


def _without_refs(kernel_fn, first, count):
    def wrapped(*refs):
        return kernel_fn(*refs[:first], *refs[first + count:])
    return wrapped


def _mixer_grid(n_seq, n_heads, seq_blocks, row_block0, carry):
    if carry:
        return ((n_seq, n_heads, seq_blocks), lambda b, h, t: row_block0 + b * seq_blocks + t,
                ("parallel", "parallel", "arbitrary"))
    return (n_seq, n_heads), lambda b, h: row_block0 + b, ("parallel", "parallel")


def _state_spec(per_block, d2, d3, layer):
    if layer is None:
        return pl.BlockSpec((per_block, HEADS_PER_STEP, d2, d3), lambda *a: (a[0], a[1], 0, 0))
    return pl.BlockSpec((None, per_block, HEADS_PER_STEP, d2, d3), lambda *a: (layer, a[0], a[1], 0, 0))


def _head_cols_spec(rows, rb, block0):
    assert block0 % HEADS_PER_STEP == 0
    return pl.BlockSpec((rows, HEADS_PER_STEP * LANE), lambda *a: (rb(*a), block0 // HEADS_PER_STEP + a[1]))


def _head_param_spec(shape):
    return pl.BlockSpec((HEADS_PER_STEP,) + shape, lambda *a: (a[1],) + (0,) * len(shape))


def _gla(z, wg, bg, gain, s0, *, n_seq, seq_blocks, rows, chunk, row_block0, carry,
         layer=None, fill=()):
    per_block = 1 if carry else rows // chunk
    grid, rb, sem = _mixer_grid(n_seq, GLA_HEADS // HEADS_PER_STEP, seq_blocks, row_block0, carry)
    zspec = functools.partial(_head_cols_spec, rows, rb)
    state_spec = _state_spec(per_block, GLA_DK, GLA_DV, layer)
    n_in = 9
    return pl.pallas_call(
        _without_refs(functools.partial(_gla_kernel, rows=rows, chunk=chunk, carry=carry), n_in, len(fill)),
        grid=grid,
        in_specs=[zspec(ZB_GQ), zspec(ZB_GK), zspec(ZB_GV), zspec(ZB_GG),
                  pl.BlockSpec((rows, LANE), lambda *a: (rb(*a), ZB_GR)),
                  _head_param_spec((LANE, LANE)), _head_param_spec((1, LANE)), _head_param_spec((1, LANE)),
                  state_spec]
                 + [pl.BlockSpec(memory_space=pl.ANY)] * len(fill),
        out_specs=[_head_cols_spec(rows, rb, 0), state_spec],
        out_shape=[jax.ShapeDtypeStruct((z.shape[0], GLA_WIDTH), F32),
                   jax.ShapeDtypeStruct(s0.shape, F32)],
        input_output_aliases={n_in + i: i for i in range(len(fill))},
        scratch_shapes=[pltpu.VMEM((HEADS_PER_STEP, GLA_DV, LANE), F32)],
        compiler_params=_cparams(sem),
        name="gla_seq" if carry else "gla_step",
    )(z, z, z, z, z, wg, bg, gain, s0, *fill)


def _mlstm_kernel(q_ref, k_ref, v_ref, og_ref, gate_ref, bif_ref, gain_ref, c0_ref, n0_ref, m0_ref,
                  h_ref, cfin_ref, nfin_ref, mfin_ref, c_scr, n_scr, m_scr, **kw):
    carry = kw["carry"]
    heads = range(HEADS_PER_STEP)
    if carry:
        @pl.when(pl.program_id(2) == 0)
        def _():
            for hh in heads:
                c_scr[hh] = c0_ref[0, hh]
                n_scr[hh] = n0_ref[0, hh]
                m_scr[hh] = m0_ref[0, hh]
    _mlstm_heads(q_ref, k_ref, v_ref, og_ref, gate_ref, bif_ref, gain_ref, c0_ref, n0_ref, m0_ref,
                 h_ref, cfin_ref, nfin_ref, mfin_ref, c_scr, n_scr, m_scr, **kw)
    if carry:
        @pl.when(pl.program_id(2) == pl.num_programs(2) - 1)
        def _():
            for hh in heads:
                cfin_ref[0, hh] = c_scr[hh]
                nfin_ref[0, hh] = n_scr[hh]
                mfin_ref[0, hh] = m_scr[hh]


def _mlstm_heads(q_ref, k_ref, v_ref, og_ref, gate_ref, bif_ref, gain_ref, c0_ref, n0_ref, m0_ref,
                 h_ref, cfin_ref, nfin_ref, mfin_ref, c_scr, n_scr, m_scr, **kw):
    for hh in range(HEADS_PER_STEP):
        ln = slice(hh * LANE, (hh + 1) * LANE)
        one = slice(hh, hh + 1)
        _mlstm_head(pl.program_id(1) * HEADS_PER_STEP + hh,
                    q_ref.at[:, ln], k_ref.at[:, ln], v_ref.at[:, ln], og_ref.at[:, ln], gate_ref, bif_ref,
                    gain_ref.at[one], c0_ref.at[:, one], n0_ref.at[:, one], m0_ref.at[:, one],
                    h_ref.at[:, ln], cfin_ref.at[:, one], nfin_ref.at[:, one], mfin_ref.at[:, one],
                    c_scr.at[hh], n_scr.at[hh], m_scr.at[hh], **kw)


def _mlstm_head(head, q_ref, k_ref, v_ref, og_ref, gate_ref, bif_ref, gain_ref, c0_ref, n0_ref, m0_ref,
                h_ref, cfin_ref, nfin_ref, mfin_ref, c_scr, n_scr, m_scr, *, rows, chunk, carry):
    n_chunks = rows // chunk
    shift = int(math.log2(chunk))
    same, tri = _chunk_masks(rows, chunk)
    neg_inf = jnp.float32(-jnp.inf)

    x = gate_ref[...] + bif_ref[...]
    log_f = jax.nn.log_sigmoid(x)
    b_cum, b_tot = _chunk_sums(same, tri, log_f)
    lane = lax.broadcasted_iota(jnp.int32, (rows, LANE), 1)
    y = jnp.where(lane < F_GATE_LANE, x, b_cum)
    pick_i = lane == head
    pick_b = lane == head + F_GATE_LANE
    i_col = jnp.sum(jnp.where(pick_i, y, 0.0), axis=-1, keepdims=True)
    b_col = jnp.sum(jnp.where(pick_b, y, 0.0), axis=-1, keepdims=True)
    bt_col = jnp.sum(jnp.where(pick_b, b_tot, 0.0), axis=-1, keepdims=True)
    sub = lax.broadcasted_iota(jnp.int32, (LANE, rows), 0)
    y_t = y.T
    i_row = jnp.sum(jnp.where(sub == head, y_t, 0.0), axis=0, keepdims=True)
    b_row = jnp.sum(jnp.where(sub == head + F_GATE_LANE, y_t, 0.0), axis=0, keepdims=True)

    a_col = bt_col - b_col + i_col
    mloc_col = bt_col + jnp.max(jnp.where(same, i_row - b_row, neg_inf), axis=-1, keepdims=True)

    q = q_ref[...] * (ML_DH ** -0.5)
    k = k_ref[...]
    qb = q.astype(BF16)
    kb = k.astype(BF16)
    vb = v_ref[...].astype(BF16)
    kw = k * jnp.exp(a_col - mloc_col)
    kw_t = kw.T
    lane_chunk = lax.broadcasted_iota(jnp.int32, (ML_DH, rows), 1) >> shift

    def advance(c, c_st, n_st, m_st, m_last):
        hi = (c + 1) * chunk
        decay = jnp.exp(bt_col[hi - 1:hi] + m_st - m_last)
        scale = jnp.exp(mloc_col[hi - 1:hi] - m_last)
        kw_c = jnp.where(lane_chunk == c, kw_t, 0.0).astype(BF16)
        c_new = decay * c_st + scale * _dot(kw_c, vb)
        n_new = decay * n_st + scale * jnp.sum(kw[hi - chunk:hi], axis=0, keepdims=True)
        return c_new, n_new

    if carry:
        c_st = c_scr[...]
        n_st = n_scr[...]
        m_st = m_scr[:, 0:1]
        ri = lax.broadcasted_iota(jnp.int32, (chunk, chunk), 0)
        ci = lax.broadcasted_iota(jnp.int32, (chunk, chunk), 1)
        tri_c = ci <= ri
        h_chunks = []
        for c in range(n_chunks):
            lo = c * chunk
            hi = lo + chunk
            bc = b_col[lo:hi]
            d_log = jnp.where(tri_c, bc - b_row[:, lo:hi] + i_row[:, lo:hi], neg_inf)
            g_inter = bc + m_st
            m_t = jnp.maximum(g_inter, jnp.max(d_log, axis=-1, keepdims=True))
            w_inter = jnp.exp(g_inter - m_t)
            s = _dot_nt(qb[lo:hi], kb[lo:hi]) * jnp.exp(d_log - m_t)
            num = w_inter * _dot(qb[lo:hi], c_st.astype(BF16)) + _dot(s.astype(BF16), vb[lo:hi])
            den = (w_inter * jnp.sum(q[lo:hi] * n_st, axis=-1, keepdims=True)
                   + jnp.sum(s, axis=-1, keepdims=True))
            h_chunks.append(num / jnp.maximum(jnp.abs(den), jnp.exp(-m_t)))
            m_last = m_t[chunk - 1:chunk]
            c_st, n_st = advance(c, c_st, n_st, m_st, m_last)
            m_st = m_last
        h = jnp.concatenate(h_chunks, axis=0)
        c_scr[...] = c_st
        n_scr[...] = n_st
        m_scr[...] = jnp.broadcast_to(m_st, (1, LANE))
    else:
        row_chunk1 = lax.broadcasted_iota(jnp.int32, (rows, 1), 0) >> shift
        row_chunk = lax.broadcasted_iota(jnp.int32, (rows, ML_DH), 0) >> shift
        m_rows = jnp.zeros((rows, 1), F32)
        n_rows = jnp.zeros((rows, ML_DH), F32)
        qc = jnp.zeros((rows, ML_DH), F32)
        for c in range(n_chunks):
            m_rows = jnp.where(row_chunk1 == c, m0_ref[c, 0][:, 0:1], m_rows)
            n_rows = jnp.where(row_chunk == c, n0_ref[c, 0], n_rows)
            qc = jnp.where(row_chunk == c, _dot(qb, c0_ref[c, 0].astype(BF16)), qc)
        d_log = jnp.where(tri, b_col - b_row + i_row, neg_inf)
        g_inter = b_col + m_rows
        m_t = jnp.maximum(g_inter, jnp.max(d_log, axis=-1, keepdims=True))
        w_inter = jnp.exp(g_inter - m_t)
        s = _dot_nt(qb, kb) * jnp.exp(d_log - m_t)
        num = w_inter * qc + _dot(s.astype(BF16), vb)
        den = w_inter * jnp.sum(q * n_rows, axis=-1, keepdims=True) + jnp.sum(s, axis=-1, keepdims=True)
        h = num / jnp.maximum(jnp.abs(den), jnp.exp(-m_t))
        for c in range(n_chunks):
            hi = (c + 1) * chunk
            m_last = m_t[hi - 1:hi]
            c_new, n_new = advance(c, c0_ref[c, 0], n0_ref[c, 0], m0_ref[c, 0][:, 0:1], m_last)
            cfin_ref[c, 0] = c_new
            nfin_ref[c, 0] = n_new
            mfin_ref[c, 0] = jnp.broadcast_to(m_last, (1, LANE))

    h_ref[...] = _rms(h) * gain_ref[0] * jax.nn.sigmoid(og_ref[...])


def _mlstm(z, bif, gain, c0, n0, m0, *, n_seq, seq_blocks, rows, chunk, row_block0, carry,
           layer=None, fill=()):
    per_block = 1 if carry else rows // chunk
    grid, rb, sem = _mixer_grid(n_seq, ML_HEADS // HEADS_PER_STEP, seq_blocks, row_block0, carry)
    zspec = functools.partial(_head_cols_spec, rows, rb)
    state_specs = [_state_spec(per_block, ML_DH, ML_DH, layer), _state_spec(per_block, 1, ML_DH, layer),
                   _state_spec(per_block, 1, LANE, layer)]
    n_in = 10
    return pl.pallas_call(
        _without_refs(functools.partial(_mlstm_kernel, rows=rows, chunk=chunk, carry=carry), n_in, len(fill)),
        grid=grid,
        in_specs=[zspec(ZB_MQ), zspec(ZB_MK), zspec(ZB_MV), zspec(ZB_MO),
                  pl.BlockSpec((rows, LANE), lambda *a: (rb(*a), ZB_MIF)),
                  pl.BlockSpec((1, LANE), lambda *a: (0, 0)),
                  _head_param_spec((1, LANE))]
                 + state_specs + [pl.BlockSpec(memory_space=pl.ANY)] * len(fill),
        out_specs=[_head_cols_spec(rows, rb, 0)] + state_specs,
        out_shape=[jax.ShapeDtypeStruct((z.shape[0], ML_WIDTH), F32),
                   jax.ShapeDtypeStruct(c0.shape, F32), jax.ShapeDtypeStruct(n0.shape, F32),
                   jax.ShapeDtypeStruct(m0.shape, F32)],
        input_output_aliases={n_in + i: i for i in range(len(fill))},
        scratch_shapes=[pltpu.VMEM((HEADS_PER_STEP, ML_DH, ML_DH), F32), pltpu.VMEM((HEADS_PER_STEP, 1, ML_DH), F32),
                        pltpu.VMEM((HEADS_PER_STEP, 1, LANE), F32)],
        compiler_params=_cparams(sem),
        name="mlstm_seq" if carry else "mlstm_step",
    )(z, z, z, z, z, bif, gain, c0, n0, m0, *fill)


def _s5_disc_kernel(lre_ref, lim_ref, ldt_ref, bre_ref, bim_ref, a_ref, bb_ref):
    lam_re = lre_ref[0]
    lam_im = lim_ref[0]
    dt = jnp.exp(ldt_ref[0])
    mag = jnp.exp(lam_re * dt)
    ar = mag * jnp.cos(lam_im * dt)
    ai = mag * jnp.sin(lam_im * dt)
    den = lam_re * lam_re + lam_im * lam_im
    fr = ((ar - 1.0) * lam_re + ai * lam_im) / den
    fi = (ai * lam_re - (ar - 1.0) * lam_im) / den
    b_re = bre_ref[0]
    b_im = bim_ref[0]
    a_ref[0, 0:1, :] = ar
    a_ref[0, 1:2, :] = ai
    bb_ref[0, 0] = fr * b_re - fi * b_im
    bb_ref[0, 1] = fr * b_im + fi * b_re


def _s5_discretise(lam_re, lam_im, log_dt, b_re, b_im):
    flat = lambda t: t.reshape(DEPTH, 1, S5_STATE)
    ldt = jnp.broadcast_to(log_dt[:, :, None], (DEPTH, S5_GROUPS, S5_P))
    chan = lambda t: t.reshape(DEPTH, S5_STATE, S5_CH).transpose(0, 2, 1)
    row = pl.BlockSpec((1, 1, S5_STATE), lambda l: (l, 0, 0))
    mat = pl.BlockSpec((1, S5_CH, S5_STATE), lambda l: (l, 0, 0))
    return pl.pallas_call(
        _s5_disc_kernel,
        grid=(DEPTH,),
        in_specs=[row, row, row, mat, mat],
        out_specs=[pl.BlockSpec((1, 2, S5_STATE), lambda l: (l, 0, 0)),
                   pl.BlockSpec((1, 2, S5_CH, S5_STATE), lambda l: (l, 0, 0, 0))],
        out_shape=[jax.ShapeDtypeStruct((DEPTH, 2, S5_STATE), F32),
                   jax.ShapeDtypeStruct((DEPTH, 2, S5_CH, S5_STATE), F32)],
        compiler_params=_cparams(("parallel",)),
        name="s5_disc",
    )(flat(lam_re), flat(lam_im), flat(ldt), chan(b_re), chan(b_im))


def _s5_kernel(u_ref, h0_ref, a_ref, wb_ref, wc_ref, d_ref, wglu_ref, bglu_ref, gain_ref,
               y_ref, hfin_ref, bu_scr, h_scr, *, steps, batch, lane_width):
    n_rows = steps * batch

    @pl.when(pl.program_id(0) == 0)
    def _():
        h_scr[...] = h0_ref[...]

    u = u_ref[...].reshape(n_rows, S5_WIDTH)
    ub = u.astype(BF16)
    for t in range(S5_TILES):
        bu = _dot(ub[:, t * LANE:(t + 1) * LANE], wb_ref[t])
        re0 = t * S5_TILE_STATE
        bu_scr[:, :, re0:re0 + S5_TILE_STATE] = bu[:, :S5_TILE_STATE].reshape(steps, batch, S5_TILE_STATE)
        bu_scr[:, :, S5_STATE + re0:S5_STATE + re0 + S5_TILE_STATE] = (
            bu[:, S5_TILE_STATE:].reshape(steps, batch, S5_TILE_STATE))

    for j in range(S5_STATE // lane_width):
        re = slice(j * lane_width, (j + 1) * lane_width)
        im = slice(S5_STATE + j * lane_width, S5_STATE + (j + 1) * lane_width)
        ar = a_ref[0:1, re]
        ai = a_ref[1:2, re]

        def step(t, carry):
            hr, hi = carry
            nr = ar * hr - ai * hi + bu_scr[t, :, re]
            ni = ar * hi + ai * hr + bu_scr[t, :, im]
            bu_scr[t, :, re] = nr
            bu_scr[t, :, im] = ni
            return nr, ni

        hr, hi = lax.fori_loop(0, steps, step, (h_scr[:, re], h_scr[:, im]), unroll=min(steps, 8))
        h_scr[:, re] = hr
        h_scr[:, im] = hi

    ys = []
    for t in range(S5_TILES):
        re0 = t * S5_TILE_STATE
        h_re = bu_scr[:, :, re0:re0 + S5_TILE_STATE].reshape(n_rows, S5_TILE_STATE).astype(BF16)
        h_im = bu_scr[:, :, S5_STATE + re0:S5_STATE + re0 + S5_TILE_STATE].reshape(n_rows, S5_TILE_STATE).astype(BF16)
        ys.append(_dot(h_re, wc_ref[t, :S5_TILE_STATE]) + _dot(h_im, wc_ref[t, S5_TILE_STATE:]))
    y = jnp.concatenate(ys, axis=1) + d_ref[...] * u
    y = jax.nn.gelu(y)
    out = y * jax.nn.sigmoid(_dot(y.astype(BF16), wglu_ref[...]) + bglu_ref[...])
    y_ref[...] = (_rms(out) * gain_ref[...]).reshape(steps, batch, S5_WIDTH)

    @pl.when(pl.program_id(0) == pl.num_programs(0) - 1)
    def _():
        hfin_ref[...] = h_scr[...]


def _s5(u_t, h0, a, wb, wc, d, wglu, bglu, gain, *, steps, lane_width):
    seq, batch, _ = u_t.shape
    const = lambda shape: _const_spec(shape, lambda t: (0,) * len(shape))
    return pl.pallas_call(
        functools.partial(_s5_kernel, steps=steps, batch=batch, lane_width=lane_width),
        grid=(seq // steps,),
        in_specs=[pl.BlockSpec((steps, batch, S5_WIDTH), lambda t: (t, 0, 0)),
                  const((batch, 2 * S5_STATE)), const((2, S5_STATE)),
                  const((S5_TILES, LANE, 2 * S5_TILE_STATE)), const((S5_TILES, 2 * S5_TILE_STATE, LANE)),
                  const((1, S5_WIDTH)), const((S5_WIDTH, S5_WIDTH)), const((1, S5_WIDTH)),
                  const((1, S5_WIDTH))],
        out_specs=[pl.BlockSpec((steps, batch, S5_WIDTH), lambda t: (t, 0, 0)),
                   pl.BlockSpec((batch, 2 * S5_STATE), lambda t: (0, 0))],
        out_shape=[jax.ShapeDtypeStruct((seq, batch, S5_WIDTH), F32),
                   jax.ShapeDtypeStruct((batch, 2 * S5_STATE), F32)],
        scratch_shapes=[pltpu.VMEM((steps, batch, 2 * S5_STATE), F32),
                        pltpu.VMEM((batch, 2 * S5_STATE), F32)],
        compiler_params=_cparams(("arbitrary",)),
        name="s5",
    )(u_t, h0, a, wb, wc, d, wglu, bglu, gain)


def _out_proj_kernel(og_ref, om_ref, os_ref, x_ref, g1_ref, sc_ref, sh_ref, gain_ref,
                     wg_ref, wm_ref, ws_ref, xo_ref, ht_ref, mix_scr, h_scr, *, groups):
    mix_scr[...] = (_dot(og_ref[...].astype(BF16), wg_ref[...])
                    + _dot(om_ref[...].astype(BF16), wm_ref[...])
                    + _dot(os_ref[...].astype(BF16), ws_ref[...]))
    gain = gain_ref[...]

    def body(g, carry):
        r0 = pl.multiple_of(g * TOKENS_PER_GROUP, TOKENS_PER_GROUP)
        rows = pl.ds(r0, TOKENS_PER_GROUP)
        xn = x_ref[rows, :] + g1_ref[pl.ds(g, 1), :] * mix_scr[rows, :]
        xo_ref[rows, :] = xn
        h_scr[rows, :] = _modulated_group(xn, gain, sc_ref[pl.ds(g, 1), :], sh_ref[pl.ds(g, 1), :])
        return carry

    lax.fori_loop(0, groups, body, 0, unroll=GROUP_UNROLL)
    ht_ref[...] = pltpu.bitcast(h_scr[...].T.astype(BF16), jnp.uint32)


def _out_proj(og, om, os_, x, mod, gains, w_out, *, layer, tt):
    n_tok = x.shape[0]
    groups = tt // TOKENS_PER_GROUP
    tok = lambda width: pl.BlockSpec((tt, width), lambda i: (i, 0))
    w_rows = lambda height, block: _const_spec((None, height, D_MODEL), lambda i: (layer, block, 0))
    assert GLA_WIDTH == ML_WIDTH and (GLA_WIDTH + ML_WIDTH) % S5_WIDTH == 0
    return pl.pallas_call(
        functools.partial(_out_proj_kernel, groups=groups),
        grid=(n_tok // tt,),
        in_specs=[tok(GLA_WIDTH), tok(ML_WIDTH), tok(S5_WIDTH), tok(D_MODEL),
                  _mod_spec(groups, layer, MOD_G1), _mod_spec(groups, layer, MOD_SC2),
                  _mod_spec(groups, layer, MOD_SH2), _gain_spec(layer, 1),
                  w_rows(GLA_WIDTH, 0), w_rows(ML_WIDTH, 1),
                  w_rows(S5_WIDTH, (GLA_WIDTH + ML_WIDTH) // S5_WIDTH)],
        out_specs=[tok(D_MODEL), pl.BlockSpec((D_MODEL // 2, tt), lambda i: (0, i))],
        out_shape=[jax.ShapeDtypeStruct((n_tok, D_MODEL), F32),
                   jax.ShapeDtypeStruct((D_MODEL // 2, n_tok), jnp.uint32)],
        scratch_shapes=[pltpu.VMEM((tt, D_MODEL), F32), pltpu.VMEM((tt, D_MODEL), F32)],
        compiler_params=_cparams(("parallel",)),
        name="out_proj",
    )(og, om, os_, x, mod, mod, mod, gains, w_out, w_out, w_out)


def _top_values(cur, count, out_scr=None, with_rank=False):
    neg_inf = jnp.float32(-jnp.inf)
    vals = []
    rank = jnp.full(cur.shape, float(count), F32) if with_rank else None
    for r in range(count):
        m = jnp.max(cur, axis=0, keepdims=True)
        vals.append(m)
        if out_scr is not None:
            out_scr[r:r + 1, :] = m
        hit = cur == m
        if with_rank:
            rank = jnp.where(hit, float(r), rank)
        if r + 1 < count:
            cur = jnp.where(hit, neg_inf, cur)
    return (vals, rank) if with_rank else vals


def _peer_route_kernel(ht_ref, wqt_ref, keys_ref, first_ref, second_ref, qt_scr, v1_scr, v2_scr, cand_scr,
                       *, tt):
    qt_scr[...] = _dot(wqt_ref[...], _unpack_rows(ht_ref[...]))
    for h in range(PEER_HEADS):
        sc = []
        for side in range(2):
            hs = 2 * h + side
            qb = qt_scr[hs * LANE:(hs + 1) * LANE, :].astype(BF16)
            sc.append(_dot(keys_ref[hs], qb))
        for lt in range(tt // LANE):
            ls = slice(lt * LANE, (lt + 1) * LANE)
            s1 = sc[0][:, ls]
            s2 = sc[1][:, ls]
            _top_values(s1, PEER_TOPK, v1_scr)
            _, rank2 = _top_values(s2, PEER_TOPK, v2_scr, with_rank=True)
            v1_head = v1_scr[0:SUBLANE, :]
            cand_scr[0:PEER_TOPK, :] = v1_scr[0:1, :] + v2_scr[...]
            cand_scr[PEER_TOPK:2 * PEER_TOPK, :] = v1_scr[...] + v2_scr[0:1, :]
            cand_scr[2 * PEER_TOPK:2 * PEER_TOPK + SUBLANE, :] = v1_scr[1:2, :] + v2_scr[0:SUBLANE, :]
            for b in range(1, 5):
                lo = 2 * PEER_TOPK + b * SUBLANE
                cand_scr[lo:lo + SUBLANE, :] = v1_head + v2_scr[b:b + 1, :]
            top = _top_values(cand_scr[...], PEER_TOPK)
            z = jnp.zeros_like(top[0])
            for tv in top:
                z = z + jnp.exp(tv - top[0])
            tau = top[PEER_TOPK - 1]
            n1 = jnp.zeros_like(s1)
            for b in range(SUBLANE):
                n1 = n1 + jnp.where(s1 + v2_scr[b:b + 1, :] >= tau, 1.0, 0.0)
            best = v1_scr[0:1, :]
            tail = jnp.zeros_like(best)
            for b in range(SUBLANE, PEER_TOPK):
                tail = tail + jnp.where(best + v2_scr[b:b + 1, :] >= tau, 1.0, 0.0)
            n1 = n1 + jnp.where(s1 == best, tail, 0.0)
            first_ref[2 * h, :, ls] = n1
            first_ref[2 * h + 1, :, ls] = jnp.exp(s1 - v1_scr[0:1, :]) / z
            second_ref[2 * h, :, ls] = rank2.astype(BF16)
            second_ref[2 * h + 1, :, ls] = jnp.exp(s2 - v2_scr[0:1, :]).astype(BF16)


def _peer_route(ht, wqt, keys, *, tt):
    n_tok = ht.shape[1]
    n_hs = 2 * PEER_HEADS
    assert PEER_TOPK == 2 * SUBLANE
    n_cand = 2 * PEER_TOPK + 5 * SUBLANE
    return pl.pallas_call(
        functools.partial(_peer_route_kernel, tt=tt),
        grid=(n_tok // tt,),
        in_specs=[pl.BlockSpec((D_MODEL // 2, tt), lambda i: (0, i)),
                  _const_spec((n_hs * LANE, D_MODEL), lambda i: (0, 0)),
                  _const_spec((n_hs, N_KEYS, LANE), lambda i: (0, 0, 0))],
        out_specs=[pl.BlockSpec((n_hs, N_KEYS, tt), lambda i: (0, 0, i)),
                   pl.BlockSpec((n_hs, N_KEYS, tt), lambda i: (0, 0, i))],
        out_shape=[jax.ShapeDtypeStruct((n_hs, N_KEYS, n_tok), F32),
                   jax.ShapeDtypeStruct((n_hs, N_KEYS, n_tok), BF16)],
        scratch_shapes=[pltpu.VMEM((n_hs * LANE, tt), F32), pltpu.VMEM((PEER_TOPK, LANE), F32),
                        pltpu.VMEM((PEER_TOPK, LANE), F32), pltpu.VMEM((n_cand, LANE), F32)],
        compiler_params=_cparams(("parallel",)),
        name="peer_route",
    )(ht, wqt, keys)


DENSE_FIRST_KEYS = 4
DENSE_SUB = DENSE_FIRST_KEYS * N_KEYS
GATE_ROWS = 4 * SUBLANE
GATE_PIECES = 4


def _gate_times_act(first_ref, row0, second_ref, act_ref, p_ref, lane_tiles):
    n_pieces = N_KEYS // GATE_ROWS
    zero = jnp.zeros((GATE_ROWS, LANE), BF16)
    for lt in lane_tiles:
        ls = slice(lt * LANE, (lt + 1) * LANE)
        for jp0 in range(0, n_pieces, GATE_PIECES):
            pieces = range(jp0, jp0 + GATE_PIECES)
            gates = [[zero for _ in pieces] for _ in range(DENSE_FIRST_KEYS)]
            for h in range(PEER_HEADS):
                counts, weights = [], []
                for ii in range(DENSE_FIRST_KEYS):
                    r = row0 + ii
                    counts.append(jnp.broadcast_to(first_ref[2 * h, 0, r:r + 1, ls], (GATE_ROWS, LANE)).astype(BF16))
                    weights.append(jnp.broadcast_to(first_ref[2 * h + 1, 0, r:r + 1, ls], (GATE_ROWS, LANE)).astype(BF16))
                for q, jp in enumerate(pieces):
                    js = slice(jp * GATE_ROWS, (jp + 1) * GATE_ROWS)
                    rank2 = second_ref[2 * h, js, ls]
                    e2 = second_ref[2 * h + 1, js, ls]
                    for ii in range(DENSE_FIRST_KEYS):
                        gates[ii][q] = gates[ii][q] + jnp.where(rank2 < counts[ii], e2, zero) * weights[ii]
            for ii in range(DENSE_FIRST_KEYS):
                for q, jp in enumerate(pieces):
                    rs = slice(ii * N_KEYS + jp * GATE_ROWS, ii * N_KEYS + (jp + 1) * GATE_ROWS)
                    p_ref[rs, ls] = gates[ii][q] * act_ref[rs, ls].astype(BF16)


def _peer_dense_kernel(ht_ref, u_ref, vt_ref, first_a_ref, first_b_ref, second_in_ref,
                       x_ref, g2_ref, o_ref, acc_scr, act0, act1, p0, p1, second_ref, *, tt, groups):
    k = pl.program_id(1)

    @pl.when(k == 0)
    def _():
        acc_scr[...] = jnp.zeros_like(acc_scr)
        act1[...] = jnp.zeros_like(act1)
        p0[...] = jnp.zeros_like(p0)
        second_ref[...] = second_in_ref[...]

    tok_half = tt // 2
    tiles_half = tok_half // LANE
    for half in range(2):
        ts = slice(half * tok_half, (half + 1) * tok_half)
        lane_tiles = range(half * tiles_half, (half + 1) * tiles_half)
        act0[:, ts] = jax.nn.gelu(_dot(u_ref[0:DENSE_SUB, :], _unpack_rows(ht_ref[:, ts])))
        _gate_times_act(first_a_ref, DENSE_FIRST_KEYS, second_ref, act1, p1, lane_tiles)
        acc_scr[:, ts] += _dot(vt_ref[:, 0:DENSE_SUB], p0[:, ts])
    for half in range(2):
        ts = slice(half * tok_half, (half + 1) * tok_half)
        lane_tiles = range(half * tiles_half, (half + 1) * tiles_half)
        act1[:, ts] = jax.nn.gelu(_dot(u_ref[DENSE_SUB:, :], _unpack_rows(ht_ref[:, ts])))
        _gate_times_act(first_b_ref, 0, second_ref, act0, p0, lane_tiles)
        acc_scr[:, ts] += _dot(vt_ref[:, DENSE_SUB:], p1[:, ts])

    @pl.when(k == pl.num_programs(1) - 1)
    def _():
        o_ref[...] = acc_scr[...].T

        def body(g, carry):
            rows = pl.ds(pl.multiple_of(g * TOKENS_PER_GROUP, TOKENS_PER_GROUP), TOKENS_PER_GROUP)
            o_ref[rows, :] = x_ref[rows, :] + g2_ref[pl.ds(g, 1), :] * o_ref[rows, :]
            return carry

        lax.fori_loop(0, groups, body, 0, unroll=GROUP_UNROLL)


def _peer_dense(ht, u_bf, vt_bf, first, second, x, mod, *, layer, tt):
    n_tok = x.shape[0]
    groups = tt // TOKENS_PER_GROUP
    n_hs = 2 * PEER_HEADS
    ne = 2 * DENSE_SUB
    n_i = ne // N_KEYS
    n_blocks = N_EXPERTS // ne
    tok_const = lambda shape, imap: pl.BlockSpec(shape, imap, pipeline_mode=pl.Buffered(1))
    by_first_key = lambda t: t.reshape(n_hs, n_blocks, n_i, n_tok)
    cur = lambda k: jnp.minimum(k, n_blocks - 1)
    prev = lambda k: jnp.maximum(k - 1, 0)
    key_spec = lambda blk: pl.BlockSpec((n_hs, 1, n_i, tt), lambda i, k: (0, blk(k), 0, i))
    return pl.pallas_call(
        functools.partial(_peer_dense_kernel, tt=tt, groups=groups),
        grid=(n_tok // tt, n_blocks + 1),
        in_specs=[tok_const((D_MODEL // 2, tt), lambda i, k: (0, i)),
                  pl.BlockSpec((None, ne, D_MODEL), lambda i, k: (layer, cur(k), 0)),
                  pl.BlockSpec((None, None, D_MODEL, ne), lambda i, k: (layer, prev(k), 0, 0)),
                  key_spec(prev), key_spec(cur),
                  tok_const((n_hs, N_KEYS, tt), lambda i, k: (0, 0, i)),
                  tok_const((tt, D_MODEL), lambda i, k: (i, 0)),
                  _mod_spec(groups, layer, MOD_G2)],
        out_specs=pl.BlockSpec((tt, D_MODEL), lambda i, k: (i, 0)),
        out_shape=jax.ShapeDtypeStruct((n_tok, D_MODEL), F32),
        scratch_shapes=[pltpu.VMEM((D_MODEL, tt), F32),
                        pltpu.VMEM((DENSE_SUB, tt), F32), pltpu.VMEM((DENSE_SUB, tt), F32),
                        pltpu.VMEM((DENSE_SUB, tt), BF16), pltpu.VMEM((DENSE_SUB, tt), BF16),
                        pltpu.VMEM((n_hs, N_KEYS, tt), BF16)],
        compiler_params=_cparams(("parallel", "arbitrary")),
        name="peer_dense",
    )(ht, u_bf, vt_bf, by_first_key(first), by_first_key(first), second, x, mod)


def _final_norm_kernel(x_ref, gain_ref, o_ref):
    o_ref[...] = _rms(x_ref[...]) * gain_ref[...]


def _final_norm(x, gain, *, tt, row0, n_rows):
    return pl.pallas_call(
        _final_norm_kernel,
        grid=(n_rows // tt,),
        in_specs=[pl.BlockSpec((tt, D_MODEL), lambda i: (row0 // tt + i, 0)),
                  pl.BlockSpec((1, D_MODEL), lambda i: (0, 0))],
        out_specs=pl.BlockSpec((tt, D_MODEL), lambda i: (i, 0)),
        out_shape=jax.ShapeDtypeStruct((n_rows, D_MODEL), F32),
        compiler_params=_cparams(("parallel",)),
        name="final_norm",
    )(x, gain)


PROMPT_ROWS = 256
STEP_ROWS = 128
S5_PROMPT_STEPS = 128


def _block_diag_in(bbar):
    rows_group = np.arange(S5_WIDTH) // S5_CH
    cols_group = np.arange(S5_STATE) // S5_P
    mask = jnp.asarray(rows_group[:, None] == cols_group[None, :])
    reps = (1,) * (bbar.ndim - 2) + (S5_GROUPS, 1)
    return jnp.where(mask, jnp.tile(bbar, reps), 0.0)


def _diag_tiles(w, rows, cols):
    return jnp.stack([w[:, t * rows:(t + 1) * rows, t * cols:(t + 1) * cols]
                      for t in range(w.shape[1] // rows)], axis=1)


def _block_diag_out(c):
    rows_group = np.arange(S5_STATE) // S5_P
    cols_group = np.arange(S5_WIDTH) // S5_CH
    mask = jnp.asarray(rows_group[:, None] == cols_group[None, :])
    per_state = c.transpose(0, 1, 3, 2).reshape(c.shape[0], S5_STATE, S5_CH)
    return jnp.where(mask, jnp.tile(per_state, (1, 1, S5_GROUPS)), 0.0)


def kernel(x_prompt, x_sample, state_gla, state_mlstm_c, state_mlstm_n, state_mlstm_m, state_s5_re, state_s5_im, c_prompt, c_sample, w_ada, b_ada, norm_gain, w_in, gla_w_gate_up, gla_b_gate, gla_norm_gain, ml_b_igate, ml_b_fgate, ml_norm_gain, s5_lambda_re, s5_lambda_im, s5_log_dt, s5_b_re, s5_b_im, s5_c_re, s5_c_im, s5_d, s5_w_glu, s5_b_glu, s5_norm_gain, w_out, peer_w_q, peer_sub_keys, peer_u, peer_v, final_gain):
    bp, lp, _ = x_prompt.shape
    bs, ls, _ = x_sample.shape
    assert ls == TOKENS_PER_GROUP and lp % PROMPT_ROWS == 0 and (bs * ls) % STEP_ROWS == 0
    n_prompt = bp * lp
    n_sample = bs * ls
    n_tok = n_prompt + n_sample
    rep = lp // TOKENS_PER_GROUP

    x = jnp.concatenate([x_prompt.reshape(n_prompt, D_MODEL), x_sample.reshape(n_sample, D_MODEL)], axis=0)

    c_all = jnp.concatenate([c_prompt, c_sample], axis=0)
    n_c = c_all.shape[0]
    c_pad = (-n_c) % SUBLANE
    mod = _adaln(jnp.pad(c_all, ((0, c_pad), (0, 0))), w_ada, b_ada)
    mod = jnp.concatenate([jnp.repeat(mod[:, :bp], rep, axis=1), mod[:, bp:n_c]], axis=1)
    gains = norm_gain.reshape(DEPTH * 2, 1, D_MODEL)

    zeros = lambda *shape: jnp.zeros(shape, F32)
    seq_blocks = lp // PROMPT_ROWS
    step_blocks = n_sample // STEP_ROWS
    step_block0 = n_prompt // STEP_ROWS
    prompt_pad = (-bp) % SUBLANE

    w_in_p = _pad_z_columns(w_in.astype(BF16))
    wg_all = jnp.zeros((DEPTH, GLA_HEADS, LANE, LANE), F32).at[:, :, :GLA_RANK, :GLA_DK].set(
        gla_w_gate_up.reshape(DEPTH, GLA_RANK, GLA_HEADS, GLA_DK).transpose(0, 2, 1, 3)).astype(BF16)
    bg_all = jnp.zeros((DEPTH, GLA_HEADS, 1, LANE), F32).at[:, :, 0, :GLA_DK].set(
        gla_b_gate.reshape(DEPTH, GLA_HEADS, GLA_DK))
    gla_gain_all = gla_norm_gain.reshape(DEPTH, GLA_HEADS, 1, GLA_DV)
    bif_all = jnp.zeros((DEPTH, 1, LANE), F32).at[:, 0, :ML_HEADS].set(ml_b_igate).at[
        :, 0, F_GATE_LANE:F_GATE_LANE + ML_HEADS].set(ml_b_fgate)
    ml_gain_all = ml_norm_gain.reshape(DEPTH, ML_HEADS, 1, ML_DH)
    a_disc, bbar = _s5_discretise(s5_lambda_re, s5_lambda_im, s5_log_dt, s5_b_re, s5_b_im)
    wb_all = jnp.concatenate([_diag_tiles(_block_diag_in(bbar[:, 0]), LANE, S5_TILE_STATE),
                              _diag_tiles(_block_diag_in(bbar[:, 1]), LANE, S5_TILE_STATE)], axis=3).astype(BF16)
    wc_all = jnp.concatenate([_diag_tiles(_block_diag_out(s5_c_re), S5_TILE_STATE, LANE),
                              -_diag_tiles(_block_diag_out(s5_c_im), S5_TILE_STATE, LANE)], axis=2).astype(BF16)
    wglu_all = s5_w_glu.astype(BF16)
    w_out_bf = w_out.astype(BF16)
    wqt_all = peer_w_q.astype(BF16).transpose(0, 2, 1)
    keys_all = peer_sub_keys.reshape(DEPTH, 2 * PEER_HEADS, N_KEYS, LANE).astype(BF16)
    u_bf = peer_u.astype(BF16)
    dense_block = 2 * DENSE_SUB
    vt_bf = peer_v.astype(BF16).reshape(DEPTH, N_EXPERTS // dense_block, dense_block, D_MODEL).transpose(0, 1, 3, 2)

    st_gla = state_gla.astype(F32)
    st_c = state_mlstm_c.astype(F32)
    st_n = state_mlstm_n.astype(F32)[:, :, :, None, :]
    st_m = jnp.broadcast_to(state_mlstm_m.astype(F32)[:, :, :, None, None], state_mlstm_m.shape + (1, LANE))
    st_s5 = jnp.concatenate([state_s5_re.reshape(DEPTH, bs, S5_STATE), state_s5_im.reshape(DEPTH, bs, S5_STATE)],
                            axis=2).astype(F32)

    sg_s = mc_s = mn_s = mm_s = None
    new_states = []
    for l in range(DEPTH):
        z = _norm_proj(x, mod, gains, w_in_p, layer=l, tt=1024, tn=1152)

        seq_kw = dict(n_seq=bp, seq_blocks=seq_blocks, rows=PROMPT_ROWS, row_block0=0, carry=True)
        step_kw = dict(n_seq=step_blocks, seq_blocks=1, rows=STEP_ROWS, chunk=ls, row_block0=step_block0,
                       carry=False, layer=l)

        gla_w = (wg_all[l], bg_all[l], gla_gain_all[l])
        og, sg_p = _gla(z, *gla_w, zeros(bp, GLA_HEADS, GLA_DK, GLA_DV), chunk=GLA_CHUNK, **seq_kw)
        og, sg_s = _gla(z, *gla_w, st_gla, fill=(og,) if l == 0 else (og, sg_s), **step_kw)

        ml_w = (bif_all[l], ml_gain_all[l])
        om, mc_p, mn_p, mm_p = _mlstm(z, *ml_w, zeros(bp, ML_HEADS, ML_DH, ML_DH), zeros(bp, ML_HEADS, 1, ML_DH),
                                      zeros(bp, ML_HEADS, 1, LANE), chunk=ML_CHUNK, **seq_kw)
        om, mc_s, mn_s, mm_s = _mlstm(z, *ml_w, st_c, st_n, st_m,
                                      fill=(om,) if l == 0 else (om, mc_s, mn_s, mm_s), **step_kw)

        su = z[:, ZB_SU * LANE:ZB_SU * LANE + S5_WIDTH]
        u_p = jnp.pad(su[:n_prompt].reshape(bp, lp, S5_WIDTH).transpose(1, 0, 2), ((0, 0), (0, prompt_pad), (0, 0)))
        u_s = su[n_prompt:].reshape(bs, ls, S5_WIDTH).transpose(1, 0, 2)
        s5_args = (a_disc[l], wb_all[l], wc_all[l], s5_d[l].reshape(1, S5_WIDTH), wglu_all[l],
                   s5_b_glu[l][None, :], s5_norm_gain[l][None, :])
        os_p, hs_p = _s5(u_p, zeros(bp + prompt_pad, 2 * S5_STATE), *s5_args,
                         steps=S5_PROMPT_STEPS, lane_width=S5_STATE)
        os_s, hs_s = _s5(u_s, st_s5[l], *s5_args, steps=ls, lane_width=LANE)
        os_ = jnp.concatenate([os_p[:, :bp].transpose(1, 0, 2).reshape(n_prompt, S5_WIDTH),
                               os_s.transpose(1, 0, 2).reshape(n_sample, S5_WIDTH)], axis=0)

        x, ht = _out_proj(og, om, os_, x, mod, gains, w_out_bf, layer=l, tt=256)

        first, second = _peer_route(ht, wqt_all[l], keys_all[l], tt=256)
        x = _peer_dense(ht, u_bf, vt_bf, first, second, x, mod, layer=l, tt=512)

        new_states.append((sg_p, mc_p, mn_p[:, :, 0, :], mm_p[:, :, 0, 0], hs_p[:bp], hs_s))

    y_prompt = _final_norm(x, final_gain[None, :], tt=512, row0=0, n_rows=n_prompt)
    y_sample = _final_norm(x, final_gain[None, :], tt=512, row0=n_prompt, n_rows=n_sample)
    stack = lambda i: jnp.stack([ns[i] for ns in new_states])
    s5_p, s5_s = stack(4), stack(5)
    split_s5 = lambda h, lo: h[:, :, lo:lo + S5_STATE].reshape(DEPTH, -1, S5_GROUPS, S5_P)
    outs = (y_prompt.reshape(bp, lp, D_MODEL), y_sample.reshape(bs, ls, D_MODEL),
            stack(0), sg_s, stack(1), mc_s, stack(2), mn_s[:, :, :, 0, :], stack(3), mm_s[:, :, :, 0, 0],
            split_s5(s5_p, 0), split_s5(s5_s, 0), split_s5(s5_p, S5_STATE), split_s5(s5_s, S5_STATE))
    refs = (x_prompt, x_sample, state_gla, state_gla, state_mlstm_c, state_mlstm_c, state_mlstm_n, state_mlstm_n,
            state_mlstm_m, state_mlstm_m, state_s5_re, state_s5_re, state_s5_im, state_s5_im)
    return tuple(o.astype(r.dtype) for o, r in zip(outs, refs))
```

```python
import functools
import math

import numpy as np
import jax
import jax.numpy as jnp
from jax import lax
from jax.experimental import pallas as pl
from jax.experimental.pallas import tpu as pltpu

F32 = jnp.float32
BF16 = jnp.bfloat16
HIGHEST = lax.Precision.HIGHEST

D_MODEL = 2048
DEPTH = 4
GLA_HEADS = 6
GLA_DK = 64
GLA_DV = 128
GLA_RANK = 16
GLA_TAU = 16.0
GLA_CHUNK = 16
ML_HEADS = 6
ML_DH = 128
ML_CHUNK = 64
S5_WIDTH = 512
S5_CH = 16
S5_GROUPS = 32
S5_P = 64
S5_STATE = S5_GROUPS * S5_P
S5_TILES = S5_WIDTH // 128
S5_TILE_STATE = S5_STATE // S5_TILES
PEER_HEADS = 8
N_KEYS = 128
N_EXPERTS = N_KEYS * N_KEYS
PEER_TOPK = 16
NORM_EPS = 1e-6
GLA_WIDTH = GLA_HEADS * GLA_DV
ML_WIDTH = ML_HEADS * ML_DH

LANE = 128
SUBLANE = 8
TOKENS_PER_GROUP = 8
GROUP_UNROLL = 4
VMEM_LIMIT = 56 * 1024 * 1024

ZB_GQ, ZB_GK, ZB_GV, ZB_GG = 0, 6, 12, 18
ZB_MQ, ZB_MK, ZB_MV, ZB_MO = 24, 30, 36, 42
ZB_GR, ZB_MIF = 48, 49
HEADS_PER_STEP = 6
ZB_SU = 50
Z_BLOCKS = 54
Z_COLS = Z_BLOCKS * LANE
F_GATE_LANE = 8


def _z_source_columns():
    src = np.full((Z_COLS,), -1, np.int32)
    off_gq, off_gk, off_gv, off_gg, off_gr = 0, 384, 768, 1536, 2304
    off_mq, off_mk, off_mv, off_mo, off_mi, off_mf, off_su = 2320, 3088, 3856, 4624, 5392, 5398, 5404
    for h in range(GLA_HEADS):
        for d in range(GLA_DK):
            src[(ZB_GQ + h) * LANE + d] = off_gq + h * GLA_DK + d
            src[(ZB_GK + h) * LANE + d] = off_gk + h * GLA_DK + d
        for d in range(GLA_DV):
            src[(ZB_GV + h) * LANE + d] = off_gv + h * GLA_DV + d
            src[(ZB_GG + h) * LANE + d] = off_gg + h * GLA_DV + d
    for d in range(GLA_RANK):
        src[ZB_GR * LANE + d] = off_gr + d
    for h in range(ML_HEADS):
        for d in range(ML_DH):
            src[(ZB_MQ + h) * LANE + d] = off_mq + h * ML_DH + d
            src[(ZB_MK + h) * LANE + d] = off_mk + h * ML_DH + d
            src[(ZB_MV + h) * LANE + d] = off_mv + h * ML_DH + d
            src[(ZB_MO + h) * LANE + d] = off_mo + h * ML_DH + d
        src[ZB_MIF * LANE + h] = off_mi + h
        src[ZB_MIF * LANE + F_GATE_LANE + h] = off_mf + h
    for d in range(S5_WIDTH):
        src[ZB_SU * LANE + d] = off_su + d
    return src


_Z_SRC = _z_source_columns()


def _z_runs():
    runs, i = [], 0
    while i < Z_COLS:
        j = i + 1
        if _Z_SRC[i] < 0:
            while j < Z_COLS and _Z_SRC[j] < 0:
                j += 1
            runs.append((-1, j - i))
        else:
            while j < Z_COLS and _Z_SRC[j] == _Z_SRC[j - 1] + 1:
                j += 1
            runs.append((int(_Z_SRC[i]), j - i))
        i = j
    return runs


_Z_RUNS = _z_runs()


def _pad_z_columns(w):
    parts = [jnp.zeros(w.shape[:-1] + (n,), w.dtype) if s < 0 else w[..., s:s + n] for s, n in _Z_RUNS]
    return jnp.concatenate(parts, axis=-1)


def _cparams(semantics):
    return pltpu.CompilerParams(dimension_semantics=semantics, vmem_limit_bytes=VMEM_LIMIT)


def _const_spec(block_shape, index_map):
    return pl.BlockSpec(block_shape, index_map, pipeline_mode=pl.Buffered(1))


def _rms(x):
    return x * lax.rsqrt(jnp.mean(x * x, axis=-1, keepdims=True) + NORM_EPS)


def _unpack_rows(x):
    return pltpu.bitcast(x, BF16)


def _dot(a, b):
    return jnp.dot(a, b, preferred_element_type=F32)


def _dot_nt(a, b):
    return lax.dot_general(a, b, (((1,), (1,)), ((), ())), preferred_element_type=F32)


def _chunk_sums(same, tri, x):
    rows = x.shape[0]
    masks = jnp.concatenate([jnp.where(tri, 1.0, 0.0), jnp.where(same, 1.0, 0.0)], axis=0).astype(BF16)
    hi = x.astype(BF16)
    r1 = x - hi.astype(F32)
    mid = r1.astype(BF16)
    lo = (r1 - mid.astype(F32)).astype(BF16)
    sums = _dot(masks, jnp.concatenate([hi, mid, lo], axis=1))
    total = sums[:, 0:LANE] + sums[:, LANE:2 * LANE] + sums[:, 2 * LANE:3 * LANE]
    return total[0:rows], total[rows:2 * rows]


def _adaln_kernel(c_ref, w_lo_ref, w_hi_ref, b_ref, o_ref):
    c = c_ref[...]
    s = (c * jax.nn.sigmoid(c)).astype(BF16)
    half = D_MODEL // 2
    o_ref[0] = (_dot(s[:, :half], w_lo_ref[0].astype(BF16)) + _dot(s[:, half:], w_hi_ref[0].astype(BF16))
                + b_ref[0])


def _adaln(c_all, w_ada, b_ada):
    n_rows = c_all.shape[0]
    n_out = w_ada.shape[-1]
    tn = 1024
    w_half = lambda which: pl.BlockSpec((1, D_MODEL // 2, tn), lambda l, j: (l, which, j))
    return pl.pallas_call(
        _adaln_kernel,
        grid=(DEPTH, n_out // tn),
        in_specs=[
            pl.BlockSpec((n_rows, D_MODEL), lambda l, j: (0, 0)),
            w_half(0), w_half(1),
            pl.BlockSpec((1, 1, tn), lambda l, j: (l, 0, j)),
        ],
        out_specs=pl.BlockSpec((1, n_rows, tn), lambda l, j: (l, 0, j)),
        out_shape=jax.ShapeDtypeStruct((DEPTH, n_rows, n_out), F32),
        compiler_params=_cparams(("parallel", "parallel")),
        name="adaln",
    )(c_all, w_ada, w_ada, b_ada.reshape(DEPTH, 1, n_out))


def _modulated_group(x8, gain, sc_row, sh_row):
    return _rms(x8) * gain * (1.0 + sc_row) + sh_row


def _norm_proj_kernel(x_ref, sc_ref, sh_ref, gain_ref, w_ref, o_ref, h_scr, *, groups):
    @pl.when(pl.program_id(1) == 0)
    def _():
        gain = gain_ref[...]

        def body(p, carry):
            hs = []
            for u in range(2):
                g = p * 2 + u
                r0 = pl.multiple_of(g * TOKENS_PER_GROUP, TOKENS_PER_GROUP)
                hs.append(_modulated_group(x_ref[pl.ds(r0, TOKENS_PER_GROUP), :], gain,
                                           sc_ref[pl.ds(g, 1), :], sh_ref[pl.ds(g, 1), :]))
            r = pl.multiple_of(p * 2 * TOKENS_PER_GROUP, 2 * TOKENS_PER_GROUP)
            h_scr[pl.ds(r, 2 * TOKENS_PER_GROUP), :] = jnp.concatenate(hs, axis=0).astype(BF16)
            return carry

        lax.fori_loop(0, groups // 2, body, 0, unroll=GROUP_UNROLL)

    o_ref[...] = _dot(h_scr[...], w_ref[...])


MOD_SH1, MOD_SC1, MOD_G1, MOD_SH2, MOD_SC2, MOD_G2 = range(6)


def _mod_spec(groups, layer, kind):
    return pl.BlockSpec((None, groups, D_MODEL), lambda *a: (layer, a[0], kind))


def _gain_spec(layer, which):
    return pl.BlockSpec((None, 1, D_MODEL), lambda *a: (2 * layer + which, 0, 0))


def _norm_proj(x, mod, gains, w, *, layer, tt, tn):
    n_tok = x.shape[0]
    n_out = w.shape[-1]
    groups = tt // TOKENS_PER_GROUP
    return pl.pallas_call(
        functools.partial(_norm_proj_kernel, groups=groups),
        grid=(n_tok // tt, n_out // tn),
        in_specs=[
            pl.BlockSpec((tt, D_MODEL), lambda i, j: (i, 0)),
            _mod_spec(groups, layer, MOD_SC1),
            _mod_spec(groups, layer, MOD_SH1),
            _gain_spec(layer, 0),
            pl.BlockSpec((None, D_MODEL, tn), lambda i, j: (layer, 0, j)),
        ],
        out_specs=pl.BlockSpec((tt, tn), lambda i, j: (i, j)),
        out_shape=jax.ShapeDtypeStruct((n_tok, n_out), F32),
        scratch_shapes=[pltpu.VMEM((tt, D_MODEL), BF16)],
        compiler_params=_cparams(("parallel", "arbitrary")),
        name="norm_proj",
    )(x, mod, mod, gains, w)


def _chunk_masks(rows, chunk):
    shift = int(math.log2(chunk))
    ri = lax.broadcasted_iota(jnp.int32, (rows, rows), 0)
    ci = lax.broadcasted_iota(jnp.int32, (rows, rows), 1)
    same = (ri >> shift) == (ci >> shift)
    tri = jnp.logical_and(same, ci <= ri)
    return same, tri


def _gla_kernel(q_ref, k_ref, v_ref, g_ref, r_ref, wg_ref, bg_ref, gain_ref, s0_ref,
                o_ref, sfin_ref, st_scr, **kw):
    carry = kw["carry"]
    heads = range(HEADS_PER_STEP)
    if carry:
        @pl.when(pl.program_id(2) == 0)
        def _():
            for hh in heads:
                st_scr[hh] = _gla_load_state(s0_ref.at[:, hh:hh + 1], 0)
    for hh in heads:
        ln = slice(hh * LANE, (hh + 1) * LANE)
        one = slice(hh, hh + 1)
        _gla_head(q_ref.at[:, ln], k_ref.at[:, ln], v_ref.at[:, ln], g_ref.at[:, ln], r_ref,
                  wg_ref.at[one], bg_ref.at[one], gain_ref.at[one], s0_ref.at[:, one],
                  o_ref.at[:, ln], sfin_ref.at[:, one], st_scr.at[hh], **kw)
    if carry:
        @pl.when(pl.program_id(2) == pl.num_programs(2) - 1)
        def _():
            for hh in heads:
                sfin_ref[0, hh] = st_scr[hh].T[0:GLA_DK, :]


def _gla_load_state(s0_ref, c):
    zero_pad = jnp.zeros((LANE - GLA_DK, GLA_DV), F32)
    return jnp.concatenate([s0_ref[c, 0], zero_pad], axis=0).T


def _gla_head(q_ref, k_ref, v_ref, g_ref, r_ref, wg_ref, bg_ref, gain_ref, s0_ref,
              o_ref, sfin_ref, st_scr, *, rows, chunk, carry):
    n_chunks = rows // chunk
    shift = int(math.log2(chunk))
    same, tri = _chunk_masks(rows, chunk)

    q = q_ref[...] * (GLA_DK ** -0.5)
    k = k_ref[...]
    v = v_ref[...]
    la = jax.nn.log_sigmoid(_dot(r_ref[...].astype(BF16), wg_ref[0]) + bg_ref[0]) / GLA_TAU
    bl, bt = _chunk_sums(same, tri, la)
    qd = (q * jnp.exp(bl)).astype(BF16)
    ki = (k * jnp.exp(-bl)).astype(BF16)
    kd = (k * jnp.exp(bt - bl)).astype(BF16)
    att = jnp.where(tri, _dot_nt(qd, ki), 0.0)
    o_intra = _dot(att.astype(BF16), v.astype(BF16))

    v_t = v.T
    lane_chunk = lax.broadcasted_iota(jnp.int32, (GLA_DV, rows), 1) >> shift
    if carry:
        s_t = st_scr[...]

    sliced = chunk % (2 * SUBLANE) == 0
    row_chunk = lax.broadcasted_iota(jnp.int32, (rows, GLA_DV), 0) >> shift
    o_inter = [] if sliced else jnp.zeros((rows, GLA_DV), F32)
    for c in range(n_chunks):
        lo = c * chunk
        if not carry:
            s_t = _gla_load_state(s0_ref, c)
        if sliced:
            o_inter.append(_dot_nt(qd[lo:lo + chunk], s_t.astype(BF16)))
        else:
            o_inter = jnp.where(row_chunk == c, _dot_nt(qd, s_t.astype(BF16)), o_inter)
        decay = jnp.exp(bt[lo:lo + 1, :])
        v_c = jnp.where(lane_chunk == c, v_t, 0.0).astype(BF16)
        s_t = s_t * decay + _dot(v_c, kd)
        if not carry:
            sfin_ref[c, 0] = s_t.T[0:GLA_DK, :]

    if carry:
        st_scr[...] = s_t

    o = o_intra + (jnp.concatenate(o_inter, axis=0) if sliced else o_inter)
    g = g_ref[...]
    o_ref[...] = _rms(o) * gain_ref[0] * (g * jax.nn.sigmoid(g))


def _without_refs(kernel_fn, first, count):
    def wrapped(*refs):
        return kernel_fn(*refs[:first], *refs[first + count:])
    return wrapped


def _mixer_grid(n_seq, n_heads, seq_blocks, row_block0, carry):
    if carry:
        return ((n_seq, n_heads, seq_blocks), lambda b, h, t: row_block0 + b * seq_blocks + t,
                ("parallel", "parallel", "arbitrary"))
    return (n_seq, n_heads), lambda b, h: row_block0 + b, ("parallel", "parallel")


def _state_spec(per_block, d2, d3, layer):
    if layer is None:
        return pl.BlockSpec((per_block, HEADS_PER_STEP, d2, d3), lambda *a: (a[0], a[1], 0, 0))
    return pl.BlockSpec((None, per_block, HEADS_PER_STEP, d2, d3), lambda *a: (layer, a[0], a[1], 0, 0))


def _head_cols_spec(rows, rb, block0):
    assert block0 % HEADS_PER_STEP == 0
    return pl.BlockSpec((rows, HEADS_PER_STEP * LANE), lambda *a: (rb(*a), block0 // HEADS_PER_STEP + a[1]))


def _head_param_spec(shape):
    return pl.BlockSpec((HEADS_PER_STEP,) + shape, lambda *a: (a[1],) + (0,) * len(shape))


def _gla(z, wg, bg, gain, s0, *, n_seq, seq_blocks, rows, chunk, row_block0, carry,
         layer=None, fill=()):
    per_block = 1 if carry else rows // chunk
    grid, rb, sem = _mixer_grid(n_seq, GLA_HEADS // HEADS_PER_STEP, seq_blocks, row_block0, carry)
    zspec = functools.partial(_head_cols_spec, rows, rb)
    state_spec = _state_spec(per_block, GLA_DK, GLA_DV, layer)
    n_in = 9
    return pl.pallas_call(
        _without_refs(functools.partial(_gla_kernel, rows=rows, chunk=chunk, carry=carry), n_in, len(fill)),
        grid=grid,
        in_specs=[zspec(ZB_GQ), zspec(ZB_GK), zspec(ZB_GV), zspec(ZB_GG),
                  pl.BlockSpec((rows, LANE), lambda *a: (rb(*a), ZB_GR)),
                  _head_param_spec((LANE, LANE)), _head_param_spec((1, LANE)), _head_param_spec((1, LANE)),
                  state_spec]
                 + [pl.BlockSpec(memory_space=pl.ANY)] * len(fill),
        out_specs=[_head_cols_spec(rows, rb, 0), state_spec],
        out_shape=[jax.ShapeDtypeStruct((z.shape[0], GLA_WIDTH), F32),
                   jax.ShapeDtypeStruct(s0.shape, F32)],
        input_output_aliases={n_in + i: i for i in range(len(fill))},
        scratch_shapes=[pltpu.VMEM((HEADS_PER_STEP, GLA_DV, LANE), F32)],
        compiler_params=_cparams(sem),
        name="gla_seq" if carry else "gla_step",
    )(z, z, z, z, z, wg, bg, gain, s0, *fill)


def _mlstm_kernel(q_ref, k_ref, v_ref, og_ref, gate_ref, bif_ref, gain_ref, c0_ref, n0_ref, m0_ref,
                  h_ref, cfin_ref, nfin_ref, mfin_ref, c_scr, n_scr, m_scr, **kw):
    carry = kw["carry"]
    heads = range(HEADS_PER_STEP)
    if carry:
        @pl.when(pl.program_id(2) == 0)
        def _():
            for hh in heads:
                c_scr[hh] = c0_ref[0, hh]
                n_scr[hh] = n0_ref[0, hh]
                m_scr[hh] = m0_ref[0, hh]
    _mlstm_heads(q_ref, k_ref, v_ref, og_ref, gate_ref, bif_ref, gain_ref, c0_ref, n0_ref, m0_ref,
                 h_ref, cfin_ref, nfin_ref, mfin_ref, c_scr, n_scr, m_scr, **kw)
    if carry:
        @pl.when(pl.program_id(2) == pl.num_programs(2) - 1)
        def _():
            for hh in heads:
                cfin_ref[0, hh] = c_scr[hh]
                nfin_ref[0, hh] = n_scr[hh]
                mfin_ref[0, hh] = m_scr[hh]


def _mlstm_heads(q_ref, k_ref, v_ref, og_ref, gate_ref, bif_ref, gain_ref, c0_ref, n0_ref, m0_ref,
                 h_ref, cfin_ref, nfin_ref, mfin_ref, c_scr, n_scr, m_scr, **kw):
    for hh in range(HEADS_PER_STEP):
        ln = slice(hh * LANE, (hh + 1) * LANE)
        one = slice(hh, hh + 1)
        _mlstm_head(pl.program_id(1) * HEADS_PER_STEP + hh,
                    q_ref.at[:, ln], k_ref.at[:, ln], v_ref.at[:, ln], og_ref.at[:, ln], gate_ref, bif_ref,
                    gain_ref.at[one], c0_ref.at[:, one], n0_ref.at[:, one], m0_ref.at[:, one],
                    h_ref.at[:, ln], cfin_ref.at[:, one], nfin_ref.at[:, one], mfin_ref.at[:, one],
                    c_scr.at[hh], n_scr.at[hh], m_scr.at[hh], **kw)


def _mlstm_head(head, q_ref, k_ref, v_ref, og_ref, gate_ref, bif_ref, gain_ref, c0_ref, n0_ref, m0_ref,
                h_ref, cfin_ref, nfin_ref, mfin_ref, c_scr, n_scr, m_scr, *, rows, chunk, carry):
    n_chunks = rows // chunk
    shift = int(math.log2(chunk))
    same, tri = _chunk_masks(rows, chunk)
    neg_inf = jnp.float32(-jnp.inf)

    x = gate_ref[...] + bif_ref[...]
    log_f = jax.nn.log_sigmoid(x)
    b_cum, b_tot = _chunk_sums(same, tri, log_f)
    lane = lax.broadcasted_iota(jnp.int32, (rows, LANE), 1)
    y = jnp.where(lane < F_GATE_LANE, x, b_cum)
    pick_i = lane == head
    pick_b = lane == head + F_GATE_LANE
    i_col = jnp.sum(jnp.where(pick_i, y, 0.0), axis=-1, keepdims=True)
    b_col = jnp.sum(jnp.where(pick_b, y, 0.0), axis=-1, keepdims=True)
    bt_col = jnp.sum(jnp.where(pick_b, b_tot, 0.0), axis=-1, keepdims=True)
    sub = lax.broadcasted_iota(jnp.int32, (LANE, rows), 0)
    y_t = y.T
    i_row = jnp.sum(jnp.where(sub == head, y_t, 0.0), axis=0, keepdims=True)
    b_row = jnp.sum(jnp.where(sub == head + F_GATE_LANE, y_t, 0.0), axis=0, keepdims=True)

    a_col = bt_col - b_col + i_col
    mloc_col = bt_col + jnp.max(jnp.where(same, i_row - b_row, neg_inf), axis=-1, keepdims=True)

    q = q_ref[...] * (ML_DH ** -0.5)
    k = k_ref[...]
    qb = q.astype(BF16)
    kb = k.astype(BF16)
    vb = v_ref[...].astype(BF16)
    kw = k * jnp.exp(a_col - mloc_col)
    kw_t = kw.T
    lane_chunk = lax.broadcasted_iota(jnp.int32, (ML_DH, rows), 1) >> shift

    def advance(c, c_st, n_st, m_st, m_last):
        hi = (c + 1) * chunk
        decay = jnp.exp(bt_col[hi - 1:hi] + m_st - m_last)
        scale = jnp.exp(mloc_col[hi - 1:hi] - m_last)
        kw_c = jnp.where(lane_chunk == c, kw_t, 0.0).astype(BF16)
        c_new = decay * c_st + scale * _dot(kw_c, vb)
        n_new = decay * n_st + scale * jnp.sum(kw[hi - chunk:hi], axis=0, keepdims=True)
        return c_new, n_new

    if carry:
        c_st = c_scr[...]
        n_st = n_scr[...]
        m_st = m_scr[:, 0:1]
        ri = lax.broadcasted_iota(jnp.int32, (chunk, chunk), 0)
        ci = lax.broadcasted_iota(jnp.int32, (chunk, chunk), 1)
        tri_c = ci <= ri
        h_chunks = []
        for c in range(n_chunks):
            lo = c * chunk
            hi = lo + chunk
            bc = b_col[lo:hi]
            d_log = jnp.where(tri_c, bc - b_row[:, lo:hi] + i_row[:, lo:hi], neg_inf)
            g_inter = bc + m_st
            m_t = jnp.maximum(g_inter, jnp.max(d_log, axis=-1, keepdims=True))
            w_inter = jnp.exp(g_inter - m_t)
            s = _dot_nt(qb[lo:hi], kb[lo:hi]) * jnp.exp(d_log - m_t)
            num = w_inter * _dot(qb[lo:hi], c_st.astype(BF16)) + _dot(s.astype(BF16), vb[lo:hi])
            den = (w_inter * jnp.sum(q[lo:hi] * n_st, axis=-1, keepdims=True)
                   + jnp.sum(s, axis=-1, keepdims=True))
            h_chunks.append(num / jnp.maximum(jnp.abs(den), jnp.exp(-m_t)))
            m_last = m_t[chunk - 1:chunk]
            c_st, n_st = advance(c, c_st, n_st, m_st, m_last)
            m_st = m_last
        h = jnp.concatenate(h_chunks, axis=0)
        c_scr[...] = c_st
        n_scr[...] = n_st
        m_scr[...] = jnp.broadcast_to(m_st, (1, LANE))
    else:
        row_chunk1 = lax.broadcasted_iota(jnp.int32, (rows, 1), 0) >> shift
        row_chunk = lax.broadcasted_iota(jnp.int32, (rows, ML_DH), 0) >> shift
        m_rows = jnp.zeros((rows, 1), F32)
        n_rows = jnp.zeros((rows, ML_DH), F32)
        qc = jnp.zeros((rows, ML_DH), F32)
        for c in range(n_chunks):
            m_rows = jnp.where(row_chunk1 == c, m0_ref[c, 0][:, 0:1], m_rows)
            n_rows = jnp.where(row_chunk == c, n0_ref[c, 0], n_rows)
            qc = jnp.where(row_chunk == c, _dot(qb, c0_ref[c, 0].astype(BF16)), qc)
        d_log = jnp.where(tri, b_col - b_row + i_row, neg_inf)
        g_inter = b_col + m_rows
        m_t = jnp.maximum(g_inter, jnp.max(d_log, axis=-1, keepdims=True))
        w_inter = jnp.exp(g_inter - m_t)
        s = _dot_nt(qb, kb) * jnp.exp(d_log - m_t)
        num = w_inter * qc + _dot(s.astype(BF16), vb)
        den = w_inter * jnp.sum(q * n_rows, axis=-1, keepdims=True) + jnp.sum(s, axis=-1, keepdims=True)
        h = num / jnp.maximum(jnp.abs(den), jnp.exp(-m_t))
        for c in range(n_chunks):
            hi = (c + 1) * chunk
            m_last = m_t[hi - 1:hi]
            c_new, n_new = advance(c, c0_ref[c, 0], n0_ref[c, 0], m0_ref[c, 0][:, 0:1], m_last)
            cfin_ref[c, 0] = c_new
            nfin_ref[c, 0] = n_new
            mfin_ref[c, 0] = jnp.broadcast_to(m_last, (1, LANE))

    h_ref[...] = _rms(h) * gain_ref[0] * jax.nn.sigmoid(og_ref[...])


def _mlstm(z, bif, gain, c0, n0, m0, *, n_seq, seq_blocks, rows, chunk, row_block0, carry,
           layer=None, fill=()):
    per_block = 1 if carry else rows // chunk
    grid, rb, sem = _mixer_grid(n_seq, ML_HEADS // HEADS_PER_STEP, seq_blocks, row_block0, carry)
    zspec = functools.partial(_head_cols_spec, rows, rb)
    state_specs = [_state_spec(per_block, ML_DH, ML_DH, layer), _state_spec(per_block, 1, ML_DH, layer),
                   _state_spec(per_block, 1, LANE, layer)]
    n_in = 10
    return pl.pallas_call(
        _without_refs(functools.partial(_mlstm_kernel, rows=rows, chunk=chunk, carry=carry), n_in, len(fill)),
        grid=grid,
        in_specs=[zspec(ZB_MQ), zspec(ZB_MK), zspec(ZB_MV), zspec(ZB_MO),
                  pl.BlockSpec((rows, LANE), lambda *a: (rb(*a), ZB_MIF)),
                  pl.BlockSpec((1, LANE), lambda *a: (0, 0)),
                  _head_param_spec((1, LANE))]
                 + state_specs + [pl.BlockSpec(memory_space=pl.ANY)] * len(fill),
        out_specs=[_head_cols_spec(rows, rb, 0)] + state_specs,
        out_shape=[jax.ShapeDtypeStruct((z.shape[0], ML_WIDTH), F32),
                   jax.ShapeDtypeStruct(c0.shape, F32), jax.ShapeDtypeStruct(n0.shape, F32),
                   jax.ShapeDtypeStruct(m0.shape, F32)],
        input_output_aliases={n_in + i: i for i in range(len(fill))},
        scratch_shapes=[pltpu.VMEM((HEADS_PER_STEP, ML_DH, ML_DH), F32), pltpu.VMEM((HEADS_PER_STEP, 1, ML_DH), F32),
                        pltpu.VMEM((HEADS_PER_STEP, 1, LANE), F32)],
        compiler_params=_cparams(sem),
        name="mlstm_seq" if carry else "mlstm_step",
    )(z, z, z, z, z, bif, gain, c0, n0, m0, *fill)


def _s5_disc_kernel(lre_ref, lim_ref, ldt_ref, bre_ref, bim_ref, a_ref, bb_ref):
    lam_re = lre_ref[0]
    lam_im = lim_ref[0]
    dt = jnp.exp(ldt_ref[0])
    mag = jnp.exp(lam_re * dt)
    ar = mag * jnp.cos(lam_im * dt)
    ai = mag * jnp.sin(lam_im * dt)
    den = lam_re * lam_re + lam_im * lam_im
    fr = ((ar - 1.0) * lam_re + ai * lam_im) / den
    fi = (ai * lam_re - (ar - 1.0) * lam_im) / den
    b_re = bre_ref[0]
    b_im = bim_ref[0]
    a_ref[0, 0:1, :] = ar
    a_ref[0, 1:2, :] = ai
    bb_ref[0, 0] = fr * b_re - fi * b_im
    bb_ref[0, 1] = fr * b_im + fi * b_re


def _s5_discretise(lam_re, lam_im, log_dt, b_re, b_im):
    flat = lambda t: t.reshape(DEPTH, 1, S5_STATE)
    ldt = jnp.broadcast_to(log_dt[:, :, None], (DEPTH, S5_GROUPS, S5_P))
    chan = lambda t: t.reshape(DEPTH, S5_STATE, S5_CH).transpose(0, 2, 1)
    row = pl.BlockSpec((1, 1, S5_STATE), lambda l: (l, 0, 0))
    mat = pl.BlockSpec((1, S5_CH, S5_STATE), lambda l: (l, 0, 0))
    return pl.pallas_call(
        _s5_disc_kernel,
        grid=(DEPTH,),
        in_specs=[row, row, row, mat, mat],
        out_specs=[pl.BlockSpec((1, 2, S5_STATE), lambda l: (l, 0, 0)),
                   pl.BlockSpec((1, 2, S5_CH, S5_STATE), lambda l: (l, 0, 0, 0))],
        out_shape=[jax.ShapeDtypeStruct((DEPTH, 2, S5_STATE), F32),
                   jax.ShapeDtypeStruct((DEPTH, 2, S5_CH, S5_STATE), F32)],
        compiler_params=_cparams(("parallel",)),
        name="s5_disc",
    )(flat(lam_re), flat(lam_im), flat(ldt), chan(b_re), chan(b_im))


def _s5_kernel(u_ref, h0_ref, a_ref, wb_ref, wc_ref, d_ref, wglu_ref, bglu_ref, gain_ref,
               y_ref, hfin_ref, bu_scr, h_scr, *, steps, batch, lane_width):
    n_rows = steps * batch

    @pl.when(pl.program_id(0) == 0)
    def _():
        h_scr[...] = h0_ref[...]

    u = u_ref[...].reshape(n_rows, S5_WIDTH)
    ub = u.astype(BF16)
    for t in range(S5_TILES):
        bu = _dot(ub[:, t * LANE:(t + 1) * LANE], wb_ref[t])
        re0 = t * S5_TILE_STATE
        bu_scr[:, :, re0:re0 + S5_TILE_STATE] = bu[:, :S5_TILE_STATE].reshape(steps, batch, S5_TILE_STATE)
        bu_scr[:, :, S5_STATE + re0:S5_STATE + re0 + S5_TILE_STATE] = (
            bu[:, S5_TILE_STATE:].reshape(steps, batch, S5_TILE_STATE))

    for j in range(S5_STATE // lane_width):
        re = slice(j * lane_width, (j + 1) * lane_width)
        im = slice(S5_STATE + j * lane_width, S5_STATE + (j + 1) * lane_width)
        ar = a_ref[0:1, re]
        ai = a_ref[1:2, re]

        def step(t, carry):
            hr, hi = carry
            nr = ar * hr - ai * hi + bu_scr[t, :, re]
            ni = ar * hi + ai * hr + bu_scr[t, :, im]
            bu_scr[t, :, re] = nr
            bu_scr[t, :, im] = ni
            return nr, ni

        hr, hi = lax.fori_loop(0, steps, step, (h_scr[:, re], h_scr[:, im]), unroll=min(steps, 8))
        h_scr[:, re] = hr
        h_scr[:, im] = hi

    ys = []
    for t in range(S5_TILES):
        re0 = t * S5_TILE_STATE
        h_re = bu_scr[:, :, re0:re0 + S5_TILE_STATE].reshape(n_rows, S5_TILE_STATE).astype(BF16)
        h_im = bu_scr[:, :, S5_STATE + re0:S5_STATE + re0 + S5_TILE_STATE].reshape(n_rows, S5_TILE_STATE).astype(BF16)
        ys.append(_dot(h_re, wc_ref[t, :S5_TILE_STATE]) + _dot(h_im, wc_ref[t, S5_TILE_STATE:]))
    y = jnp.concatenate(ys, axis=1) + d_ref[...] * u
    y = jax.nn.gelu(y)
    out = y * jax.nn.sigmoid(_dot(y.astype(BF16), wglu_ref[...]) + bglu_ref[...])
    y_ref[...] = (_rms(out) * gain_ref[...]).reshape(steps, batch, S5_WIDTH)

    @pl.when(pl.program_id(0) == pl.num_programs(0) - 1)
    def _():
        hfin_ref[...] = h_scr[...]


def _s5(u_t, h0, a, wb, wc, d, wglu, bglu, gain, *, steps, lane_width):
    seq, batch, _ = u_t.shape
    const = lambda shape: _const_spec(shape, lambda t: (0,) * len(shape))
    return pl.pallas_call(
        functools.partial(_s5_kernel, steps=steps, batch=batch, lane_width=lane_width),
        grid=(seq // steps,),
        in_specs=[pl.BlockSpec((steps, batch, S5_WIDTH), lambda t: (t, 0, 0)),
                  const((batch, 2 * S5_STATE)), const((2, S5_STATE)),
                  const((S5_TILES, LANE, 2 * S5_TILE_STATE)), const((S5_TILES, 2 * S5_TILE_STATE, LANE)),
                  const((1, S5_WIDTH)), const((S5_WIDTH, S5_WIDTH)), const((1, S5_WIDTH)),
                  const((1, S5_WIDTH))],
        out_specs=[pl.BlockSpec((steps, batch, S5_WIDTH), lambda t: (t, 0, 0)),
                   pl.BlockSpec((batch, 2 * S5_STATE), lambda t: (0, 0))],
        out_shape=[jax.ShapeDtypeStruct((seq, batch, S5_WIDTH), F32),
                   jax.ShapeDtypeStruct((batch, 2 * S5_STATE), F32)],
        scratch_shapes=[pltpu.VMEM((steps, batch, 2 * S5_STATE), F32),
                        pltpu.VMEM((batch, 2 * S5_STATE), F32)],
        compiler_params=_cparams(("arbitrary",)),
        name="s5",
    )(u_t, h0, a, wb, wc, d, wglu, bglu, gain)


def _out_proj_kernel(og_ref, om_ref, os_ref, x_ref, g1_ref, sc_ref, sh_ref, gain_ref,
                     wg_ref, wm_ref, ws_ref, xo_ref, ht_ref, mix_scr, h_scr, *, groups):
    mix_scr[...] = (_dot(og_ref[...].astype(BF16), wg_ref[...])
                    + _dot(om_ref[...].astype(BF16), wm_ref[...])
                    + _dot(os_ref[...].astype(BF16), ws_ref[...]))
    gain = gain_ref[...]

    def body(g, carry):
        r0 = pl.multiple_of(g * TOKENS_PER_GROUP, TOKENS_PER_GROUP)
        rows = pl.ds(r0, TOKENS_PER_GROUP)
        xn = x_ref[rows, :] + g1_ref[pl.ds(g, 1), :] * mix_scr[rows, :]
        xo_ref[rows, :] = xn
        h_scr[rows, :] = _modulated_group(xn, gain, sc_ref[pl.ds(g, 1), :], sh_ref[pl.ds(g, 1), :])
        return carry

    lax.fori_loop(0, groups, body, 0, unroll=GROUP_UNROLL)
    ht_ref[...] = pltpu.bitcast(h_scr[...].T.astype(BF16), jnp.uint32)


def _out_proj(og, om, os_, x, mod, gains, w_out, *, layer, tt):
    n_tok = x.shape[0]
    groups = tt // TOKENS_PER_GROUP
    tok = lambda width: pl.BlockSpec((tt, width), lambda i: (i, 0))
    w_rows = lambda height, block: _const_spec((None, height, D_MODEL), lambda i: (layer, block, 0))
    assert GLA_WIDTH == ML_WIDTH and (GLA_WIDTH + ML_WIDTH) % S5_WIDTH == 0
    return pl.pallas_call(
        functools.partial(_out_proj_kernel, groups=groups),
        grid=(n_tok // tt,),
        in_specs=[tok(GLA_WIDTH), tok(ML_WIDTH), tok(S5_WIDTH), tok(D_MODEL),
                  _mod_spec(groups, layer, MOD_G1), _mod_spec(groups, layer, MOD_SC2),
                  _mod_spec(groups, layer, MOD_SH2), _gain_spec(layer, 1),
                  w_rows(GLA_WIDTH, 0), w_rows(ML_WIDTH, 1),
                  w_rows(S5_WIDTH, (GLA_WIDTH + ML_WIDTH) // S5_WIDTH)],
        out_specs=[tok(D_MODEL), pl.BlockSpec((D_MODEL // 2, tt), lambda i: (0, i))],
        out_shape=[jax.ShapeDtypeStruct((n_tok, D_MODEL), F32),
                   jax.ShapeDtypeStruct((D_MODEL // 2, n_tok), jnp.uint32)],
        scratch_shapes=[pltpu.VMEM((tt, D_MODEL), F32), pltpu.VMEM((tt, D_MODEL), F32)],
        compiler_params=_cparams(("parallel",)),
        name="out_proj",
    )(og, om, os_, x, mod, mod, mod, gains, w_out, w_out, w_out)


def _top_values(cur, count, out_scr=None, with_rank=False):
    neg_inf = jnp.float32(-jnp.inf)
    vals = []
    rank = jnp.full(cur.shape, float(count), F32) if with_rank else None
    for r in range(count):
        m = jnp.max(cur, axis=0, keepdims=True)
        vals.append(m)
        if out_scr is not None:
            out_scr[r:r + 1, :] = m
        hit = cur == m
        if with_rank:
            rank = jnp.where(hit, float(r), rank)
        if r + 1 < count:
            cur = jnp.where(hit, neg_inf, cur)
    return (vals, rank) if with_rank else vals


def _peer_route_kernel(ht_ref, wqt_ref, keys_ref, first_ref, second_ref, qt_scr, v1_scr, v2_scr, cand_scr,
                       *, tt):
    qt_scr[...] = _dot(wqt_ref[...], _unpack_rows(ht_ref[...]))
    for h in range(PEER_HEADS):
        sc = []
        for side in range(2):
            hs = 2 * h + side
            qb = qt_scr[hs * LANE:(hs + 1) * LANE, :].astype(BF16)
            sc.append(_dot(keys_ref[hs], qb))
        for lt in range(tt // LANE):
            ls = slice(lt * LANE, (lt + 1) * LANE)
            s1 = sc[0][:, ls]
            s2 = sc[1][:, ls]
            _top_values(s1, PEER_TOPK, v1_scr)
            _, rank2 = _top_values(s2, PEER_TOPK, v2_scr, with_rank=True)
            v1_head = v1_scr[0:SUBLANE, :]
            cand_scr[0:PEER_TOPK, :] = v1_scr[0:1, :] + v2_scr[...]
            cand_scr[PEER_TOPK:2 * PEER_TOPK, :] = v1_scr[...] + v2_scr[0:1, :]
            cand_scr[2 * PEER_TOPK:2 * PEER_TOPK + SUBLANE, :] = v1_scr[1:2, :] + v2_scr[0:SUBLANE, :]
            for b in range(1, 5):
                lo = 2 * PEER_TOPK + b * SUBLANE
                cand_scr[lo:lo + SUBLANE, :] = v1_head + v2_scr[b:b + 1, :]
            top = _top_values(cand_scr[...], PEER_TOPK)
            z = jnp.zeros_like(top[0])
            for tv in top:
                z = z + jnp.exp(tv - top[0])
            tau = top[PEER_TOPK - 1]
            n1 = jnp.zeros_like(s1)
            for b in range(SUBLANE):
                n1 = n1 + jnp.where(s1 + v2_scr[b:b + 1, :] >= tau, 1.0, 0.0)
            best = v1_scr[0:1, :]
            tail = jnp.zeros_like(best)
            for b in range(SUBLANE, PEER_TOPK):
                tail = tail + jnp.where(best + v2_scr[b:b + 1, :] >= tau, 1.0, 0.0)
            n1 = n1 + jnp.where(s1 == best, tail, 0.0)
            first_ref[2 * h, :, ls] = n1
            first_ref[2 * h + 1, :, ls] = jnp.exp(s1 - v1_scr[0:1, :]) / z
            second_ref[2 * h, :, ls] = rank2.astype(BF16)
            second_ref[2 * h + 1, :, ls] = jnp.exp(s2 - v2_scr[0:1, :]).astype(BF16)


def _peer_route(ht, wqt, keys, *, tt):
    n_tok = ht.shape[1]
    n_hs = 2 * PEER_HEADS
    assert PEER_TOPK == 2 * SUBLANE
    n_cand = 2 * PEER_TOPK + 5 * SUBLANE
    return pl.pallas_call(
        functools.partial(_peer_route_kernel, tt=tt),
        grid=(n_tok // tt,),
        in_specs=[pl.BlockSpec((D_MODEL // 2, tt), lambda i: (0, i)),
                  _const_spec((n_hs * LANE, D_MODEL), lambda i: (0, 0)),
                  _const_spec((n_hs, N_KEYS, LANE), lambda i: (0, 0, 0))],
        out_specs=[pl.BlockSpec((n_hs, N_KEYS, tt), lambda i: (0, 0, i)),
                   pl.BlockSpec((n_hs, N_KEYS, tt), lambda i: (0, 0, i))],
        out_shape=[jax.ShapeDtypeStruct((n_hs, N_KEYS, n_tok), F32),
                   jax.ShapeDtypeStruct((n_hs, N_KEYS, n_tok), BF16)],
        scratch_shapes=[pltpu.VMEM((n_hs * LANE, tt), F32), pltpu.VMEM((PEER_TOPK, LANE), F32),
                        pltpu.VMEM((PEER_TOPK, LANE), F32), pltpu.VMEM((n_cand, LANE), F32)],
        compiler_params=_cparams(("parallel",)),
        name="peer_route",
    )(ht, wqt, keys)


DENSE_FIRST_KEYS = 4
DENSE_SUB = DENSE_FIRST_KEYS * N_KEYS
GATE_ROWS = 4 * SUBLANE
GATE_PIECES = 4


def _gate_times_act(first_ref, row0, second_ref, act_ref, p_ref, lane_tiles):
    n_pieces = N_KEYS // GATE_ROWS
    zero = jnp.zeros((GATE_ROWS, LANE), BF16)
    for lt in lane_tiles:
        ls = slice(lt * LANE, (lt + 1) * LANE)
        for jp0 in range(0, n_pieces, GATE_PIECES):
            pieces = range(jp0, jp0 + GATE_PIECES)
            gates = [[zero for _ in pieces] for _ in range(DENSE_FIRST_KEYS)]
            for h in range(PEER_HEADS):
                counts, weights = [], []
                for ii in range(DENSE_FIRST_KEYS):
                    r = row0 + ii
                    counts.append(jnp.broadcast_to(first_ref[2 * h, 0, r:r + 1, ls], (GATE_ROWS, LANE)).astype(BF16))
                    weights.append(jnp.broadcast_to(first_ref[2 * h + 1, 0, r:r + 1, ls], (GATE_ROWS, LANE)).astype(BF16))
                for q, jp in enumerate(pieces):
                    js = slice(jp * GATE_ROWS, (jp + 1) * GATE_ROWS)
                    rank2 = second_ref[2 * h, js, ls]
                    e2 = second_ref[2 * h + 1, js, ls]
                    for ii in range(DENSE_FIRST_KEYS):
                        gates[ii][q] = gates[ii][q] + jnp.where(rank2 < counts[ii], e2, zero) * weights[ii]
            for ii in range(DENSE_FIRST_KEYS):
                for q, jp in enumerate(pieces):
                    rs = slice(ii * N_KEYS + jp * GATE_ROWS, ii * N_KEYS + (jp + 1) * GATE_ROWS)
                    p_ref[rs, ls] = gates[ii][q] * act_ref[rs, ls].astype(BF16)


def _peer_dense_kernel(ht_ref, u_ref, vt_ref, first_a_ref, first_b_ref, second_in_ref,
                       x_ref, g2_ref, o_ref, acc_scr, act0, act1, p0, p1, second_ref, *, tt, groups):
    k = pl.program_id(1)

    @pl.when(k == 0)
    def _():
        acc_scr[...] = jnp.zeros_like(acc_scr)
        act1[...] = jnp.zeros_like(act1)
        p0[...] = jnp.zeros_like(p0)
        second_ref[...] = second_in_ref[...]

    tok_half = tt // 2
    tiles_half = tok_half // LANE
    for half in range(2):
        ts = slice(half * tok_half, (half + 1) * tok_half)
        lane_tiles = range(half * tiles_half, (half + 1) * tiles_half)
        act0[:, ts] = jax.nn.gelu(_dot(_unpack_rows(u_ref[0:DENSE_SUB // 2, :]), _unpack_rows(ht_ref[:, ts])))
        _gate_times_act(first_a_ref, DENSE_FIRST_KEYS, second_ref, act1, p1, lane_tiles)
        acc_scr[:, ts] += _dot(_unpack_rows(vt_ref[:, 0:DENSE_SUB]), p0[:, ts])
    for half in range(2):
        ts = slice(half * tok_half, (half + 1) * tok_half)
        lane_tiles = range(half * tiles_half, (half + 1) * tiles_half)
        act1[:, ts] = jax.nn.gelu(_dot(_unpack_rows(u_ref[DENSE_SUB // 2:, :]), _unpack_rows(ht_ref[:, ts])))
        _gate_times_act(first_b_ref, 0, second_ref, act0, p0, lane_tiles)
        acc_scr[:, ts] += _dot(_unpack_rows(vt_ref[:, DENSE_SUB:]), p1[:, ts])

    @pl.when(k == pl.num_programs(1) - 1)
    def _():
        o_ref[...] = acc_scr[...].T

        def body(g, carry):
            rows = pl.ds(pl.multiple_of(g * TOKENS_PER_GROUP, TOKENS_PER_GROUP), TOKENS_PER_GROUP)
            o_ref[rows, :] = x_ref[rows, :] + g2_ref[pl.ds(g, 1), :] * o_ref[rows, :]
            return carry

        lax.fori_loop(0, groups, body, 0, unroll=GROUP_UNROLL)


def _peer_dense(ht, u_bf, vt_bf, first, second, x, mod, *, layer, tt):
    n_tok = x.shape[0]
    groups = tt // TOKENS_PER_GROUP
    n_hs = 2 * PEER_HEADS
    ne = 2 * DENSE_SUB
    n_i = ne // N_KEYS
    n_blocks = N_EXPERTS // ne
    tok_const = lambda shape, imap: pl.BlockSpec(shape, imap, pipeline_mode=pl.Buffered(1))
    by_first_key = lambda t: t.reshape(n_hs, n_blocks, n_i, n_tok)
    cur = lambda k: jnp.minimum(k, n_blocks - 1)
    prev = lambda k: jnp.maximum(k - 1, 0)
    key_spec = lambda blk: pl.BlockSpec((n_hs, 1, n_i, tt), lambda i, k: (0, blk(k), 0, i))
    return pl.pallas_call(
        functools.partial(_peer_dense_kernel, tt=tt, groups=groups),
        grid=(n_tok // tt, n_blocks + 1),
        in_specs=[tok_const((D_MODEL // 2, tt), lambda i, k: (0, i)),
                  pl.BlockSpec((None, ne // 2, D_MODEL), lambda i, k: (layer, cur(k), 0)),
                  pl.BlockSpec((None, None, D_MODEL // 2, ne), lambda i, k: (layer, prev(k), 0, 0)),
                  key_spec(prev), key_spec(cur),
                  tok_const((n_hs, N_KEYS, tt), lambda i, k: (0, 0, i)),
                  tok_const((tt, D_MODEL), lambda i, k: (i, 0)),
                  _mod_spec(groups, layer, MOD_G2)],
        out_specs=pl.BlockSpec((tt, D_MODEL), lambda i, k: (i, 0)),
        out_shape=jax.ShapeDtypeStruct((n_tok, D_MODEL), F32),
        scratch_shapes=[pltpu.VMEM((D_MODEL, tt), F32),
                        pltpu.VMEM((DENSE_SUB, tt), F32), pltpu.VMEM((DENSE_SUB, tt), F32),
                        pltpu.VMEM((DENSE_SUB, tt), BF16), pltpu.VMEM((DENSE_SUB, tt), BF16),
                        pltpu.VMEM((n_hs, N_KEYS, tt), BF16)],
        compiler_params=_cparams(("parallel", "arbitrary")),
        name="peer_dense",
    )(ht, u_bf, vt_bf, by_first_key(first), by_first_key(first), second, x, mod)


def _pack_experts_kernel(w_ref, o_ref, *, transpose):
    w = w_ref[...]
    if transpose:
        w = w.T
    o_ref[...] = pltpu.bitcast(w.astype(BF16), jnp.uint32)


def _pack_experts(w, *, transpose):
    blk = 2 * DENSE_SUB
    n_blocks = N_EXPERTS // blk
    if transpose:
        out_spec = pl.BlockSpec((None, None, D_MODEL // 2, blk), lambda l, j: (l, j, 0, 0))
        out_shape = (DEPTH, n_blocks, D_MODEL // 2, blk)
    else:
        out_spec = pl.BlockSpec((None, blk // 2, D_MODEL), lambda l, j: (l, j, 0))
        out_shape = (DEPTH, N_EXPERTS // 2, D_MODEL)
    return pl.pallas_call(
        functools.partial(_pack_experts_kernel, transpose=transpose),
        grid=(DEPTH, n_blocks),
        in_specs=[pl.BlockSpec((None, blk, D_MODEL), lambda l, j: (l, j, 0))],
        out_specs=out_spec,
        out_shape=jax.ShapeDtypeStruct(out_shape, jnp.uint32),
        compiler_params=_cparams(("parallel", "parallel")),
        name="pack_experts",
    )(w)


def _final_norm_kernel(x_ref, gain_ref, o_ref):
    o_ref[...] = _rms(x_ref[...]) * gain_ref[...]


def _final_norm(x, gain, *, tt, row0, n_rows):
    return pl.pallas_call(
        _final_norm_kernel,
        grid=(n_rows // tt,),
        in_specs=[pl.BlockSpec((tt, D_MODEL), lambda i: (row0 // tt + i, 0)),
                  pl.BlockSpec((1, D_MODEL), lambda i: (0, 0))],
        out_specs=pl.BlockSpec((tt, D_MODEL), lambda i: (i, 0)),
        out_shape=jax.ShapeDtypeStruct((n_rows, D_MODEL), F32),
        compiler_params=_cparams(("parallel",)),
        name="final_norm",
    )(x, gain)


PROMPT_ROWS = 256
STEP_ROWS = 128
S5_PROMPT_STEPS = 128


def _block_diag_in(bbar):
    rows_group = np.arange(S5_WIDTH) // S5_CH
    cols_group = np.arange(S5_STATE) // S5_P
    mask = jnp.asarray(rows_group[:, None] == cols_group[None, :])
    reps = (1,) * (bbar.ndim - 2) + (S5_GROUPS, 1)
    return jnp.where(mask, jnp.tile(bbar, reps), 0.0)


def _diag_tiles(w, rows, cols):
    return jnp.stack([w[:, t * rows:(t + 1) * rows, t * cols:(t + 1) * cols]
                      for t in range(w.shape[1] // rows)], axis=1)


def _block_diag_out(c):
    rows_group = np.arange(S5_STATE) // S5_P
    cols_group = np.arange(S5_WIDTH) // S5_CH
    mask = jnp.asarray(rows_group[:, None] == cols_group[None, :])
    per_state = c.transpose(0, 1, 3, 2).reshape(c.shape[0], S5_STATE, S5_CH)
    return jnp.where(mask, jnp.tile(per_state, (1, 1, S5_GROUPS)), 0.0)


def kernel(x_prompt, x_sample, state_gla, state_mlstm_c, state_mlstm_n, state_mlstm_m, state_s5_re, state_s5_im, c_prompt, c_sample, w_ada, b_ada, norm_gain, w_in, gla_w_gate_up, gla_b_gate, gla_norm_gain, ml_b_igate, ml_b_fgate, ml_norm_gain, s5_lambda_re, s5_lambda_im, s5_log_dt, s5_b_re, s5_b_im, s5_c_re, s5_c_im, s5_d, s5_w_glu, s5_b_glu, s5_norm_gain, w_out, peer_w_q, peer_sub_keys, peer_u, peer_v, final_gain):
    bp, lp, _ = x_prompt.shape
    bs, ls, _ = x_sample.shape
    assert ls == TOKENS_PER_GROUP and lp % PROMPT_ROWS == 0 and (bs * ls) % STEP_ROWS == 0
    n_prompt = bp * lp
    n_sample = bs * ls
    n_tok = n_prompt + n_sample
    rep = lp // TOKENS_PER_GROUP

    x = jnp.concatenate([x_prompt.reshape(n_prompt, D_MODEL), x_sample.reshape(n_sample, D_MODEL)], axis=0)

    c_all = jnp.concatenate([c_prompt, c_sample], axis=0)
    n_c = c_all.shape[0]
    c_pad = (-n_c) % SUBLANE
    mod = _adaln(jnp.pad(c_all, ((0, c_pad), (0, 0))), w_ada, b_ada)
    mod = jnp.concatenate([jnp.repeat(mod[:, :bp], rep, axis=1), mod[:, bp:n_c]], axis=1)
    gains = norm_gain.reshape(DEPTH * 2, 1, D_MODEL)

    zeros = lambda *shape: jnp.zeros(shape, F32)
    seq_blocks = lp // PROMPT_ROWS
    step_blocks = n_sample // STEP_ROWS
    step_block0 = n_prompt // STEP_ROWS
    prompt_pad = (-bp) % SUBLANE

    w_in_p = _pad_z_columns(w_in.astype(BF16))
    wg_all = jnp.zeros((DEPTH, GLA_HEADS, LANE, LANE), F32).at[:, :, :GLA_RANK, :GLA_DK].set(
        gla_w_gate_up.reshape(DEPTH, GLA_RANK, GLA_HEADS, GLA_DK).transpose(0, 2, 1, 3)).astype(BF16)
    bg_all = jnp.zeros((DEPTH, GLA_HEADS, 1, LANE), F32).at[:, :, 0, :GLA_DK].set(
        gla_b_gate.reshape(DEPTH, GLA_HEADS, GLA_DK))
    gla_gain_all = gla_norm_gain.reshape(DEPTH, GLA_HEADS, 1, GLA_DV)
    bif_all = jnp.zeros((DEPTH, 1, LANE), F32).at[:, 0, :ML_HEADS].set(ml_b_igate).at[
        :, 0, F_GATE_LANE:F_GATE_LANE + ML_HEADS].set(ml_b_fgate)
    ml_gain_all = ml_norm_gain.reshape(DEPTH, ML_HEADS, 1, ML_DH)
    a_disc, bbar = _s5_discretise(s5_lambda_re, s5_lambda_im, s5_log_dt, s5_b_re, s5_b_im)
    wb_all = jnp.concatenate([_diag_tiles(_block_diag_in(bbar[:, 0]), LANE, S5_TILE_STATE),
                              _diag_tiles(_block_diag_in(bbar[:, 1]), LANE, S5_TILE_STATE)], axis=3).astype(BF16)
    wc_all = jnp.concatenate([_diag_tiles(_block_diag_out(s5_c_re), S5_TILE_STATE, LANE),
                              -_diag_tiles(_block_diag_out(s5_c_im), S5_TILE_STATE, LANE)], axis=2).astype(BF16)
    wglu_all = s5_w_glu.astype(BF16)
    w_out_bf = w_out.astype(BF16)
    wqt_all = peer_w_q.astype(BF16).transpose(0, 2, 1)
    keys_all = peer_sub_keys.reshape(DEPTH, 2 * PEER_HEADS, N_KEYS, LANE).astype(BF16)
    u_bf = _pack_experts(peer_u.astype(F32), transpose=False)
    vt_bf = _pack_experts(peer_v.astype(F32), transpose=True)

    st_gla = state_gla.astype(F32)
    st_c = state_mlstm_c.astype(F32)
    st_n = state_mlstm_n.astype(F32)[:, :, :, None, :]
    st_m = jnp.broadcast_to(state_mlstm_m.astype(F32)[:, :, :, None, None], state_mlstm_m.shape + (1, LANE))
    st_s5 = jnp.concatenate([state_s5_re.reshape(DEPTH, bs, S5_STATE), state_s5_im.reshape(DEPTH, bs, S5_STATE)],
                            axis=2).astype(F32)

    sg_s = mc_s = mn_s = mm_s = None
    new_states = []
    for l in range(DEPTH):
        z = _norm_proj(x, mod, gains, w_in_p, layer=l, tt=1024, tn=1152)

        seq_kw = dict(n_seq=bp, seq_blocks=seq_blocks, rows=PROMPT_ROWS, row_block0=0, carry=True)
        step_kw = dict(n_seq=step_blocks, seq_blocks=1, rows=STEP_ROWS, chunk=ls, row_block0=step_block0,
                       carry=False, layer=l)

        gla_w = (wg_all[l], bg_all[l], gla_gain_all[l])
        og, sg_p = _gla(z, *gla_w, zeros(bp, GLA_HEADS, GLA_DK, GLA_DV), chunk=GLA_CHUNK, **seq_kw)
        og, sg_s = _gla(z, *gla_w, st_gla, fill=(og,) if l == 0 else (og, sg_s), **step_kw)

        ml_w = (bif_all[l], ml_gain_all[l])
        om, mc_p, mn_p, mm_p = _mlstm(z, *ml_w, zeros(bp, ML_HEADS, ML_DH, ML_DH), zeros(bp, ML_HEADS, 1, ML_DH),
                                      zeros(bp, ML_HEADS, 1, LANE), chunk=ML_CHUNK, **seq_kw)
        om, mc_s, mn_s, mm_s = _mlstm(z, *ml_w, st_c, st_n, st_m,
                                      fill=(om,) if l == 0 else (om, mc_s, mn_s, mm_s), **step_kw)

        su = z[:, ZB_SU * LANE:ZB_SU * LANE + S5_WIDTH]
        u_p = jnp.pad(su[:n_prompt].reshape(bp, lp, S5_WIDTH).transpose(1, 0, 2), ((0, 0), (0, prompt_pad), (0, 0)))
        u_s = su[n_prompt:].reshape(bs, ls, S5_WIDTH).transpose(1, 0, 2)
        s5_args = (a_disc[l], wb_all[l], wc_all[l], s5_d[l].reshape(1, S5_WIDTH), wglu_all[l],
                   s5_b_glu[l][None, :], s5_norm_gain[l][None, :])
        os_p, hs_p = _s5(u_p, zeros(bp + prompt_pad, 2 * S5_STATE), *s5_args,
                         steps=S5_PROMPT_STEPS, lane_width=S5_STATE)
        os_s, hs_s = _s5(u_s, st_s5[l], *s5_args, steps=ls, lane_width=LANE)
        os_ = jnp.concatenate([os_p[:, :bp].transpose(1, 0, 2).reshape(n_prompt, S5_WIDTH),
                               os_s.transpose(1, 0, 2).reshape(n_sample, S5_WIDTH)], axis=0)

        x, ht = _out_proj(og, om, os_, x, mod, gains, w_out_bf, layer=l, tt=256)

        first, second = _peer_route(ht, wqt_all[l], keys_all[l], tt=256)
        x = _peer_dense(ht, u_bf, vt_bf, first, second, x, mod, layer=l, tt=512)

        new_states.append((sg_p, mc_p, mn_p[:, :, 0, :], mm_p[:, :, 0, 0], hs_p[:bp], hs_s))

    y_prompt = _final_norm(x, final_gain[None, :], tt=512, row0=0, n_rows=n_prompt)
    y_sample = _final_norm(x, final_gain[None, :], tt=512, row0=n_prompt, n_rows=n_sample)
    stack = lambda i: jnp.stack([ns[i] for ns in new_states])
    s5_p, s5_s = stack(4), stack(5)
    split_s5 = lambda h, lo: h[:, :, lo:lo + S5_STATE].reshape(DEPTH, -1, S5_GROUPS, S5_P)
    outs = (y_prompt.reshape(bp, lp, D_MODEL), y_sample.reshape(bs, ls, D_MODEL),
            stack(0), sg_s, stack(1), mc_s, stack(2), mn_s[:, :, :, 0, :], stack(3), mm_s[:, :, :, 0, 0],
            split_s5(s5_p, 0), split_s5(s5_s, 0), split_s5(s5_p, S5_STATE), split_s5(s5_s, S5_STATE))
    refs = (x_prompt, x_sample, state_gla, state_gla, state_mlstm_c, state_mlstm_c, state_mlstm_n, state_mlstm_n,
            state_mlstm_m, state_mlstm_m, state_s5_re, state_s5_re, state_s5_im, state_s5_im)
    return tuple(o.astype(r.dtype) for o, r in zip(outs, refs))
```

```python
import functools
import math

import numpy as np
import jax
import jax.numpy as jnp
from jax import lax
from jax.experimental import pallas as pl
from jax.experimental.pallas import tpu as pltpu

F32 = jnp.float32
BF16 = jnp.bfloat16

D_MODEL = 2048
DEPTH = 4
GLA_HEADS = 6
GLA_DK = 64
GLA_DV = 128
GLA_RANK = 16
GLA_TAU = 16.0
GLA_CHUNK = 16
ML_HEADS = 6
ML_DH = 128
ML_CHUNK = 64
S5_WIDTH = 512
S5_CH = 16
S5_GROUPS = 32
S5_P = 64
S5_STATE = S5_GROUPS * S5_P
S5_TILES = S5_WIDTH // 128
S5_TILE_STATE = S5_STATE // S5_TILES
PEER_HEADS = 8
N_KEYS = 128
N_EXPERTS = N_KEYS * N_KEYS
PEER_TOPK = 16
NORM_EPS = 1e-6
GLA_WIDTH = GLA_HEADS * GLA_DV
ML_WIDTH = ML_HEADS * ML_DH

LANE = 128
SUBLANE = 8
TOKENS_PER_GROUP = 8
GROUP_UNROLL = 4
VMEM_LIMIT = 56 * 1024 * 1024

ZB_GQ, ZB_GK, ZB_GV, ZB_GG = 0, 6, 12, 18
ZB_MQ, ZB_MK, ZB_MV, ZB_MO = 24, 30, 36, 42
ZB_GR, ZB_MIF = 48, 49
HEADS_PER_STEP = 6
ZB_SU = 50
Z_BLOCKS = 54
Z_COLS = Z_BLOCKS * LANE
F_GATE_LANE = 8


def _z_source_columns():
    src = np.full((Z_COLS,), -1, np.int32)
    off_gq, off_gk, off_gv, off_gg, off_gr = 0, 384, 768, 1536, 2304
    off_mq, off_mk, off_mv, off_mo, off_mi, off_mf, off_su = 2320, 3088, 3856, 4624, 5392, 5398, 5404
    for h in range(GLA_HEADS):
        for d in range(GLA_DK):
            src[(ZB_GQ + h) * LANE + d] = off_gq + h * GLA_DK + d
            src[(ZB_GK + h) * LANE + d] = off_gk + h * GLA_DK + d
        for d in range(GLA_DV):
            src[(ZB_GV + h) * LANE + d] = off_gv + h * GLA_DV + d
            src[(ZB_GG + h) * LANE + d] = off_gg + h * GLA_DV + d
    for d in range(GLA_RANK):
        src[ZB_GR * LANE + d] = off_gr + d
    for h in range(ML_HEADS):
        for d in range(ML_DH):
            src[(ZB_MQ + h) * LANE + d] = off_mq + h * ML_DH + d
            src[(ZB_MK + h) * LANE + d] = off_mk + h * ML_DH + d
            src[(ZB_MV + h) * LANE + d] = off_mv + h * ML_DH + d
            src[(ZB_MO + h) * LANE + d] = off_mo + h * ML_DH + d
        src[ZB_MIF * LANE + h] = off_mi + h
        src[ZB_MIF * LANE + F_GATE_LANE + h] = off_mf + h
    for d in range(S5_WIDTH):
        src[ZB_SU * LANE + d] = off_su + d
    return src


_Z_SRC = _z_source_columns()


def _z_runs():
    runs, i = [], 0
    while i < Z_COLS:
        j = i + 1
        if _Z_SRC[i] < 0:
            while j < Z_COLS and _Z_SRC[j] < 0:
                j += 1
            runs.append((-1, j - i))
        else:
            while j < Z_COLS and _Z_SRC[j] == _Z_SRC[j - 1] + 1:
                j += 1
            runs.append((int(_Z_SRC[i]), j - i))
        i = j
    return runs


_Z_RUNS = _z_runs()


def _pad_z_columns(w):
    parts = [jnp.zeros(w.shape[:-1] + (n,), w.dtype) if s < 0 else w[..., s:s + n] for s, n in _Z_RUNS]
    return jnp.concatenate(parts, axis=-1)


def _cparams(semantics):
    return pltpu.CompilerParams(dimension_semantics=semantics, vmem_limit_bytes=VMEM_LIMIT)


def _const_spec(block_shape, index_map):
    return pl.BlockSpec(block_shape, index_map, pipeline_mode=pl.Buffered(1))


def _rms(x):
    return x * lax.rsqrt(jnp.mean(x * x, axis=-1, keepdims=True) + NORM_EPS)


def _unpack_rows(x):
    return pltpu.bitcast(x, BF16)


def _dot(a, b):
    return jnp.dot(a, b, preferred_element_type=F32)


def _dot_nt(a, b):
    return lax.dot_general(a, b, (((1,), (1,)), ((), ())), preferred_element_type=F32)


def _chunk_sums(same, tri, x):
    rows = x.shape[0]
    masks = jnp.concatenate([jnp.where(tri, 1.0, 0.0), jnp.where(same, 1.0, 0.0)], axis=0).astype(BF16)
    hi = x.astype(BF16)
    r1 = x - hi.astype(F32)
    mid = r1.astype(BF16)
    lo = (r1 - mid.astype(F32)).astype(BF16)
    sums = _dot(masks, jnp.concatenate([hi, mid, lo], axis=1))
    total = sums[:, 0:LANE] + sums[:, LANE:2 * LANE] + sums[:, 2 * LANE:3 * LANE]
    return total[0:rows], total[rows:2 * rows]


def _adaln_kernel(c_ref, w_lo_ref, w_hi_ref, b_ref, o_ref):
    c = c_ref[...]
    s = (c * jax.nn.sigmoid(c)).astype(BF16)
    half = D_MODEL // 2
    o_ref[0] = (_dot(s[:, :half], w_lo_ref[0].astype(BF16)) + _dot(s[:, half:], w_hi_ref[0].astype(BF16))
                + b_ref[0])


def _adaln(c_all, w_ada, b_ada):
    n_rows = c_all.shape[0]
    n_out = w_ada.shape[-1]
    tn = 1024
    w_half = lambda which: pl.BlockSpec((1, D_MODEL // 2, tn), lambda l, j: (l, which, j))
    return pl.pallas_call(
        _adaln_kernel,
        grid=(DEPTH, n_out // tn),
        in_specs=[
            pl.BlockSpec((n_rows, D_MODEL), lambda l, j: (0, 0)),
            w_half(0), w_half(1),
            pl.BlockSpec((1, 1, tn), lambda l, j: (l, 0, j)),
        ],
        out_specs=pl.BlockSpec((1, n_rows, tn), lambda l, j: (l, 0, j)),
        out_shape=jax.ShapeDtypeStruct((DEPTH, n_rows, n_out), F32),
        compiler_params=_cparams(("parallel", "parallel")),
        name="adaln",
    )(c_all, w_ada, w_ada, b_ada.reshape(DEPTH, 1, n_out))


def _modulated_group(x8, gain, sc_row, sh_row):
    return _rms(x8) * gain * (1.0 + sc_row) + sh_row


class _ModRows:
    def __init__(self, prompt_ref, sample_ref, prompt_tiles, tiles_per_seq):
        tile = pl.program_id(0)
        self.is_prompt = tile < prompt_tiles
        self.seq = jnp.minimum(tile // tiles_per_seq, SUBLANE - 1)
        self.prompt_ref = prompt_ref
        self.sample_ref = sample_ref

    def row(self, g):
        return jnp.where(self.is_prompt, self.prompt_ref[pl.ds(self.seq, 1), :], self.sample_ref[pl.ds(g, 1), :])


def _norm_proj_kernel(scp_ref, scs_ref, shp_ref, shs_ref, x_ref, gain_ref, w_ref, o_ref, h_scr, *, groups, tiles):
    sc = _ModRows(scp_ref, scs_ref, *tiles)
    sh = _ModRows(shp_ref, shs_ref, *tiles)

    @pl.when(pl.program_id(1) == 0)
    def _():
        gain = gain_ref[...]

        def body(p, carry):
            hs = []
            for u in range(2):
                g = p * 2 + u
                r0 = pl.multiple_of(g * TOKENS_PER_GROUP, TOKENS_PER_GROUP)
                hs.append(_modulated_group(x_ref[pl.ds(r0, TOKENS_PER_GROUP), :], gain, sc.row(g), sh.row(g)))
            r = pl.multiple_of(p * 2 * TOKENS_PER_GROUP, 2 * TOKENS_PER_GROUP)
            h_scr[pl.ds(r, 2 * TOKENS_PER_GROUP), :] = jnp.concatenate(hs, axis=0).astype(BF16)
            return carry

        lax.fori_loop(0, groups // 2, body, 0, unroll=GROUP_UNROLL)

    o_ref[...] = _dot(h_scr[...], w_ref[...])


MOD_SH1, MOD_SC1, MOD_G1, MOD_SH2, MOD_SC2, MOD_G2 = range(6)


class _TokenLayout:
    def __init__(self, n_prompt, seq_len, n_sample_seq):
        self.n_prompt, self.seq_len, self.n_sample_seq = n_prompt, seq_len, n_sample_seq

    def tiles(self, tt):
        assert self.n_prompt % tt == 0 and self.seq_len % tt == 0
        return self.n_prompt // tt, self.seq_len // tt

    def mod_specs(self, tt, layer, kind):
        groups = tt // TOKENS_PER_GROUP
        prompt_tiles, _ = self.tiles(tt)
        assert self.n_sample_seq % groups == 0 and self.n_sample_seq % SUBLANE == 0
        return [pl.BlockSpec((None, SUBLANE, D_MODEL), lambda *a: (layer, self.n_sample_seq // SUBLANE, kind)),
                pl.BlockSpec((None, groups, D_MODEL), lambda *a: (layer, jnp.maximum(a[0] - prompt_tiles, 0), kind))]


def _gain_spec(layer, which):
    return pl.BlockSpec((None, 1, D_MODEL), lambda *a: (2 * layer + which, 0, 0))


def _norm_proj(x, mod, gains, w, *, tokens, layer, tt, tn):
    n_tok = x.shape[0]
    n_out = w.shape[-1]
    groups = tt // TOKENS_PER_GROUP
    return pl.pallas_call(
        functools.partial(_norm_proj_kernel, groups=groups, tiles=tokens.tiles(tt)),
        grid=(n_tok // tt, n_out // tn),
        in_specs=tokens.mod_specs(tt, layer, MOD_SC1) + tokens.mod_specs(tt, layer, MOD_SH1) + [
            pl.BlockSpec((tt, D_MODEL), lambda i, j: (i, 0)),
            _gain_spec(layer, 0),
            pl.BlockSpec((None, D_MODEL, tn), lambda i, j: (layer, 0, j)),
        ],
        out_specs=pl.BlockSpec((tt, tn), lambda i, j: (i, j)),
        out_shape=jax.ShapeDtypeStruct((n_tok, n_out), F32),
        scratch_shapes=[pltpu.VMEM((tt, D_MODEL), BF16)],
        compiler_params=_cparams(("parallel", "arbitrary")),
        name="norm_proj",
    )(mod, mod, mod, mod, x, gains, w)


def _chunk_masks(rows, chunk):
    shift = int(math.log2(chunk))
    ri = lax.broadcasted_iota(jnp.int32, (rows, rows), 0)
    ci = lax.broadcasted_iota(jnp.int32, (rows, rows), 1)
    same = (ri >> shift) == (ci >> shift)
    tri = jnp.logical_and(same, ci <= ri)
    return same, tri


def _gla_kernel(q_ref, k_ref, v_ref, g_ref, r_ref, wg_ref, bg_ref, gain_ref, s0_ref,
                o_ref, sfin_ref, st_scr, **kw):
    carry = kw["carry"]
    heads = range(HEADS_PER_STEP)
    if carry:
        @pl.when(pl.program_id(2) == 0)
        def _():
            for hh in heads:
                st_scr[hh] = _gla_load_state(s0_ref.at[:, hh:hh + 1], 0)
    for hh in heads:
        ln = slice(hh * LANE, (hh + 1) * LANE)
        one = slice(hh, hh + 1)
        _gla_head(q_ref.at[:, ln], k_ref.at[:, ln], v_ref.at[:, ln], g_ref.at[:, ln], r_ref,
                  wg_ref.at[one], bg_ref.at[one], gain_ref.at[one], s0_ref.at[:, one],
                  o_ref.at[:, ln], sfin_ref.at[:, one], st_scr.at[hh], **kw)
    if carry:
        @pl.when(pl.program_id(2) == pl.num_programs(2) - 1)
        def _():
            for hh in heads:
                sfin_ref[0, hh] = st_scr[hh].T[0:GLA_DK, :]


def _gla_load_state(s0_ref, c):
    zero_pad = jnp.zeros((LANE - GLA_DK, GLA_DV), F32)
    return jnp.concatenate([s0_ref[c, 0], zero_pad], axis=0).T


def _gla_head(q_ref, k_ref, v_ref, g_ref, r_ref, wg_ref, bg_ref, gain_ref, s0_ref,
              o_ref, sfin_ref, st_scr, *, rows, chunk, carry):
    n_chunks = rows // chunk
    shift = int(math.log2(chunk))
    same, tri = _chunk_masks(rows, chunk)

    q = q_ref[...] * (GLA_DK ** -0.5)
    k = k_ref[...]
    v = v_ref[...]
    la = jax.nn.log_sigmoid(_dot(r_ref[...].astype(BF16), wg_ref[0]) + bg_ref[0]) / GLA_TAU
    bl, bt = _chunk_sums(same, tri, la)
    qd = (q * jnp.exp(bl)).astype(BF16)
    ki = (k * jnp.exp(-bl)).astype(BF16)
    kd = (k * jnp.exp(bt - bl)).astype(BF16)
    att = jnp.where(tri, _dot_nt(qd, ki), 0.0)
    o_intra = _dot(att.astype(BF16), v.astype(BF16))

    v_t = v.T
    lane_chunk = lax.broadcasted_iota(jnp.int32, (GLA_DV, rows), 1) >> shift
    if carry:
        s_t = st_scr[...]

    sliced = chunk % (2 * SUBLANE) == 0
    row_chunk = lax.broadcasted_iota(jnp.int32, (rows, GLA_DV), 0) >> shift
    o_inter = [] if sliced else jnp.zeros((rows, GLA_DV), F32)
    for c in range(n_chunks):
        lo = c * chunk
        if not carry:
            s_t = _gla_load_state(s0_ref, c)
        if sliced:
            o_inter.append(_dot_nt(qd[lo:lo + chunk], s_t.astype(BF16)))
        else:
            o_inter = jnp.where(row_chunk == c, _dot_nt(qd, s_t.astype(BF16)), o_inter)
        decay = jnp.exp(bt[lo:lo + 1, :])
        v_c = jnp.where(lane_chunk == c, v_t, 0.0).astype(BF16)
        s_t = s_t * decay + _dot(v_c, kd)
        if not carry:
            sfin_ref[c, 0] = s_t.T[0:GLA_DK, :]

    if carry:
        st_scr[...] = s_t

    o = o_intra + (jnp.concatenate(o_inter, axis=0) if sliced else o_inter)
    g = g_ref[...]
    o_ref[...] = _rms(o) * gain_ref[0] * (g * jax.nn.sigmoid(g))


def _without_refs(kernel_fn, first, count):
    def wrapped(*refs):
        return kernel_fn(*refs[:first], *refs[first + count:])
    return wrapped


def _mixer_grid(n_seq, n_heads, seq_blocks, row_block0, carry):
    if carry:
        return ((n_seq, n_heads, seq_blocks), lambda b, h, t: row_block0 + b * seq_blocks + t,
                ("parallel", "parallel", "arbitrary"))
    return (n_seq, n_heads), lambda b, h: row_block0 + b, ("parallel", "parallel")


def _state_spec(per_block, d2, d3, layer):
    if layer is None:
        return pl.BlockSpec((per_block, HEADS_PER_STEP, d2, d3), lambda *a: (a[0], a[1], 0, 0))
    return pl.BlockSpec((None, per_block, HEADS_PER_STEP, d2, d3), lambda *a: (layer, a[0], a[1], 0, 0))


def _head_cols_spec(rows, rb, block0):
    assert block0 % HEADS_PER_STEP == 0
    return pl.BlockSpec((rows, HEADS_PER_STEP * LANE), lambda *a: (rb(*a), block0 // HEADS_PER_STEP + a[1]))


def _head_param_spec(shape):
    return pl.BlockSpec((HEADS_PER_STEP,) + shape, lambda *a: (a[1],) + (0,) * len(shape))


def _gla(z, wg, bg, gain, s0, *, n_seq, seq_blocks, rows, chunk, row_block0, carry,
         layer=None, fill=()):
    per_block = 1 if carry else rows // chunk
    grid, rb, sem = _mixer_grid(n_seq, GLA_HEADS // HEADS_PER_STEP, seq_blocks, row_block0, carry)
    zspec = functools.partial(_head_cols_spec, rows, rb)
    state_spec = _state_spec(per_block, GLA_DK, GLA_DV, layer)
    n_in = 9
    return pl.pallas_call(
        _without_refs(functools.partial(_gla_kernel, rows=rows, chunk=chunk, carry=carry), n_in, len(fill)),
        grid=grid,
        in_specs=[zspec(ZB_GQ), zspec(ZB_GK), zspec(ZB_GV), zspec(ZB_GG),
                  pl.BlockSpec((rows, LANE), lambda *a: (rb(*a), ZB_GR)),
                  _head_param_spec((LANE, LANE)), _head_param_spec((1, LANE)), _head_param_spec((1, LANE)),
                  state_spec]
                 + [pl.BlockSpec(memory_space=pl.ANY)] * len(fill),
        out_specs=[_head_cols_spec(rows, rb, 0), state_spec],
        out_shape=[jax.ShapeDtypeStruct((z.shape[0], GLA_WIDTH), F32),
                   jax.ShapeDtypeStruct(s0.shape, F32)],
        input_output_aliases={n_in + i: i for i in range(len(fill))},
        scratch_shapes=[pltpu.VMEM((HEADS_PER_STEP, GLA_DV, LANE), F32)],
        compiler_params=_cparams(sem),
        name="gla_seq" if carry else "gla_step",
    )(z, z, z, z, z, wg, bg, gain, s0, *fill)


def _mlstm_kernel(q_ref, k_ref, v_ref, og_ref, gate_ref, bif_ref, gain_ref, c0_ref, n0_ref, m0_ref,
                  h_ref, cfin_ref, nfin_ref, mfin_ref, c_scr, n_scr, m_scr, **kw):
    carry = kw["carry"]
    heads = range(HEADS_PER_STEP)
    if carry:
        @pl.when(pl.program_id(2) == 0)
        def _():
            for hh in heads:
                c_scr[hh] = c0_ref[0, hh]
                n_scr[hh] = n0_ref[0, hh]
                m_scr[hh] = m0_ref[0, hh]
    _mlstm_heads(q_ref, k_ref, v_ref, og_ref, gate_ref, bif_ref, gain_ref, c0_ref, n0_ref, m0_ref,
                 h_ref, cfin_ref, nfin_ref, mfin_ref, c_scr, n_scr, m_scr, **kw)
    if carry:
        @pl.when(pl.program_id(2) == pl.num_programs(2) - 1)
        def _():
            for hh in heads:
                cfin_ref[0, hh] = c_scr[hh]
                nfin_ref[0, hh] = n_scr[hh]
                mfin_ref[0, hh] = m_scr[hh]


def _mlstm_heads(q_ref, k_ref, v_ref, og_ref, gate_ref, bif_ref, gain_ref, c0_ref, n0_ref, m0_ref,
                 h_ref, cfin_ref, nfin_ref, mfin_ref, c_scr, n_scr, m_scr, **kw):
    for hh in range(HEADS_PER_STEP):
        ln = slice(hh * LANE, (hh + 1) * LANE)
        one = slice(hh, hh + 1)
        _mlstm_head(pl.program_id(1) * HEADS_PER_STEP + hh,
                    q_ref.at[:, ln], k_ref.at[:, ln], v_ref.at[:, ln], og_ref.at[:, ln], gate_ref, bif_ref,
                    gain_ref.at[one], c0_ref.at[:, one], n0_ref.at[:, one], m0_ref.at[:, one],
                    h_ref.at[:, ln], cfin_ref.at[:, one], nfin_ref.at[:, one], mfin_ref.at[:, one],
                    c_scr.at[hh], n_scr.at[hh], m_scr.at[hh], **kw)


def _mlstm_head(head, q_ref, k_ref, v_ref, og_ref, gate_ref, bif_ref, gain_ref, c0_ref, n0_ref, m0_ref,
                h_ref, cfin_ref, nfin_ref, mfin_ref, c_scr, n_scr, m_scr, *, rows, chunk, carry):
    n_chunks = rows // chunk
    shift = int(math.log2(chunk))
    same, tri = _chunk_masks(rows, chunk)
    neg_inf = jnp.float32(-jnp.inf)

    x = gate_ref[...] + bif_ref[...]
    log_f = jax.nn.log_sigmoid(x)
    b_cum, b_tot = _chunk_sums(same, tri, log_f)
    lane = lax.broadcasted_iota(jnp.int32, (rows, LANE), 1)
    y = jnp.where(lane < F_GATE_LANE, x, b_cum)
    pick_i = lane == head
    pick_b = lane == head + F_GATE_LANE
    i_col = jnp.sum(jnp.where(pick_i, y, 0.0), axis=-1, keepdims=True)
    b_col = jnp.sum(jnp.where(pick_b, y, 0.0), axis=-1, keepdims=True)
    bt_col = jnp.sum(jnp.where(pick_b, b_tot, 0.0), axis=-1, keepdims=True)
    sub = lax.broadcasted_iota(jnp.int32, (LANE, rows), 0)
    y_t = y.T
    i_row = jnp.sum(jnp.where(sub == head, y_t, 0.0), axis=0, keepdims=True)
    b_row = jnp.sum(jnp.where(sub == head + F_GATE_LANE, y_t, 0.0), axis=0, keepdims=True)

    a_col = bt_col - b_col + i_col
    mloc_col = bt_col + jnp.max(jnp.where(same, i_row - b_row, neg_inf), axis=-1, keepdims=True)

    q = q_ref[...] * (ML_DH ** -0.5)
    k = k_ref[...]
    qb = q.astype(BF16)
    kb = k.astype(BF16)
    vb = v_ref[...].astype(BF16)
    kw = k * jnp.exp(a_col - mloc_col)
    kw_t = kw.T
    lane_chunk = lax.broadcasted_iota(jnp.int32, (ML_DH, rows), 1) >> shift

    def advance(c, c_st, n_st, m_st, m_last):
        hi = (c + 1) * chunk
        decay = jnp.exp(bt_col[hi - 1:hi] + m_st - m_last)
        scale = jnp.exp(mloc_col[hi - 1:hi] - m_last)
        kw_c = jnp.where(lane_chunk == c, kw_t, 0.0).astype(BF16)
        c_new = decay * c_st + scale * _dot(kw_c, vb)
        n_new = decay * n_st + scale * jnp.sum(kw[hi - chunk:hi], axis=0, keepdims=True)
        return c_new, n_new

    if carry:
        c_st = c_scr[...]
        n_st = n_scr[...]
        m_st = m_scr[:, 0:1]
        ri = lax.broadcasted_iota(jnp.int32, (chunk, chunk), 0)
        ci = lax.broadcasted_iota(jnp.int32, (chunk, chunk), 1)
        tri_c = ci <= ri
        h_chunks = []
        for c in range(n_chunks):
            lo = c * chunk
            hi = lo + chunk
            bc = b_col[lo:hi]
            d_log = jnp.where(tri_c, bc - b_row[:, lo:hi] + i_row[:, lo:hi], neg_inf)
            g_inter = bc + m_st
            m_t = jnp.maximum(g_inter, jnp.max(d_log, axis=-1, keepdims=True))
            w_inter = jnp.exp(g_inter - m_t)
            s = _dot_nt(qb[lo:hi], kb[lo:hi]) * jnp.exp(d_log - m_t)
            num = w_inter * _dot(qb[lo:hi], c_st.astype(BF16)) + _dot(s.astype(BF16), vb[lo:hi])
            den = (w_inter * jnp.sum(q[lo:hi] * n_st, axis=-1, keepdims=True)
                   + jnp.sum(s, axis=-1, keepdims=True))
            h_chunks.append(num / jnp.maximum(jnp.abs(den), jnp.exp(-m_t)))
            m_last = m_t[chunk - 1:chunk]
            c_st, n_st = advance(c, c_st, n_st, m_st, m_last)
            m_st = m_last
        h = jnp.concatenate(h_chunks, axis=0)
        c_scr[...] = c_st
        n_scr[...] = n_st
        m_scr[...] = jnp.broadcast_to(m_st, (1, LANE))
    else:
        row_chunk1 = lax.broadcasted_iota(jnp.int32, (rows, 1), 0) >> shift
        row_chunk = lax.broadcasted_iota(jnp.int32, (rows, ML_DH), 0) >> shift
        m_rows = jnp.zeros((rows, 1), F32)
        n_rows = jnp.zeros((rows, ML_DH), F32)
        qc = jnp.zeros((rows, ML_DH), F32)
        for c in range(n_chunks):
            m_rows = jnp.where(row_chunk1 == c, m0_ref[c, 0][:, 0:1], m_rows)
            n_rows = jnp.where(row_chunk == c, n0_ref[c, 0], n_rows)
            qc = jnp.where(row_chunk == c, _dot(qb, c0_ref[c, 0].astype(BF16)), qc)
        d_log = jnp.where(tri, b_col - b_row + i_row, neg_inf)
        g_inter = b_col + m_rows
        m_t = jnp.maximum(g_inter, jnp.max(d_log, axis=-1, keepdims=True))
        w_inter = jnp.exp(g_inter - m_t)
        s = _dot_nt(qb, kb) * jnp.exp(d_log - m_t)
        num = w_inter * qc + _dot(s.astype(BF16), vb)
        den = w_inter * jnp.sum(q * n_rows, axis=-1, keepdims=True) + jnp.sum(s, axis=-1, keepdims=True)
        h = num / jnp.maximum(jnp.abs(den), jnp.exp(-m_t))
        for c in range(n_chunks):
            hi = (c + 1) * chunk
            m_last = m_t[hi - 1:hi]
            c_new, n_new = advance(c, c0_ref[c, 0], n0_ref[c, 0], m0_ref[c, 0][:, 0:1], m_last)
            cfin_ref[c, 0] = c_new
            nfin_ref[c, 0] = n_new
            mfin_ref[c, 0] = jnp.broadcast_to(m_last, (1, LANE))

    h_ref[...] = _rms(h) * gain_ref[0] * jax.nn.sigmoid(og_ref[...])


def _mlstm(z, bif, gain, c0, n0, m0, *, n_seq, seq_blocks, rows, chunk, row_block0, carry,
           layer=None, fill=()):
    per_block = 1 if carry else rows // chunk
    grid, rb, sem = _mixer_grid(n_seq, ML_HEADS // HEADS_PER_STEP, seq_blocks, row_block0, carry)
    zspec = functools.partial(_head_cols_spec, rows, rb)
    state_specs = [_state_spec(per_block, ML_DH, ML_DH, layer), _state_spec(per_block, 1, ML_DH, layer),
                   _state_spec(per_block, 1, LANE, layer)]
    n_in = 10
    return pl.pallas_call(
        _without_refs(functools.partial(_mlstm_kernel, rows=rows, chunk=chunk, carry=carry), n_in, len(fill)),
        grid=grid,
        in_specs=[zspec(ZB_MQ), zspec(ZB_MK), zspec(ZB_MV), zspec(ZB_MO),
                  pl.BlockSpec((rows, LANE), lambda *a: (rb(*a), ZB_MIF)),
                  pl.BlockSpec((1, LANE), lambda *a: (0, 0)),
                  _head_param_spec((1, LANE))]
                 + state_specs + [pl.BlockSpec(memory_space=pl.ANY)] * len(fill),
        out_specs=[_head_cols_spec(rows, rb, 0)] + state_specs,
        out_shape=[jax.ShapeDtypeStruct((z.shape[0], ML_WIDTH), F32),
                   jax.ShapeDtypeStruct(c0.shape, F32), jax.ShapeDtypeStruct(n0.shape, F32),
                   jax.ShapeDtypeStruct(m0.shape, F32)],
        input_output_aliases={n_in + i: i for i in range(len(fill))},
        scratch_shapes=[pltpu.VMEM((HEADS_PER_STEP, ML_DH, ML_DH), F32), pltpu.VMEM((HEADS_PER_STEP, 1, ML_DH), F32),
                        pltpu.VMEM((HEADS_PER_STEP, 1, LANE), F32)],
        compiler_params=_cparams(sem),
        name="mlstm_seq" if carry else "mlstm_step",
    )(z, z, z, z, z, bif, gain, c0, n0, m0, *fill)


def _s5_disc_kernel(lre_ref, lim_ref, ldt_ref, bre_ref, bim_ref, a_ref, bb_ref):
    lam_re = lre_ref[0]
    lam_im = lim_ref[0]
    dt = jnp.exp(ldt_ref[0])
    mag = jnp.exp(lam_re * dt)
    ar = mag * jnp.cos(lam_im * dt)
    ai = mag * jnp.sin(lam_im * dt)
    den = lam_re * lam_re + lam_im * lam_im
    fr = ((ar - 1.0) * lam_re + ai * lam_im) / den
    fi = (ai * lam_re - (ar - 1.0) * lam_im) / den
    b_re = bre_ref[0]
    b_im = bim_ref[0]
    a_ref[0, 0:1, :] = ar
    a_ref[0, 1:2, :] = ai
    bb_ref[0, 0] = fr * b_re - fi * b_im
    bb_ref[0, 1] = fr * b_im + fi * b_re


def _s5_discretise(lam_re, lam_im, log_dt, b_re, b_im):
    flat = lambda t: t.reshape(DEPTH, 1, S5_STATE)
    ldt = jnp.broadcast_to(log_dt[:, :, None], (DEPTH, S5_GROUPS, S5_P))
    chan = lambda t: t.reshape(DEPTH, S5_STATE, S5_CH).transpose(0, 2, 1)
    row = pl.BlockSpec((1, 1, S5_STATE), lambda l: (l, 0, 0))
    mat = pl.BlockSpec((1, S5_CH, S5_STATE), lambda l: (l, 0, 0))
    return pl.pallas_call(
        _s5_disc_kernel,
        grid=(DEPTH,),
        in_specs=[row, row, row, mat, mat],
        out_specs=[pl.BlockSpec((1, 2, S5_STATE), lambda l: (l, 0, 0)),
                   pl.BlockSpec((1, 2, S5_CH, S5_STATE), lambda l: (l, 0, 0, 0))],
        out_shape=[jax.ShapeDtypeStruct((DEPTH, 2, S5_STATE), F32),
                   jax.ShapeDtypeStruct((DEPTH, 2, S5_CH, S5_STATE), F32)],
        compiler_params=_cparams(("parallel",)),
        name="s5_disc",
    )(flat(lam_re), flat(lam_im), flat(ldt), chan(b_re), chan(b_im))


def _s5_kernel(u_ref, h0_ref, a_ref, wb_ref, wc_ref, d_ref, wglu_ref, bglu_ref, gain_ref,
               y_ref, hfin_ref, bu_scr, h_scr, *, steps, batch, lane_width):
    n_rows = steps * batch

    @pl.when(pl.program_id(0) == 0)
    def _():
        h_scr[...] = h0_ref[...]

    u = u_ref[...].reshape(n_rows, S5_WIDTH)
    ub = u.astype(BF16)
    for t in range(S5_TILES):
        bu = _dot(ub[:, t * LANE:(t + 1) * LANE], wb_ref[t])
        re0 = t * S5_TILE_STATE
        bu_scr[:, :, re0:re0 + S5_TILE_STATE] = bu[:, :S5_TILE_STATE].reshape(steps, batch, S5_TILE_STATE)
        bu_scr[:, :, S5_STATE + re0:S5_STATE + re0 + S5_TILE_STATE] = (
            bu[:, S5_TILE_STATE:].reshape(steps, batch, S5_TILE_STATE))

    for j in range(S5_STATE // lane_width):
        re = slice(j * lane_width, (j + 1) * lane_width)
        im = slice(S5_STATE + j * lane_width, S5_STATE + (j + 1) * lane_width)
        ar = a_ref[0:1, re]
        ai = a_ref[1:2, re]

        def step(t, carry):
            hr, hi = carry
            nr = ar * hr - ai * hi + bu_scr[t, :, re]
            ni = ar * hi + ai * hr + bu_scr[t, :, im]
            bu_scr[t, :, re] = nr
            bu_scr[t, :, im] = ni
            return nr, ni

        hr, hi = lax.fori_loop(0, steps, step, (h_scr[:, re], h_scr[:, im]), unroll=min(steps, 8))
        h_scr[:, re] = hr
        h_scr[:, im] = hi

    ys = []
    for t in range(S5_TILES):
        re0 = t * S5_TILE_STATE
        h_re = bu_scr[:, :, re0:re0 + S5_TILE_STATE].reshape(n_rows, S5_TILE_STATE).astype(BF16)
        h_im = bu_scr[:, :, S5_STATE + re0:S5_STATE + re0 + S5_TILE_STATE].reshape(n_rows, S5_TILE_STATE).astype(BF16)
        ys.append(_dot(h_re, wc_ref[t, :S5_TILE_STATE]) + _dot(h_im, wc_ref[t, S5_TILE_STATE:]))
    y = jnp.concatenate(ys, axis=1) + d_ref[...] * u
    y = jax.nn.gelu(y)
    out = y * jax.nn.sigmoid(_dot(y.astype(BF16), wglu_ref[...]) + bglu_ref[...])
    y_ref[...] = (_rms(out) * gain_ref[...]).reshape(steps, batch, S5_WIDTH)

    @pl.when(pl.program_id(0) == pl.num_programs(0) - 1)
    def _():
        hfin_ref[...] = h_scr[...]


def _s5(u_t, h0, a, wb, wc, d, wglu, bglu, gain, *, steps, lane_width):
    seq, batch, _ = u_t.shape
    const = lambda shape: _const_spec(shape, lambda t: (0,) * len(shape))
    return pl.pallas_call(
        functools.partial(_s5_kernel, steps=steps, batch=batch, lane_width=lane_width),
        grid=(seq // steps,),
        in_specs=[pl.BlockSpec((steps, batch, S5_WIDTH), lambda t: (t, 0, 0)),
                  const((batch, 2 * S5_STATE)), const((2, S5_STATE)),
                  const((S5_TILES, LANE, 2 * S5_TILE_STATE)), const((S5_TILES, 2 * S5_TILE_STATE, LANE)),
                  const((1, S5_WIDTH)), const((S5_WIDTH, S5_WIDTH)), const((1, S5_WIDTH)),
                  const((1, S5_WIDTH))],
        out_specs=[pl.BlockSpec((steps, batch, S5_WIDTH), lambda t: (t, 0, 0)),
                   pl.BlockSpec((batch, 2 * S5_STATE), lambda t: (0, 0))],
        out_shape=[jax.ShapeDtypeStruct((seq, batch, S5_WIDTH), F32),
                   jax.ShapeDtypeStruct((batch, 2 * S5_STATE), F32)],
        scratch_shapes=[pltpu.VMEM((steps, batch, 2 * S5_STATE), F32),
                        pltpu.VMEM((batch, 2 * S5_STATE), F32)],
        compiler_params=_cparams(("arbitrary",)),
        name="s5",
    )(u_t, h0, a, wb, wc, d, wglu, bglu, gain)


def _out_proj_kernel(g1p_ref, g1s_ref, scp_ref, scs_ref, shp_ref, shs_ref, og_ref, om_ref, os_ref, x_ref, gain_ref,
                     wg_ref, wm_ref, ws_ref, xo_ref, ht_ref, mix_scr, h_scr, *, groups, tiles):
    g1 = _ModRows(g1p_ref, g1s_ref, *tiles)
    sc = _ModRows(scp_ref, scs_ref, *tiles)
    sh = _ModRows(shp_ref, shs_ref, *tiles)
    mix_scr[...] = (_dot(og_ref[...].astype(BF16), wg_ref[...])
                    + _dot(om_ref[...].astype(BF16), wm_ref[...])
                    + _dot(os_ref[...].astype(BF16), ws_ref[...]))
    gain = gain_ref[...]

    def body(g, carry):
        r0 = pl.multiple_of(g * TOKENS_PER_GROUP, TOKENS_PER_GROUP)
        rows = pl.ds(r0, TOKENS_PER_GROUP)
        xn = x_ref[rows, :] + g1.row(g) * mix_scr[rows, :]
        xo_ref[rows, :] = xn
        h_scr[rows, :] = _modulated_group(xn, gain, sc.row(g), sh.row(g))
        return carry

    lax.fori_loop(0, groups, body, 0, unroll=GROUP_UNROLL)
    ht_ref[...] = pltpu.bitcast(h_scr[...].T.astype(BF16), jnp.uint32)


def _out_proj(og, om, os_, x, mod, gains, w_out, *, tokens, layer, tt):
    n_tok = x.shape[0]
    groups = tt // TOKENS_PER_GROUP
    tok = lambda width: pl.BlockSpec((tt, width), lambda i: (i, 0))
    w_rows = lambda height, block: _const_spec((None, height, D_MODEL), lambda i: (layer, block, 0))
    assert GLA_WIDTH == ML_WIDTH and (GLA_WIDTH + ML_WIDTH) % S5_WIDTH == 0
    return pl.pallas_call(
        functools.partial(_out_proj_kernel, groups=groups, tiles=tokens.tiles(tt)),
        grid=(n_tok // tt,),
        in_specs=(tokens.mod_specs(tt, layer, MOD_G1) + tokens.mod_specs(tt, layer, MOD_SC2)
                  + tokens.mod_specs(tt, layer, MOD_SH2)) + [
                  tok(GLA_WIDTH), tok(ML_WIDTH), tok(S5_WIDTH), tok(D_MODEL), _gain_spec(layer, 1),
                  w_rows(GLA_WIDTH, 0), w_rows(ML_WIDTH, 1),
                  w_rows(S5_WIDTH, (GLA_WIDTH + ML_WIDTH) // S5_WIDTH)],
        out_specs=[tok(D_MODEL), pl.BlockSpec((D_MODEL // 2, tt), lambda i: (0, i))],
        out_shape=[jax.ShapeDtypeStruct((n_tok, D_MODEL), F32),
                   jax.ShapeDtypeStruct((D_MODEL // 2, n_tok), jnp.uint32)],
        scratch_shapes=[pltpu.VMEM((tt, D_MODEL), F32), pltpu.VMEM((tt, D_MODEL), F32)],
        compiler_params=_cparams(("parallel",)),
        name="out_proj",
    )(mod, mod, mod, mod, mod, mod, og, om, os_, x, gains, w_out, w_out, w_out)


def _top_values(cur, count, out_scr=None, with_rank=False):
    neg_inf = jnp.float32(-jnp.inf)
    vals = []
    rank = jnp.full(cur.shape, float(count), F32) if with_rank else None
    for r in range(count):
        m = jnp.max(cur, axis=0, keepdims=True)
        vals.append(m)
        if out_scr is not None:
            out_scr[r:r + 1, :] = m
        hit = cur == m
        if with_rank:
            rank = jnp.where(hit, float(r), rank)
        if r + 1 < count:
            cur = jnp.where(hit, neg_inf, cur)
    return (vals, rank) if with_rank else vals


def _peer_route_kernel(ht_ref, wqt_ref, keys_ref, first_ref, second_ref, qt_scr, v1_scr, v2_scr, cand_scr,
                       *, tt):
    qt_scr[...] = _dot(wqt_ref[...], _unpack_rows(ht_ref[...]))
    for h in range(PEER_HEADS):
        sc = []
        for side in range(2):
            hs = 2 * h + side
            qb = qt_scr[hs * LANE:(hs + 1) * LANE, :].astype(BF16)
            sc.append(_dot(keys_ref[hs], qb))
        for lt in range(tt // LANE):
            ls = slice(lt * LANE, (lt + 1) * LANE)
            s1 = sc[0][:, ls]
            s2 = sc[1][:, ls]
            _top_values(s1, PEER_TOPK, v1_scr)
            _, rank2 = _top_values(s2, PEER_TOPK, v2_scr, with_rank=True)
            v1_head = v1_scr[0:SUBLANE, :]
            cand_scr[0:PEER_TOPK, :] = v1_scr[0:1, :] + v2_scr[...]
            cand_scr[PEER_TOPK:2 * PEER_TOPK, :] = v1_scr[...] + v2_scr[0:1, :]
            cand_scr[2 * PEER_TOPK:2 * PEER_TOPK + SUBLANE, :] = v1_scr[1:2, :] + v2_scr[0:SUBLANE, :]
            for b in range(1, 5):
                lo = 2 * PEER_TOPK + b * SUBLANE
                cand_scr[lo:lo + SUBLANE, :] = v1_head + v2_scr[b:b + 1, :]
            top = _top_values(cand_scr[...], PEER_TOPK)
            z = jnp.zeros_like(top[0])
            for tv in top:
                z = z + jnp.exp(tv - top[0])
            tau = top[PEER_TOPK - 1]
            n1 = jnp.zeros_like(s1)
            for b in range(SUBLANE):
                n1 = n1 + jnp.where(s1 + v2_scr[b:b + 1, :] >= tau, 1.0, 0.0)
            best = v1_scr[0:1, :]
            tail = jnp.zeros_like(best)
            for b in range(SUBLANE, PEER_TOPK):
                tail = tail + jnp.where(best + v2_scr[b:b + 1, :] >= tau, 1.0, 0.0)
            n1 = n1 + jnp.where(s1 == best, tail, 0.0)
            first_ref[2 * h, :, ls] = n1
            first_ref[2 * h + 1, :, ls] = jnp.exp(s1 - v1_scr[0:1, :]) / z
            second_ref[2 * h, :, ls] = rank2.astype(BF16)
            second_ref[2 * h + 1, :, ls] = jnp.exp(s2 - v2_scr[0:1, :]).astype(BF16)


def _peer_route(ht, wqt, keys, *, tt):
    n_tok = ht.shape[1]
    n_hs = 2 * PEER_HEADS
    assert PEER_TOPK == 2 * SUBLANE
    n_cand = 2 * PEER_TOPK + 5 * SUBLANE
    return pl.pallas_call(
        functools.partial(_peer_route_kernel, tt=tt),
        grid=(n_tok // tt,),
        in_specs=[pl.BlockSpec((D_MODEL // 2, tt), lambda i: (0, i)),
                  _const_spec((n_hs * LANE, D_MODEL), lambda i: (0, 0)),
                  _const_spec((n_hs, N_KEYS, LANE), lambda i: (0, 0, 0))],
        out_specs=[pl.BlockSpec((n_hs, N_KEYS, tt), lambda i: (0, 0, i)),
                   pl.BlockSpec((n_hs, N_KEYS, tt), lambda i: (0, 0, i))],
        out_shape=[jax.ShapeDtypeStruct((n_hs, N_KEYS, n_tok), F32),
                   jax.ShapeDtypeStruct((n_hs, N_KEYS, n_tok), BF16)],
        scratch_shapes=[pltpu.VMEM((n_hs * LANE, tt), F32), pltpu.VMEM((PEER_TOPK, LANE), F32),
                        pltpu.VMEM((PEER_TOPK, LANE), F32), pltpu.VMEM((n_cand, LANE), F32)],
        compiler_params=_cparams(("parallel",)),
        name="peer_route",
    )(ht, wqt, keys)


DENSE_FIRST_KEYS = 4
DENSE_SUB = DENSE_FIRST_KEYS * N_KEYS
GATE_ROWS = 4 * SUBLANE
GATE_PIECES = 4


def _gate_times_act(first_ref, row0, second_ref, act_ref, p_ref, lane_tiles):
    n_pieces = N_KEYS // GATE_ROWS
    zero = jnp.zeros((GATE_ROWS, LANE), BF16)
    for lt in lane_tiles:
        ls = slice(lt * LANE, (lt + 1) * LANE)
        for jp0 in range(0, n_pieces, GATE_PIECES):
            pieces = range(jp0, jp0 + GATE_PIECES)
            gates = [[zero for _ in pieces] for _ in range(DENSE_FIRST_KEYS)]
            for h in range(PEER_HEADS):
                counts, weights = [], []
                for ii in range(DENSE_FIRST_KEYS):
                    r = row0 + ii
                    counts.append(jnp.broadcast_to(first_ref[2 * h, 0, r:r + 1, ls], (GATE_ROWS, LANE)).astype(BF16))
                    weights.append(jnp.broadcast_to(first_ref[2 * h + 1, 0, r:r + 1, ls], (GATE_ROWS, LANE)).astype(BF16))
                for q, jp in enumerate(pieces):
                    js = slice(jp * GATE_ROWS, (jp + 1) * GATE_ROWS)
                    rank2 = second_ref[2 * h, js, ls]
                    e2 = second_ref[2 * h + 1, js, ls]
                    for ii in range(DENSE_FIRST_KEYS):
                        gates[ii][q] = gates[ii][q] + jnp.where(rank2 < counts[ii], e2, zero) * weights[ii]
            for ii in range(DENSE_FIRST_KEYS):
                for q, jp in enumerate(pieces):
                    rs = slice(ii * N_KEYS + jp * GATE_ROWS, ii * N_KEYS + (jp + 1) * GATE_ROWS)
                    p_ref[rs, ls] = gates[ii][q] * act_ref[rs, ls].astype(BF16)


def _peer_dense_kernel(ht_ref, u_ref, vt_ref, first_a_ref, first_b_ref, second_in_ref,
                       x_ref, g2p_ref, g2s_ref, o_ref, acc_scr, act0, act1, p0, p1, second_ref,
                       *, tt, groups, tiles):
    k = pl.program_id(1)

    @pl.when(k == 0)
    def _():
        acc_scr[...] = jnp.zeros_like(acc_scr)
        act1[...] = jnp.zeros_like(act1)
        p0[...] = jnp.zeros_like(p0)
        second_ref[...] = second_in_ref[...]

    tok_half = tt // 2
    tiles_half = tok_half // LANE
    for half in range(2):
        ts = slice(half * tok_half, (half + 1) * tok_half)
        lane_tiles = range(half * tiles_half, (half + 1) * tiles_half)
        act0[:, ts] = jax.nn.gelu(_dot(_unpack_rows(u_ref[0:DENSE_SUB // 2, :]), _unpack_rows(ht_ref[:, ts])))
        _gate_times_act(first_a_ref, DENSE_FIRST_KEYS, second_ref, act1, p1, lane_tiles)
        acc_scr[:, ts] += _dot(_unpack_rows(vt_ref[:, 0:DENSE_SUB]), p0[:, ts])
    for half in range(2):
        ts = slice(half * tok_half, (half + 1) * tok_half)
        lane_tiles = range(half * tiles_half, (half + 1) * tiles_half)
        act1[:, ts] = jax.nn.gelu(_dot(_unpack_rows(u_ref[DENSE_SUB // 2:, :]), _unpack_rows(ht_ref[:, ts])))
        _gate_times_act(first_b_ref, 0, second_ref, act0, p0, lane_tiles)
        acc_scr[:, ts] += _dot(_unpack_rows(vt_ref[:, DENSE_SUB:]), p1[:, ts])

    @pl.when(k == pl.num_programs(1) - 1)
    def _():
        o_ref[...] = acc_scr[...].T
        g2 = _ModRows(g2p_ref, g2s_ref, *tiles)

        def body(g, carry):
            rows = pl.ds(pl.multiple_of(g * TOKENS_PER_GROUP, TOKENS_PER_GROUP), TOKENS_PER_GROUP)
            o_ref[rows, :] = x_ref[rows, :] + g2.row(g) * o_ref[rows, :]
            return carry

        lax.fori_loop(0, groups, body, 0, unroll=GROUP_UNROLL)


def _peer_dense(ht, u_bf, vt_bf, first, second, x, mod, *, tokens, layer, tt):
    n_tok = x.shape[0]
    groups = tt // TOKENS_PER_GROUP
    n_hs = 2 * PEER_HEADS
    ne = 2 * DENSE_SUB
    n_i = ne // N_KEYS
    n_blocks = N_EXPERTS // ne
    tok_const = lambda shape, imap: pl.BlockSpec(shape, imap, pipeline_mode=pl.Buffered(1))
    by_first_key = lambda t: t.reshape(n_hs, n_blocks, n_i, n_tok)
    cur = lambda k: jnp.minimum(k, n_blocks - 1)
    prev = lambda k: jnp.maximum(k - 1, 0)
    key_spec = lambda blk: pl.BlockSpec((n_hs, 1, n_i, tt), lambda i, k: (0, blk(k), 0, i))
    return pl.pallas_call(
        functools.partial(_peer_dense_kernel, tt=tt, groups=groups, tiles=tokens.tiles(tt)),
        grid=(n_tok // tt, n_blocks + 1),
        in_specs=[tok_const((D_MODEL // 2, tt), lambda i, k: (0, i)),
                  pl.BlockSpec((None, ne // 2, D_MODEL), lambda i, k: (layer, cur(k), 0)),
                  pl.BlockSpec((None, None, D_MODEL // 2, ne), lambda i, k: (layer, prev(k), 0, 0)),
                  key_spec(prev), key_spec(cur),
                  tok_const((n_hs, N_KEYS, tt), lambda i, k: (0, 0, i)),
                  tok_const((tt, D_MODEL), lambda i, k: (i, 0))] + tokens.mod_specs(tt, layer, MOD_G2),
        out_specs=pl.BlockSpec((tt, D_MODEL), lambda i, k: (i, 0)),
        out_shape=jax.ShapeDtypeStruct((n_tok, D_MODEL), F32),
        scratch_shapes=[pltpu.VMEM((D_MODEL, tt), F32),
                        pltpu.VMEM((DENSE_SUB, tt), F32), pltpu.VMEM((DENSE_SUB, tt), F32),
                        pltpu.VMEM((DENSE_SUB, tt), BF16), pltpu.VMEM((DENSE_SUB, tt), BF16),
                        pltpu.VMEM((n_hs, N_KEYS, tt), BF16)],
        compiler_params=_cparams(("parallel", "arbitrary")),
        name="peer_dense",
    )(ht, u_bf, vt_bf, by_first_key(first), by_first_key(first), second, x, mod, mod)


def _pack_experts_kernel(w_ref, o_ref, *, transpose):
    w = w_ref[...]
    if transpose:
        w = w.T
    o_ref[...] = pltpu.bitcast(w.astype(BF16), jnp.uint32)


def _pack_experts(w, *, transpose):
    blk = 2 * DENSE_SUB
    n_blocks = N_EXPERTS // blk
    if transpose:
        out_spec = pl.BlockSpec((None, None, D_MODEL // 2, blk), lambda l, j: (l, j, 0, 0))
        out_shape = (DEPTH, n_blocks, D_MODEL // 2, blk)
    else:
        out_spec = pl.BlockSpec((None, blk // 2, D_MODEL), lambda l, j: (l, j, 0))
        out_shape = (DEPTH, N_EXPERTS // 2, D_MODEL)
    return pl.pallas_call(
        functools.partial(_pack_experts_kernel, transpose=transpose),
        grid=(DEPTH, n_blocks),
        in_specs=[pl.BlockSpec((None, blk, D_MODEL), lambda l, j: (l, j, 0))],
        out_specs=out_spec,
        out_shape=jax.ShapeDtypeStruct(out_shape, jnp.uint32),
        compiler_params=_cparams(("parallel", "parallel")),
        name="pack_experts",
    )(w)


def _final_norm_kernel(x_ref, gain_ref, o_ref):
    o_ref[...] = _rms(x_ref[...]) * gain_ref[...]


def _final_norm(x, gain, *, tt, row0, n_rows):
    return pl.pallas_call(
        _final_norm_kernel,
        grid=(n_rows // tt,),
        in_specs=[pl.BlockSpec((tt, D_MODEL), lambda i: (row0 // tt + i, 0)),
                  pl.BlockSpec((1, D_MODEL), lambda i: (0, 0))],
        out_specs=pl.BlockSpec((tt, D_MODEL), lambda i: (i, 0)),
        out_shape=jax.ShapeDtypeStruct((n_rows, D_MODEL), F32),
        compiler_params=_cparams(("parallel",)),
        name="final_norm",
    )(x, gain)


PROMPT_ROWS = 256
STEP_ROWS = 128
S5_PROMPT_STEPS = 128


def _block_diag_in(bbar):
    rows_group = np.arange(S5_WIDTH) // S5_CH
    cols_group = np.arange(S5_STATE) // S5_P
    mask = jnp.asarray(rows_group[:, None] == cols_group[None, :])
    reps = (1,) * (bbar.ndim - 2) + (S5_GROUPS, 1)
    return jnp.where(mask, jnp.tile(bbar, reps), 0.0)


def _diag_tiles(w, rows, cols):
    return jnp.stack([w[:, t * rows:(t + 1) * rows, t * cols:(t + 1) * cols]
                      for t in range(w.shape[1] // rows)], axis=1)


def _block_diag_out(c):
    rows_group = np.arange(S5_STATE) // S5_P
    cols_group = np.arange(S5_WIDTH) // S5_CH
    mask = jnp.asarray(rows_group[:, None] == cols_group[None, :])
    per_state = c.transpose(0, 1, 3, 2).reshape(c.shape[0], S5_STATE, S5_CH)
    return jnp.where(mask, jnp.tile(per_state, (1, 1, S5_GROUPS)), 0.0)


def kernel(x_prompt, x_sample, state_gla, state_mlstm_c, state_mlstm_n, state_mlstm_m, state_s5_re, state_s5_im, c_prompt, c_sample, w_ada, b_ada, norm_gain, w_in, gla_w_gate_up, gla_b_gate, gla_norm_gain, ml_b_igate, ml_b_fgate, ml_norm_gain, s5_lambda_re, s5_lambda_im, s5_log_dt, s5_b_re, s5_b_im, s5_c_re, s5_c_im, s5_d, s5_w_glu, s5_b_glu, s5_norm_gain, w_out, peer_w_q, peer_sub_keys, peer_u, peer_v, final_gain):
    bp, lp, _ = x_prompt.shape
    bs, ls, _ = x_sample.shape
    assert ls == TOKENS_PER_GROUP and lp % PROMPT_ROWS == 0 and (bs * ls) % STEP_ROWS == 0
    n_prompt = bp * lp
    n_sample = bs * ls
    assert bp <= SUBLANE
    tokens = _TokenLayout(n_prompt, lp, bs)

    x = jnp.concatenate([x_prompt.reshape(n_prompt, D_MODEL), x_sample.reshape(n_sample, D_MODEL)], axis=0)

    c_all = jnp.concatenate([c_sample, c_prompt], axis=0)
    mod = _adaln(jnp.pad(c_all, ((0, SUBLANE - bp), (0, 0))), w_ada, b_ada)
    gains = norm_gain.reshape(DEPTH * 2, 1, D_MODEL)

    zeros = lambda *shape: jnp.zeros(shape, F32)
    seq_blocks = lp // PROMPT_ROWS
    step_blocks = n_sample // STEP_ROWS
    step_block0 = n_prompt // STEP_ROWS
    prompt_pad = (-bp) % SUBLANE

    w_in_p = _pad_z_columns(w_in.astype(BF16))
    wg_all = jnp.zeros((DEPTH, GLA_HEADS, LANE, LANE), F32).at[:, :, :GLA_RANK, :GLA_DK].set(
        gla_w_gate_up.reshape(DEPTH, GLA_RANK, GLA_HEADS, GLA_DK).transpose(0, 2, 1, 3)).astype(BF16)
    bg_all = jnp.zeros((DEPTH, GLA_HEADS, 1, LANE), F32).at[:, :, 0, :GLA_DK].set(
        gla_b_gate.reshape(DEPTH, GLA_HEADS, GLA_DK))
    gla_gain_all = gla_norm_gain.reshape(DEPTH, GLA_HEADS, 1, GLA_DV)
    bif_all = jnp.zeros((DEPTH, 1, LANE), F32).at[:, 0, :ML_HEADS].set(ml_b_igate).at[
        :, 0, F_GATE_LANE:F_GATE_LANE + ML_HEADS].set(ml_b_fgate)
    ml_gain_all = ml_norm_gain.reshape(DEPTH, ML_HEADS, 1, ML_DH)
    a_disc, bbar = _s5_discretise(s5_lambda_re, s5_lambda_im, s5_log_dt, s5_b_re, s5_b_im)
    wb_all = jnp.concatenate([_diag_tiles(_block_diag_in(bbar[:, 0]), LANE, S5_TILE_STATE),
                              _diag_tiles(_block_diag_in(bbar[:, 1]), LANE, S5_TILE_STATE)], axis=3).astype(BF16)
    wc_all = jnp.concatenate([_diag_tiles(_block_diag_out(s5_c_re), S5_TILE_STATE, LANE),
                              -_diag_tiles(_block_diag_out(s5_c_im), S5_TILE_STATE, LANE)], axis=2).astype(BF16)
    wglu_all = s5_w_glu.astype(BF16)
    w_out_bf = w_out.astype(BF16)
    wqt_all = peer_w_q.astype(BF16).transpose(0, 2, 1)
    keys_all = peer_sub_keys.reshape(DEPTH, 2 * PEER_HEADS, N_KEYS, LANE).astype(BF16)
    u_bf = _pack_experts(peer_u.astype(F32), transpose=False)
    vt_bf = _pack_experts(peer_v.astype(F32), transpose=True)

    st_gla = state_gla.astype(F32)
    st_c = state_mlstm_c.astype(F32)
    st_n = state_mlstm_n.astype(F32)[:, :, :, None, :]
    st_m = jnp.broadcast_to(state_mlstm_m.astype(F32)[:, :, :, None, None], state_mlstm_m.shape + (1, LANE))
    st_s5 = jnp.concatenate([state_s5_re.reshape(DEPTH, bs, S5_STATE), state_s5_im.reshape(DEPTH, bs, S5_STATE)],
                            axis=2).astype(F32)

    sg_s = mc_s = mn_s = mm_s = None
    new_states = []
    for l in range(DEPTH):
        z = _norm_proj(x, mod, gains, w_in_p, tokens=tokens, layer=l, tt=1024, tn=1152)

        seq_kw = dict(n_seq=bp, seq_blocks=seq_blocks, rows=PROMPT_ROWS, row_block0=0, carry=True)
        step_kw = dict(n_seq=step_blocks, seq_blocks=1, rows=STEP_ROWS, chunk=ls, row_block0=step_block0,
                       carry=False, layer=l)

        gla_w = (wg_all[l], bg_all[l], gla_gain_all[l])
        og, sg_p = _gla(z, *gla_w, zeros(bp, GLA_HEADS, GLA_DK, GLA_DV), chunk=GLA_CHUNK, **seq_kw)
        og, sg_s = _gla(z, *gla_w, st_gla, fill=(og,) if l == 0 else (og, sg_s), **step_kw)

        ml_w = (bif_all[l], ml_gain_all[l])
        om, mc_p, mn_p, mm_p = _mlstm(z, *ml_w, zeros(bp, ML_HEADS, ML_DH, ML_DH), zeros(bp, ML_HEADS, 1, ML_DH),
                                      zeros(bp, ML_HEADS, 1, LANE), chunk=ML_CHUNK, **seq_kw)
        om, mc_s, mn_s, mm_s = _mlstm(z, *ml_w, st_c, st_n, st_m,
                                      fill=(om,) if l == 0 else (om, mc_s, mn_s, mm_s), **step_kw)

        su = z[:, ZB_SU * LANE:ZB_SU * LANE + S5_WIDTH]
        u_p = jnp.pad(su[:n_prompt].reshape(bp, lp, S5_WIDTH).transpose(1, 0, 2), ((0, 0), (0, prompt_pad), (0, 0)))
        u_s = su[n_prompt:].reshape(bs, ls, S5_WIDTH).transpose(1, 0, 2)
        s5_args = (a_disc[l], wb_all[l], wc_all[l], s5_d[l].reshape(1, S5_WIDTH), wglu_all[l],
                   s5_b_glu[l][None, :], s5_norm_gain[l][None, :])
        os_p, hs_p = _s5(u_p, zeros(bp + prompt_pad, 2 * S5_STATE), *s5_args,
                         steps=S5_PROMPT_STEPS, lane_width=S5_STATE)
        os_s, hs_s = _s5(u_s, st_s5[l], *s5_args, steps=ls, lane_width=LANE)
        os_ = jnp.concatenate([os_p[:, :bp].transpose(1, 0, 2).reshape(n_prompt, S5_WIDTH),
                               os_s.transpose(1, 0, 2).reshape(n_sample, S5_WIDTH)], axis=0)

        x, ht = _out_proj(og, om, os_, x, mod, gains, w_out_bf, tokens=tokens, layer=l, tt=256)

        first, second = _peer_route(ht, wqt_all[l], keys_all[l], tt=256)
        x = _peer_dense(ht, u_bf, vt_bf, first, second, x, mod, tokens=tokens, layer=l, tt=512)

        new_states.append((sg_p, mc_p, mn_p[:, :, 0, :], mm_p[:, :, 0, 0], hs_p[:bp], hs_s))

    y_prompt = _final_norm(x, final_gain[None, :], tt=512, row0=0, n_rows=n_prompt)
    y_sample = _final_norm(x, final_gain[None, :], tt=512, row0=n_prompt, n_rows=n_sample)
    stack = lambda i: jnp.stack([ns[i] for ns in new_states])
    s5_p, s5_s = stack(4), stack(5)
    split_s5 = lambda h, lo: h[:, :, lo:lo + S5_STATE].reshape(DEPTH, -1, S5_GROUPS, S5_P)
    outs = (y_prompt.reshape(bp, lp, D_MODEL), y_sample.reshape(bs, ls, D_MODEL),
            stack(0), sg_s, stack(1), mc_s, stack(2), mn_s[:, :, :, 0, :], stack(3), mm_s[:, :, :, 0, 0],
            split_s5(s5_p, 0), split_s5(s5_s, 0), split_s5(s5_p, S5_STATE), split_s5(s5_s, S5_STATE))
    refs = (x_prompt, x_sample, state_gla, state_gla, state_mlstm_c, state_mlstm_c, state_mlstm_n, state_mlstm_n,
            state_mlstm_m, state_mlstm_m, state_s5_re, state_s5_re, state_s5_im, state_s5_im)
    return tuple(o.astype(r.dtype) for o, r in zip(outs, refs))
```

```python
import functools
import math

import numpy as np
import jax
import jax.numpy as jnp
from jax import lax
from jax.experimental import pallas as pl
from jax.experimental.pallas import tpu as pltpu

F32 = jnp.float32
BF16 = jnp.bfloat16

D_MODEL = 2048
DEPTH = 4
GLA_HEADS = 6
GLA_DK = 64
GLA_DV = 128
GLA_RANK = 16
GLA_TAU = 16.0
GLA_CHUNK = 16
ML_HEADS = 6
ML_DH = 128
ML_CHUNK = 64
S5_WIDTH = 512
S5_CH = 16
S5_GROUPS = 32
S5_P = 64
S5_STATE = S5_GROUPS * S5_P
S5_TILES = S5_WIDTH // 128
S5_TILE_STATE = S5_STATE // S5_TILES
PEER_HEADS = 8
N_KEYS = 128
N_EXPERTS = N_KEYS * N_KEYS
PEER_TOPK = 16
NORM_EPS = 1e-6
GLA_WIDTH = GLA_HEADS * GLA_DV
ML_WIDTH = ML_HEADS * ML_DH

LANE = 128
SUBLANE = 8
TOKENS_PER_GROUP = 8
GROUP_UNROLL = 4
VMEM_LIMIT = 56 * 1024 * 1024

ZB_GQ, ZB_GK, ZB_GV, ZB_GG = 0, 6, 12, 18
ZB_MQ, ZB_MK, ZB_MV, ZB_MO = 24, 30, 36, 42
ZB_GR, ZB_MIF = 48, 49
HEADS_PER_STEP = 6
ZB_SU = 50
Z_BLOCKS = 54
Z_COLS = Z_BLOCKS * LANE
F_GATE_LANE = 8


def _z_source_columns():
    src = np.full((Z_COLS,), -1, np.int32)
    off_gq, off_gk, off_gv, off_gg, off_gr = 0, 384, 768, 1536, 2304
    off_mq, off_mk, off_mv, off_mo, off_mi, off_mf, off_su = 2320, 3088, 3856, 4624, 5392, 5398, 5404
    for h in range(GLA_HEADS):
        for d in range(GLA_DK):
            src[(ZB_GQ + h) * LANE + d] = off_gq + h * GLA_DK + d
            src[(ZB_GK + h) * LANE + d] = off_gk + h * GLA_DK + d
        for d in range(GLA_DV):
            src[(ZB_GV + h) * LANE + d] = off_gv + h * GLA_DV + d
            src[(ZB_GG + h) * LANE + d] = off_gg + h * GLA_DV + d
    for d in range(GLA_RANK):
        src[ZB_GR * LANE + d] = off_gr + d
    for h in range(ML_HEADS):
        for d in range(ML_DH):
            src[(ZB_MQ + h) * LANE + d] = off_mq + h * ML_DH + d
            src[(ZB_MK + h) * LANE + d] = off_mk + h * ML_DH + d
            src[(ZB_MV + h) * LANE + d] = off_mv + h * ML_DH + d
            src[(ZB_MO + h) * LANE + d] = off_mo + h * ML_DH + d
        src[ZB_MIF * LANE + h] = off_mi + h
        src[ZB_MIF * LANE + F_GATE_LANE + h] = off_mf + h
    for d in range(S5_WIDTH):
        src[ZB_SU * LANE + d] = off_su + d
    return src


_Z_SRC = _z_source_columns()


def _z_runs():
    runs, i = [], 0
    while i < Z_COLS:
        j = i + 1
        if _Z_SRC[i] < 0:
            while j < Z_COLS and _Z_SRC[j] < 0:
                j += 1
            runs.append((-1, j - i))
        else:
            while j < Z_COLS and _Z_SRC[j] == _Z_SRC[j - 1] + 1:
                j += 1
            runs.append((int(_Z_SRC[i]), j - i))
        i = j
    return runs


_Z_RUNS = _z_runs()


def _pad_z_columns(w):
    parts = [jnp.zeros(w.shape[:-1] + (n,), w.dtype) if s < 0 else w[..., s:s + n] for s, n in _Z_RUNS]
    return jnp.concatenate(parts, axis=-1)


def _cparams(semantics):
    return pltpu.CompilerParams(dimension_semantics=semantics, vmem_limit_bytes=VMEM_LIMIT)


def _const_spec(block_shape, index_map):
    return pl.BlockSpec(block_shape, index_map, pipeline_mode=pl.Buffered(1))


def _rms(x):
    return x * lax.rsqrt(jnp.mean(x * x, axis=-1, keepdims=True) + NORM_EPS)


def _unpack_rows(x):
    return pltpu.bitcast(x, BF16)


def _dot(a, b):
    return jnp.dot(a, b, preferred_element_type=F32)


def _dot_nt(a, b):
    return lax.dot_general(a, b, (((1,), (1,)), ((), ())), preferred_element_type=F32)


def _chunk_sums(same, tri, x):
    rows = x.shape[0]
    masks = jnp.concatenate([jnp.where(tri, 1.0, 0.0), jnp.where(same, 1.0, 0.0)], axis=0).astype(BF16)
    hi = x.astype(BF16)
    r1 = x - hi.astype(F32)
    mid = r1.astype(BF16)
    lo = (r1 - mid.astype(F32)).astype(BF16)
    sums = _dot(masks, jnp.concatenate([hi, mid, lo], axis=1))
    total = sums[:, 0:LANE] + sums[:, LANE:2 * LANE] + sums[:, 2 * LANE:3 * LANE]
    return total[0:rows], total[rows:2 * rows]


def _adaln_kernel(c_ref, w_lo_ref, w_hi_ref, b_ref, o_ref):
    c = c_ref[...]
    s = (c * jax.nn.sigmoid(c)).astype(BF16)
    half = D_MODEL // 2
    o_ref[0] = (_dot(s[:, :half], w_lo_ref[0].astype(BF16)) + _dot(s[:, half:], w_hi_ref[0].astype(BF16))
                + b_ref[0])


def _adaln(c_all, w_ada, b_ada):
    n_rows = c_all.shape[0]
    n_out = w_ada.shape[-1]
    tn = 1024
    w_half = lambda which: pl.BlockSpec((1, D_MODEL // 2, tn), lambda l, j: (l, which, j))
    return pl.pallas_call(
        _adaln_kernel,
        grid=(DEPTH, n_out // tn),
        in_specs=[
            pl.BlockSpec((n_rows, D_MODEL), lambda l, j: (0, 0)),
            w_half(0), w_half(1),
            pl.BlockSpec((1, 1, tn), lambda l, j: (l, 0, j)),
        ],
        out_specs=pl.BlockSpec((1, n_rows, tn), lambda l, j: (l, 0, j)),
        out_shape=jax.ShapeDtypeStruct((DEPTH, n_rows, n_out), F32),
        compiler_params=_cparams(("parallel", "parallel")),
        name="adaln",
    )(c_all, w_ada, w_ada, b_ada.reshape(DEPTH, 1, n_out))


def _modulated_group(x8, gain, sc_row, sh_row):
    return _rms(x8) * gain * (1.0 + sc_row) + sh_row


class _ModRows:
    def __init__(self, prompt_ref, sample_ref, prompt_tiles, tiles_per_seq):
        tile = pl.program_id(0)
        self.is_prompt = tile < prompt_tiles
        self.seq = jnp.minimum(tile // tiles_per_seq, SUBLANE - 1)
        self.prompt_ref = prompt_ref
        self.sample_ref = sample_ref

    def row(self, g):
        return jnp.where(self.is_prompt, self.prompt_ref[pl.ds(self.seq, 1), :], self.sample_ref[pl.ds(g, 1), :])


def _norm_proj_kernel(scp_ref, scs_ref, shp_ref, shs_ref, x_ref, gain_ref, w_lo_ref, w_hi_ref, o_ref, h_scr,
                      *, groups, tiles):
    sc = _ModRows(scp_ref, scs_ref, *tiles)
    sh = _ModRows(shp_ref, shs_ref, *tiles)

    @pl.when(pl.program_id(1) == 0)
    def _():
        gain = gain_ref[...]

        def body(p, carry):
            hs = []
            for u in range(2):
                g = p * 2 + u
                r0 = pl.multiple_of(g * TOKENS_PER_GROUP, TOKENS_PER_GROUP)
                hs.append(_modulated_group(x_ref[pl.ds(r0, TOKENS_PER_GROUP), :], gain, sc.row(g), sh.row(g)))
            r = pl.multiple_of(p * 2 * TOKENS_PER_GROUP, 2 * TOKENS_PER_GROUP)
            h_scr[pl.ds(r, 2 * TOKENS_PER_GROUP), :] = jnp.concatenate(hs, axis=0).astype(BF16)
            return carry

        lax.fori_loop(0, groups // 2, body, 0, unroll=GROUP_UNROLL)

    half = D_MODEL // 2
    o_ref[...] = _dot(h_scr[:, :half], w_lo_ref[...]) + _dot(h_scr[:, half:], w_hi_ref[...])


MOD_SH1, MOD_SC1, MOD_G1, MOD_SH2, MOD_SC2, MOD_G2 = range(6)


class _TokenLayout:
    def __init__(self, n_prompt, seq_len, n_sample_seq):
        self.n_prompt, self.seq_len, self.n_sample_seq = n_prompt, seq_len, n_sample_seq

    def tiles(self, tt):
        assert self.n_prompt % tt == 0 and self.seq_len % tt == 0
        return self.n_prompt // tt, self.seq_len // tt

    def mod_specs(self, tt, layer, kind):
        groups = tt // TOKENS_PER_GROUP
        prompt_tiles, _ = self.tiles(tt)
        assert self.n_sample_seq % groups == 0 and self.n_sample_seq % SUBLANE == 0
        return [pl.BlockSpec((None, SUBLANE, D_MODEL), lambda *a: (layer, self.n_sample_seq // SUBLANE, kind)),
                pl.BlockSpec((None, groups, D_MODEL), lambda *a: (layer, jnp.maximum(a[0] - prompt_tiles, 0), kind))]


def _gain_spec(layer, which):
    return pl.BlockSpec((None, 1, D_MODEL), lambda *a: (2 * layer + which, 0, 0))


def _norm_proj(x, mod, gains, w, *, tokens, layer, tt, tn):
    n_tok = x.shape[0]
    n_out = w.shape[-1]
    groups = tt // TOKENS_PER_GROUP
    return pl.pallas_call(
        functools.partial(_norm_proj_kernel, groups=groups, tiles=tokens.tiles(tt)),
        grid=(n_tok // tt, n_out // tn),
        in_specs=tokens.mod_specs(tt, layer, MOD_SC1) + tokens.mod_specs(tt, layer, MOD_SH1) + [
            pl.BlockSpec((tt, D_MODEL), lambda i, j: (i, 0)),
            _gain_spec(layer, 0),
            pl.BlockSpec((None, D_MODEL // 2, tn), lambda i, j: (layer, 0, j)),
            pl.BlockSpec((None, D_MODEL // 2, tn), lambda i, j: (layer, 1, j)),
        ],
        out_specs=pl.BlockSpec((tt, tn), lambda i, j: (i, j)),
        out_shape=jax.ShapeDtypeStruct((n_tok, n_out), F32),
        scratch_shapes=[pltpu.VMEM((tt, D_MODEL), BF16)],
        compiler_params=_cparams(("parallel", "arbitrary")),
        name="norm_proj",
    )(mod, mod, mod, mod, x, gains, w, w)


def _chunk_masks(rows, chunk):
    shift = int(math.log2(chunk))
    ri = lax.broadcasted_iota(jnp.int32, (rows, rows), 0)
    ci = lax.broadcasted_iota(jnp.int32, (rows, rows), 1)
    same = (ri >> shift) == (ci >> shift)
    tri = jnp.logical_and(same, ci <= ri)
    return same, tri


def _gla_kernel(q_ref, k_ref, v_ref, g_ref, r_ref, wg_ref, bg_ref, gain_ref, s0_ref,
                o_ref, sfin_ref, st_scr, **kw):
    carry = kw["carry"]
    heads = range(HEADS_PER_STEP)
    if carry:
        @pl.when(pl.program_id(2) == 0)
        def _():
            for hh in heads:
                st_scr[hh] = _gla_load_state(s0_ref.at[:, hh:hh + 1], 0)
    for hh in heads:
        ln = slice(hh * LANE, (hh + 1) * LANE)
        one = slice(hh, hh + 1)
        _gla_head(q_ref.at[:, ln], k_ref.at[:, ln], v_ref.at[:, ln], g_ref.at[:, ln], r_ref,
                  wg_ref.at[one], bg_ref.at[one], gain_ref.at[one], s0_ref.at[:, one],
                  o_ref.at[:, ln], sfin_ref.at[:, one], st_scr.at[hh], **kw)
    if carry:
        @pl.when(pl.program_id(2) == pl.num_programs(2) - 1)
        def _():
            for hh in heads:
                sfin_ref[0, hh] = st_scr[hh].T[0:GLA_DK, :]


def _gla_load_state(s0_ref, c):
    zero_pad = jnp.zeros((LANE - GLA_DK, GLA_DV), F32)
    return jnp.concatenate([s0_ref[c, 0], zero_pad], axis=0).T


def _gla_head(q_ref, k_ref, v_ref, g_ref, r_ref, wg_ref, bg_ref, gain_ref, s0_ref,
              o_ref, sfin_ref, st_scr, *, rows, chunk, carry):
    n_chunks = rows // chunk
    shift = int(math.log2(chunk))
    same, tri = _chunk_masks(rows, chunk)

    q = q_ref[...] * (GLA_DK ** -0.5)
    k = k_ref[...]
    v = v_ref[...]
    la = jax.nn.log_sigmoid(_dot(r_ref[...].astype(BF16), wg_ref[0]) + bg_ref[0]) / GLA_TAU
    bl, bt = _chunk_sums(same, tri, la)
    qd = (q * jnp.exp(bl)).astype(BF16)
    ki = (k * jnp.exp(-bl)).astype(BF16)
    kd = (k * jnp.exp(bt - bl)).astype(BF16)
    att = jnp.where(tri, _dot_nt(qd, ki), 0.0)
    o_intra = _dot(att.astype(BF16), v.astype(BF16))

    v_t = v.T
    lane_chunk = lax.broadcasted_iota(jnp.int32, (GLA_DV, rows), 1) >> shift
    if carry:
        s_t = st_scr[...]

    sliced = chunk % (2 * SUBLANE) == 0
    row_chunk = lax.broadcasted_iota(jnp.int32, (rows, GLA_DV), 0) >> shift
    o_inter = [] if sliced else jnp.zeros((rows, GLA_DV), F32)
    for c in range(n_chunks):
        lo = c * chunk
        if not carry:
            s_t = _gla_load_state(s0_ref, c)
        if sliced:
            o_inter.append(_dot_nt(qd[lo:lo + chunk], s_t.astype(BF16)))
        else:
            o_inter = jnp.where(row_chunk == c, _dot_nt(qd, s_t.astype(BF16)), o_inter)
        decay = jnp.exp(bt[lo:lo + 1, :])
        v_c = jnp.where(lane_chunk == c, v_t, 0.0).astype(BF16)
        s_t = s_t * decay + _dot(v_c, kd)
        if not carry:
            sfin_ref[c, 0] = s_t.T[0:GLA_DK, :]

    if carry:
        st_scr[...] = s_t

    o = o_intra + (jnp.concatenate(o_inter, axis=0) if sliced else o_inter)
    g = g_ref[...]
    o_ref[...] = _rms(o) * gain_ref[0] * (g * jax.nn.sigmoid(g))


def _without_refs(kernel_fn, first, count):
    def wrapped(*refs):
        return kernel_fn(*refs[:first], *refs[first + count:])
    return wrapped


def _mixer_grid(n_seq, n_heads, seq_blocks, row_block0, carry):
    if carry:
        return ((n_seq, n_heads, seq_blocks), lambda b, h, t: row_block0 + b * seq_blocks + t,
                ("parallel", "parallel", "arbitrary"))
    return (n_seq, n_heads), lambda b, h: row_block0 + b, ("parallel", "parallel")


def _state_spec(per_block, d2, d3, layer):
    if layer is None:
        return pl.BlockSpec((per_block, HEADS_PER_STEP, d2, d3), lambda *a: (a[0], a[1], 0, 0))
    return pl.BlockSpec((None, per_block, HEADS_PER_STEP, d2, d3), lambda *a: (layer, a[0], a[1], 0, 0))


def _head_cols_spec(rows, rb, block0):
    assert block0 % HEADS_PER_STEP == 0
    return pl.BlockSpec((rows, HEADS_PER_STEP * LANE), lambda *a: (rb(*a), block0 // HEADS_PER_STEP + a[1]))


def _head_param_spec(shape):
    return pl.BlockSpec((HEADS_PER_STEP,) + shape, lambda *a: (a[1],) + (0,) * len(shape))


def _gla(z, wg, bg, gain, s0, *, n_seq, seq_blocks, rows, chunk, row_block0, carry,
         layer=None, fill=()):
    per_block = 1 if carry else rows // chunk
    grid, rb, sem = _mixer_grid(n_seq, GLA_HEADS // HEADS_PER_STEP, seq_blocks, row_block0, carry)
    zspec = functools.partial(_head_cols_spec, rows, rb)
    state_spec = _state_spec(per_block, GLA_DK, GLA_DV, layer)
    n_in = 9
    return pl.pallas_call(
        _without_refs(functools.partial(_gla_kernel, rows=rows, chunk=chunk, carry=carry), n_in, len(fill)),
        grid=grid,
        in_specs=[zspec(ZB_GQ), zspec(ZB_GK), zspec(ZB_GV), zspec(ZB_GG),
                  pl.BlockSpec((rows, LANE), lambda *a: (rb(*a), ZB_GR)),
                  _head_param_spec((LANE, LANE)), _head_param_spec((1, LANE)), _head_param_spec((1, LANE)),
                  state_spec]
                 + [pl.BlockSpec(memory_space=pl.ANY)] * len(fill),
        out_specs=[_head_cols_spec(rows, rb, 0), state_spec],
        out_shape=[jax.ShapeDtypeStruct((z.shape[0], GLA_WIDTH), F32),
                   jax.ShapeDtypeStruct(s0.shape, F32)],
        input_output_aliases={n_in + i: i for i in range(len(fill))},
        scratch_shapes=[pltpu.VMEM((HEADS_PER_STEP, GLA_DV, LANE), F32)],
        compiler_params=_cparams(sem),
        name="gla_seq" if carry else "gla_step",
    )(z, z, z, z, z, wg, bg, gain, s0, *fill)


def _mlstm_kernel(q_ref, k_ref, v_ref, og_ref, gate_ref, bif_ref, gain_ref, c0_ref, n0_ref, m0_ref,
                  h_ref, cfin_ref, nfin_ref, mfin_ref, c_scr, n_scr, m_scr, **kw):
    carry = kw["carry"]
    heads = range(HEADS_PER_STEP)
    if carry:
        @pl.when(pl.program_id(2) == 0)
        def _():
            for hh in heads:
                c_scr[hh] = c0_ref[0, hh]
                n_scr[hh] = n0_ref[0, hh]
                m_scr[hh] = m0_ref[0, hh]
    _mlstm_heads(q_ref, k_ref, v_ref, og_ref, gate_ref, bif_ref, gain_ref, c0_ref, n0_ref, m0_ref,
                 h_ref, cfin_ref, nfin_ref, mfin_ref, c_scr, n_scr, m_scr, **kw)
    if carry:
        @pl.when(pl.program_id(2) == pl.num_programs(2) - 1)
        def _():
            for hh in heads:
                cfin_ref[0, hh] = c_scr[hh]
                nfin_ref[0, hh] = n_scr[hh]
                mfin_ref[0, hh] = m_scr[hh]


def _mlstm_heads(q_ref, k_ref, v_ref, og_ref, gate_ref, bif_ref, gain_ref, c0_ref, n0_ref, m0_ref,
                 h_ref, cfin_ref, nfin_ref, mfin_ref, c_scr, n_scr, m_scr, **kw):
    for hh in range(HEADS_PER_STEP):
        ln = slice(hh * LANE, (hh + 1) * LANE)
        one = slice(hh, hh + 1)
        _mlstm_head(pl.program_id(1) * HEADS_PER_STEP + hh,
                    q_ref.at[:, ln], k_ref.at[:, ln], v_ref.at[:, ln], og_ref.at[:, ln], gate_ref, bif_ref,
                    gain_ref.at[one], c0_ref.at[:, one], n0_ref.at[:, one], m0_ref.at[:, one],
                    h_ref.at[:, ln], cfin_ref.at[:, one], nfin_ref.at[:, one], mfin_ref.at[:, one],
                    c_scr.at[hh], n_scr.at[hh], m_scr.at[hh], **kw)


def _mlstm_head(head, q_ref, k_ref, v_ref, og_ref, gate_ref, bif_ref, gain_ref, c0_ref, n0_ref, m0_ref,
                h_ref, cfin_ref, nfin_ref, mfin_ref, c_scr, n_scr, m_scr, *, rows, chunk, carry):
    n_chunks = rows // chunk
    shift = int(math.log2(chunk))
    same, tri = _chunk_masks(rows, chunk)
    neg_inf = jnp.float32(-jnp.inf)

    x = gate_ref[...] + bif_ref[...]
    log_f = jax.nn.log_sigmoid(x)
    b_cum, b_tot = _chunk_sums(same, tri, log_f)
    lane = lax.broadcasted_iota(jnp.int32, (rows, LANE), 1)
    y = jnp.where(lane < F_GATE_LANE, x, b_cum)
    pick_i = lane == head
    pick_b = lane == head + F_GATE_LANE
    i_col = jnp.sum(jnp.where(pick_i, y, 0.0), axis=-1, keepdims=True)
    b_col = jnp.sum(jnp.where(pick_b, y, 0.0), axis=-1, keepdims=True)
    bt_col = jnp.sum(jnp.where(pick_b, b_tot, 0.0), axis=-1, keepdims=True)
    sub = lax.broadcasted_iota(jnp.int32, (LANE, rows), 0)
    y_t = y.T
    i_row = jnp.sum(jnp.where(sub == head, y_t, 0.0), axis=0, keepdims=True)
    b_row = jnp.sum(jnp.where(sub == head + F_GATE_LANE, y_t, 0.0), axis=0, keepdims=True)

    a_col = bt_col - b_col + i_col
    mloc_col = bt_col + jnp.max(jnp.where(same, i_row - b_row, neg_inf), axis=-1, keepdims=True)

    q = q_ref[...] * (ML_DH ** -0.5)
    k = k_ref[...]
    qb = q.astype(BF16)
    kb = k.astype(BF16)
    vb = v_ref[...].astype(BF16)
    kw = k * jnp.exp(a_col - mloc_col)
    kw_t = kw.T
    lane_chunk = lax.broadcasted_iota(jnp.int32, (ML_DH, rows), 1) >> shift

    def advance(c, c_st, n_st, m_st, m_last):
        hi = (c + 1) * chunk
        decay = jnp.exp(bt_col[hi - 1:hi] + m_st - m_last)
        scale = jnp.exp(mloc_col[hi - 1:hi] - m_last)
        kw_c = jnp.where(lane_chunk == c, kw_t, 0.0).astype(BF16)
        c_new = decay * c_st + scale * _dot(kw_c, vb)
        n_new = decay * n_st + scale * jnp.sum(kw[hi - chunk:hi], axis=0, keepdims=True)
        return c_new, n_new

    if carry:
        c_st = c_scr[...]
        n_st = n_scr[...]
        m_st = m_scr[:, 0:1]
        ri = lax.broadcasted_iota(jnp.int32, (chunk, chunk), 0)
        ci = lax.broadcasted_iota(jnp.int32, (chunk, chunk), 1)
        tri_c = ci <= ri
        h_chunks = []
        for c in range(n_chunks):
            lo = c * chunk
            hi = lo + chunk
            bc = b_col[lo:hi]
            d_log = jnp.where(tri_c, bc - b_row[:, lo:hi] + i_row[:, lo:hi], neg_inf)
            g_inter = bc + m_st
            m_t = jnp.maximum(g_inter, jnp.max(d_log, axis=-1, keepdims=True))
            w_inter = jnp.exp(g_inter - m_t)
            s = _dot_nt(qb[lo:hi], kb[lo:hi]) * jnp.exp(d_log - m_t)
            num = w_inter * _dot(qb[lo:hi], c_st.astype(BF16)) + _dot(s.astype(BF16), vb[lo:hi])
            den = (w_inter * jnp.sum(q[lo:hi] * n_st, axis=-1, keepdims=True)
                   + jnp.sum(s, axis=-1, keepdims=True))
            h_chunks.append(num / jnp.maximum(jnp.abs(den), jnp.exp(-m_t)))
            m_last = m_t[chunk - 1:chunk]
            c_st, n_st = advance(c, c_st, n_st, m_st, m_last)
            m_st = m_last
        h = jnp.concatenate(h_chunks, axis=0)
        c_scr[...] = c_st
        n_scr[...] = n_st
        m_scr[...] = jnp.broadcast_to(m_st, (1, LANE))
    else:
        row_chunk1 = lax.broadcasted_iota(jnp.int32, (rows, 1), 0) >> shift
        row_chunk = lax.broadcasted_iota(jnp.int32, (rows, ML_DH), 0) >> shift
        m_rows = jnp.zeros((rows, 1), F32)
        n_rows = jnp.zeros((rows, ML_DH), F32)
        qc = jnp.zeros((rows, ML_DH), F32)
        for c in range(n_chunks):
            m_rows = jnp.where(row_chunk1 == c, m0_ref[c, 0][:, 0:1], m_rows)
            n_rows = jnp.where(row_chunk == c, n0_ref[c, 0], n_rows)
            qc = jnp.where(row_chunk == c, _dot(qb, c0_ref[c, 0].astype(BF16)), qc)
        d_log = jnp.where(tri, b_col - b_row + i_row, neg_inf)
        g_inter = b_col + m_rows
        m_t = jnp.maximum(g_inter, jnp.max(d_log, axis=-1, keepdims=True))
        w_inter = jnp.exp(g_inter - m_t)
        s = _dot_nt(qb, kb) * jnp.exp(d_log - m_t)
        num = w_inter * qc + _dot(s.astype(BF16), vb)
        den = w_inter * jnp.sum(q * n_rows, axis=-1, keepdims=True) + jnp.sum(s, axis=-1, keepdims=True)
        h = num / jnp.maximum(jnp.abs(den), jnp.exp(-m_t))
        for c in range(n_chunks):
            hi = (c + 1) * chunk
            m_last = m_t[hi - 1:hi]
            c_new, n_new = advance(c, c0_ref[c, 0], n0_ref[c, 0], m0_ref[c, 0][:, 0:1], m_last)
            cfin_ref[c, 0] = c_new
            nfin_ref[c, 0] = n_new
            mfin_ref[c, 0] = jnp.broadcast_to(m_last, (1, LANE))

    h_ref[...] = _rms(h) * gain_ref[0] * jax.nn.sigmoid(og_ref[...])


def _mlstm(z, bif, gain, c0, n0, m0, *, n_seq, seq_blocks, rows, chunk, row_block0, carry,
           layer=None, fill=()):
    per_block = 1 if carry else rows // chunk
    grid, rb, sem = _mixer_grid(n_seq, ML_HEADS // HEADS_PER_STEP, seq_blocks, row_block0, carry)
    zspec = functools.partial(_head_cols_spec, rows, rb)
    state_specs = [_state_spec(per_block, ML_DH, ML_DH, layer), _state_spec(per_block, 1, ML_DH, layer),
                   _state_spec(per_block, 1, LANE, layer)]
    n_in = 10
    return pl.pallas_call(
        _without_refs(functools.partial(_mlstm_kernel, rows=rows, chunk=chunk, carry=carry), n_in, len(fill)),
        grid=grid,
        in_specs=[zspec(ZB_MQ), zspec(ZB_MK), zspec(ZB_MV), zspec(ZB_MO),
                  pl.BlockSpec((rows, LANE), lambda *a: (rb(*a), ZB_MIF)),
                  pl.BlockSpec((1, LANE), lambda *a: (0, 0)),
                  _head_param_spec((1, LANE))]
                 + state_specs + [pl.BlockSpec(memory_space=pl.ANY)] * len(fill),
        out_specs=[_head_cols_spec(rows, rb, 0)] + state_specs,
        out_shape=[jax.ShapeDtypeStruct((z.shape[0], ML_WIDTH), F32),
                   jax.ShapeDtypeStruct(c0.shape, F32), jax.ShapeDtypeStruct(n0.shape, F32),
                   jax.ShapeDtypeStruct(m0.shape, F32)],
        input_output_aliases={n_in + i: i for i in range(len(fill))},
        scratch_shapes=[pltpu.VMEM((HEADS_PER_STEP, ML_DH, ML_DH), F32), pltpu.VMEM((HEADS_PER_STEP, 1, ML_DH), F32),
                        pltpu.VMEM((HEADS_PER_STEP, 1, LANE), F32)],
        compiler_params=_cparams(sem),
        name="mlstm_seq" if carry else "mlstm_step",
    )(z, z, z, z, z, bif, gain, c0, n0, m0, *fill)


def _s5_disc_kernel(lre_ref, lim_ref, ldt_ref, bre_ref, bim_ref, a_ref, bb_ref):
    lam_re = lre_ref[0]
    lam_im = lim_ref[0]
    dt = jnp.exp(ldt_ref[0])
    mag = jnp.exp(lam_re * dt)
    ar = mag * jnp.cos(lam_im * dt)
    ai = mag * jnp.sin(lam_im * dt)
    den = lam_re * lam_re + lam_im * lam_im
    fr = ((ar - 1.0) * lam_re + ai * lam_im) / den
    fi = (ai * lam_re - (ar - 1.0) * lam_im) / den
    b_re = bre_ref[0]
    b_im = bim_ref[0]
    a_ref[0, 0:1, :] = ar
    a_ref[0, 1:2, :] = ai
    bb_ref[0, 0] = fr * b_re - fi * b_im
    bb_ref[0, 1] = fr * b_im + fi * b_re


def _s5_discretise(lam_re, lam_im, log_dt, b_re, b_im):
    flat = lambda t: t.reshape(DEPTH, 1, S5_STATE)
    ldt = jnp.broadcast_to(log_dt[:, :, None], (DEPTH, S5_GROUPS, S5_P))
    chan = lambda t: t.reshape(DEPTH, S5_STATE, S5_CH).transpose(0, 2, 1)
    row = pl.BlockSpec((1, 1, S5_STATE), lambda l: (l, 0, 0))
    mat = pl.BlockSpec((1, S5_CH, S5_STATE), lambda l: (l, 0, 0))
    return pl.pallas_call(
        _s5_disc_kernel,
        grid=(DEPTH,),
        in_specs=[row, row, row, mat, mat],
        out_specs=[pl.BlockSpec((1, 2, S5_STATE), lambda l: (l, 0, 0)),
                   pl.BlockSpec((1, 2, S5_CH, S5_STATE), lambda l: (l, 0, 0, 0))],
        out_shape=[jax.ShapeDtypeStruct((DEPTH, 2, S5_STATE), F32),
                   jax.ShapeDtypeStruct((DEPTH, 2, S5_CH, S5_STATE), F32)],
        compiler_params=_cparams(("parallel",)),
        name="s5_disc",
    )(flat(lam_re), flat(lam_im), flat(ldt), chan(b_re), chan(b_im))


def _s5_kernel(u_ref, h0_ref, a_ref, wb_ref, wc_ref, d_ref, wglu_ref, bglu_ref, gain_ref,
               y_ref, hfin_ref, bu_scr, h_scr, *, steps, batch, lane_width):
    n_rows = steps * batch

    @pl.when(pl.program_id(0) == 0)
    def _():
        h_scr[...] = h0_ref[...]

    u = u_ref[...].reshape(n_rows, S5_WIDTH)
    ub = u.astype(BF16)
    for t in range(S5_TILES):
        bu = _dot(ub[:, t * LANE:(t + 1) * LANE], wb_ref[t])
        re0 = t * S5_TILE_STATE
        bu_scr[:, :, re0:re0 + S5_TILE_STATE] = bu[:, :S5_TILE_STATE].reshape(steps, batch, S5_TILE_STATE)
        bu_scr[:, :, S5_STATE + re0:S5_STATE + re0 + S5_TILE_STATE] = (
            bu[:, S5_TILE_STATE:].reshape(steps, batch, S5_TILE_STATE))

    for j in range(S5_STATE // lane_width):
        re = slice(j * lane_width, (j + 1) * lane_width)
        im = slice(S5_STATE + j * lane_width, S5_STATE + (j + 1) * lane_width)
        ar = a_ref[0:1, re]
        ai = a_ref[1:2, re]

        def step(t, carry):
            hr, hi = carry
            nr = ar * hr - ai * hi + bu_scr[t, :, re]
            ni = ar * hi + ai * hr + bu_scr[t, :, im]
            bu_scr[t, :, re] = nr
            bu_scr[t, :, im] = ni
            return nr, ni

        hr, hi = lax.fori_loop(0, steps, step, (h_scr[:, re], h_scr[:, im]), unroll=min(steps, 8))
        h_scr[:, re] = hr
        h_scr[:, im] = hi

    ys = []
    for t in range(S5_TILES):
        re0 = t * S5_TILE_STATE
        h_re = bu_scr[:, :, re0:re0 + S5_TILE_STATE].reshape(n_rows, S5_TILE_STATE).astype(BF16)
        h_im = bu_scr[:, :, S5_STATE + re0:S5_STATE + re0 + S5_TILE_STATE].reshape(n_rows, S5_TILE_STATE).astype(BF16)
        ys.append(_dot(h_re, wc_ref[t, :S5_TILE_STATE]) + _dot(h_im, wc_ref[t, S5_TILE_STATE:]))
    y = jnp.concatenate(ys, axis=1) + d_ref[...] * u
    y = jax.nn.gelu(y)
    out = y * jax.nn.sigmoid(_dot(y.astype(BF16), wglu_ref[...]) + bglu_ref[...])
    y_ref[...] = (_rms(out) * gain_ref[...]).reshape(steps, batch, S5_WIDTH)

    @pl.when(pl.program_id(0) == pl.num_programs(0) - 1)
    def _():
        hfin_ref[...] = h_scr[...]


def _s5(u_t, h0, a, wb, wc, d, wglu, bglu, gain, *, steps, lane_width):
    seq, batch, _ = u_t.shape
    const = lambda shape: _const_spec(shape, lambda t: (0,) * len(shape))
    return pl.pallas_call(
        functools.partial(_s5_kernel, steps=steps, batch=batch, lane_width=lane_width),
        grid=(seq // steps,),
        in_specs=[pl.BlockSpec((steps, batch, S5_WIDTH), lambda t: (t, 0, 0)),
                  const((batch, 2 * S5_STATE)), const((2, S5_STATE)),
                  const((S5_TILES, LANE, 2 * S5_TILE_STATE)), const((S5_TILES, 2 * S5_TILE_STATE, LANE)),
                  const((1, S5_WIDTH)), const((S5_WIDTH, S5_WIDTH)), const((1, S5_WIDTH)),
                  const((1, S5_WIDTH))],
        out_specs=[pl.BlockSpec((steps, batch, S5_WIDTH), lambda t: (t, 0, 0)),
                   pl.BlockSpec((batch, 2 * S5_STATE), lambda t: (0, 0))],
        out_shape=[jax.ShapeDtypeStruct((seq, batch, S5_WIDTH), F32),
                   jax.ShapeDtypeStruct((batch, 2 * S5_STATE), F32)],
        scratch_shapes=[pltpu.VMEM((steps, batch, 2 * S5_STATE), F32),
                        pltpu.VMEM((batch, 2 * S5_STATE), F32)],
        compiler_params=_cparams(("arbitrary",)),
        name="s5",
    )(u_t, h0, a, wb, wc, d, wglu, bglu, gain)


def _out_proj_kernel(g1p_ref, g1s_ref, scp_ref, scs_ref, shp_ref, shs_ref, og_ref, om_ref, os_ref, x_ref, gain_ref,
                     wg_ref, wm_ref, ws_ref, xo_ref, ht_ref, mix_scr, h_scr, *, groups, tiles):
    g1 = _ModRows(g1p_ref, g1s_ref, *tiles)
    sc = _ModRows(scp_ref, scs_ref, *tiles)
    sh = _ModRows(shp_ref, shs_ref, *tiles)
    mix_scr[...] = (_dot(og_ref[...].astype(BF16), wg_ref[...])
                    + _dot(om_ref[...].astype(BF16), wm_ref[...])
                    + _dot(os_ref[...].astype(BF16), ws_ref[...]))
    gain = gain_ref[...]

    def body(g, carry):
        r0 = pl.multiple_of(g * TOKENS_PER_GROUP, TOKENS_PER_GROUP)
        rows = pl.ds(r0, TOKENS_PER_GROUP)
        xn = x_ref[rows, :] + g1.row(g) * mix_scr[rows, :]
        xo_ref[rows, :] = xn
        h_scr[rows, :] = _modulated_group(xn, gain, sc.row(g), sh.row(g))
        return carry

    lax.fori_loop(0, groups, body, 0, unroll=GROUP_UNROLL)
    ht_ref[...] = pltpu.bitcast(h_scr[...].T.astype(BF16), jnp.uint32)


def _out_proj(og, om, os_, x, mod, gains, w_out, *, tokens, layer, tt):
    n_tok = x.shape[0]
    groups = tt // TOKENS_PER_GROUP
    tok = lambda width: pl.BlockSpec((tt, width), lambda i: (i, 0))
    w_rows = lambda height, block: _const_spec((None, height, D_MODEL), lambda i: (layer, block, 0))
    assert GLA_WIDTH == ML_WIDTH and (GLA_WIDTH + ML_WIDTH) % S5_WIDTH == 0
    return pl.pallas_call(
        functools.partial(_out_proj_kernel, groups=groups, tiles=tokens.tiles(tt)),
        grid=(n_tok // tt,),
        in_specs=(tokens.mod_specs(tt, layer, MOD_G1) + tokens.mod_specs(tt, layer, MOD_SC2)
                  + tokens.mod_specs(tt, layer, MOD_SH2)) + [
                  tok(GLA_WIDTH), tok(ML_WIDTH), tok(S5_WIDTH), tok(D_MODEL), _gain_spec(layer, 1),
                  w_rows(GLA_WIDTH, 0), w_rows(ML_WIDTH, 1),
                  w_rows(S5_WIDTH, (GLA_WIDTH + ML_WIDTH) // S5_WIDTH)],
        out_specs=[tok(D_MODEL), pl.BlockSpec((D_MODEL // 2, tt), lambda i: (0, i))],
        out_shape=[jax.ShapeDtypeStruct((n_tok, D_MODEL), F32),
                   jax.ShapeDtypeStruct((D_MODEL // 2, n_tok), jnp.uint32)],
        scratch_shapes=[pltpu.VMEM((tt, D_MODEL), F32), pltpu.VMEM((tt, D_MODEL), F32)],
        compiler_params=_cparams(("parallel",)),
        name="out_proj",
    )(mod, mod, mod, mod, mod, mod, og, om, os_, x, gains, w_out, w_out, w_out)


def _top_values(cur, count, out_scr=None, with_rank=False):
    neg_inf = jnp.float32(-jnp.inf)
    vals = []
    rank = jnp.full(cur.shape, float(count), F32) if with_rank else None
    for r in range(count):
        m = jnp.max(cur, axis=0, keepdims=True)
        vals.append(m)
        if out_scr is not None:
            out_scr[r:r + 1, :] = m
        hit = cur == m
        if with_rank:
            rank = jnp.where(hit, float(r), rank)
        if r + 1 < count:
            cur = jnp.where(hit, neg_inf, cur)
    return (vals, rank) if with_rank else vals


def _peer_route_kernel(ht_ref, wqt_ref, keys_ref, first_ref, second_ref, qt_scr, v1_scr, v2_scr, cand_scr,
                       *, tt):
    qt_scr[...] = _dot(wqt_ref[...], _unpack_rows(ht_ref[...]))
    for h in range(PEER_HEADS):
        sc = []
        for side in range(2):
            hs = 2 * h + side
            qb = qt_scr[hs * LANE:(hs + 1) * LANE, :].astype(BF16)
            sc.append(_dot(keys_ref[hs], qb))
        for lt in range(tt // LANE):
            ls = slice(lt * LANE, (lt + 1) * LANE)
            s1 = sc[0][:, ls]
            s2 = sc[1][:, ls]
            _top_values(s1, PEER_TOPK, v1_scr)
            _, rank2 = _top_values(s2, PEER_TOPK, v2_scr, with_rank=True)
            v1_head = v1_scr[0:SUBLANE, :]
            cand_scr[0:PEER_TOPK, :] = v1_scr[0:1, :] + v2_scr[...]
            cand_scr[PEER_TOPK:2 * PEER_TOPK, :] = v1_scr[...] + v2_scr[0:1, :]
            cand_scr[2 * PEER_TOPK:2 * PEER_TOPK + SUBLANE, :] = v1_scr[1:2, :] + v2_scr[0:SUBLANE, :]
            for b in range(1, 5):
                lo = 2 * PEER_TOPK + b * SUBLANE
                cand_scr[lo:lo + SUBLANE, :] = v1_head + v2_scr[b:b + 1, :]
            top = _top_values(cand_scr[...], PEER_TOPK)
            z = jnp.zeros_like(top[0])
            for tv in top:
                z = z + jnp.exp(tv - top[0])
            tau = top[PEER_TOPK - 1]
            n1 = jnp.zeros_like(s1)
            for b in range(SUBLANE):
                n1 = n1 + jnp.where(s1 + v2_scr[b:b + 1, :] >= tau, 1.0, 0.0)
            best = v1_scr[0:1, :]
            tail = jnp.zeros_like(best)
            for b in range(SUBLANE, PEER_TOPK):
                tail = tail + jnp.where(best + v2_scr[b:b + 1, :] >= tau, 1.0, 0.0)
            n1 = n1 + jnp.where(s1 == best, tail, 0.0)
            first_ref[2 * h, :, ls] = n1
            first_ref[2 * h + 1, :, ls] = jnp.exp(s1 - v1_scr[0:1, :]) / z
            second_ref[2 * h, :, ls] = rank2.astype(BF16)
            second_ref[2 * h + 1, :, ls] = jnp.exp(s2 - v2_scr[0:1, :]).astype(BF16)


def _peer_route(ht, wqt, keys, *, tt):
    n_tok = ht.shape[1]
    n_hs = 2 * PEER_HEADS
    assert PEER_TOPK == 2 * SUBLANE
    n_cand = 2 * PEER_TOPK + 5 * SUBLANE
    return pl.pallas_call(
        functools.partial(_peer_route_kernel, tt=tt),
        grid=(n_tok // tt,),
        in_specs=[pl.BlockSpec((D_MODEL // 2, tt), lambda i: (0, i)),
                  _const_spec((n_hs * LANE, D_MODEL), lambda i: (0, 0)),
                  _const_spec((n_hs, N_KEYS, LANE), lambda i: (0, 0, 0))],
        out_specs=[pl.BlockSpec((n_hs, N_KEYS, tt), lambda i: (0, 0, i)),
                   pl.BlockSpec((n_hs, N_KEYS, tt), lambda i: (0, 0, i))],
        out_shape=[jax.ShapeDtypeStruct((n_hs, N_KEYS, n_tok), F32),
                   jax.ShapeDtypeStruct((n_hs, N_KEYS, n_tok), BF16)],
        scratch_shapes=[pltpu.VMEM((n_hs * LANE, tt), F32), pltpu.VMEM((PEER_TOPK, LANE), F32),
                        pltpu.VMEM((PEER_TOPK, LANE), F32), pltpu.VMEM((n_cand, LANE), F32)],
        compiler_params=_cparams(("parallel",)),
        name="peer_route",
    )(ht, wqt, keys)


DENSE_FIRST_KEYS = 4
DENSE_SUB = DENSE_FIRST_KEYS * N_KEYS
GATE_ROWS = 4 * SUBLANE
GATE_PIECES = 4


def _gate_times_act(first_ref, row0, second_ref, act_ref, p_ref, lane_tiles):
    n_pieces = N_KEYS // GATE_ROWS
    zero = jnp.zeros((GATE_ROWS, LANE), BF16)
    for lt in lane_tiles:
        ls = slice(lt * LANE, (lt + 1) * LANE)
        for jp0 in range(0, n_pieces, GATE_PIECES):
            pieces = range(jp0, jp0 + GATE_PIECES)
            gates = [[zero for _ in pieces] for _ in range(DENSE_FIRST_KEYS)]
            for h in range(PEER_HEADS):
                counts, weights = [], []
                for ii in range(DENSE_FIRST_KEYS):
                    r = row0 + ii
                    counts.append(jnp.broadcast_to(first_ref[2 * h, 0, r:r + 1, ls], (GATE_ROWS, LANE)).astype(BF16))
                    weights.append(jnp.broadcast_to(first_ref[2 * h + 1, 0, r:r + 1, ls], (GATE_ROWS, LANE)).astype(BF16))
                for q, jp in enumerate(pieces):
                    js = slice(jp * GATE_ROWS, (jp + 1) * GATE_ROWS)
                    rank2 = second_ref[2 * h, js, ls]
                    e2 = second_ref[2 * h + 1, js, ls]
                    for ii in range(DENSE_FIRST_KEYS):
                        gates[ii][q] = gates[ii][q] + jnp.where(rank2 < counts[ii], e2, zero) * weights[ii]
            for ii in range(DENSE_FIRST_KEYS):
                for q, jp in enumerate(pieces):
                    rs = slice(ii * N_KEYS + jp * GATE_ROWS, ii * N_KEYS + (jp + 1) * GATE_ROWS)
                    p_ref[rs, ls] = gates[ii][q] * act_ref[rs, ls].astype(BF16)


def _peer_dense_kernel(ht_ref, u0_ref, u1_ref, vt0_ref, vt1_ref, first_a_ref, first_b_ref, second_in_ref,
                       x_ref, g2p_ref, g2s_ref, o_ref, acc_scr, act0, act1, p0, p1, second_ref,
                       *, tt, groups, tiles):
    k = pl.program_id(1)

    @pl.when(k == 0)
    def _():
        acc_scr[...] = jnp.zeros_like(acc_scr)
        act1[...] = jnp.zeros_like(act1)
        p0[...] = jnp.zeros_like(p0)
        second_ref[...] = second_in_ref[...]

    tok_half = tt // 2
    tiles_half = tok_half // LANE
    for half in range(2):
        ts = slice(half * tok_half, (half + 1) * tok_half)
        lane_tiles = range(half * tiles_half, (half + 1) * tiles_half)
        act0[:, ts] = jax.nn.gelu(_dot(_unpack_rows(u0_ref[...]), _unpack_rows(ht_ref[:, ts])))
        _gate_times_act(first_a_ref, DENSE_FIRST_KEYS, second_ref, act1, p1, lane_tiles)
        acc_scr[:, ts] += _dot(_unpack_rows(vt0_ref[...]), p0[:, ts])
    for half in range(2):
        ts = slice(half * tok_half, (half + 1) * tok_half)
        lane_tiles = range(half * tiles_half, (half + 1) * tiles_half)
        act1[:, ts] = jax.nn.gelu(_dot(_unpack_rows(u1_ref[...]), _unpack_rows(ht_ref[:, ts])))
        _gate_times_act(first_b_ref, 0, second_ref, act0, p0, lane_tiles)
        acc_scr[:, ts] += _dot(_unpack_rows(vt1_ref[...]), p1[:, ts])

    @pl.when(k == pl.num_programs(1) - 1)
    def _():
        o_ref[...] = acc_scr[...].T
        g2 = _ModRows(g2p_ref, g2s_ref, *tiles)

        def body(g, carry):
            rows = pl.ds(pl.multiple_of(g * TOKENS_PER_GROUP, TOKENS_PER_GROUP), TOKENS_PER_GROUP)
            o_ref[rows, :] = x_ref[rows, :] + g2.row(g) * o_ref[rows, :]
            return carry

        lax.fori_loop(0, groups, body, 0, unroll=GROUP_UNROLL)


def _peer_dense(ht, u_bf, vt_bf, first, second, x, mod, *, tokens, layer, tt):
    n_tok = x.shape[0]
    groups = tt // TOKENS_PER_GROUP
    n_hs = 2 * PEER_HEADS
    ne = 2 * DENSE_SUB
    n_i = ne // N_KEYS
    n_blocks = N_EXPERTS // ne
    tok_const = lambda shape, imap: pl.BlockSpec(shape, imap, pipeline_mode=pl.Buffered(1))
    by_first_key = lambda t: t.reshape(n_hs, n_blocks, n_i, n_tok)
    cur = lambda k: jnp.minimum(k, n_blocks - 1)
    prev = lambda k: jnp.maximum(k - 1, 0)
    key_spec = lambda blk: pl.BlockSpec((n_hs, 1, n_i, tt), lambda i, k: (0, blk(k), 0, i))
    u_spec = lambda sub: pl.BlockSpec((None, DENSE_SUB // 2, D_MODEL), lambda i, k: (layer, 2 * cur(k) + sub, 0))
    vt_spec = lambda sub: pl.BlockSpec((None, None, D_MODEL // 2, DENSE_SUB),
                                       lambda i, k: (layer, 2 * prev(k) + sub, 0, 0))
    return pl.pallas_call(
        functools.partial(_peer_dense_kernel, tt=tt, groups=groups, tiles=tokens.tiles(tt)),
        grid=(n_tok // tt, n_blocks + 1),
        in_specs=[tok_const((D_MODEL // 2, tt), lambda i, k: (0, i)),
                  u_spec(0), u_spec(1), vt_spec(0), vt_spec(1),
                  key_spec(prev), key_spec(cur),
                  tok_const((n_hs, N_KEYS, tt), lambda i, k: (0, 0, i)),
                  tok_const((tt, D_MODEL), lambda i, k: (i, 0))] + tokens.mod_specs(tt, layer, MOD_G2),
        out_specs=pl.BlockSpec((tt, D_MODEL), lambda i, k: (i, 0)),
        out_shape=jax.ShapeDtypeStruct((n_tok, D_MODEL), F32),
        scratch_shapes=[pltpu.VMEM((D_MODEL, tt), F32),
                        pltpu.VMEM((DENSE_SUB, tt), F32), pltpu.VMEM((DENSE_SUB, tt), F32),
                        pltpu.VMEM((DENSE_SUB, tt), BF16), pltpu.VMEM((DENSE_SUB, tt), BF16),
                        pltpu.VMEM((n_hs, N_KEYS, tt), BF16)],
        compiler_params=_cparams(("parallel", "arbitrary")),
        name="peer_dense",
    )(ht, u_bf, u_bf, vt_bf, vt_bf, by_first_key(first), by_first_key(first), second, x, mod, mod)


def _pack_experts_kernel(w_ref, o_ref, *, transpose):
    w = w_ref[...]
    if transpose:
        w = w.T
    o_ref[...] = pltpu.bitcast(w.astype(BF16), jnp.uint32)


def _pack_experts(w, *, transpose):
    blk = DENSE_SUB
    n_blocks = N_EXPERTS // blk
    if transpose:
        out_spec = pl.BlockSpec((None, None, D_MODEL // 2, blk), lambda l, j: (l, j, 0, 0))
        out_shape = (DEPTH, n_blocks, D_MODEL // 2, blk)
    else:
        out_spec = pl.BlockSpec((None, blk // 2, D_MODEL), lambda l, j: (l, j, 0))
        out_shape = (DEPTH, N_EXPERTS // 2, D_MODEL)
    return pl.pallas_call(
        functools.partial(_pack_experts_kernel, transpose=transpose),
        grid=(DEPTH, n_blocks),
        in_specs=[pl.BlockSpec((None, blk, D_MODEL), lambda l, j: (l, j, 0))],
        out_specs=out_spec,
        out_shape=jax.ShapeDtypeStruct(out_shape, jnp.uint32),
        compiler_params=_cparams(("parallel", "parallel")),
        name="pack_experts",
    )(w)


def _final_norm_kernel(x_ref, gain_ref, o_ref):
    o_ref[...] = _rms(x_ref[...]) * gain_ref[...]


def _final_norm(x, gain, *, tt, row0, n_rows):
    return pl.pallas_call(
        _final_norm_kernel,
        grid=(n_rows // tt,),
        in_specs=[pl.BlockSpec((tt, D_MODEL), lambda i: (row0 // tt + i, 0)),
                  pl.BlockSpec((1, D_MODEL), lambda i: (0, 0))],
        out_specs=pl.BlockSpec((tt, D_MODEL), lambda i: (i, 0)),
        out_shape=jax.ShapeDtypeStruct((n_rows, D_MODEL), F32),
        compiler_params=_cparams(("parallel",)),
        name="final_norm",
    )(x, gain)


PROMPT_ROWS = 256
STEP_ROWS = 128
S5_PROMPT_STEPS = 128


def _block_diag_in(bbar):
    rows_group = np.arange(S5_WIDTH) // S5_CH
    cols_group = np.arange(S5_STATE) // S5_P
    mask = jnp.asarray(rows_group[:, None] == cols_group[None, :])
    reps = (1,) * (bbar.ndim - 2) + (S5_GROUPS, 1)
    return jnp.where(mask, jnp.tile(bbar, reps), 0.0)


def _diag_tiles(w, rows, cols):
    return jnp.stack([w[:, t * rows:(t + 1) * rows, t * cols:(t + 1) * cols]
                      for t in range(w.shape[1] // rows)], axis=1)


def _block_diag_out(c):
    rows_group = np.arange(S5_STATE) // S5_P
    cols_group = np.arange(S5_WIDTH) // S5_CH
    mask = jnp.asarray(rows_group[:, None] == cols_group[None, :])
    per_state = c.transpose(0, 1, 3, 2).reshape(c.shape[0], S5_STATE, S5_CH)
    return jnp.where(mask, jnp.tile(per_state, (1, 1, S5_GROUPS)), 0.0)


def kernel(x_prompt, x_sample, state_gla, state_mlstm_c, state_mlstm_n, state_mlstm_m, state_s5_re, state_s5_im, c_prompt, c_sample, w_ada, b_ada, norm_gain, w_in, gla_w_gate_up, gla_b_gate, gla_norm_gain, ml_b_igate, ml_b_fgate, ml_norm_gain, s5_lambda_re, s5_lambda_im, s5_log_dt, s5_b_re, s5_b_im, s5_c_re, s5_c_im, s5_d, s5_w_glu, s5_b_glu, s5_norm_gain, w_out, peer_w_q, peer_sub_keys, peer_u, peer_v, final_gain):
    bp, lp, _ = x_prompt.shape
    bs, ls, _ = x_sample.shape
    assert ls == TOKENS_PER_GROUP and lp % PROMPT_ROWS == 0 and (bs * ls) % STEP_ROWS == 0
    n_prompt = bp * lp
    n_sample = bs * ls
    assert bp <= SUBLANE
    tokens = _TokenLayout(n_prompt, lp, bs)

    x = jnp.concatenate([x_prompt.reshape(n_prompt, D_MODEL), x_sample.reshape(n_sample, D_MODEL)], axis=0)

    c_all = jnp.concatenate([c_sample, c_prompt], axis=0)
    mod = _adaln(jnp.pad(c_all, ((0, SUBLANE - bp), (0, 0))), w_ada, b_ada)
    gains = norm_gain.reshape(DEPTH * 2, 1, D_MODEL)

    zeros = lambda *shape: jnp.zeros(shape, F32)
    seq_blocks = lp // PROMPT_ROWS
    step_blocks = n_sample // STEP_ROWS
    step_block0 = n_prompt // STEP_ROWS
    prompt_pad = (-bp) % SUBLANE

    w_in_p = _pad_z_columns(w_in.astype(BF16))
    wg_all = jnp.zeros((DEPTH, GLA_HEADS, LANE, LANE), F32).at[:, :, :GLA_RANK, :GLA_DK].set(
        gla_w_gate_up.reshape(DEPTH, GLA_RANK, GLA_HEADS, GLA_DK).transpose(0, 2, 1, 3)).astype(BF16)
    bg_all = jnp.zeros((DEPTH, GLA_HEADS, 1, LANE), F32).at[:, :, 0, :GLA_DK].set(
        gla_b_gate.reshape(DEPTH, GLA_HEADS, GLA_DK))
    gla_gain_all = gla_norm_gain.reshape(DEPTH, GLA_HEADS, 1, GLA_DV)
    bif_all = jnp.zeros((DEPTH, 1, LANE), F32).at[:, 0, :ML_HEADS].set(ml_b_igate).at[
        :, 0, F_GATE_LANE:F_GATE_LANE + ML_HEADS].set(ml_b_fgate)
    ml_gain_all = ml_norm_gain.reshape(DEPTH, ML_HEADS, 1, ML_DH)
    a_disc, bbar = _s5_discretise(s5_lambda_re, s5_lambda_im, s5_log_dt, s5_b_re, s5_b_im)
    wb_all = jnp.concatenate([_diag_tiles(_block_diag_in(bbar[:, 0]), LANE, S5_TILE_STATE),
                              _diag_tiles(_block_diag_in(bbar[:, 1]), LANE, S5_TILE_STATE)], axis=3).astype(BF16)
    wc_all = jnp.concatenate([_diag_tiles(_block_diag_out(s5_c_re), S5_TILE_STATE, LANE),
                              -_diag_tiles(_block_diag_out(s5_c_im), S5_TILE_STATE, LANE)], axis=2).astype(BF16)
    wglu_all = s5_w_glu.astype(BF16)
    w_out_bf = w_out.astype(BF16)
    wqt_all = peer_w_q.astype(BF16).transpose(0, 2, 1)
    keys_all = peer_sub_keys.reshape(DEPTH, 2 * PEER_HEADS, N_KEYS, LANE).astype(BF16)
    u_bf = _pack_experts(peer_u.astype(F32), transpose=False)
    vt_bf = _pack_experts(peer_v.astype(F32), transpose=True)

    st_gla = state_gla.astype(F32)
    st_c = state_mlstm_c.astype(F32)
    st_n = state_mlstm_n.astype(F32)[:, :, :, None, :]
    st_m = jnp.broadcast_to(state_mlstm_m.astype(F32)[:, :, :, None, None], state_mlstm_m.shape + (1, LANE))
    st_s5 = jnp.concatenate([state_s5_re.reshape(DEPTH, bs, S5_STATE), state_s5_im.reshape(DEPTH, bs, S5_STATE)],
                            axis=2).astype(F32)

    sg_s = mc_s = mn_s = mm_s = None
    new_states = []
    for l in range(DEPTH):
        z = _norm_proj(x, mod, gains, w_in_p, tokens=tokens, layer=l, tt=1024, tn=1152)

        seq_kw = dict(n_seq=bp, seq_blocks=seq_blocks, rows=PROMPT_ROWS, row_block0=0, carry=True)
        step_kw = dict(n_seq=step_blocks, seq_blocks=1, rows=STEP_ROWS, chunk=ls, row_block0=step_block0,
                       carry=False, layer=l)

        gla_w = (wg_all[l], bg_all[l], gla_gain_all[l])
        og, sg_p = _gla(z, *gla_w, zeros(bp, GLA_HEADS, GLA_DK, GLA_DV), chunk=GLA_CHUNK, **seq_kw)
        og, sg_s = _gla(z, *gla_w, st_gla, fill=(og,) if l == 0 else (og, sg_s), **step_kw)

        ml_w = (bif_all[l], ml_gain_all[l])
        om, mc_p, mn_p, mm_p = _mlstm(z, *ml_w, zeros(bp, ML_HEADS, ML_DH, ML_DH), zeros(bp, ML_HEADS, 1, ML_DH),
                                      zeros(bp, ML_HEADS, 1, LANE), chunk=ML_CHUNK, **seq_kw)
        om, mc_s, mn_s, mm_s = _mlstm(z, *ml_w, st_c, st_n, st_m,
                                      fill=(om,) if l == 0 else (om, mc_s, mn_s, mm_s), **step_kw)

        su = z[:, ZB_SU * LANE:ZB_SU * LANE + S5_WIDTH]
        u_p = jnp.pad(su[:n_prompt].reshape(bp, lp, S5_WIDTH).transpose(1, 0, 2), ((0, 0), (0, prompt_pad), (0, 0)))
        u_s = su[n_prompt:].reshape(bs, ls, S5_WIDTH).transpose(1, 0, 2)
        s5_args = (a_disc[l], wb_all[l], wc_all[l], s5_d[l].reshape(1, S5_WIDTH), wglu_all[l],
                   s5_b_glu[l][None, :], s5_norm_gain[l][None, :])
        os_p, hs_p = _s5(u_p, zeros(bp + prompt_pad, 2 * S5_STATE), *s5_args,
                         steps=S5_PROMPT_STEPS, lane_width=S5_STATE)
        os_s, hs_s = _s5(u_s, st_s5[l], *s5_args, steps=ls, lane_width=LANE)
        os_ = jnp.concatenate([os_p[:, :bp].transpose(1, 0, 2).reshape(n_prompt, S5_WIDTH),
                               os_s.transpose(1, 0, 2).reshape(n_sample, S5_WIDTH)], axis=0)

        x, ht = _out_proj(og, om, os_, x, mod, gains, w_out_bf, tokens=tokens, layer=l, tt=256)

        first, second = _peer_route(ht, wqt_all[l], keys_all[l], tt=256)
        x = _peer_dense(ht, u_bf, vt_bf, first, second, x, mod, tokens=tokens, layer=l, tt=512)

        new_states.append((sg_p, mc_p, mn_p[:, :, 0, :], mm_p[:, :, 0, 0], hs_p[:bp], hs_s))

    y_prompt = _final_norm(x, final_gain[None, :], tt=512, row0=0, n_rows=n_prompt)
    y_sample = _final_norm(x, final_gain[None, :], tt=512, row0=n_prompt, n_rows=n_sample)
    stack = lambda i: jnp.stack([ns[i] for ns in new_states])
    s5_p, s5_s = stack(4), stack(5)
    split_s5 = lambda h, lo: h[:, :, lo:lo + S5_STATE].reshape(DEPTH, -1, S5_GROUPS, S5_P)
    outs = (y_prompt.reshape(bp, lp, D_MODEL), y_sample.reshape(bs, ls, D_MODEL),
            stack(0), sg_s, stack(1), mc_s, stack(2), mn_s[:, :, :, 0, :], stack(3), mm_s[:, :, :, 0, 0],
            split_s5(s5_p, 0), split_s5(s5_s, 0), split_s5(s5_p, S5_STATE), split_s5(s5_s, S5_STATE))
    refs = (x_prompt, x_sample, state_gla, state_gla, state_mlstm_c, state_mlstm_c, state_mlstm_n, state_mlstm_n,
            state_mlstm_m, state_mlstm_m, state_s5_re, state_s5_re, state_s5_im, state_s5_im)
    return tuple(o.astype(r.dtype) for o, r in zip(outs, refs))
```

```python
import functools
import math

import numpy as np
import jax
import jax.numpy as jnp
from jax import lax
from jax.experimental import pallas as pl
from jax.experimental.pallas import tpu as pltpu

F32 = jnp.float32
BF16 = jnp.bfloat16

D_MODEL = 2048
DEPTH = 4
GLA_HEADS = 6
GLA_DK = 64
GLA_DV = 128
GLA_RANK = 16
GLA_TAU = 16.0
GLA_CHUNK = 16
ML_HEADS = 6
ML_DH = 128
ML_CHUNK = 64
S5_WIDTH = 512
S5_CH = 16
S5_GROUPS = 32
S5_P = 64
S5_STATE = S5_GROUPS * S5_P
S5_TILES = S5_WIDTH // 128
S5_TILE_STATE = S5_STATE // S5_TILES
PEER_HEADS = 8
N_KEYS = 128
N_EXPERTS = N_KEYS * N_KEYS
PEER_TOPK = 16
NORM_EPS = 1e-6
GLA_WIDTH = GLA_HEADS * GLA_DV
ML_WIDTH = ML_HEADS * ML_DH

LANE = 128
SUBLANE = 8
TOKENS_PER_GROUP = 8
GROUP_UNROLL = 4
VMEM_LIMIT = 56 * 1024 * 1024

ZB_GQ, ZB_GK, ZB_GV, ZB_GG = 0, 6, 12, 18
ZB_MQ, ZB_MK, ZB_MV, ZB_MO = 24, 30, 36, 42
ZB_GR, ZB_MIF = 48, 49
HEADS_PER_STEP = 6
ZB_SU = 50
Z_BLOCKS = 54
Z_COLS = Z_BLOCKS * LANE
F_GATE_LANE = 8


def _z_source_columns():
    src = np.full((Z_COLS,), -1, np.int32)
    off_gq, off_gk, off_gv, off_gg, off_gr = 0, 384, 768, 1536, 2304
    off_mq, off_mk, off_mv, off_mo, off_mi, off_mf, off_su = 2320, 3088, 3856, 4624, 5392, 5398, 5404
    for h in range(GLA_HEADS):
        for d in range(GLA_DK):
            src[(ZB_GQ + h) * LANE + d] = off_gq + h * GLA_DK + d
            src[(ZB_GK + h) * LANE + d] = off_gk + h * GLA_DK + d
        for d in range(GLA_DV):
            src[(ZB_GV + h) * LANE + d] = off_gv + h * GLA_DV + d
            src[(ZB_GG + h) * LANE + d] = off_gg + h * GLA_DV + d
    for d in range(GLA_RANK):
        src[ZB_GR * LANE + d] = off_gr + d
    for h in range(ML_HEADS):
        for d in range(ML_DH):
            src[(ZB_MQ + h) * LANE + d] = off_mq + h * ML_DH + d
            src[(ZB_MK + h) * LANE + d] = off_mk + h * ML_DH + d
            src[(ZB_MV + h) * LANE + d] = off_mv + h * ML_DH + d
            src[(ZB_MO + h) * LANE + d] = off_mo + h * ML_DH + d
        src[ZB_MIF * LANE + h] = off_mi + h
        src[ZB_MIF * LANE + F_GATE_LANE + h] = off_mf + h
    for d in range(S5_WIDTH):
        src[ZB_SU * LANE + d] = off_su + d
    return src


_Z_SRC = _z_source_columns()


def _z_runs():
    runs, i = [], 0
    while i < Z_COLS:
        j = i + 1
        if _Z_SRC[i] < 0:
            while j < Z_COLS and _Z_SRC[j] < 0:
                j += 1
            runs.append((-1, j - i))
        else:
            while j < Z_COLS and _Z_SRC[j] == _Z_SRC[j - 1] + 1:
                j += 1
            runs.append((int(_Z_SRC[i]), j - i))
        i = j
    return runs


_Z_RUNS = _z_runs()


def _pad_z_columns(w):
    parts = [jnp.zeros(w.shape[:-1] + (n,), w.dtype) if s < 0 else w[..., s:s + n] for s, n in _Z_RUNS]
    return jnp.concatenate(parts, axis=-1)


def _cparams(semantics):
    return pltpu.CompilerParams(dimension_semantics=semantics, vmem_limit_bytes=VMEM_LIMIT)


def _const_spec(block_shape, index_map):
    return pl.BlockSpec(block_shape, index_map, pipeline_mode=pl.Buffered(1))


def _rms(x):
    return x * lax.rsqrt(jnp.mean(x * x, axis=-1, keepdims=True) + NORM_EPS)


def _unpack_rows(x):
    return pltpu.bitcast(x, BF16)


def _dot(a, b):
    return jnp.dot(a, b, preferred_element_type=F32)


def _dot_nt(a, b):
    return lax.dot_general(a, b, (((1,), (1,)), ((), ())), preferred_element_type=F32)


def _chunk_sums(same, tri, x):
    rows = x.shape[0]
    masks = jnp.concatenate([jnp.where(tri, 1.0, 0.0), jnp.where(same, 1.0, 0.0)], axis=0).astype(BF16)
    hi = x.astype(BF16)
    r1 = x - hi.astype(F32)
    mid = r1.astype(BF16)
    lo = (r1 - mid.astype(F32)).astype(BF16)
    sums = _dot(masks, jnp.concatenate([hi, mid, lo], axis=1))
    total = sums[:, 0:LANE] + sums[:, LANE:2 * LANE] + sums[:, 2 * LANE:3 * LANE]
    return total[0:rows], total[rows:2 * rows]


def _adaln_kernel(c_ref, w_lo_ref, w_hi_ref, b_ref, o_ref):
    c = c_ref[...]
    s = (c * jax.nn.sigmoid(c)).astype(BF16)
    half = D_MODEL // 2
    o_ref[0] = (_dot(s[:, :half], w_lo_ref[0].astype(BF16)) + _dot(s[:, half:], w_hi_ref[0].astype(BF16))
                + b_ref[0])


def _adaln(c_all, w_ada, b_ada):
    n_rows = c_all.shape[0]
    n_out = w_ada.shape[-1]
    tn = 1024
    w_half = lambda which: pl.BlockSpec((1, D_MODEL // 2, tn), lambda l, j: (l, which, j))
    return pl.pallas_call(
        _adaln_kernel,
        grid=(DEPTH, n_out // tn),
        in_specs=[
            pl.BlockSpec((n_rows, D_MODEL), lambda l, j: (0, 0)),
            w_half(0), w_half(1),
            pl.BlockSpec((1, 1, tn), lambda l, j: (l, 0, j)),
        ],
        out_specs=pl.BlockSpec((1, n_rows, tn), lambda l, j: (l, 0, j)),
        out_shape=jax.ShapeDtypeStruct((DEPTH, n_rows, n_out), F32),
        compiler_params=_cparams(("parallel", "parallel")),
        name="adaln",
    )(c_all, w_ada, w_ada, b_ada.reshape(DEPTH, 1, n_out))


def _modulated_group(x8, gain, sc_row, sh_row):
    return _rms(x8) * gain * (1.0 + sc_row) + sh_row


class _ModRows:
    def __init__(self, prompt_ref, sample_ref, prompt_tiles, tiles_per_seq):
        tile = pl.program_id(0)
        self.is_prompt = tile < prompt_tiles
        self.seq = jnp.minimum(tile // tiles_per_seq, SUBLANE - 1)
        self.prompt_ref = prompt_ref
        self.sample_ref = sample_ref

    def row(self, g):
        return jnp.where(self.is_prompt, self.prompt_ref[pl.ds(self.seq, 1), :], self.sample_ref[pl.ds(g, 1), :])


def _norm_proj_kernel(scp_ref, scs_ref, shp_ref, shs_ref, x_ref, gain_ref, w_lo_ref, w_hi_ref, o_ref, h_scr,
                      *, groups, tiles):
    sc = _ModRows(scp_ref, scs_ref, *tiles)
    sh = _ModRows(shp_ref, shs_ref, *tiles)

    @pl.when(pl.program_id(1) == 0)
    def _():
        gain = gain_ref[...]

        def body(p, carry):
            hs = []
            for u in range(2):
                g = p * 2 + u
                r0 = pl.multiple_of(g * TOKENS_PER_GROUP, TOKENS_PER_GROUP)
                hs.append(_modulated_group(x_ref[pl.ds(r0, TOKENS_PER_GROUP), :], gain, sc.row(g), sh.row(g)))
            r = pl.multiple_of(p * 2 * TOKENS_PER_GROUP, 2 * TOKENS_PER_GROUP)
            h_scr[pl.ds(r, 2 * TOKENS_PER_GROUP), :] = jnp.concatenate(hs, axis=0).astype(BF16)
            return carry

        lax.fori_loop(0, groups // 2, body, 0, unroll=GROUP_UNROLL)

    half = D_MODEL // 2
    o_ref[...] = _dot(h_scr[:, :half], w_lo_ref[...]) + _dot(h_scr[:, half:], w_hi_ref[...])


MOD_SH1, MOD_SC1, MOD_G1, MOD_SH2, MOD_SC2, MOD_G2 = range(6)


class _TokenLayout:
    def __init__(self, n_prompt, seq_len, n_sample_seq):
        self.n_prompt, self.seq_len, self.n_sample_seq = n_prompt, seq_len, n_sample_seq

    def tiles(self, tt):
        assert self.n_prompt % tt == 0 and self.seq_len % tt == 0
        return self.n_prompt // tt, self.seq_len // tt

    def mod_specs(self, tt, layer, kind):
        groups = tt // TOKENS_PER_GROUP
        prompt_tiles, _ = self.tiles(tt)
        assert self.n_sample_seq % groups == 0 and self.n_sample_seq % SUBLANE == 0
        return [pl.BlockSpec((None, SUBLANE, D_MODEL), lambda *a: (layer, self.n_sample_seq // SUBLANE, kind)),
                pl.BlockSpec((None, groups, D_MODEL), lambda *a: (layer, jnp.maximum(a[0] - prompt_tiles, 0), kind))]


def _gain_spec(layer, which):
    return pl.BlockSpec((None, 1, D_MODEL), lambda *a: (2 * layer + which, 0, 0))


def _norm_proj(x, mod, gains, w, *, tokens, layer, tt, tn):
    n_tok = x.shape[0]
    n_out = w.shape[-1]
    groups = tt // TOKENS_PER_GROUP
    return pl.pallas_call(
        functools.partial(_norm_proj_kernel, groups=groups, tiles=tokens.tiles(tt)),
        grid=(n_tok // tt, n_out // tn),
        in_specs=tokens.mod_specs(tt, layer, MOD_SC1) + tokens.mod_specs(tt, layer, MOD_SH1) + [
            pl.BlockSpec((tt, D_MODEL), lambda i, j: (i, 0)),
            _gain_spec(layer, 0),
            pl.BlockSpec((None, D_MODEL // 2, tn), lambda i, j: (layer, 0, j)),
            pl.BlockSpec((None, D_MODEL // 2, tn), lambda i, j: (layer, 1, j)),
        ],
        out_specs=pl.BlockSpec((tt, tn), lambda i, j: (i, j)),
        out_shape=jax.ShapeDtypeStruct((n_tok, n_out), F32),
        scratch_shapes=[pltpu.VMEM((tt, D_MODEL), BF16)],
        compiler_params=_cparams(("parallel", "arbitrary")),
        name="norm_proj",
    )(mod, mod, mod, mod, x, gains, w, w)


def _chunk_masks(rows, chunk):
    shift = int(math.log2(chunk))
    ri = lax.broadcasted_iota(jnp.int32, (rows, rows), 0)
    ci = lax.broadcasted_iota(jnp.int32, (rows, rows), 1)
    same = (ri >> shift) == (ci >> shift)
    tri = jnp.logical_and(same, ci <= ri)
    return same, tri


def _gla_kernel(q_ref, k_ref, v_ref, g_ref, r_ref, wg_ref, bg_ref, gain_ref, s0_ref,
                o_ref, sfin_ref, st_scr, **kw):
    carry = kw["carry"]
    heads = range(HEADS_PER_STEP)
    if carry:
        @pl.when(pl.program_id(2) == 0)
        def _():
            for hh in heads:
                st_scr[hh] = _gla_load_state(s0_ref.at[:, hh:hh + 1], 0)
    for hh in heads:
        ln = slice(hh * LANE, (hh + 1) * LANE)
        one = slice(hh, hh + 1)
        _gla_head(q_ref.at[:, ln], k_ref.at[:, ln], v_ref.at[:, ln], g_ref.at[:, ln], r_ref,
                  wg_ref.at[one], bg_ref.at[one], gain_ref.at[one], s0_ref.at[:, one],
                  o_ref.at[:, ln], sfin_ref.at[:, one], st_scr.at[hh], **kw)
    if carry:
        @pl.when(pl.program_id(2) == pl.num_programs(2) - 1)
        def _():
            for hh in heads:
                sfin_ref[0, hh] = st_scr[hh].T[0:GLA_DK, :]


def _gla_load_state(s0_ref, c):
    zero_pad = jnp.zeros((LANE - GLA_DK, GLA_DV), F32)
    return jnp.concatenate([s0_ref[c, 0], zero_pad], axis=0).T


def _gla_head(q_ref, k_ref, v_ref, g_ref, r_ref, wg_ref, bg_ref, gain_ref, s0_ref,
              o_ref, sfin_ref, st_scr, *, rows, chunk, carry):
    n_chunks = rows // chunk
    shift = int(math.log2(chunk))
    same, tri = _chunk_masks(rows, chunk)

    q = q_ref[...] * (GLA_DK ** -0.5)
    k = k_ref[...]
    v = v_ref[...]
    la = jax.nn.log_sigmoid(_dot(r_ref[...].astype(BF16), wg_ref[0]) + bg_ref[0]) / GLA_TAU
    bl, bt = _chunk_sums(same, tri, la)
    qd = (q * jnp.exp(bl)).astype(BF16)
    ki = (k * jnp.exp(-bl)).astype(BF16)
    kd = (k * jnp.exp(bt - bl)).astype(BF16)
    att = jnp.where(tri, _dot_nt(qd, ki), 0.0)
    o_intra = _dot(att.astype(BF16), v.astype(BF16))

    v_t = v.T
    lane_chunk = lax.broadcasted_iota(jnp.int32, (GLA_DV, rows), 1) >> shift
    if carry:
        s_t = st_scr[...]

    sliced = chunk % (2 * SUBLANE) == 0
    row_chunk = lax.broadcasted_iota(jnp.int32, (rows, GLA_DV), 0) >> shift
    o_inter = [] if sliced else jnp.zeros((rows, GLA_DV), F32)
    for c in range(n_chunks):
        lo = c * chunk
        if not carry:
            s_t = _gla_load_state(s0_ref, c)
        if sliced:
            o_inter.append(_dot_nt(qd[lo:lo + chunk], s_t.astype(BF16)))
        else:
            o_inter = jnp.where(row_chunk == c, _dot_nt(qd, s_t.astype(BF16)), o_inter)
        decay = jnp.exp(bt[lo:lo + 1, :])
        v_c = jnp.where(lane_chunk == c, v_t, 0.0).astype(BF16)
        s_t = s_t * decay + _dot(v_c, kd)
        if not carry:
            sfin_ref[c, 0] = s_t.T[0:GLA_DK, :]

    if carry:
        st_scr[...] = s_t

    o = o_intra + (jnp.concatenate(o_inter, axis=0) if sliced else o_inter)
    g = g_ref[...]
    o_ref[...] = _rms(o) * gain_ref[0] * (g * jax.nn.sigmoid(g))


def _without_refs(kernel_fn, first, count):
    def wrapped(*refs):
        return kernel_fn(*refs[:first], *refs[first + count:])
    return wrapped


def _mixer_grid(n_seq, n_heads, seq_blocks, row_block0, carry):
    if carry:
        return ((n_seq, n_heads, seq_blocks), lambda b, h, t: row_block0 + b * seq_blocks + t,
                ("parallel", "parallel", "arbitrary"))
    return (n_seq, n_heads), lambda b, h: row_block0 + b, ("parallel", "parallel")


def _state_spec(per_block, d2, d3, layer):
    if layer is None:
        return pl.BlockSpec((per_block, HEADS_PER_STEP, d2, d3), lambda *a: (a[0], a[1], 0, 0))
    return pl.BlockSpec((None, per_block, HEADS_PER_STEP, d2, d3), lambda *a: (layer, a[0], a[1], 0, 0))


def _head_cols_spec(rows, rb, block0):
    assert block0 % HEADS_PER_STEP == 0
    return pl.BlockSpec((rows, HEADS_PER_STEP * LANE), lambda *a: (rb(*a), block0 // HEADS_PER_STEP + a[1]))


def _head_param_spec(shape):
    return pl.BlockSpec((HEADS_PER_STEP,) + shape, lambda *a: (a[1],) + (0,) * len(shape))


def _gla(z, wg, bg, gain, s0, *, n_seq, seq_blocks, rows, chunk, row_block0, carry,
         layer=None, fill=()):
    per_block = 1 if carry else rows // chunk
    grid, rb, sem = _mixer_grid(n_seq, GLA_HEADS // HEADS_PER_STEP, seq_blocks, row_block0, carry)
    zspec = functools.partial(_head_cols_spec, rows, rb)
    state_spec = _state_spec(per_block, GLA_DK, GLA_DV, layer)
    n_in = 9
    return pl.pallas_call(
        _without_refs(functools.partial(_gla_kernel, rows=rows, chunk=chunk, carry=carry), n_in, len(fill)),
        grid=grid,
        in_specs=[zspec(ZB_GQ), zspec(ZB_GK), zspec(ZB_GV), zspec(ZB_GG),
                  pl.BlockSpec((rows, LANE), lambda *a: (rb(*a), ZB_GR)),
                  _head_param_spec((LANE, LANE)), _head_param_spec((1, LANE)), _head_param_spec((1, LANE)),
                  state_spec]
                 + [pl.BlockSpec(memory_space=pl.ANY)] * len(fill),
        out_specs=[_head_cols_spec(rows, rb, 0), state_spec],
        out_shape=[jax.ShapeDtypeStruct((z.shape[0], GLA_WIDTH), F32),
                   jax.ShapeDtypeStruct(s0.shape, F32)],
        input_output_aliases={n_in + i: i for i in range(len(fill))},
        scratch_shapes=[pltpu.VMEM((HEADS_PER_STEP, GLA_DV, LANE), F32)],
        compiler_params=_cparams(sem),
        name="gla_seq" if carry else "gla_step",
    )(z, z, z, z, z, wg, bg, gain, s0, *fill)


def _mlstm_kernel(q_ref, k_ref, v_ref, og_ref, gate_ref, bif_ref, gain_ref, c0_ref, n0_ref, m0_ref,
                  h_ref, cfin_ref, nfin_ref, mfin_ref, c_scr, n_scr, m_scr, **kw):
    carry = kw["carry"]
    heads = range(HEADS_PER_STEP)
    if carry:
        @pl.when(pl.program_id(2) == 0)
        def _():
            for hh in heads:
                c_scr[hh] = c0_ref[0, hh]
                n_scr[hh] = n0_ref[0, hh]
                m_scr[hh] = m0_ref[0, hh]
    _mlstm_heads(q_ref, k_ref, v_ref, og_ref, gate_ref, bif_ref, gain_ref, c0_ref, n0_ref, m0_ref,
                 h_ref, cfin_ref, nfin_ref, mfin_ref, c_scr, n_scr, m_scr, **kw)
    if carry:
        @pl.when(pl.program_id(2) == pl.num_programs(2) - 1)
        def _():
            for hh in heads:
                cfin_ref[0, hh] = c_scr[hh]
                nfin_ref[0, hh] = n_scr[hh]
                mfin_ref[0, hh] = m_scr[hh]


def _mlstm_heads(q_ref, k_ref, v_ref, og_ref, gate_ref, bif_ref, gain_ref, c0_ref, n0_ref, m0_ref,
                 h_ref, cfin_ref, nfin_ref, mfin_ref, c_scr, n_scr, m_scr, **kw):
    for hh in range(HEADS_PER_STEP):
        ln = slice(hh * LANE, (hh + 1) * LANE)
        one = slice(hh, hh + 1)
        _mlstm_head(pl.program_id(1) * HEADS_PER_STEP + hh,
                    q_ref.at[:, ln], k_ref.at[:, ln], v_ref.at[:, ln], og_ref.at[:, ln], gate_ref, bif_ref,
                    gain_ref.at[one], c0_ref.at[:, one], n0_ref.at[:, one], m0_ref.at[:, one],
                    h_ref.at[:, ln], cfin_ref.at[:, one], nfin_ref.at[:, one], mfin_ref.at[:, one],
                    c_scr.at[hh], n_scr.at[hh], m_scr.at[hh], **kw)


def _mlstm_head(head, q_ref, k_ref, v_ref, og_ref, gate_ref, bif_ref, gain_ref, c0_ref, n0_ref, m0_ref,
                h_ref, cfin_ref, nfin_ref, mfin_ref, c_scr, n_scr, m_scr, *, rows, chunk, carry):
    n_chunks = rows // chunk
    shift = int(math.log2(chunk))
    same, tri = _chunk_masks(rows, chunk)
    neg_inf = jnp.float32(-jnp.inf)

    x = gate_ref[...] + bif_ref[...]
    log_f = jax.nn.log_sigmoid(x)
    b_cum, b_tot = _chunk_sums(same, tri, log_f)
    lane = lax.broadcasted_iota(jnp.int32, (rows, LANE), 1)
    y = jnp.where(lane < F_GATE_LANE, x, b_cum)
    pick_i = lane == head
    pick_b = lane == head + F_GATE_LANE
    i_col = jnp.sum(jnp.where(pick_i, y, 0.0), axis=-1, keepdims=True)
    b_col = jnp.sum(jnp.where(pick_b, y, 0.0), axis=-1, keepdims=True)
    bt_col = jnp.sum(jnp.where(pick_b, b_tot, 0.0), axis=-1, keepdims=True)
    sub = lax.broadcasted_iota(jnp.int32, (LANE, rows), 0)
    y_t = y.T
    i_row = jnp.sum(jnp.where(sub == head, y_t, 0.0), axis=0, keepdims=True)
    b_row = jnp.sum(jnp.where(sub == head + F_GATE_LANE, y_t, 0.0), axis=0, keepdims=True)

    a_col = bt_col - b_col + i_col
    mloc_col = bt_col + jnp.max(jnp.where(same, i_row - b_row, neg_inf), axis=-1, keepdims=True)

    q = q_ref[...] * (ML_DH ** -0.5)
    k = k_ref[...]
    qb = q.astype(BF16)
    kb = k.astype(BF16)
    vb = v_ref[...].astype(BF16)
    kw = k * jnp.exp(a_col - mloc_col)
    kw_t = kw.T
    lane_chunk = lax.broadcasted_iota(jnp.int32, (ML_DH, rows), 1) >> shift

    def advance(c, c_st, n_st, m_st, m_last):
        hi = (c + 1) * chunk
        decay = jnp.exp(bt_col[hi - 1:hi] + m_st - m_last)
        scale = jnp.exp(mloc_col[hi - 1:hi] - m_last)
        kw_c = jnp.where(lane_chunk == c, kw_t, 0.0).astype(BF16)
        c_new = decay * c_st + scale * _dot(kw_c, vb)
        n_new = decay * n_st + scale * jnp.sum(kw[hi - chunk:hi], axis=0, keepdims=True)
        return c_new, n_new

    if carry:
        c_st = c_scr[...]
        n_st = n_scr[...]
        m_st = m_scr[:, 0:1]
        ri = lax.broadcasted_iota(jnp.int32, (chunk, chunk), 0)
        ci = lax.broadcasted_iota(jnp.int32, (chunk, chunk), 1)
        tri_c = ci <= ri
        h_chunks = []
        for c in range(n_chunks):
            lo = c * chunk
            hi = lo + chunk
            bc = b_col[lo:hi]
            d_log = jnp.where(tri_c, bc - b_row[:, lo:hi] + i_row[:, lo:hi], neg_inf)
            g_inter = bc + m_st
            m_t = jnp.maximum(g_inter, jnp.max(d_log, axis=-1, keepdims=True))
            w_inter = jnp.exp(g_inter - m_t)
            s = _dot_nt(qb[lo:hi], kb[lo:hi]) * jnp.exp(d_log - m_t)
            num = w_inter * _dot(qb[lo:hi], c_st.astype(BF16)) + _dot(s.astype(BF16), vb[lo:hi])
            den = (w_inter * jnp.sum(q[lo:hi] * n_st, axis=-1, keepdims=True)
                   + jnp.sum(s, axis=-1, keepdims=True))
            h_chunks.append(num / jnp.maximum(jnp.abs(den), jnp.exp(-m_t)))
            m_last = m_t[chunk - 1:chunk]
            c_st, n_st = advance(c, c_st, n_st, m_st, m_last)
            m_st = m_last
        h = jnp.concatenate(h_chunks, axis=0)
        c_scr[...] = c_st
        n_scr[...] = n_st
        m_scr[...] = jnp.broadcast_to(m_st, (1, LANE))
    else:
        row_chunk1 = lax.broadcasted_iota(jnp.int32, (rows, 1), 0) >> shift
        row_chunk = lax.broadcasted_iota(jnp.int32, (rows, ML_DH), 0) >> shift
        m_rows = jnp.zeros((rows, 1), F32)
        n_rows = jnp.zeros((rows, ML_DH), F32)
        qc = jnp.zeros((rows, ML_DH), F32)
        for c in range(n_chunks):
            m_rows = jnp.where(row_chunk1 == c, m0_ref[c, 0][:, 0:1], m_rows)
            n_rows = jnp.where(row_chunk == c, n0_ref[c, 0], n_rows)
            qc = jnp.where(row_chunk == c, _dot(qb, c0_ref[c, 0].astype(BF16)), qc)
        d_log = jnp.where(tri, b_col - b_row + i_row, neg_inf)
        g_inter = b_col + m_rows
        m_t = jnp.maximum(g_inter, jnp.max(d_log, axis=-1, keepdims=True))
        w_inter = jnp.exp(g_inter - m_t)
        s = _dot_nt(qb, kb) * jnp.exp(d_log - m_t)
        num = w_inter * qc + _dot(s.astype(BF16), vb)
        den = w_inter * jnp.sum(q * n_rows, axis=-1, keepdims=True) + jnp.sum(s, axis=-1, keepdims=True)
        h = num / jnp.maximum(jnp.abs(den), jnp.exp(-m_t))
        for c in range(n_chunks):
            hi = (c + 1) * chunk
            m_last = m_t[hi - 1:hi]
            c_new, n_new = advance(c, c0_ref[c, 0], n0_ref[c, 0], m0_ref[c, 0][:, 0:1], m_last)
            cfin_ref[c, 0] = c_new
            nfin_ref[c, 0] = n_new
            mfin_ref[c, 0] = jnp.broadcast_to(m_last, (1, LANE))

    h_ref[...] = _rms(h) * gain_ref[0] * jax.nn.sigmoid(og_ref[...])


def _mlstm(z, bif, gain, c0, n0, m0, *, n_seq, seq_blocks, rows, chunk, row_block0, carry,
           layer=None, fill=()):
    per_block = 1 if carry else rows // chunk
    grid, rb, sem = _mixer_grid(n_seq, ML_HEADS // HEADS_PER_STEP, seq_blocks, row_block0, carry)
    zspec = functools.partial(_head_cols_spec, rows, rb)
    state_specs = [_state_spec(per_block, ML_DH, ML_DH, layer), _state_spec(per_block, 1, ML_DH, layer),
                   _state_spec(per_block, 1, LANE, layer)]
    n_in = 10
    return pl.pallas_call(
        _without_refs(functools.partial(_mlstm_kernel, rows=rows, chunk=chunk, carry=carry), n_in, len(fill)),
        grid=grid,
        in_specs=[zspec(ZB_MQ), zspec(ZB_MK), zspec(ZB_MV), zspec(ZB_MO),
                  pl.BlockSpec((rows, LANE), lambda *a: (rb(*a), ZB_MIF)),
                  pl.BlockSpec((1, LANE), lambda *a: (0, 0)),
                  _head_param_spec((1, LANE))]
                 + state_specs + [pl.BlockSpec(memory_space=pl.ANY)] * len(fill),
        out_specs=[_head_cols_spec(rows, rb, 0)] + state_specs,
        out_shape=[jax.ShapeDtypeStruct((z.shape[0], ML_WIDTH), F32),
                   jax.ShapeDtypeStruct(c0.shape, F32), jax.ShapeDtypeStruct(n0.shape, F32),
                   jax.ShapeDtypeStruct(m0.shape, F32)],
        input_output_aliases={n_in + i: i for i in range(len(fill))},
        scratch_shapes=[pltpu.VMEM((HEADS_PER_STEP, ML_DH, ML_DH), F32), pltpu.VMEM((HEADS_PER_STEP, 1, ML_DH), F32),
                        pltpu.VMEM((HEADS_PER_STEP, 1, LANE), F32)],
        compiler_params=_cparams(sem),
        name="mlstm_seq" if carry else "mlstm_step",
    )(z, z, z, z, z, bif, gain, c0, n0, m0, *fill)


def _s5_disc_kernel(lre_ref, lim_ref, ldt_ref, bre_ref, bim_ref, a_ref, bb_ref):
    lam_re = lre_ref[0]
    lam_im = lim_ref[0]
    dt = jnp.exp(ldt_ref[0])
    mag = jnp.exp(lam_re * dt)
    ar = mag * jnp.cos(lam_im * dt)
    ai = mag * jnp.sin(lam_im * dt)
    den = lam_re * lam_re + lam_im * lam_im
    fr = ((ar - 1.0) * lam_re + ai * lam_im) / den
    fi = (ai * lam_re - (ar - 1.0) * lam_im) / den
    b_re = bre_ref[0]
    b_im = bim_ref[0]
    a_ref[0, 0:1, :] = ar
    a_ref[0, 1:2, :] = ai
    bb_ref[0, 0] = fr * b_re - fi * b_im
    bb_ref[0, 1] = fr * b_im + fi * b_re


def _s5_discretise(lam_re, lam_im, log_dt, b_re, b_im):
    flat = lambda t: t.reshape(DEPTH, 1, S5_STATE)
    ldt = jnp.broadcast_to(log_dt[:, :, None], (DEPTH, S5_GROUPS, S5_P))
    chan = lambda t: t.reshape(DEPTH, S5_STATE, S5_CH).transpose(0, 2, 1)
    row = pl.BlockSpec((1, 1, S5_STATE), lambda l: (l, 0, 0))
    mat = pl.BlockSpec((1, S5_CH, S5_STATE), lambda l: (l, 0, 0))
    return pl.pallas_call(
        _s5_disc_kernel,
        grid=(DEPTH,),
        in_specs=[row, row, row, mat, mat],
        out_specs=[pl.BlockSpec((1, 2, S5_STATE), lambda l: (l, 0, 0)),
                   pl.BlockSpec((1, 2, S5_CH, S5_STATE), lambda l: (l, 0, 0, 0))],
        out_shape=[jax.ShapeDtypeStruct((DEPTH, 2, S5_STATE), F32),
                   jax.ShapeDtypeStruct((DEPTH, 2, S5_CH, S5_STATE), F32)],
        compiler_params=_cparams(("parallel",)),
        name="s5_disc",
    )(flat(lam_re), flat(lam_im), flat(ldt), chan(b_re), chan(b_im))


def _s5_kernel(u_ref, h0_ref, a_ref, wb_ref, wc_ref, d_ref, wglu_ref, bglu_ref, gain_ref,
               y_ref, hfin_ref, bu_scr, h_scr, *, steps, batch, lane_width):
    n_rows = steps * batch

    @pl.when(pl.program_id(0) == 0)
    def _():
        h_scr[...] = h0_ref[...]

    u = u_ref[...].reshape(n_rows, S5_WIDTH)
    ub = u.astype(BF16)
    for t in range(S5_TILES):
        bu = _dot(ub[:, t * LANE:(t + 1) * LANE], wb_ref[t])
        re0 = t * S5_TILE_STATE
        bu_scr[:, :, re0:re0 + S5_TILE_STATE] = bu[:, :S5_TILE_STATE].reshape(steps, batch, S5_TILE_STATE)
        bu_scr[:, :, S5_STATE + re0:S5_STATE + re0 + S5_TILE_STATE] = (
            bu[:, S5_TILE_STATE:].reshape(steps, batch, S5_TILE_STATE))

    for j in range(S5_STATE // lane_width):
        re = slice(j * lane_width, (j + 1) * lane_width)
        im = slice(S5_STATE + j * lane_width, S5_STATE + (j + 1) * lane_width)
        ar = a_ref[0:1, re]
        ai = a_ref[1:2, re]

        def step(t, carry):
            hr, hi = carry
            nr = ar * hr - ai * hi + bu_scr[t, :, re]
            ni = ar * hi + ai * hr + bu_scr[t, :, im]
            bu_scr[t, :, re] = nr
            bu_scr[t, :, im] = ni
            return nr, ni

        hr, hi = lax.fori_loop(0, steps, step, (h_scr[:, re], h_scr[:, im]), unroll=min(steps, 8))
        h_scr[:, re] = hr
        h_scr[:, im] = hi

    ys = []
    for t in range(S5_TILES):
        re0 = t * S5_TILE_STATE
        h_re = bu_scr[:, :, re0:re0 + S5_TILE_STATE].reshape(n_rows, S5_TILE_STATE).astype(BF16)
        h_im = bu_scr[:, :, S5_STATE + re0:S5_STATE + re0 + S5_TILE_STATE].reshape(n_rows, S5_TILE_STATE).astype(BF16)
        ys.append(_dot(h_re, wc_ref[t, :S5_TILE_STATE]) + _dot(h_im, wc_ref[t, S5_TILE_STATE:]))
    y = jnp.concatenate(ys, axis=1) + d_ref[...] * u
    y = jax.nn.gelu(y)
    out = y * jax.nn.sigmoid(_dot(y.astype(BF16), wglu_ref[...]) + bglu_ref[...])
    y_ref[...] = (_rms(out) * gain_ref[...]).reshape(steps, batch, S5_WIDTH)

    @pl.when(pl.program_id(0) == pl.num_programs(0) - 1)
    def _():
        hfin_ref[...] = h_scr[...]


def _s5(u_t, h0, a, wb, wc, d, wglu, bglu, gain, *, steps, lane_width):
    seq, batch, _ = u_t.shape
    const = lambda shape: _const_spec(shape, lambda t: (0,) * len(shape))
    return pl.pallas_call(
        functools.partial(_s5_kernel, steps=steps, batch=batch, lane_width=lane_width),
        grid=(seq // steps,),
        in_specs=[pl.BlockSpec((steps, batch, S5_WIDTH), lambda t: (t, 0, 0)),
                  const((batch, 2 * S5_STATE)), const((2, S5_STATE)),
                  const((S5_TILES, LANE, 2 * S5_TILE_STATE)), const((S5_TILES, 2 * S5_TILE_STATE, LANE)),
                  const((1, S5_WIDTH)), const((S5_WIDTH, S5_WIDTH)), const((1, S5_WIDTH)),
                  const((1, S5_WIDTH))],
        out_specs=[pl.BlockSpec((steps, batch, S5_WIDTH), lambda t: (t, 0, 0)),
                   pl.BlockSpec((batch, 2 * S5_STATE), lambda t: (0, 0))],
        out_shape=[jax.ShapeDtypeStruct((seq, batch, S5_WIDTH), F32),
                   jax.ShapeDtypeStruct((batch, 2 * S5_STATE), F32)],
        scratch_shapes=[pltpu.VMEM((steps, batch, 2 * S5_STATE), F32),
                        pltpu.VMEM((batch, 2 * S5_STATE), F32)],
        compiler_params=_cparams(("arbitrary",)),
        name="s5",
    )(u_t, h0, a, wb, wc, d, wglu, bglu, gain)


def _out_proj_kernel(g1p_ref, g1s_ref, scp_ref, scs_ref, shp_ref, shs_ref, og_ref, om_ref, os_ref, x_ref, gain_ref,
                     wg_ref, wm_ref, ws_ref, xo_ref, ht_ref, mix_scr, h_scr, *, groups, tiles):
    g1 = _ModRows(g1p_ref, g1s_ref, *tiles)
    sc = _ModRows(scp_ref, scs_ref, *tiles)
    sh = _ModRows(shp_ref, shs_ref, *tiles)
    mix_scr[...] = (_dot(og_ref[...].astype(BF16), wg_ref[...])
                    + _dot(om_ref[...].astype(BF16), wm_ref[...])
                    + _dot(os_ref[...].astype(BF16), ws_ref[...]))
    gain = gain_ref[...]

    def body(g, carry):
        r0 = pl.multiple_of(g * TOKENS_PER_GROUP, TOKENS_PER_GROUP)
        rows = pl.ds(r0, TOKENS_PER_GROUP)
        xn = x_ref[rows, :] + g1.row(g) * mix_scr[rows, :]
        xo_ref[rows, :] = xn
        h_scr[rows, :] = _modulated_group(xn, gain, sc.row(g), sh.row(g))
        return carry

    lax.fori_loop(0, groups, body, 0, unroll=GROUP_UNROLL)
    ht_ref[...] = pltpu.bitcast(h_scr[...].T.astype(BF16), jnp.uint32)


def _out_proj(og, om, os_, x, mod, gains, w_out, *, tokens, layer, tt):
    n_tok = x.shape[0]
    groups = tt // TOKENS_PER_GROUP
    tok = lambda width: pl.BlockSpec((tt, width), lambda i: (i, 0))
    w_rows = lambda height, block: _const_spec((None, height, D_MODEL), lambda i: (layer, block, 0))
    assert GLA_WIDTH == ML_WIDTH and (GLA_WIDTH + ML_WIDTH) % S5_WIDTH == 0
    return pl.pallas_call(
        functools.partial(_out_proj_kernel, groups=groups, tiles=tokens.tiles(tt)),
        grid=(n_tok // tt,),
        in_specs=(tokens.mod_specs(tt, layer, MOD_G1) + tokens.mod_specs(tt, layer, MOD_SC2)
                  + tokens.mod_specs(tt, layer, MOD_SH2)) + [
                  tok(GLA_WIDTH), tok(ML_WIDTH), tok(S5_WIDTH), tok(D_MODEL), _gain_spec(layer, 1),
                  w_rows(GLA_WIDTH, 0), w_rows(ML_WIDTH, 1),
                  w_rows(S5_WIDTH, (GLA_WIDTH + ML_WIDTH) // S5_WIDTH)],
        out_specs=[tok(D_MODEL), pl.BlockSpec((D_MODEL // 2, tt), lambda i: (0, i))],
        out_shape=[jax.ShapeDtypeStruct((n_tok, D_MODEL), F32),
                   jax.ShapeDtypeStruct((D_MODEL // 2, n_tok), jnp.uint32)],
        scratch_shapes=[pltpu.VMEM((tt, D_MODEL), F32), pltpu.VMEM((tt, D_MODEL), F32)],
        compiler_params=_cparams(("parallel",)),
        name="out_proj",
    )(mod, mod, mod, mod, mod, mod, og, om, os_, x, gains, w_out, w_out, w_out)


def _top_values(cur, count, out_scr=None, with_rank=False):
    neg_inf = jnp.float32(-jnp.inf)
    vals = []
    rank = jnp.full(cur.shape, float(count), F32) if with_rank else None
    for r in range(count):
        m = jnp.max(cur, axis=0, keepdims=True)
        vals.append(m)
        if out_scr is not None:
            out_scr[r:r + 1, :] = m
        hit = cur == m
        if with_rank:
            rank = jnp.where(hit, float(r), rank)
        if r + 1 < count:
            cur = jnp.where(hit, neg_inf, cur)
    return (vals, rank) if with_rank else vals


def _peer_route_kernel(ht_ref, wqt_ref, keys_ref, first_ref, second_ref, qt_scr, v1_scr, v2_scr, cand_scr,
                       *, tt):
    qt_scr[...] = _dot(wqt_ref[...], _unpack_rows(ht_ref[...]))
    for h in range(PEER_HEADS):
        sc = []
        for side in range(2):
            hs = 2 * h + side
            qb = qt_scr[hs * LANE:(hs + 1) * LANE, :].astype(BF16)
            sc.append(_dot(keys_ref[hs], qb))
        for lt in range(tt // LANE):
            ls = slice(lt * LANE, (lt + 1) * LANE)
            s1 = sc[0][:, ls]
            s2 = sc[1][:, ls]
            _top_values(s1, PEER_TOPK, v1_scr)
            _, rank2 = _top_values(s2, PEER_TOPK, v2_scr, with_rank=True)
            v1_head = v1_scr[0:SUBLANE, :]
            cand_scr[0:PEER_TOPK, :] = v1_scr[0:1, :] + v2_scr[...]
            cand_scr[PEER_TOPK:2 * PEER_TOPK, :] = v1_scr[...] + v2_scr[0:1, :]
            cand_scr[2 * PEER_TOPK:2 * PEER_TOPK + SUBLANE, :] = v1_scr[1:2, :] + v2_scr[0:SUBLANE, :]
            for b in range(1, 5):
                lo = 2 * PEER_TOPK + b * SUBLANE
                cand_scr[lo:lo + SUBLANE, :] = v1_head + v2_scr[b:b + 1, :]
            top = _top_values(cand_scr[...], PEER_TOPK)
            z = jnp.zeros_like(top[0])
            for tv in top:
                z = z + jnp.exp(tv - top[0])
            tau = top[PEER_TOPK - 1]
            n1 = jnp.zeros_like(s1)
            for b in range(SUBLANE):
                n1 = n1 + jnp.where(s1 + v2_scr[b:b + 1, :] >= tau, 1.0, 0.0)
            best = v1_scr[0:1, :]
            tail = jnp.zeros_like(best)
            for b in range(SUBLANE, PEER_TOPK):
                tail = tail + jnp.where(best + v2_scr[b:b + 1, :] >= tau, 1.0, 0.0)
            n1 = n1 + jnp.where(s1 == best, tail, 0.0)
            first_ref[2 * h, :, ls] = n1
            first_ref[2 * h + 1, :, ls] = jnp.exp(s1 - v1_scr[0:1, :]) / z
            second_ref[2 * h, :, ls] = rank2.astype(BF16)
            second_ref[2 * h + 1, :, ls] = jnp.exp(s2 - v2_scr[0:1, :]).astype(BF16)


def _peer_route(ht, wqt, keys, *, tt):
    n_tok = ht.shape[1]
    n_hs = 2 * PEER_HEADS
    assert PEER_TOPK == 2 * SUBLANE
    n_cand = 2 * PEER_TOPK + 5 * SUBLANE
    return pl.pallas_call(
        functools.partial(_peer_route_kernel, tt=tt),
        grid=(n_tok // tt,),
        in_specs=[pl.BlockSpec((D_MODEL // 2, tt), lambda i: (0, i)),
                  _const_spec((n_hs * LANE, D_MODEL), lambda i: (0, 0)),
                  _const_spec((n_hs, N_KEYS, LANE), lambda i: (0, 0, 0))],
        out_specs=[pl.BlockSpec((n_hs, N_KEYS, tt), lambda i: (0, 0, i)),
                   pl.BlockSpec((n_hs, N_KEYS, tt), lambda i: (0, 0, i))],
        out_shape=[jax.ShapeDtypeStruct((n_hs, N_KEYS, n_tok), F32),
                   jax.ShapeDtypeStruct((n_hs, N_KEYS, n_tok), BF16)],
        scratch_shapes=[pltpu.VMEM((n_hs * LANE, tt), F32), pltpu.VMEM((PEER_TOPK, LANE), F32),
                        pltpu.VMEM((PEER_TOPK, LANE), F32), pltpu.VMEM((n_cand, LANE), F32)],
        compiler_params=_cparams(("parallel",)),
        name="peer_route",
    )(ht, wqt, keys)


DENSE_FIRST_KEYS = 4
DENSE_SUB = DENSE_FIRST_KEYS * N_KEYS
GATE_ROWS = 4 * SUBLANE
GATE_PIECES = 4


def _gate_times_act(first_ref, row0, second_ref, act_ref, p_ref, lane_tiles):
    n_pieces = N_KEYS // GATE_ROWS
    zero = jnp.zeros((GATE_ROWS, LANE), BF16)
    for lt in lane_tiles:
        ls = slice(lt * LANE, (lt + 1) * LANE)
        for jp0 in range(0, n_pieces, GATE_PIECES):
            pieces = range(jp0, jp0 + GATE_PIECES)
            gates = [[zero for _ in pieces] for _ in range(DENSE_FIRST_KEYS)]
            for h in range(PEER_HEADS):
                counts, weights = [], []
                for ii in range(DENSE_FIRST_KEYS):
                    r = row0 + ii
                    counts.append(jnp.broadcast_to(first_ref[2 * h, 0, r:r + 1, ls], (GATE_ROWS, LANE)).astype(BF16))
                    weights.append(jnp.broadcast_to(first_ref[2 * h + 1, 0, r:r + 1, ls], (GATE_ROWS, LANE)).astype(BF16))
                for q, jp in enumerate(pieces):
                    js = slice(jp * GATE_ROWS, (jp + 1) * GATE_ROWS)
                    rank2 = second_ref[2 * h, js, ls]
                    e2 = second_ref[2 * h + 1, js, ls]
                    for ii in range(DENSE_FIRST_KEYS):
                        gates[ii][q] = gates[ii][q] + jnp.where(rank2 < counts[ii], e2, zero) * weights[ii]
            for ii in range(DENSE_FIRST_KEYS):
                for q, jp in enumerate(pieces):
                    rs = slice(ii * N_KEYS + jp * GATE_ROWS, ii * N_KEYS + (jp + 1) * GATE_ROWS)
                    p_ref[rs, ls] = gates[ii][q] * act_ref[rs, ls].astype(BF16)


def _peer_dense_kernel(ht_ref, u0_ref, u1_ref, vt0_ref, vt1_ref, first_a_ref, first_b_ref, second_in_ref,
                       x_ref, g2p_ref, g2s_ref, o_ref, acc_scr, act0, act1, p0, p1, second_ref,
                       *, tt, groups, tiles):
    k = pl.program_id(1)

    @pl.when(k == 0)
    def _():
        acc_scr[...] = jnp.zeros_like(acc_scr)
        act1[...] = jnp.zeros_like(act1)
        p0[...] = jnp.zeros_like(p0)
        second_ref[...] = second_in_ref[...]

    tok_half = tt // 2
    tiles_half = tok_half // LANE
    for half in range(2):
        ts = slice(half * tok_half, (half + 1) * tok_half)
        lane_tiles = range(half * tiles_half, (half + 1) * tiles_half)
        act0[:, ts] = jax.nn.gelu(_dot(_unpack_rows(u0_ref[...]), _unpack_rows(ht_ref[:, ts])))
        _gate_times_act(first_a_ref, DENSE_FIRST_KEYS, second_ref, act1, p1, lane_tiles)
        acc_scr[:, ts] += _dot(_unpack_rows(vt0_ref[...]), p0[:, ts])
    for half in range(2):
        ts = slice(half * tok_half, (half + 1) * tok_half)
        lane_tiles = range(half * tiles_half, (half + 1) * tiles_half)
        act1[:, ts] = jax.nn.gelu(_dot(_unpack_rows(u1_ref[...]), _unpack_rows(ht_ref[:, ts])))
        _gate_times_act(first_b_ref, 0, second_ref, act0, p0, lane_tiles)
        acc_scr[:, ts] += _dot(_unpack_rows(vt1_ref[...]), p1[:, ts])

    @pl.when(k == pl.num_programs(1) - 1)
    def _():
        o_ref[...] = acc_scr[...].T
        g2 = _ModRows(g2p_ref, g2s_ref, *tiles)

        def body(g, carry):
            rows = pl.ds(pl.multiple_of(g * TOKENS_PER_GROUP, TOKENS_PER_GROUP), TOKENS_PER_GROUP)
            o_ref[rows, :] = x_ref[rows, :] + g2.row(g) * o_ref[rows, :]
            return carry

        lax.fori_loop(0, groups, body, 0, unroll=GROUP_UNROLL)


def _peer_dense(ht, u_bf, vt_bf, first, second, x, mod, *, tokens, layer, tt):
    n_tok = x.shape[0]
    groups = tt // TOKENS_PER_GROUP
    n_hs = 2 * PEER_HEADS
    ne = 2 * DENSE_SUB
    n_i = ne // N_KEYS
    n_blocks = N_EXPERTS // ne
    tok_const = lambda shape, imap: pl.BlockSpec(shape, imap, pipeline_mode=pl.Buffered(1))
    by_first_key = lambda t: t.reshape(n_hs, n_blocks, n_i, n_tok)
    cur = lambda k: jnp.minimum(k, n_blocks - 1)
    prev = lambda k: jnp.maximum(k - 1, 0)
    key_spec = lambda blk: pl.BlockSpec((n_hs, 1, n_i, tt), lambda i, k: (0, blk(k), 0, i))
    u_spec = lambda sub: pl.BlockSpec((None, DENSE_SUB // 2, D_MODEL), lambda i, k: (layer, 2 * cur(k) + sub, 0))
    vt_spec = lambda sub: pl.BlockSpec((None, None, D_MODEL // 2, DENSE_SUB),
                                       lambda i, k: (layer, 2 * prev(k) + sub, 0, 0))
    return pl.pallas_call(
        functools.partial(_peer_dense_kernel, tt=tt, groups=groups, tiles=tokens.tiles(tt)),
        grid=(n_tok // tt, n_blocks + 1),
        in_specs=[tok_const((D_MODEL // 2, tt), lambda i, k: (0, i)),
                  u_spec(0), u_spec(1), vt_spec(0), vt_spec(1),
                  key_spec(prev), key_spec(cur),
                  tok_const((n_hs, N_KEYS, tt), lambda i, k: (0, 0, i)),
                  tok_const((tt, D_MODEL), lambda i, k: (i, 0))] + tokens.mod_specs(tt, layer, MOD_G2),
        out_specs=pl.BlockSpec((tt, D_MODEL), lambda i, k: (i, 0)),
        out_shape=jax.ShapeDtypeStruct((n_tok, D_MODEL), F32),
        scratch_shapes=[pltpu.VMEM((D_MODEL, tt), F32),
                        pltpu.VMEM((DENSE_SUB, tt), F32), pltpu.VMEM((DENSE_SUB, tt), F32),
                        pltpu.VMEM((DENSE_SUB, tt), BF16), pltpu.VMEM((DENSE_SUB, tt), BF16),
                        pltpu.VMEM((n_hs, N_KEYS, tt), BF16)],
        compiler_params=_cparams(("parallel", "arbitrary")),
        name="peer_dense",
    )(ht, u_bf, u_bf, vt_bf, vt_bf, by_first_key(first), by_first_key(first), second, x, mod, mod)


def _pack_experts_kernel(w_ref, o_ref, *, transpose):
    w = w_ref[...]
    if transpose:
        w = w.T
    o_ref[...] = pltpu.bitcast(w.astype(BF16), jnp.uint32)


def _pack_experts(w, *, transpose):
    blk = DENSE_SUB
    n_blocks = N_EXPERTS // blk
    if transpose:
        out_spec = pl.BlockSpec((None, None, D_MODEL // 2, blk), lambda l, j: (l, j, 0, 0))
        out_shape = (DEPTH, n_blocks, D_MODEL // 2, blk)
    else:
        out_spec = pl.BlockSpec((None, blk // 2, D_MODEL), lambda l, j: (l, j, 0))
        out_shape = (DEPTH, N_EXPERTS // 2, D_MODEL)
    return pl.pallas_call(
        functools.partial(_pack_experts_kernel, transpose=transpose),
        grid=(DEPTH, n_blocks),
        in_specs=[pl.BlockSpec((None, blk, D_MODEL), lambda l, j: (l, j, 0))],
        out_specs=out_spec,
        out_shape=jax.ShapeDtypeStruct(out_shape, jnp.uint32),
        compiler_params=_cparams(("parallel", "parallel")),
        name="pack_experts",
    )(w)


def _final_norm_kernel(x_ref, gain_ref, o_ref):
    o_ref[...] = _rms(x_ref[...]) * gain_ref[...]


def _final_norm(x, gain, *, tt, row0, n_rows):
    return pl.pallas_call(
        _final_norm_kernel,
        grid=(n_rows // tt,),
        in_specs=[pl.BlockSpec((tt, D_MODEL), lambda i: (row0 // tt + i, 0)),
                  pl.BlockSpec((1, D_MODEL), lambda i: (0, 0))],
        out_specs=pl.BlockSpec((tt, D_MODEL), lambda i: (i, 0)),
        out_shape=jax.ShapeDtypeStruct((n_rows, D_MODEL), F32),
        compiler_params=_cparams(("parallel",)),
        name="final_norm",
    )(x, gain)


PROMPT_ROWS = 256
STEP_ROWS = 128
S5_PROMPT_STEPS = 128


def _block_diag_in(bbar):
    rows_group = np.arange(S5_WIDTH) // S5_CH
    cols_group = np.arange(S5_STATE) // S5_P
    mask = jnp.asarray(rows_group[:, None] == cols_group[None, :])
    reps = (1,) * (bbar.ndim - 2) + (S5_GROUPS, 1)
    return jnp.where(mask, jnp.tile(bbar, reps), 0.0)


def _diag_tiles(w, rows, cols):
    return jnp.stack([w[:, t * rows:(t + 1) * rows, t * cols:(t + 1) * cols]
                      for t in range(w.shape[1] // rows)], axis=1)


def _block_diag_out(c):
    rows_group = np.arange(S5_STATE) // S5_P
    cols_group = np.arange(S5_WIDTH) // S5_CH
    mask = jnp.asarray(rows_group[:, None] == cols_group[None, :])
    per_state = c.transpose(0, 1, 3, 2).reshape(c.shape[0], S5_STATE, S5_CH)
    return jnp.where(mask, jnp.tile(per_state, (1, 1, S5_GROUPS)), 0.0)


def kernel(x_prompt, x_sample, state_gla, state_mlstm_c, state_mlstm_n, state_mlstm_m, state_s5_re, state_s5_im, c_prompt, c_sample, w_ada, b_ada, norm_gain, w_in, gla_w_gate_up, gla_b_gate, gla_norm_gain, ml_b_igate, ml_b_fgate, ml_norm_gain, s5_lambda_re, s5_lambda_im, s5_log_dt, s5_b_re, s5_b_im, s5_c_re, s5_c_im, s5_d, s5_w_glu, s5_b_glu, s5_norm_gain, w_out, peer_w_q, peer_sub_keys, peer_u, peer_v, final_gain):
    bp, lp, _ = x_prompt.shape
    bs, ls, _ = x_sample.shape
    assert ls == TOKENS_PER_GROUP and lp % PROMPT_ROWS == 0 and (bs * ls) % STEP_ROWS == 0
    n_prompt = bp * lp
    n_sample = bs * ls
    assert bp <= SUBLANE
    tokens = _TokenLayout(n_prompt, lp, bs)

    x = jnp.concatenate([x_prompt.reshape(n_prompt, D_MODEL), x_sample.reshape(n_sample, D_MODEL)], axis=0)

    c_all = jnp.concatenate([c_sample, c_prompt], axis=0)
    mod = _adaln(jnp.pad(c_all, ((0, SUBLANE - bp), (0, 0))), w_ada, b_ada)
    gains = norm_gain.reshape(DEPTH * 2, 1, D_MODEL)

    zeros = lambda *shape: jnp.zeros(shape, F32)
    seq_blocks = lp // PROMPT_ROWS
    step_blocks = n_sample // STEP_ROWS
    step_block0 = n_prompt // STEP_ROWS
    prompt_pad = (-bp) % SUBLANE

    w_in_p = _pad_z_columns(w_in.astype(BF16))
    wg_all = jnp.zeros((DEPTH, GLA_HEADS, LANE, LANE), F32).at[:, :, :GLA_RANK, :GLA_DK].set(
        gla_w_gate_up.reshape(DEPTH, GLA_RANK, GLA_HEADS, GLA_DK).transpose(0, 2, 1, 3)).astype(BF16)
    bg_all = jnp.zeros((DEPTH, GLA_HEADS, 1, LANE), F32).at[:, :, 0, :GLA_DK].set(
        gla_b_gate.reshape(DEPTH, GLA_HEADS, GLA_DK))
    gla_gain_all = gla_norm_gain.reshape(DEPTH, GLA_HEADS, 1, GLA_DV)
    bif_all = jnp.zeros((DEPTH, 1, LANE), F32).at[:, 0, :ML_HEADS].set(ml_b_igate).at[
        :, 0, F_GATE_LANE:F_GATE_LANE + ML_HEADS].set(ml_b_fgate)
    ml_gain_all = ml_norm_gain.reshape(DEPTH, ML_HEADS, 1, ML_DH)
    a_disc, bbar = _s5_discretise(s5_lambda_re, s5_lambda_im, s5_log_dt, s5_b_re, s5_b_im)
    wb_all = jnp.concatenate([_diag_tiles(_block_diag_in(bbar[:, 0]), LANE, S5_TILE_STATE),
                              _diag_tiles(_block_diag_in(bbar[:, 1]), LANE, S5_TILE_STATE)], axis=3).astype(BF16)
    wc_all = jnp.concatenate([_diag_tiles(_block_diag_out(s5_c_re), S5_TILE_STATE, LANE),
                              -_diag_tiles(_block_diag_out(s5_c_im), S5_TILE_STATE, LANE)], axis=2).astype(BF16)
    wglu_all = s5_w_glu.astype(BF16)
    w_out_bf = w_out.astype(BF16)
    wqt_all = peer_w_q.astype(BF16).transpose(0, 2, 1)
    keys_all = peer_sub_keys.reshape(DEPTH, 2 * PEER_HEADS, N_KEYS, LANE).astype(BF16)
    u_bf = _pack_experts(peer_u.astype(F32), transpose=False)
    vt_bf = _pack_experts(peer_v.astype(F32), transpose=True)

    st_gla = state_gla.astype(F32)
    st_c = state_mlstm_c.astype(F32)
    st_n = state_mlstm_n.astype(F32)[:, :, :, None, :]
    st_m = jnp.broadcast_to(state_mlstm_m.astype(F32)[:, :, :, None, None], state_mlstm_m.shape + (1, LANE))
    st_s5 = jnp.concatenate([state_s5_re.reshape(DEPTH, bs, S5_STATE), state_s5_im.reshape(DEPTH, bs, S5_STATE)],
                            axis=2).astype(F32)

    sg_s = mc_s = mn_s = mm_s = None
    new_states = []
    for l in range(DEPTH):
        z = _norm_proj(x, mod, gains, w_in_p, tokens=tokens, layer=l, tt=1024, tn=1152)

        seq_kw = dict(n_seq=bp, seq_blocks=seq_blocks, rows=PROMPT_ROWS, row_block0=0, carry=True)
        step_kw = dict(n_seq=step_blocks, seq_blocks=1, rows=STEP_ROWS, chunk=ls, row_block0=step_block0,
                       carry=False, layer=l)

        gla_w = (wg_all[l], bg_all[l], gla_gain_all[l])
        og, sg_p = _gla(z, *gla_w, zeros(bp, GLA_HEADS, GLA_DK, GLA_DV), chunk=GLA_CHUNK, **seq_kw)
        og, sg_s = _gla(z, *gla_w, st_gla, fill=(og,) if l == 0 else (og, sg_s), **step_kw)

        ml_w = (bif_all[l], ml_gain_all[l])
        om, mc_p, mn_p, mm_p = _mlstm(z, *ml_w, zeros(bp, ML_HEADS, ML_DH, ML_DH), zeros(bp, ML_HEADS, 1, ML_DH),
                                      zeros(bp, ML_HEADS, 1, LANE), chunk=ML_CHUNK, **seq_kw)
        om, mc_s, mn_s, mm_s = _mlstm(z, *ml_w, st_c, st_n, st_m,
                                      fill=(om,) if l == 0 else (om, mc_s, mn_s, mm_s), **step_kw)

        su = z[:, ZB_SU * LANE:ZB_SU * LANE + S5_WIDTH]
        u_p = jnp.pad(su[:n_prompt].reshape(bp, lp, S5_WIDTH).transpose(1, 0, 2), ((0, 0), (0, prompt_pad), (0, 0)))
        u_s = su[n_prompt:].reshape(bs, ls, S5_WIDTH).transpose(1, 0, 2)
        s5_args = (a_disc[l], wb_all[l], wc_all[l], s5_d[l].reshape(1, S5_WIDTH), wglu_all[l],
                   s5_b_glu[l][None, :], s5_norm_gain[l][None, :])
        os_p, hs_p = _s5(u_p, zeros(bp + prompt_pad, 2 * S5_STATE), *s5_args,
                         steps=S5_PROMPT_STEPS, lane_width=S5_STATE)
        os_s, hs_s = _s5(u_s, st_s5[l], *s5_args, steps=ls, lane_width=LANE)
        os_ = jnp.concatenate([os_p[:, :bp].transpose(1, 0, 2).reshape(n_prompt, S5_WIDTH),
                               os_s.transpose(1, 0, 2).reshape(n_sample, S5_WIDTH)], axis=0)

        x, ht = _out_proj(og, om, os_, x, mod, gains, w_out_bf, tokens=tokens, layer=l, tt=512)

        first, second = _peer_route(ht, wqt_all[l], keys_all[l], tt=512)
        x = _peer_dense(ht, u_bf, vt_bf, first, second, x, mod, tokens=tokens, layer=l, tt=512)

        new_states.append((sg_p, mc_p, mn_p[:, :, 0, :], mm_p[:, :, 0, 0], hs_p[:bp], hs_s))

    y_prompt = _final_norm(x, final_gain[None, :], tt=512, row0=0, n_rows=n_prompt)
    y_sample = _final_norm(x, final_gain[None, :], tt=512, row0=n_prompt, n_rows=n_sample)
    stack = lambda i: jnp.stack([ns[i] for ns in new_states])
    s5_p, s5_s = stack(4), stack(5)
    split_s5 = lambda h, lo: h[:, :, lo:lo + S5_STATE].reshape(DEPTH, -1, S5_GROUPS, S5_P)
    outs = (y_prompt.reshape(bp, lp, D_MODEL), y_sample.reshape(bs, ls, D_MODEL),
            stack(0), sg_s, stack(1), mc_s, stack(2), mn_s[:, :, :, 0, :], stack(3), mm_s[:, :, :, 0, 0],
            split_s5(s5_p, 0), split_s5(s5_s, 0), split_s5(s5_p, S5_STATE), split_s5(s5_s, S5_STATE))
    refs = (x_prompt, x_sample, state_gla, state_gla, state_mlstm_c, state_mlstm_c, state_mlstm_n, state_mlstm_n,
            state_mlstm_m, state_mlstm_m, state_s5_re, state_s5_re, state_s5_im, state_s5_im)
    return tuple(o.astype(r.dtype) for o, r in zip(outs, refs))
```
